```python
import jax, jax.numpy as jnp
from jax import lax
import numpy as np

D_MODEL = 1024
BATCH = 8
SEQ = 2048
DEPTH = 1
DEC_BATCH = 128
DEC_SEQ = 1
PAST_LEN = 16384
PAGE_SIZE = 128

N_META = 16
CHUNK = 64
HA = 4
DKA = 128
DVA = 128
CONV_A = 4
C_A = 2 * HA * DKA + HA * DVA
HB = 4
DKB = 128
DVB = 256
GATE_RANK = 16
GLA_GATE_NORM = 16.0
D_FF = 2816
CONV_F = 3
EPS = 1e-6

PROJ_SIZES = [HA * DKA, HA * DKA, HA * DVA, HA * DVA, HA, HA,
              HB * DKB, HB * DKB, HB * DVB, HB * DVB, GATE_RANK, D_MODEL, D_MODEL]
D_IN = int(sum(PROJ_SIZES))

kernel_name = "hybrid_gdn_gla_convffn_meta_step"


def rmsnorm(x, w):
    xf = x.astype(jnp.float32)
    y = xf * lax.rsqrt(jnp.mean(xf * xf, axis=-1, keepdims=True) + EPS)
    return (y * w.astype(jnp.float32)).astype(x.dtype)


def l2norm(a):
    return a * lax.rsqrt(jnp.sum(a * a, axis=-1, keepdims=True) + EPS)


def causal_dwconv(x, w, prev, bias=None):
    width = w.shape[0]
    t = x.shape[1]
    xp = jnp.concatenate([prev.astype(x.dtype), x], axis=1)
    y = xp[:, 0:t] * w[0]
    for i in range(1, width):
        y = y + xp[:, i:i + t] * w[i]
    if bias is not None:
        y = y + bias
    return y, xp[:, t:]


def _to_chunks(a):
    b, t, h = a.shape[:3]
    a = a.reshape((b, t // CHUNK, CHUNK, h) + a.shape[3:])
    return jnp.moveaxis(a, (1, 3), (0, 2))


def _from_chunks(o):
    n, b, h, c, d = o.shape
    return jnp.transpose(o, (1, 0, 3, 2, 4)).reshape(b, n * c, h, d)


def _delta_chunk_step(S, inp):
    q, k, v, g, beta = inp
    causal = jnp.tril(jnp.ones((CHUNK, CHUNK), bool))
    strict = jnp.tril(jnp.ones((CHUNK, CHUNK), bool), -1)
    G = jnp.cumsum(g, axis=-1)
    decay = jnp.exp(jnp.where(causal, G[..., :, None] - G[..., None, :], -jnp.inf))
    kb = k * beta[..., None]
    M = jnp.where(strict, jnp.einsum('bhid,bhjd->bhij', kb, k) * decay, 0.0)
    eye = jnp.broadcast_to(jnp.eye(CHUNK, dtype=M.dtype), M.shape)
    T = lax.linalg.triangular_solve(eye + M, eye, left_side=True, lower=True)
    u = jnp.einsum('bhij,bhje->bhie', T, v * beta[..., None])
    w = jnp.einsum('bhij,bhjd->bhid', T, kb * jnp.exp(G)[..., None])
    v_new = u - jnp.einsum('bhcd,bhde->bhce', w, S)
    attn = jnp.einsum('bhid,bhjd->bhij', q, k) * decay
    o = (jnp.einsum('bhcd,bhde->bhce', q * jnp.exp(G)[..., None], S)
         + jnp.einsum('bhij,bhje->bhie', attn, v_new))
    gl = G[..., -1]
    S = (S * jnp.exp(gl)[..., None, None]
         + jnp.einsum('bhcd,bhce->bhde', k * jnp.exp(gl[..., None] - G)[..., None], v_new))
    return S, o


def _delta_chunked(q, k, v, g, beta, s0):
    inp = tuple(_to_chunks(a) for a in (q, k, v, g, beta))
    S, o = lax.scan(_delta_chunk_step, s0.astype(jnp.float32), inp)
    return _from_chunks(o), S


def _delta_recurrent(q, k, v, g, beta, s0):
    def step(S, inp):
        qt, kt, vt, gt, bt = inp
        S = S * jnp.exp(gt)[..., None, None]
        err = vt - jnp.einsum('bhk,bhkv->bhv', kt, S)
        S = S + jnp.einsum('bhk,bhv->bhkv', kt, err * bt[..., None])
        return S, jnp.einsum('bhk,bhkv->bhv', qt, S)
    inp = tuple(jnp.moveaxis(a, 1, 0) for a in (q, k, v, g, beta))
    S, o = lax.scan(step, s0.astype(jnp.float32), inp)
    return jnp.moveaxis(o, 0, 1), S


def _gla_chunk_step(S, inp):
    q, k, v, lg = inp
    causal = jnp.tril(jnp.ones((CHUNK, CHUNK), bool))
    Bc = jnp.cumsum(lg, axis=-2)
    diff = Bc[:, :, :, None, :] - Bc[:, :, None, :, :]
    dec = jnp.exp(jnp.where(causal[..., None], diff, -jnp.inf))
    attn = jnp.einsum('bhid,bhjd,bhijd->bhij', q, k, dec)
    o = (jnp.einsum('bhcd,bhde->bhce', q * jnp.exp(Bc), S)
         + jnp.einsum('bhij,bhje->bhie', attn, v))
    bl = Bc[:, :, -1]
    S = (S * jnp.exp(bl)[..., None]
         + jnp.einsum('bhcd,bhce->bhde', k * jnp.exp(bl[:, :, None] - Bc), v))
    return S, o


def _gla_chunked(q, k, v, lg, s0):
    inp = tuple(_to_chunks(a) for a in (q, k, v, lg))
    S, o = lax.scan(_gla_chunk_step, s0.astype(jnp.float32), inp)
    return _from_chunks(o), S


def _gla_recurrent(q, k, v, lg, s0):
    def step(S, inp):
        qt, kt, vt, lt = inp
        S = S * jnp.exp(lt)[..., None] + jnp.einsum('bhk,bhv->bhkv', kt, vt)
        return S, jnp.einsum('bhk,bhkv->bhv', qt, S)
    inp = tuple(jnp.moveaxis(a, 1, 0) for a in (q, k, v, lg))
    S, o = lax.scan(step, s0.astype(jnp.float32), inp)
    return jnp.moveaxis(o, 0, 1), S


def _block(x, s_a0, conv_a_prev, s_b0, conv_f_prev, chunked,
           norm_mix, w_in, w_conv_a, a_log, dt_bias, w_gk2, b_gk, onorm_a, onorm_b,
           w_a_out, w_b_out, w_o, norm_ffn, w_ffn_in, w_conv_f, b_conv_f, w_ffn_out):
    f32 = jnp.float32
    bsz, t, _ = x.shape
    h = rmsnorm(x, norm_mix)
    idx = np.cumsum(PROJ_SIZES)[:-1].tolist()
    (qa, ka, va, za, ba, aa, qb, kb, vb, rb, lrb, gate_a, gate_b) = jnp.split(h @ w_in, idx, axis=-1)
    qkv_a, conv_a_new = causal_dwconv(jnp.concatenate([qa, ka, va], -1), w_conv_a, conv_a_prev)
    qkv_a = jax.nn.silu(qkv_a.astype(f32))
    qa, ka, va = jnp.split(qkv_a, [HA * DKA, 2 * HA * DKA], axis=-1)
    qa = l2norm(qa.reshape(bsz, t, HA, DKA)) * (DKA ** -0.5)
    ka = l2norm(ka.reshape(bsz, t, HA, DKA))
    va = va.reshape(bsz, t, HA, DVA)
    beta_a = jax.nn.sigmoid(ba.astype(f32))
    g_a = -jnp.exp(a_log.astype(f32)) * jax.nn.softplus(aa.astype(f32) + dt_bias.astype(f32))
    qb = qb.astype(f32).reshape(bsz, t, HB, DKB) * (DKB ** -0.5)
    kb = kb.astype(f32).reshape(bsz, t, HB, DKB)
    vb = vb.astype(f32).reshape(bsz, t, HB, DVB)
    lg_b = (jax.nn.log_sigmoid((lrb @ w_gk2 + b_gk).astype(f32)) / GLA_GATE_NORM).reshape(bsz, t, HB, DKB)
    if chunked:
        lead = (-N_META) % CHUNK
        pad = lambda a: jnp.pad(a, [(0, 0), (lead, 0)] + [(0, 0)] * (a.ndim - 2))
        o_a, s_a = _delta_chunked(pad(qa), pad(ka), pad(va), pad(g_a), pad(beta_a), s_a0)
        o_b, s_b = _gla_chunked(pad(qb), pad(kb), pad(vb), pad(lg_b), s_b0)
        o_a = o_a[:, lead:]
        o_b = o_b[:, lead:]
    else:
        o_a, s_a = _delta_recurrent(qa, ka, va, g_a, beta_a, s_a0)
        o_b, s_b = _gla_recurrent(qb, kb, vb, lg_b, s_b0)
    o_a = rmsnorm(o_a, onorm_a) * jax.nn.silu(za.astype(f32).reshape(bsz, t, HA, DVA))
    o_b = rmsnorm(o_b, onorm_b) * jax.nn.silu(rb.astype(f32).reshape(bsz, t, HB, DVB))
    y_a = o_a.reshape(bsz, t, HA * DVA).astype(x.dtype) @ w_a_out
    y_b = o_b.reshape(bsz, t, HB * DVB).astype(x.dtype) @ w_b_out
    mix = jax.nn.sigmoid(gate_a) * y_a + jax.nn.sigmoid(gate_b) * y_b
    x = x + mix @ w_o
    h = rmsnorm(x, norm_ffn)
    u, gf = jnp.split(h @ w_ffn_in, [D_FF], axis=-1)
    u, conv_f_new = causal_dwconv(u, w_conv_f, conv_f_prev, b_conv_f)
    x = x + (jax.nn.gelu(u) * gf) @ w_ffn_out
    return x, s_a.astype(x.dtype), conv_a_new, s_b.astype(x.dtype), conv_f_new


def setup_inputs(seed: int = 0) -> dict:
    key = jax.random.key(seed)
    ks = jax.random.split(key, 32)
    nrm = lambda k, s, sc: jax.random.normal(k, s, jnp.float32) * sc
    gain = lambda k, s: 1.0 + 0.01 * jax.random.normal(k, s, jnp.float32)
    dt = jnp.exp(jax.random.uniform(ks[10], (DEPTH, HA), jnp.float32) * (jnp.log(0.1) - jnp.log(1e-3)) + jnp.log(1e-3))
    return {
        'x_prompt': nrm(ks[0], (BATCH, SEQ, D_MODEL), 1.0),
        'x_sample': nrm(ks[1], (DEC_BATCH, DEC_SEQ, D_MODEL), 1.0),
        'state_delta': nrm(ks[2], (DEPTH, DEC_BATCH, HA, DKA, DVA), 0.1),
        'state_delta_conv': nrm(ks[3], (DEPTH, DEC_BATCH, CONV_A - 1, C_A), 1.0),
        'state_gla': nrm(ks[4], (DEPTH, DEC_BATCH, HB, DKB, DVB), 0.1),
        'state_ffn_conv': nrm(ks[5], (DEPTH, DEC_BATCH, CONV_F - 1, D_FF), 1.0),
        'meta_tokens': nrm(ks[6], (N_META, D_MODEL), 1.0),
        'norm_mix': gain(ks[7], (DEPTH, D_MODEL)),
        'w_in': nrm(ks[8], (DEPTH, D_MODEL, D_IN), D_MODEL ** -0.5),
        'w_conv_a': nrm(ks[9], (DEPTH, CONV_A, C_A), CONV_A ** -0.5),
        'a_log': jnp.log(jax.random.uniform(ks[11], (DEPTH, HA), jnp.float32, 1.0, 16.0)),
        'dt_bias': dt + jnp.log(-jnp.expm1(-dt)),
        'w_gk2': nrm(ks[12], (DEPTH, GATE_RANK, HB * DKB), GATE_RANK ** -0.5),
        'b_gk': nrm(ks[13], (DEPTH, HB * DKB), 0.1),
        'onorm_a': gain(ks[14], (DEPTH, DVA)),
        'onorm_b': gain(ks[15], (DEPTH, DVB)),
        'w_a_out': nrm(ks[16], (DEPTH, HA * DVA, D_MODEL), (HA * DVA) ** -0.5),
        'w_b_out': nrm(ks[17], (DEPTH, HB * DVB, D_MODEL), (HB * DVB) ** -0.5),
        'w_o': nrm(ks[18], (DEPTH, D_MODEL, D_MODEL), D_MODEL ** -0.5),
        'norm_ffn': gain(ks[19], (DEPTH, D_MODEL)),
        'w_ffn_in': nrm(ks[20], (DEPTH, D_MODEL, 2 * D_FF), D_MODEL ** -0.5),
        'w_conv_f': nrm(ks[21], (DEPTH, CONV_F, D_FF), CONV_F ** -0.5),
        'b_conv_f': nrm(ks[22], (DEPTH, D_FF), 0.01),
        'w_ffn_out': nrm(ks[23], (DEPTH, D_FF, D_MODEL), D_FF ** -0.5),
        'norm_final': gain(ks[24], (D_MODEL,)),
    }


def reference(x_prompt, x_sample, state_delta, state_delta_conv, state_gla, state_ffn_conv,
              meta_tokens, norm_mix, w_in, w_conv_a, a_log, dt_bias, w_gk2, b_gk, onorm_a, onorm_b,
              w_a_out, w_b_out, w_o, norm_ffn, w_ffn_in, w_conv_f, b_conv_f, w_ffn_out, norm_final):
    bsz = x_prompt.shape[0]
    meta = jnp.broadcast_to(meta_tokens.astype(x_prompt.dtype)[None], (bsz, N_META, D_MODEL))
    xp = jnp.concatenate([meta, x_prompt], axis=1)
    xs = x_sample
    new_p = ([], [], [], [])
    new_s = ([], [], [], [])
    for l in range(DEPTH):
        lw = (norm_mix[l], w_in[l], w_conv_a[l], a_log[l], dt_bias[l], w_gk2[l], b_gk[l],
              onorm_a[l], onorm_b[l], w_a_out[l], w_b_out[l], w_o[l], norm_ffn[l],
              w_ffn_in[l], w_conv_f[l], b_conv_f[l], w_ffn_out[l])
        xp, sa, ca, sb, cf = _block(
            xp, jnp.zeros((bsz, HA, DKA, DVA), jnp.float32),
            jnp.zeros((bsz, CONV_A - 1, C_A), xp.dtype),
            jnp.zeros((bsz, HB, DKB, DVB), jnp.float32),
            jnp.zeros((bsz, CONV_F - 1, D_FF), xp.dtype), True, *lw)
        for lst, val in zip(new_p, (sa, ca, sb, cf)):
            lst.append(val)
        xs, sa, ca, sb, cf = _block(xs, state_delta[l], state_delta_conv[l], state_gla[l],
                                    state_ffn_conv[l], False, *lw)
        for lst, val in zip(new_s, (sa, ca, sb, cf)):
            lst.append(val)
    y_prompt = rmsnorm(xp, norm_final)[:, N_META:]
    y_sample = rmsnorm(xs, norm_final)
    return (y_prompt, y_sample,
            jnp.stack(new_p[0]), jnp.stack(new_p[1]), jnp.stack(new_p[2]), jnp.stack(new_p[3]),
            jnp.stack(new_s[0]), jnp.stack(new_s[1]), jnp.stack(new_s[2]), jnp.stack(new_s[3]))
```

```python
import functools

import jax
import jax.numpy as jnp
from jax import lax
from jax.experimental import pallas as pl
from jax.experimental.pallas import tpu as pltpu

F32 = jnp.float32
BF16 = jnp.bfloat16
HIGHEST = lax.Precision.HIGHEST

D_MODEL = 1024
N_META = 16
CHUNK = 64
HA, DKA, DVA = 4, 128, 128
HB, DKB, DVB = 4, 128, 256
CONV_A = 4
C_A = 2 * HA * DKA + HA * DVA
GATE_RANK = 16
GLA_GATE_NORM = 16.0
D_FF = 2816
CONV_F = 3
EPS = 1e-6
LANES = 128
SUBLANES = 8

O_QKVA = 0
O_QB = O_QKVA + C_A
O_KB = O_QB + HB * DKB
O_VB = O_KB + HB * DKB
O_SMALL = O_VB + HB * DVB
O_POST = O_SMALL + LANES
P_ZA = 0
P_RB = P_ZA + HA * DVA
P_GA = P_RB + HB * DVB
P_GB = P_GA + D_MODEL
W_POST = P_GB + D_MODEL
W_IN_COLS = O_POST + W_POST
LANE_BETA = 0
LANE_G = HA

VMEM_LIMIT = 56 * 1024 * 1024


def _dot(a, b):
    return jnp.dot(a.astype(BF16), b.astype(BF16), preferred_element_type=F32)


def _dot_nt(a, b):
    return lax.dot_general(a.astype(BF16), b.astype(BF16), (((1,), (1,)), ((), ())),
                           preferred_element_type=F32)


def _dot_tn(a, b):
    return lax.dot_general(a.astype(BF16), b.astype(BF16), (((0,), (0,)), ((), ())),
                           preferred_element_type=F32)


def _dot_exact(a, b):
    return jnp.dot(a, b, precision=HIGHEST, preferred_element_type=F32)


def _rms(x, w):
    return x * lax.rsqrt(jnp.mean(x * x, axis=-1, keepdims=True) + EPS) * w


def _silu(x):
    return x * jax.nn.sigmoid(x)


def _gelu_tanh(x):
    return 0.5 * x * (1.0 + jnp.tanh(0.7978845608028654 * (x + 0.044715 * (x * x * x))))


def _l2n(x):
    return x * lax.rsqrt(jnp.sum(x * x, axis=-1, keepdims=True) + EPS)


def _iota2(n, m, axis):
    return lax.broadcasted_iota(jnp.int32, (n, m), axis)


def _gates(small, alog_row, dt_row):
    beta = jax.nn.sigmoid(small)
    g = -jnp.exp(alog_row) * jax.nn.softplus(small + dt_row)
    return beta, g


def _qkv_a_post(conv_out):
    act = _silu(conv_out)
    parts = []
    for h in range(HA):
        parts.append(_l2n(act[:, h * DKA:(h + 1) * DKA]) * (DKA ** -0.5))
    for h in range(HA):
        o = HA * DKA + h * DKA
        parts.append(_l2n(act[:, o:o + DKA]))
    parts.append(act[:, 2 * HA * DKA:])
    return jnp.concatenate(parts, axis=1)


def _mix_out(x, oa, ob, post, onorm_a, onorm_b, wa_ref, wb_ref, wo_ref):
    pa = []
    for h in range(HA):
        o = oa[:, h * DVA:(h + 1) * DVA]
        pa.append(_rms(o, onorm_a))
    oa_n = jnp.concatenate(pa, axis=1) * _silu(post[:, P_ZA:P_ZA + HA * DVA])
    pb = []
    for h in range(HB):
        o = ob[:, h * DVB:(h + 1) * DVB]
        pb.append(_rms(o, onorm_b))
    ob_n = jnp.concatenate(pb, axis=1) * _silu(post[:, P_RB:P_RB + HB * DVB])
    y_a = _dot(oa_n, wa_ref[...])
    y_b = _dot(ob_n, wb_ref[...])
    mix = (jax.nn.sigmoid(post[:, P_GA:P_GA + D_MODEL]) * y_a
           + jax.nn.sigmoid(post[:, P_GB:P_GB + D_MODEL]) * y_b)
    return x + _dot(mix, wo_ref[...])


def _inv_unit_lower(m, c):
    ri = _iota2(c, c, 0)
    ci = _iota2(c, c, 1)
    eye = (ri == ci).astype(F32)
    base = min(16, c)
    same = (ri // base) == (ci // base)
    a = jnp.where(same, -m, 0.0)
    p = eye + a
    pw = a
    k = 2
    while k < base:
        pw = _dot(pw, pw)
        p = p + _dot(p, pw)
        k *= 2
    size = base
    while size < c:
        nxt = size * 2
        same_nxt = (ri // nxt) == (ci // nxt)
        off = jnp.where(same_nxt & jnp.logical_not(same), m, 0.0)
        p = p - _dot(_dot(p, off), p)
        same = same_nxt
        size = nxt
    return p


def _delta_chunk(q, k, v, beta_col, g_col, g_row, s, c):
    ri = _iota2(c, c, 0)
    ci = _iota2(c, c, 1)
    causal = ri >= ci
    strict = ri > ci
    decay = jnp.where(causal, jnp.exp(jnp.where(causal, g_col - g_row, 0.0)), 0.0)
    kb = k * beta_col
    eg = jnp.exp(g_col)
    kq = _dot_nt(jnp.concatenate([kb, q], axis=0), k)
    m = jnp.where(strict, kq[0:c] * decay, 0.0)
    t = _inv_unit_lower(m, c)
    uw = _dot(t, jnp.concatenate([v * beta_col, kb * eg], axis=1))
    u = uw[:, 0:DVA]
    w = uw[:, DVA:]
    wq = _dot(jnp.concatenate([w, q * eg], axis=0), s)
    v_new = u - wq[0:c]
    attn = kq[c:] * decay
    o = wq[c:] + _dot(attn, v_new)
    g_last = g_col[c - 1:c, :]
    s_new = s * jnp.exp(g_last) + _dot_tn(k * jnp.exp(g_last - g_col), v_new)
    return o, s_new


def _level_ref_rows(bc, half, c):
    w = bc.shape[1]
    parts = []
    if half >= SUBLANES // 2:
        for blk in range(c // (2 * half)):
            mrow = blk * 2 * half + half
            parts.append(jnp.broadcast_to(bc[mrow:mrow + 1, :], (2 * half, w)))
    else:
        sub = _iota2(SUBLANES, w, 0)
        for grp in range(c // SUBLANES):
            acc = None
            for blk in range(SUBLANES // (2 * half)):
                mrow = grp * SUBLANES + blk * 2 * half + half
                b = jnp.broadcast_to(bc[mrow:mrow + 1, :], (SUBLANES, w))
                acc = b if acc is None else jnp.where(sub >= blk * 2 * half, b, acc)
            parts.append(acc)
    return jnp.concatenate(parts, axis=0)


def _gla_intra(q, k, bc, c):
    ri = _iota2(c, c, 0)
    ci = _iota2(c, c, 1)
    rowi = _iota2(c, q.shape[1], 0)
    attn = jnp.where(ri == ci, jnp.sum(q * k, axis=-1, keepdims=True), 0.0)
    half = c // 2
    while half >= 1:
        d = bc - _level_ref_rows(bc, half, c)
        upper = (rowi % (2 * half)) >= half
        e = jnp.exp(jnp.where(upper, d, -d))
        qs = jnp.where(upper, q * e, 0.0)
        ks = jnp.where(upper, 0.0, k * e)
        sc = _dot_nt(qs, ks)
        valid = ((ri // (2 * half)) == (ci // (2 * half))) & ((ri % (2 * half)) >= half) & ((ci % (2 * half)) < half)
        attn = attn + jnp.where(valid, sc, 0.0)
        half //= 2
    return attn


def _gla_chunk(q, k, v, bc, s, c):
    o = _dot(q * jnp.exp(bc), s) + _dot(_gla_intra(q, k, bc, c), v)
    b_last = bc[c - 1:c, :]
    e_col = jnp.exp(b_last).T
    s_new = s * e_col + _dot_tn(k * jnp.exp(b_last - bc), v)
    return o, s_new


def _mixer_body(x_ref, sa0_ref, sb0_ref, ca0_ref, nmix_ref, wca_ref, alog_ref, dt_ref, bgk_ref, ona_ref, onb_ref,
                w1_ref, w2_ref, wa_ref, wb_ref, wo_ref,
                x1_ref, sa_ref, ca_ref, sb_ref,
                cbuf, hb_s, qkv_s, beta_s, g_s, qkb_s, vb_s, lg_s, oa_s, ob_s, *, tt, c):
    t_idx = pl.program_id(1)

    @pl.when(t_idx == 0)
    def _():
        sa_ref[...] = sa0_ref[...]
        sb_ref[...] = sb0_ref[...]
        cbuf[SUBLANES - (CONV_A - 1):SUBLANES, :] = ca0_ref[...]

    x = x_ref[...]
    hb_s[...] = _rms(x, nmix_ref[...]).astype(BF16)
    hb = hb_s[...]
    cbuf[SUBLANES:SUBLANES + tt, :] = jnp.dot(hb, w1_ref[:, O_QKVA:O_QKVA + C_A], preferred_element_type=F32)
    base = SUBLANES - (CONV_A - 1)
    acc = cbuf[base:base + tt, :] * wca_ref[0:1, :]
    for i in range(1, CONV_A):
        acc = acc + cbuf[base + i:base + i + tt, :] * wca_ref[i:i + 1, :]
    tail = cbuf[tt + base:tt + SUBLANES, :]
    cbuf[base:SUBLANES, :] = tail
    ca_ref[...] = tail
    qkv_s[...] = _qkv_a_post(acc)
    small = jnp.dot(hb, w1_ref[:, O_SMALL:O_SMALL + LANES], preferred_element_type=F32)
    beta, g = _gates(small, alog_ref[...], dt_ref[...])
    beta_s[...] = beta
    g_s[...] = g
    lg_pre = _dot(small, w2_ref[...]) + bgk_ref[...]
    lg_s[...] = jax.nn.log_sigmoid(lg_pre) / GLA_GATE_NORM
    qkb = jnp.dot(hb, w1_ref[:, O_QB:O_QB + 2 * HB * DKB], preferred_element_type=F32)
    qkb_s[:, 0:HB * DKB] = qkb[:, 0:HB * DKB] * (DKB ** -0.5)
    qkb_s[:, HB * DKB:] = qkb[:, HB * DKB:]
    vb_s[...] = jnp.dot(hb, w1_ref[:, O_VB:O_VB + HB * DVB], preferred_element_type=F32)

    tri = (_iota2(c, c, 0) >= _iota2(c, c, 1)).astype(F32)

    def chunk_body(ci, carry):
        r0 = pl.multiple_of(ci * c, c)
        rows = pl.ds(r0, c)
        g_cum = _dot_exact(tri, g_s[rows, :])
        g_cum_t = g_cum.T
        beta_c = beta_s[rows, :]
        bc_all = _dot_exact(tri, lg_s[rows, :])
        for h in range(HA):
            q = qkv_s[rows, h * DKA:(h + 1) * DKA]
            k = qkv_s[rows, HA * DKA + h * DKA:HA * DKA + (h + 1) * DKA]
            v = qkv_s[rows, 2 * HA * DKA + h * DVA:2 * HA * DKA + (h + 1) * DVA]
            o, s_new = _delta_chunk(q, k, v, beta_c[:, LANE_BETA + h:LANE_BETA + h + 1],
                                    g_cum[:, LANE_G + h:LANE_G + h + 1],
                                    g_cum_t[LANE_G + h:LANE_G + h + 1, :], sa_ref[h], c)
            sa_ref[h] = s_new
            oa_s[rows, h * DVA:(h + 1) * DVA] = o
        for h in range(HB):
            q = qkb_s[rows, h * DKB:(h + 1) * DKB]
            k = qkb_s[rows, HB * DKB + h * DKB:HB * DKB + (h + 1) * DKB]
            v = vb_s[rows, h * DVB:(h + 1) * DVB]
            o, s_new = _gla_chunk(q, k, v, bc_all[:, h * DKB:(h + 1) * DKB], sb_ref[h], c)
            sb_ref[h] = s_new
            ob_s[rows, h * DVB:(h + 1) * DVB] = o
        return carry

    lax.fori_loop(0, tt // c, chunk_body, 0)

    post = jnp.dot(hb, w1_ref[:, O_POST:O_POST + W_POST], preferred_element_type=F32)
    x1_ref[...] = _mix_out(x, oa_s[...], ob_s[...], post, ona_ref[...], onb_ref[...], wa_ref, wb_ref, wo_ref)


def _const_spec(shape):
    nd = len(shape)
    return pl.BlockSpec(shape, lambda *_: (0,) * nd, pipeline_mode=pl.Buffered(1))


def _mixer_call(x, sa0, sb0, ca0, p, tt):
    b, t, _ = x.shape
    c = CHUNK
    nt = t // tt
    row = lambda n: _const_spec((1, n))
    in_specs = [
        pl.BlockSpec((None, tt, D_MODEL), lambda i, j: (i, j, 0)),
        _const_spec((HA, DKA, DVA)), _const_spec((HB, DKB, DVB)), _const_spec((CONV_A - 1, C_A)),
        row(D_MODEL), _const_spec((CONV_A, C_A)), row(LANES), row(LANES), row(HB * DKB), row(DVA), row(DVB),
        _const_spec((D_MODEL, W_IN_COLS)), _const_spec((LANES, HB * DKB)),
        _const_spec((HA * DVA, D_MODEL)), _const_spec((HB * DVB, D_MODEL)), _const_spec((D_MODEL, D_MODEL)),
    ]
    out_specs = [
        pl.BlockSpec((None, tt, D_MODEL), lambda i, j: (i, j, 0)),
        pl.BlockSpec((None, HA, DKA, DVA), lambda i, j: (i, 0, 0, 0)),
        pl.BlockSpec((None, CONV_A - 1, C_A), lambda i, j: (i, 0, 0)),
        pl.BlockSpec((None, HB, DKB, DVB), lambda i, j: (i, 0, 0, 0)),
    ]
    out_shape = [
        jax.ShapeDtypeStruct((b, t, D_MODEL), F32),
        jax.ShapeDtypeStruct((b, HA, DKA, DVA), F32),
        jax.ShapeDtypeStruct((b, CONV_A - 1, C_A), F32),
        jax.ShapeDtypeStruct((b, HB, DKB, DVB), F32),
    ]
    scratch = [
        pltpu.VMEM((tt + SUBLANES, C_A), F32),
        pltpu.VMEM((tt, D_MODEL), BF16),
        pltpu.VMEM((tt, C_A), F32),
        pltpu.VMEM((tt, LANES), F32),
        pltpu.VMEM((tt, LANES), F32),
        pltpu.VMEM((tt, 2 * HB * DKB), F32),
        pltpu.VMEM((tt, HB * DVB), F32),
        pltpu.VMEM((tt, HB * DKB), F32),
        pltpu.VMEM((tt, HA * DVA), F32),
        pltpu.VMEM((tt, HB * DVB), F32),
    ]
    return pl.pallas_call(
        functools.partial(_mixer_body, tt=tt, c=c),
        grid=(b, nt), in_specs=in_specs, out_specs=out_specs, out_shape=out_shape, scratch_shapes=scratch,
        compiler_params=pltpu.CompilerParams(dimension_semantics=("arbitrary", "arbitrary"),
                                             vmem_limit_bytes=VMEM_LIMIT),
        name="mixer",
    )(x, sa0, sb0, ca0, p["norm_mix"], p["w_conv_a"], p["alog_row"], p["dt_row"], p["b_gk"], p["onorm_a"],
      p["onorm_b"], p["w1"], p["w2"], p["w_a_out"], p["w_b_out"], p["w_o"])


def _ffn_core(x1, u_conv, gf, wdn_ref, nfin):
    act = _gelu_tanh(u_conv) * gf
    x2 = x1 + _dot(act, wdn_ref[...])
    return _rms(x2, nfin)


def _ffn_body(x1_ref, cf0_ref, nffn_ref, wcf_ref, bcf_ref, nfin_ref, wup_ref, wdn_ref,
              y_ref, cf_ref, ubuf, *, tt):
    t_idx = pl.program_id(1)
    base = SUBLANES - (CONV_F - 1)

    @pl.when(t_idx == 0)
    def _():
        ubuf[base:SUBLANES, :] = cf0_ref[...]

    x1 = x1_ref[...]
    h2 = _rms(x1, nffn_ref[...]).astype(BF16)
    ubuf[SUBLANES:SUBLANES + tt, :] = jnp.dot(h2, wup_ref[:, 0:D_FF], preferred_element_type=F32)
    gf = jnp.dot(h2, wup_ref[:, D_FF:], preferred_element_type=F32)
    acc = ubuf[base:base + tt, :] * wcf_ref[0:1, :]
    for i in range(1, CONV_F):
        acc = acc + ubuf[base + i:base + i + tt, :] * wcf_ref[i:i + 1, :]
    acc = acc + bcf_ref[...]
    tail = ubuf[tt + base:tt + SUBLANES, :]
    ubuf[base:SUBLANES, :] = tail
    cf_ref[...] = tail
    y_ref[...] = _ffn_core(x1, acc, gf, wdn_ref, nfin_ref[...])


def _ffn_call(x1, cf0, p, tt):
    b, t, _ = x1.shape
    nt = t // tt
    row = lambda n: _const_spec((1, n))
    in_specs = [
        pl.BlockSpec((None, tt, D_MODEL), lambda i, j: (i, j, 0)),
        _const_spec((CONV_F - 1, D_FF)), row(D_MODEL), _const_spec((CONV_F, D_FF)), row(D_FF), row(D_MODEL),
        _const_spec((D_MODEL, 2 * D_FF)), _const_spec((D_FF, D_MODEL)),
    ]
    out_specs = [
        pl.BlockSpec((None, tt, D_MODEL), lambda i, j: (i, j, 0)),
        pl.BlockSpec((None, CONV_F - 1, D_FF), lambda i, j: (i, 0, 0)),
    ]
    out_shape = [jax.ShapeDtypeStruct((b, t, D_MODEL), F32), jax.ShapeDtypeStruct((b, CONV_F - 1, D_FF), F32)]
    return pl.pallas_call(
        functools.partial(_ffn_body, tt=tt),
        grid=(b, nt), in_specs=in_specs, out_specs=out_specs, out_shape=out_shape,
        scratch_shapes=[pltpu.VMEM((tt + SUBLANES, D_FF), F32)],
        compiler_params=pltpu.CompilerParams(dimension_semantics=("arbitrary", "arbitrary"),
                                             vmem_limit_bytes=VMEM_LIMIT),
        name="convffn",
    )(x1, cf0, p["norm_ffn"], p["w_conv_f"], p["b_conv_f"], p["norm_final"], p["w_ffn_in"], p["w_ffn_out"])


def _dec_head_body(x_ref, cs_ref, nmix_ref, wca_ref, alog_ref, dt_ref, bgk_ref, w1_ref, w2_ref,
                   qkva_ref, beta_ref, g_ref, qkvb_ref, lg_ref, post_ref, csn_ref):
    x = x_ref[...]
    hb = _rms(x, nmix_ref[...]).astype(BF16)
    pre = jnp.dot(hb, w1_ref[:, O_QKVA:O_QKVA + C_A], preferred_element_type=F32)
    acc = cs_ref[:, 0:C_A] * wca_ref[0:1, :]
    for i in range(1, CONV_A - 1):
        acc = acc + cs_ref[:, i * C_A:(i + 1) * C_A] * wca_ref[i:i + 1, :]
    acc = acc + pre * wca_ref[CONV_A - 1:CONV_A, :]
    for i in range(CONV_A - 2):
        csn_ref[:, i * C_A:(i + 1) * C_A] = cs_ref[:, (i + 1) * C_A:(i + 2) * C_A]
    csn_ref[:, (CONV_A - 2) * C_A:] = pre
    qkva_ref[...] = _qkv_a_post(acc)
    small = jnp.dot(hb, w1_ref[:, O_SMALL:O_SMALL + LANES], preferred_element_type=F32)
    beta, g = _gates(small, alog_ref[...], dt_ref[...])
    beta_ref[...] = beta
    g_ref[...] = g
    lg_ref[...] = jax.nn.log_sigmoid(_dot(small, w2_ref[...]) + bgk_ref[...]) / GLA_GATE_NORM
    qkb = jnp.dot(hb, w1_ref[:, O_QB:O_QB + 2 * HB * DKB], preferred_element_type=F32)
    qkvb_ref[:, 0:HB * DKB] = qkb[:, 0:HB * DKB] * (DKB ** -0.5)
    qkvb_ref[:, HB * DKB:2 * HB * DKB] = qkb[:, HB * DKB:]
    qkvb_ref[:, 2 * HB * DKB:] = jnp.dot(hb, w1_ref[:, O_VB:O_VB + HB * DVB], preferred_element_type=F32)
    post_ref[...] = jnp.dot(hb, w1_ref[:, O_POST:O_POST + W_POST], preferred_element_type=F32)


def _dec_head_call(xs, cs, p):
    n = xs.shape[0]
    shapes = [(n, C_A), (n, LANES), (n, LANES), (n, 2 * HB * DKB + HB * DVB), (n, HB * DKB), (n, W_POST),
              (n, (CONV_A - 1) * C_A)]
    return pl.pallas_call(
        _dec_head_body,
        out_shape=[jax.ShapeDtypeStruct(s, F32) for s in shapes],
        compiler_params=pltpu.CompilerParams(vmem_limit_bytes=VMEM_LIMIT),
        name="decode_head",
    )(xs, cs, p["norm_mix"], p["w_conv_a"], p["alog_row"], p["dt_row"], p["b_gk"], p["w1"], p["w2"])


def _dec_rec_body(qkva_ref, beta_ref, g_ref, qkvb_ref, lg_ref, sa_ref, sb_ref,
                  oa_ref, ob_ref, san_ref, sbn_ref, *, tb):
    beta = beta_ref[...]
    eg = jnp.exp(g_ref[...])
    for h in range(HA):
        q_t = qkva_ref[:, h * DKA:(h + 1) * DKA].T
        k_t = qkva_ref[:, HA * DKA + h * DKA:HA * DKA + (h + 1) * DKA].T
        for j in range(tb):
            k_col = k_t[:, j:j + 1]
            s = sa_ref[j, h] * eg[j:j + 1, LANE_G + h:LANE_G + h + 1]
            v = qkva_ref[j:j + 1, 2 * HA * DKA + h * DVA:2 * HA * DKA + (h + 1) * DVA]
            err = (v - jnp.sum(k_col * s, axis=0, keepdims=True)) * beta[j:j + 1, LANE_BETA + h:LANE_BETA + h + 1]
            s = s + k_col * err
            san_ref[j, h] = s
            oa_ref[j:j + 1, h * DVA:(h + 1) * DVA] = jnp.sum(q_t[:, j:j + 1] * s, axis=0, keepdims=True)
    for h in range(HB):
        q_t = qkvb_ref[:, h * DKB:(h + 1) * DKB].T
        k_t = qkvb_ref[:, HB * DKB + h * DKB:HB * DKB + (h + 1) * DKB].T
        d_t = jnp.exp(lg_ref[:, h * DKB:(h + 1) * DKB]).T
        for j in range(tb):
            v = qkvb_ref[j:j + 1, 2 * HB * DKB + h * DVB:2 * HB * DKB + (h + 1) * DVB]
            s = sb_ref[j, h] * d_t[:, j:j + 1] + k_t[:, j:j + 1] * v
            sbn_ref[j, h] = s
            ob_ref[j:j + 1, h * DVB:(h + 1) * DVB] = jnp.sum(q_t[:, j:j + 1] * s, axis=0, keepdims=True)


def _dec_rec_call(qkva, beta, g, qkvb, lg, sa, sb, tb):
    n = qkva.shape[0]
    rows = lambda w: pl.BlockSpec((tb, w), lambda i: (i, 0))
    in_specs = [rows(C_A), rows(LANES), rows(LANES), rows(2 * HB * DKB + HB * DVB), rows(HB * DKB),
                pl.BlockSpec((tb, HA, DKA, DVA), lambda i: (i, 0, 0, 0)),
                pl.BlockSpec((tb, HB, DKB, DVB), lambda i: (i, 0, 0, 0))]
    out_specs = [rows(HA * DVA), rows(HB * DVB),
                 pl.BlockSpec((tb, HA, DKA, DVA), lambda i: (i, 0, 0, 0)),
                 pl.BlockSpec((tb, HB, DKB, DVB), lambda i: (i, 0, 0, 0))]
    out_shape = [jax.ShapeDtypeStruct((n, HA * DVA), F32), jax.ShapeDtypeStruct((n, HB * DVB), F32),
                 jax.ShapeDtypeStruct(sa.shape, F32), jax.ShapeDtypeStruct(sb.shape, F32)]
    return pl.pallas_call(
        functools.partial(_dec_rec_body, tb=tb),
        grid=(n // tb,), in_specs=in_specs, out_specs=out_specs, out_shape=out_shape,
        compiler_params=pltpu.CompilerParams(dimension_semantics=("arbitrary",), vmem_limit_bytes=VMEM_LIMIT),
        name="decode_recurrence",
    )(qkva, beta, g, qkvb, lg, sa, sb)


def _dec_tail_body(x_ref, oa_ref, ob_ref, post_ref, cf_ref, ona_ref, onb_ref, nffn_ref, wcf_ref, bcf_ref, nfin_ref,
                   wa_ref, wb_ref, wo_ref, wup_ref, wdn_ref, y_ref, cfn_ref):
    x1 = _mix_out(x_ref[...], oa_ref[...], ob_ref[...], post_ref[...], ona_ref[...], onb_ref[...],
                  wa_ref, wb_ref, wo_ref)
    h2 = _rms(x1, nffn_ref[...]).astype(BF16)
    u = jnp.dot(h2, wup_ref[:, 0:D_FF], preferred_element_type=F32)
    gf = jnp.dot(h2, wup_ref[:, D_FF:], preferred_element_type=F32)
    acc = cf_ref[:, 0:D_FF] * wcf_ref[0:1, :]
    for i in range(1, CONV_F - 1):
        acc = acc + cf_ref[:, i * D_FF:(i + 1) * D_FF] * wcf_ref[i:i + 1, :]
    acc = acc + u * wcf_ref[CONV_F - 1:CONV_F, :] + bcf_ref[...]
    for i in range(CONV_F - 2):
        cfn_ref[:, i * D_FF:(i + 1) * D_FF] = cf_ref[:, (i + 1) * D_FF:(i + 2) * D_FF]
    cfn_ref[:, (CONV_F - 2) * D_FF:] = u
    y_ref[...] = _ffn_core(x1, acc, gf, wdn_ref, nfin_ref[...])


def _dec_tail_call(xs, oa, ob, post, cf, p):
    n = xs.shape[0]
    return pl.pallas_call(
        _dec_tail_body,
        out_shape=[jax.ShapeDtypeStruct((n, D_MODEL), F32), jax.ShapeDtypeStruct((n, (CONV_F - 1) * D_FF), F32)],
        compiler_params=pltpu.CompilerParams(vmem_limit_bytes=VMEM_LIMIT),
        name="decode_tail",
    )(xs, oa, ob, post, cf, p["onorm_a"], p["onorm_b"], p["norm_ffn"], p["w_conv_f"], p["b_conv_f"],
      p["norm_final"], p["w_a_out"], p["w_b_out"], p["w_o"], p["w_ffn_in"], p["w_ffn_out"])


def _prep_params(l, norm_mix, w_in, w_conv_a, a_log, dt_bias, w_gk2, b_gk, onorm_a, onorm_b, w_a_out, w_b_out,
                 w_o, norm_ffn, w_ffn_in, w_conv_f, b_conv_f, w_ffn_out, norm_final):
    sizes = [HA * DKA, HA * DKA, HA * DVA, HA * DVA, HA, HA, HB * DKB, HB * DKB, HB * DVB, HB * DVB, GATE_RANK,
             D_MODEL, D_MODEL]
    offs = [0]
    for s in sizes:
        offs.append(offs[-1] + s)
    w = w_in[l]
    seg = lambda i: w[:, offs[i]:offs[i + 1]]
    n_small = 2 * HA + GATE_RANK
    pad = jnp.zeros((D_MODEL, LANES - n_small), w.dtype)
    w1 = jnp.concatenate([seg(0), seg(1), seg(2), seg(6), seg(7), seg(8), seg(4), seg(5), seg(10), pad,
                          seg(3), seg(9), seg(11), seg(12)], axis=1).astype(BF16)
    w2 = jnp.zeros((LANES, HB * DKB), F32).at[2 * HA:n_small].set(w_gk2[l]).astype(BF16)
    lane_row = lambda v: jnp.zeros((1, LANES), F32).at[0, LANE_G:LANE_G + HA].set(v)
    return dict(
        norm_mix=norm_mix[l][None], w_conv_a=w_conv_a[l], alog_row=lane_row(a_log[l]), dt_row=lane_row(dt_bias[l]),
        b_gk=b_gk[l][None], onorm_a=onorm_a[l][None], onorm_b=onorm_b[l][None], w1=w1, w2=w2,
        w_a_out=w_a_out[l].astype(BF16), w_b_out=w_b_out[l].astype(BF16), w_o=w_o[l].astype(BF16),
        norm_ffn=norm_ffn[l][None], w_ffn_in=w_ffn_in[l].astype(BF16), w_conv_f=w_conv_f[l],
        b_conv_f=b_conv_f[l][None], w_ffn_out=w_ffn_out[l].astype(BF16), norm_final=norm_final[None])


PROMPT_TILE = 256
DECODE_TILE = 8


def kernel(x_prompt, x_sample, state_delta, state_delta_conv, state_gla, state_ffn_conv, meta_tokens, norm_mix, w_in, w_conv_a, a_log, dt_bias, w_gk2, b_gk, onorm_a, onorm_b, w_a_out, w_b_out, w_o, norm_ffn, w_ffn_in, w_conv_f, b_conv_f, w_ffn_out, norm_final):
    assert w_in.shape[0] == 1, "single layer only"
    l = 0
    p = _prep_params(l, norm_mix, w_in, w_conv_a, a_log, dt_bias, w_gk2, b_gk, onorm_a, onorm_b, w_a_out, w_b_out,
                     w_o, norm_ffn, w_ffn_in, w_conv_f, b_conv_f, w_ffn_out, norm_final)
    bsz, seq, _ = x_prompt.shape
    n_dec = x_sample.shape[0]

    xm = jnp.concatenate([jnp.zeros((CHUNK - N_META, D_MODEL), F32), meta_tokens.astype(F32)], axis=0)[None]
    x1m, sa0, ca0, sb0 = _mixer_call(xm, jnp.zeros((HA, DKA, DVA), F32), jnp.zeros((HB, DKB, DVB), F32),
                                     jnp.zeros((CONV_A - 1, C_A), F32), p, CHUNK)
    _, cf0 = _ffn_call(x1m, jnp.zeros((CONV_F - 1, D_FF), F32), p, CHUNK)

    x1, sa_p, ca_p, sb_p = _mixer_call(x_prompt, sa0[0], sb0[0], ca0[0], p, PROMPT_TILE)
    y_prompt, cf_p = _ffn_call(x1, cf0[0], p, PROMPT_TILE)

    xs = x_sample.reshape(n_dec, D_MODEL)
    cs = state_delta_conv[l].reshape(n_dec, (CONV_A - 1) * C_A)
    cfs = state_ffn_conv[l].reshape(n_dec, (CONV_F - 1) * D_FF)
    qkva, beta, g, qkvb, lg, post, cs_new = _dec_head_call(xs, cs, p)
    oa, ob, sa_s, sb_s = _dec_rec_call(qkva, beta, g, qkvb, lg, state_delta[l], state_gla[l], DECODE_TILE)
    y_s, cf_s = _dec_tail_call(xs, oa, ob, post, cfs, p)

    return (y_prompt, y_s.reshape(n_dec, 1, D_MODEL),
            sa_p[None], ca_p[None], sb_p[None], cf_p[None],
            sa_s[None], cs_new.reshape(1, n_dec, CONV_A - 1, C_A), sb_s[None],
            cf_s.reshape(1, n_dec, CONV_F - 1, D_FF))
```

```python
import functools

import jax
import jax.numpy as jnp
from jax import lax
from jax.experimental import pallas as pl
from jax.experimental.pallas import tpu as pltpu

F32 = jnp.float32
BF16 = jnp.bfloat16
HIGHEST = lax.Precision.HIGHEST

D_MODEL = 1024
N_META = 16
CHUNK = 64
HA, DKA, DVA = 4, 128, 128
HB, DKB, DVB = 4, 128, 256
CONV_A = 4
C_A = 2 * HA * DKA + HA * DVA
GATE_RANK = 16
GLA_GATE_NORM = 16.0
D_FF = 2816
CONV_F = 3
EPS = 1e-6
LANES = 128
SUBLANES = 8

O_QKVA = 0
O_QB = O_QKVA + C_A
O_KB = O_QB + HB * DKB
O_VB = O_KB + HB * DKB
O_SMALL = O_VB + HB * DVB
O_POST = O_SMALL + LANES
P_ZA = 0
P_RB = P_ZA + HA * DVA
P_GA = P_RB + HB * DVB
P_GB = P_GA + D_MODEL
W_POST = P_GB + D_MODEL
W_IN_COLS = O_POST + W_POST
LANE_BETA = 0
LANE_G = HA

VMEM_LIMIT = 56 * 1024 * 1024


def _dot(a, b):
    return jnp.dot(a.astype(BF16), b.astype(BF16), preferred_element_type=F32)


def _dot_nt(a, b):
    return lax.dot_general(a.astype(BF16), b.astype(BF16), (((1,), (1,)), ((), ())),
                           preferred_element_type=F32)


def _dot_tn(a, b):
    return lax.dot_general(a.astype(BF16), b.astype(BF16), (((0,), (0,)), ((), ())),
                           preferred_element_type=F32)


def _dot_exact(a, b):
    return jnp.dot(a, b, precision=HIGHEST, preferred_element_type=F32)


def _rms(x, w):
    return x * lax.rsqrt(jnp.mean(x * x, axis=-1, keepdims=True) + EPS) * w


def _silu(x):
    return x * jax.nn.sigmoid(x)


def _gelu_tanh(x):
    return 0.5 * x * (1.0 + jnp.tanh(0.7978845608028654 * (x + 0.044715 * (x * x * x))))


def _l2n(x):
    return x * lax.rsqrt(jnp.sum(x * x, axis=-1, keepdims=True) + EPS)


def _iota2(n, m, axis):
    return lax.broadcasted_iota(jnp.int32, (n, m), axis)


def _gates(small, alog_row, dt_row):
    beta = jax.nn.sigmoid(small)
    g = -jnp.exp(alog_row) * jax.nn.softplus(small + dt_row)
    return beta, g


def _qkv_a_post(conv_out):
    act = _silu(conv_out)
    parts = []
    for h in range(HA):
        parts.append(_l2n(act[:, h * DKA:(h + 1) * DKA]) * (DKA ** -0.5))
    for h in range(HA):
        o = HA * DKA + h * DKA
        parts.append(_l2n(act[:, o:o + DKA]))
    parts.append(act[:, 2 * HA * DKA:])
    return jnp.concatenate(parts, axis=1)


def _mix_out(x, oa, ob, post, onorm_a, onorm_b, wa_ref, wb_ref, wo_ref):
    pa = []
    for h in range(HA):
        o = oa[:, h * DVA:(h + 1) * DVA]
        pa.append(_rms(o, onorm_a))
    oa_n = jnp.concatenate(pa, axis=1) * _silu(post[:, P_ZA:P_ZA + HA * DVA])
    pb = []
    for h in range(HB):
        o = ob[:, h * DVB:(h + 1) * DVB]
        pb.append(_rms(o, onorm_b))
    ob_n = jnp.concatenate(pb, axis=1) * _silu(post[:, P_RB:P_RB + HB * DVB])
    y_a = _dot(oa_n, wa_ref[...])
    y_b = _dot(ob_n, wb_ref[...])
    mix = (jax.nn.sigmoid(post[:, P_GA:P_GA + D_MODEL]) * y_a
           + jax.nn.sigmoid(post[:, P_GB:P_GB + D_MODEL]) * y_b)
    return x + _dot(mix, wo_ref[...])


def _inv_unit_lower(m, c):
    ri = _iota2(c, c, 0)
    ci = _iota2(c, c, 1)
    eye = (ri == ci).astype(F32)
    base = min(16, c)
    same = (ri // base) == (ci // base)
    a = jnp.where(same, -m, 0.0)
    p = eye + a
    pw = a
    k = 2
    while k < base:
        pw = _dot(pw, pw)
        p = p + _dot(p, pw)
        k *= 2
    size = base
    while size < c:
        nxt = size * 2
        same_nxt = (ri // nxt) == (ci // nxt)
        off = jnp.where(same_nxt & jnp.logical_not(same), m, 0.0)
        p = p - _dot(_dot(p, off), p)
        same = same_nxt
        size = nxt
    return p


def _delta_chunk(q, k, v, beta_col, g_col, g_row, s, c):
    ri = _iota2(c, c, 0)
    ci = _iota2(c, c, 1)
    causal = ri >= ci
    strict = ri > ci
    decay = jnp.where(causal, jnp.exp(jnp.where(causal, g_col - g_row, 0.0)), 0.0)
    kb = k * beta_col
    eg = jnp.exp(g_col)
    kq = _dot_nt(jnp.concatenate([kb, q], axis=0), k)
    m = jnp.where(strict, kq[0:c] * decay, 0.0)
    t = _inv_unit_lower(m, c)
    uw = _dot(t, jnp.concatenate([v * beta_col, kb * eg], axis=1))
    u = uw[:, 0:DVA]
    w = uw[:, DVA:]
    wq = _dot(jnp.concatenate([w, q * eg], axis=0), s)
    v_new = u - wq[0:c]
    attn = kq[c:] * decay
    o = wq[c:] + _dot(attn, v_new)
    g_last = g_col[c - 1:c, :]
    s_new = s * jnp.exp(g_last) + _dot_tn(k * jnp.exp(g_last - g_col), v_new)
    return o, s_new


def _level_ref_rows(bc, half, c):
    w = bc.shape[1]
    parts = []
    if half >= SUBLANES // 2:
        for blk in range(c // (2 * half)):
            mrow = blk * 2 * half + half
            parts.append(jnp.broadcast_to(bc[mrow:mrow + 1, :], (2 * half, w)))
    else:
        sub = _iota2(SUBLANES, w, 0)
        for grp in range(c // SUBLANES):
            acc = None
            for blk in range(SUBLANES // (2 * half)):
                mrow = grp * SUBLANES + blk * 2 * half + half
                b = jnp.broadcast_to(bc[mrow:mrow + 1, :], (SUBLANES, w))
                acc = b if acc is None else jnp.where(sub >= blk * 2 * half, b, acc)
            parts.append(acc)
    return jnp.concatenate(parts, axis=0)


def _gla_intra(q, k, bc, c):
    ri = _iota2(c, c, 0)
    ci = _iota2(c, c, 1)
    rowi = _iota2(c, q.shape[1], 0)
    attn = jnp.where(ri == ci, jnp.sum(q * k, axis=-1, keepdims=True), 0.0)
    half = c // 2
    while half >= 1:
        d = bc - _level_ref_rows(bc, half, c)
        upper = (rowi % (2 * half)) >= half
        e = jnp.exp(jnp.where(upper, d, -d))
        qs = jnp.where(upper, q * e, 0.0)
        ks = jnp.where(upper, 0.0, k * e)
        sc = _dot_nt(qs, ks)
        valid = ((ri // (2 * half)) == (ci // (2 * half))) & ((ri % (2 * half)) >= half) & ((ci % (2 * half)) < half)
        attn = attn + jnp.where(valid, sc, 0.0)
        half //= 2
    return attn


def _gla_chunk(q, k, v, bc, s, c):
    o = _dot(q * jnp.exp(bc), s) + _dot(_gla_intra(q, k, bc, c), v)
    b_last = bc[c - 1:c, :]
    e_col = jnp.exp(b_last).T
    s_new = s * e_col + _dot_tn(k * jnp.exp(b_last - bc), v)
    return o, s_new


def _mixer_body(x_ref, sa0_ref, sb0_ref, ca0_ref, nmix_ref, wca_ref, alog_ref, dt_ref, bgk_ref, ona_ref, onb_ref,
                w1_ref, w2_ref, wa_ref, wb_ref, wo_ref,
                x1_ref, sa_ref, ca_ref, sb_ref,
                cbuf, hb_s, qkv_s, beta_s, g_s, qkb_s, vb_s, lg_s, oa_s, ob_s, *, tt, c):
    t_idx = pl.program_id(1)

    @pl.when(t_idx == 0)
    def _():
        sa_ref[...] = sa0_ref[...]
        sb_ref[...] = sb0_ref[...]
        cbuf[SUBLANES - (CONV_A - 1):SUBLANES, :] = ca0_ref[...]

    x = x_ref[...]
    hb_s[...] = _rms(x, nmix_ref[...]).astype(BF16)
    hb = hb_s[...]
    cbuf[SUBLANES:SUBLANES + tt, :] = jnp.dot(hb, w1_ref[:, O_QKVA:O_QKVA + C_A], preferred_element_type=F32)
    base = SUBLANES - (CONV_A - 1)
    acc = cbuf[base:base + tt, :] * wca_ref[0:1, :]
    for i in range(1, CONV_A):
        acc = acc + cbuf[base + i:base + i + tt, :] * wca_ref[i:i + 1, :]
    tail = cbuf[tt + base:tt + SUBLANES, :]
    cbuf[base:SUBLANES, :] = tail
    ca_ref[...] = tail
    qkv_s[...] = _qkv_a_post(acc)
    small = jnp.dot(hb, w1_ref[:, O_SMALL:O_SMALL + LANES], preferred_element_type=F32)
    beta, g = _gates(small, alog_ref[...], dt_ref[...])
    beta_s[...] = beta
    g_s[...] = g
    lg_pre = _dot(small, w2_ref[...]) + bgk_ref[...]
    lg_s[...] = jax.nn.log_sigmoid(lg_pre) / GLA_GATE_NORM
    qkb = jnp.dot(hb, w1_ref[:, O_QB:O_QB + 2 * HB * DKB], preferred_element_type=F32)
    qkb_s[:, 0:HB * DKB] = qkb[:, 0:HB * DKB] * (DKB ** -0.5)
    qkb_s[:, HB * DKB:] = qkb[:, HB * DKB:]
    vb_s[...] = jnp.dot(hb, w1_ref[:, O_VB:O_VB + HB * DVB], preferred_element_type=F32)

    tri = (_iota2(c, c, 0) >= _iota2(c, c, 1)).astype(F32)

    def chunk_body(ci, carry):
        r0 = pl.multiple_of(ci * c, c)
        rows = pl.ds(r0, c)
        g_cum = _dot_exact(tri, g_s[rows, :])
        g_cum_t = g_cum.T
        beta_c = beta_s[rows, :]
        bc_all = _dot_exact(tri, lg_s[rows, :])
        for h in range(HA):
            q = qkv_s[rows, h * DKA:(h + 1) * DKA]
            k = qkv_s[rows, HA * DKA + h * DKA:HA * DKA + (h + 1) * DKA]
            v = qkv_s[rows, 2 * HA * DKA + h * DVA:2 * HA * DKA + (h + 1) * DVA]
            o, s_new = _delta_chunk(q, k, v, beta_c[:, LANE_BETA + h:LANE_BETA + h + 1],
                                    g_cum[:, LANE_G + h:LANE_G + h + 1],
                                    g_cum_t[LANE_G + h:LANE_G + h + 1, :], sa_ref[h], c)
            sa_ref[h] = s_new
            oa_s[rows, h * DVA:(h + 1) * DVA] = o
        for h in range(HB):
            q = qkb_s[rows, h * DKB:(h + 1) * DKB]
            k = qkb_s[rows, HB * DKB + h * DKB:HB * DKB + (h + 1) * DKB]
            v = vb_s[rows, h * DVB:(h + 1) * DVB]
            o, s_new = _gla_chunk(q, k, v, bc_all[:, h * DKB:(h + 1) * DKB], sb_ref[h], c)
            sb_ref[h] = s_new
            ob_s[rows, h * DVB:(h + 1) * DVB] = o
        return carry

    lax.fori_loop(0, tt // c, chunk_body, 0)

    post = jnp.dot(hb, w1_ref[:, O_POST:O_POST + W_POST], preferred_element_type=F32)
    x1_ref[...] = _mix_out(x, oa_s[...], ob_s[...], post, ona_ref[...], onb_ref[...], wa_ref, wb_ref, wo_ref)


def _const_spec(shape):
    nd = len(shape)
    return pl.BlockSpec(shape, lambda *_: (0,) * nd, pipeline_mode=pl.Buffered(1))


def _mixer_call(x, sa0, sb0, ca0, p, tt):
    b, t, _ = x.shape
    c = CHUNK
    nt = t // tt
    row = lambda n: _const_spec((1, n))
    in_specs = [
        pl.BlockSpec((None, tt, D_MODEL), lambda i, j: (i, j, 0)),
        _const_spec((HA, DKA, DVA)), _const_spec((HB, DKB, DVB)), _const_spec((CONV_A - 1, C_A)),
        row(D_MODEL), _const_spec((CONV_A, C_A)), row(LANES), row(LANES), row(HB * DKB), row(DVA), row(DVB),
        _const_spec((D_MODEL, W_IN_COLS)), _const_spec((LANES, HB * DKB)),
        _const_spec((HA * DVA, D_MODEL)), _const_spec((HB * DVB, D_MODEL)), _const_spec((D_MODEL, D_MODEL)),
    ]
    out_specs = [
        pl.BlockSpec((None, tt, D_MODEL), lambda i, j: (i, j, 0)),
        pl.BlockSpec((None, HA, DKA, DVA), lambda i, j: (i, 0, 0, 0)),
        pl.BlockSpec((None, CONV_A - 1, C_A), lambda i, j: (i, 0, 0)),
        pl.BlockSpec((None, HB, DKB, DVB), lambda i, j: (i, 0, 0, 0)),
    ]
    out_shape = [
        jax.ShapeDtypeStruct((b, t, D_MODEL), F32),
        jax.ShapeDtypeStruct((b, HA, DKA, DVA), F32),
        jax.ShapeDtypeStruct((b, CONV_A - 1, C_A), F32),
        jax.ShapeDtypeStruct((b, HB, DKB, DVB), F32),
    ]
    scratch = [
        pltpu.VMEM((tt + SUBLANES, C_A), F32),
        pltpu.VMEM((tt, D_MODEL), BF16),
        pltpu.VMEM((tt, C_A), F32),
        pltpu.VMEM((tt, LANES), F32),
        pltpu.VMEM((tt, LANES), F32),
        pltpu.VMEM((tt, 2 * HB * DKB), F32),
        pltpu.VMEM((tt, HB * DVB), F32),
        pltpu.VMEM((tt, HB * DKB), F32),
        pltpu.VMEM((tt, HA * DVA), F32),
        pltpu.VMEM((tt, HB * DVB), F32),
    ]
    return pl.pallas_call(
        functools.partial(_mixer_body, tt=tt, c=c),
        grid=(b, nt), in_specs=in_specs, out_specs=out_specs, out_shape=out_shape, scratch_shapes=scratch,
        compiler_params=pltpu.CompilerParams(dimension_semantics=("arbitrary", "arbitrary"),
                                             vmem_limit_bytes=VMEM_LIMIT),
        name="mixer",
    )(x, sa0, sb0, ca0, p["norm_mix"], p["w_conv_a"], p["alog_row"], p["dt_row"], p["b_gk"], p["onorm_a"],
      p["onorm_b"], p["w1"], p["w2"], p["w_a_out"], p["w_b_out"], p["w_o"])


def _cumsum_chunks(x, c):
    rowi = _iota2(x.shape[0], x.shape[1], 0) % c
    sh = 1
    while sh < c:
        x = x + jnp.where(rowi >= sh, pltpu.roll(x, sh, axis=0), 0.0)
        sh *= 2
    return x


def _delta_prepare(qkv_s, beta_s, gcum_s, gt_s, m_s, qk_s, rhs_s, qe_s, kdec_s, tt, c):
    ri = _iota2(c, c, 0)
    ci = _iota2(c, c, 1)
    causal = ri >= ci
    strict = ri > ci
    for ch in range(tt // c):
        rows = slice(ch * c, (ch + 1) * c)
        g_blk = gcum_s[rows, :]
        beta_blk = beta_s[rows, :]
        eg_blk = jnp.exp(g_blk)
        ed_blk = jnp.exp(g_blk[c - 1:c, :] - g_blk)
        for h in range(HA):
            n = ch * HA + h
            hs = slice(h * DKA, (h + 1) * DKA)
            q = qkv_s[rows, h * DKA:(h + 1) * DKA]
            k = qkv_s[rows, HA * DKA + h * DKA:HA * DKA + (h + 1) * DKA]
            v = qkv_s[rows, 2 * HA * DKA + h * DVA:2 * HA * DKA + (h + 1) * DVA]
            g_col = g_blk[:, LANE_G + h:LANE_G + h + 1]
            g_row = gt_s[LANE_G + h:LANE_G + h + 1, rows]
            decay = jnp.where(causal, jnp.exp(jnp.where(causal, g_col - g_row, 0.0)), 0.0)
            beta_col = beta_blk[:, LANE_BETA + h:LANE_BETA + h + 1]
            eg = eg_blk[:, LANE_G + h:LANE_G + h + 1]
            kb = k * beta_col
            kq = _dot_nt(jnp.concatenate([kb, q], axis=0), k)
            m_s[n] = jnp.where(strict, kq[0:c] * decay, 0.0)
            qk_s[n] = kq[c:] * decay
            rhs_s[rows, h * 2 * DVA:(h + 1) * 2 * DVA] = jnp.concatenate([v * beta_col, kb * eg], axis=1)
            qe_s[rows, hs] = q * eg
            kdec_s[rows, hs] = k * ed_blk[:, LANE_G + h:LANE_G + h + 1]


def _inverse_stages(m_s, p_s, pw_s, n_inst, c):
    ri = _iota2(c, c, 0)
    ci = _iota2(c, c, 1)
    eye = (ri == ci).astype(F32)
    base = min(16, c)
    same = (ri // base) == (ci // base)
    for n in range(n_inst):
        a = jnp.where(same, -m_s[n], 0.0)
        pw_s[n] = a
        p_s[n] = eye + a
    k = 2
    while k < base:
        for n in range(n_inst):
            pw = pw_s[n]
            pw_s[n] = _dot(pw, pw)
        for n in range(n_inst):
            p = p_s[n]
            p_s[n] = p + _dot(p, pw_s[n])
        k *= 2
    size = base
    while size < c:
        nxt = size * 2
        same_nxt = (ri // nxt) == (ci // nxt)
        off_mask = same_nxt & jnp.logical_not(same)
        for n in range(n_inst):
            pw_s[n] = _dot(p_s[n], jnp.where(off_mask, m_s[n], 0.0))
        for n in range(n_inst):
            p = p_s[n]
            p_s[n] = p - _dot(pw_s[n], p)
        same = same_nxt
        size = nxt


def _gla_prepare(qkb_s, vb_s, bc_s, attn_s, ob_s, pkv_s, qeb_s, tt, c):
    ri = _iota2(c, c, 0)
    ci = _iota2(c, c, 1)
    n_ch = tt // c

    def operands(ch, h):
        rows = slice(ch * c, (ch + 1) * c)
        q = qkb_s[rows, h * DKB:(h + 1) * DKB]
        k = qkb_s[rows, HB * DKB + h * DKB:HB * DKB + (h + 1) * DKB]
        return rows, q, k

    for ch in range(n_ch):
        for h in range(HB):
            _, q, k = operands(ch, h)
            attn_s[ch * HB + h] = jnp.where(ri == ci, jnp.sum(q * k, axis=-1, keepdims=True), 0.0)
    half = c // 2
    while half >= 1:
        valid = (((ri // (2 * half)) == (ci // (2 * half))) & ((ri % (2 * half)) >= half)
                 & ((ci % (2 * half)) < half))
        for ch in range(n_ch):
            bc_all = bc_s[ch * c:(ch + 1) * c, :]
            d_all = bc_all - _level_ref_rows(bc_all, half, c)
            upper_all = (_iota2(c, HB * DKB, 0) % (2 * half)) >= half
            e_all = jnp.exp(jnp.where(upper_all, d_all, -d_all))
            upper = upper_all[:, 0:DKB]
            for h in range(HB):
                _, q, k = operands(ch, h)
                e = e_all[:, h * DKB:(h + 1) * DKB]
                sc = _dot_nt(jnp.where(upper, q * e, 0.0), jnp.where(upper, 0.0, k * e))
                attn_s[ch * HB + h] += jnp.where(valid, sc, 0.0)
        half //= 2
    for ch in range(n_ch):
        for h in range(HB):
            rows, q, k = operands(ch, h)
            v = vb_s[rows, h * DVB:(h + 1) * DVB]
            bc = bc_s[rows, h * DKB:(h + 1) * DKB]
            ob_s[rows, h * DVB:(h + 1) * DVB] = _dot(attn_s[ch * HB + h], v)
            pkv_s[ch * HB + h] = _dot_tn(k * jnp.exp(bc[c - 1:c, :] - bc), v)
            qeb_s[rows, h * DKB:(h + 1) * DKB] = q * jnp.exp(bc)


def _mixer_body(x_ref, sa0_ref, sb0_ref, ca0_ref, nmix_ref, wca_ref, alog_ref, dt_ref, bgk_ref, ona_ref, onb_ref,
                w1_ref, w2_ref, wa_ref, wb_ref, wo_ref,
                x1_ref, sa_ref, ca_ref, sb_ref,
                cbuf, hb_s, qkv_s, beta_s, gcum_s, gt_s, qkb_s, vb_s, bc_s, oa_s, ob_s,
                m_s, p_s, pw_s, qk_s, rhs_s, uw_s, qe_s, kdec_s, attn_s, pkv_s, qeb_s, *, tt, c):
    t_idx = pl.program_id(1)
    n_ch = tt // c

    @pl.when(t_idx == 0)
    def _():
        sa_ref[...] = sa0_ref[...]
        sb_ref[...] = sb0_ref[...]
        cbuf[SUBLANES - (CONV_A - 1):SUBLANES, :] = ca0_ref[...]

    x = x_ref[...]
    hb_s[...] = _rms(x, nmix_ref[...]).astype(BF16)
    hb = hb_s[...]
    cbuf[SUBLANES:SUBLANES + tt, :] = jnp.dot(hb, w1_ref[:, O_QKVA:O_QKVA + C_A], preferred_element_type=F32)
    base = SUBLANES - (CONV_A - 1)
    acc = cbuf[base:base + tt, :] * wca_ref[0:1, :]
    for i in range(1, CONV_A):
        acc = acc + cbuf[base + i:base + i + tt, :] * wca_ref[i:i + 1, :]
    tail = cbuf[tt + base:tt + SUBLANES, :]
    cbuf[base:SUBLANES, :] = tail
    ca_ref[...] = tail
    qkv_s[...] = _qkv_a_post(acc)
    small = jnp.dot(hb, w1_ref[:, O_SMALL:O_SMALL + LANES], preferred_element_type=F32)
    beta, g = _gates(small, alog_ref[...], dt_ref[...])
    beta_s[...] = beta
    g_cum = _cumsum_chunks(g, c)
    gcum_s[...] = g_cum
    gt_s[...] = g_cum.T
    lg_pre = _dot(small, w2_ref[...]) + bgk_ref[...]
    bc_s[...] = _cumsum_chunks(jax.nn.log_sigmoid(lg_pre) / GLA_GATE_NORM, c)
    qkb = jnp.dot(hb, w1_ref[:, O_QB:O_QB + 2 * HB * DKB], preferred_element_type=F32)
    qkb_s[:, 0:HB * DKB] = qkb[:, 0:HB * DKB] * (DKB ** -0.5)
    qkb_s[:, HB * DKB:] = qkb[:, HB * DKB:]
    vb_s[...] = jnp.dot(hb, w1_ref[:, O_VB:O_VB + HB * DVB], preferred_element_type=F32)

    _delta_prepare(qkv_s, beta_s, gcum_s, gt_s, m_s, qk_s, rhs_s, qe_s, kdec_s, tt, c)
    _inverse_stages(m_s, p_s, pw_s, n_ch * HA, c)
    for ch in range(n_ch):
        rows = slice(ch * c, (ch + 1) * c)
        for h in range(HA):
            cols = slice(h * 2 * DVA, (h + 1) * 2 * DVA)
            uw_s[rows, cols] = _dot(p_s[ch * HA + h], rhs_s[rows, cols])
    _gla_prepare(qkb_s, vb_s, bc_s, attn_s, ob_s, pkv_s, qeb_s, tt, c)

    for ch in range(n_ch):
        rows = slice(ch * c, (ch + 1) * c)
        last = slice(ch * c + c - 1, ch * c + c)
        for h in range(HA):
            hs = slice(h * DKA, (h + 1) * DKA)
            s = sa_ref[h]
            u = uw_s[rows, h * 2 * DVA:h * 2 * DVA + DVA]
            w = uw_s[rows, h * 2 * DVA + DVA:(h + 1) * 2 * DVA]
            wq = _dot(jnp.concatenate([w, qe_s[rows, hs]], axis=0), s)
            v_new = u - wq[0:c]
            oa_s[rows, h * DVA:(h + 1) * DVA] = wq[c:] + _dot(qk_s[ch * HA + h], v_new)
            g_last = gcum_s[last, LANE_G + h:LANE_G + h + 1]
            sa_ref[h] = s * jnp.exp(g_last) + _dot_tn(kdec_s[rows, hs], v_new)
        for h in range(HB):
            s = sb_ref[h]
            ob_s[rows, h * DVB:(h + 1) * DVB] += _dot(qeb_s[rows, h * DKB:(h + 1) * DKB], s)
            e_col = jnp.exp(bc_s[last, h * DKB:(h + 1) * DKB]).T
            sb_ref[h] = s * e_col + pkv_s[ch * HB + h]

    post = jnp.dot(hb, w1_ref[:, O_POST:O_POST + W_POST], preferred_element_type=F32)
    x1_ref[...] = _mix_out(x, oa_s[...], ob_s[...], post, ona_ref[...], onb_ref[...], wa_ref, wb_ref, wo_ref)


def _mixer_call(x, sa0, sb0, ca0, p, tt):
    b, t, _ = x.shape
    c = CHUNK
    nt = t // tt
    n_inst = (tt // c) * HA
    row = lambda n: _const_spec((1, n))
    in_specs = [
        pl.BlockSpec((None, tt, D_MODEL), lambda i, j: (i, j, 0)),
        _const_spec((HA, DKA, DVA)), _const_spec((HB, DKB, DVB)), _const_spec((CONV_A - 1, C_A)),
        row(D_MODEL), _const_spec((CONV_A, C_A)), row(LANES), row(LANES), row(HB * DKB), row(DVA), row(DVB),
        _const_spec((D_MODEL, W_IN_COLS)), _const_spec((LANES, HB * DKB)),
        _const_spec((HA * DVA, D_MODEL)), _const_spec((HB * DVB, D_MODEL)), _const_spec((D_MODEL, D_MODEL)),
    ]
    out_specs = [
        pl.BlockSpec((None, tt, D_MODEL), lambda i, j: (i, j, 0)),
        pl.BlockSpec((None, HA, DKA, DVA), lambda i, j: (i, 0, 0, 0)),
        pl.BlockSpec((None, CONV_A - 1, C_A), lambda i, j: (i, 0, 0)),
        pl.BlockSpec((None, HB, DKB, DVB), lambda i, j: (i, 0, 0, 0)),
    ]
    out_shape = [
        jax.ShapeDtypeStruct((b, t, D_MODEL), F32),
        jax.ShapeDtypeStruct((b, HA, DKA, DVA), F32),
        jax.ShapeDtypeStruct((b, CONV_A - 1, C_A), F32),
        jax.ShapeDtypeStruct((b, HB, DKB, DVB), F32),
    ]
    mats = lambda: pltpu.VMEM((n_inst, c, c), F32)
    scratch = [
        pltpu.VMEM((tt + SUBLANES, C_A), F32),
        pltpu.VMEM((tt, D_MODEL), BF16),
        pltpu.VMEM((tt, C_A), F32),
        pltpu.VMEM((tt, LANES), F32),
        pltpu.VMEM((tt, LANES), F32),
        pltpu.VMEM((LANES, tt), F32),
        pltpu.VMEM((tt, 2 * HB * DKB), F32),
        pltpu.VMEM((tt, HB * DVB), F32),
        pltpu.VMEM((tt, HB * DKB), F32),
        pltpu.VMEM((tt, HA * DVA), F32),
        pltpu.VMEM((tt, HB * DVB), F32),
        mats(), mats(), mats(), mats(),
        pltpu.VMEM((tt, HA * 2 * DVA), F32),
        pltpu.VMEM((tt, HA * 2 * DVA), F32),
        pltpu.VMEM((tt, HA * DKA), F32),
        pltpu.VMEM((tt, HA * DKA), F32),
        mats(),
        pltpu.VMEM((n_inst, DKB, DVB), F32),
        pltpu.VMEM((tt, HB * DKB), F32),
    ]
    return pl.pallas_call(
        functools.partial(_mixer_body, tt=tt, c=c),
        grid=(b, nt), in_specs=in_specs, out_specs=out_specs, out_shape=out_shape, scratch_shapes=scratch,
        compiler_params=pltpu.CompilerParams(dimension_semantics=("arbitrary", "arbitrary"),
                                             vmem_limit_bytes=VMEM_LIMIT),
        name="mixer",
    )(x, sa0, sb0, ca0, p["norm_mix"], p["w_conv_a"], p["alog_row"], p["dt_row"], p["b_gk"], p["onorm_a"],
      p["onorm_b"], p["w1"], p["w2"], p["w_a_out"], p["w_b_out"], p["w_o"])


def _ffn_core(x1, u_conv, gf, wdn_ref, nfin):
    act = _gelu_tanh(u_conv) * gf
    x2 = x1 + _dot(act, wdn_ref[...])
    return _rms(x2, nfin)


def _ffn_body(x1_ref, cf0_ref, nffn_ref, wcf_ref, bcf_ref, nfin_ref, wup_ref, wdn_ref,
              y_ref, cf_ref, ubuf, *, tt):
    t_idx = pl.program_id(1)
    base = SUBLANES - (CONV_F - 1)

    @pl.when(t_idx == 0)
    def _():
        ubuf[base:SUBLANES, :] = cf0_ref[...]

    x1 = x1_ref[...]
    h2 = _rms(x1, nffn_ref[...]).astype(BF16)
    ubuf[SUBLANES:SUBLANES + tt, :] = jnp.dot(h2, wup_ref[:, 0:D_FF], preferred_element_type=F32)
    gf = jnp.dot(h2, wup_ref[:, D_FF:], preferred_element_type=F32)
    acc = ubuf[base:base + tt, :] * wcf_ref[0:1, :]
    for i in range(1, CONV_F):
        acc = acc + ubuf[base + i:base + i + tt, :] * wcf_ref[i:i + 1, :]
    acc = acc + bcf_ref[...]
    tail = ubuf[tt + base:tt + SUBLANES, :]
    ubuf[base:SUBLANES, :] = tail
    cf_ref[...] = tail
    y_ref[...] = _ffn_core(x1, acc, gf, wdn_ref, nfin_ref[...])


def _ffn_call(x1, cf0, p, tt):
    b, t, _ = x1.shape
    nt = t // tt
    row = lambda n: _const_spec((1, n))
    in_specs = [
        pl.BlockSpec((None, tt, D_MODEL), lambda i, j: (i, j, 0)),
        _const_spec((CONV_F - 1, D_FF)), row(D_MODEL), _const_spec((CONV_F, D_FF)), row(D_FF), row(D_MODEL),
        _const_spec((D_MODEL, 2 * D_FF)), _const_spec((D_FF, D_MODEL)),
    ]
    out_specs = [
        pl.BlockSpec((None, tt, D_MODEL), lambda i, j: (i, j, 0)),
        pl.BlockSpec((None, CONV_F - 1, D_FF), lambda i, j: (i, 0, 0)),
    ]
    out_shape = [jax.ShapeDtypeStruct((b, t, D_MODEL), F32), jax.ShapeDtypeStruct((b, CONV_F - 1, D_FF), F32)]
    return pl.pallas_call(
        functools.partial(_ffn_body, tt=tt),
        grid=(b, nt), in_specs=in_specs, out_specs=out_specs, out_shape=out_shape,
        scratch_shapes=[pltpu.VMEM((tt + SUBLANES, D_FF), F32)],
        compiler_params=pltpu.CompilerParams(dimension_semantics=("arbitrary", "arbitrary"),
                                             vmem_limit_bytes=VMEM_LIMIT),
        name="convffn",
    )(x1, cf0, p["norm_ffn"], p["w_conv_f"], p["b_conv_f"], p["norm_final"], p["w_ffn_in"], p["w_ffn_out"])


def _dec_head_body(x_ref, cs_ref, nmix_ref, wca_ref, alog_ref, dt_ref, bgk_ref, w1_ref, w2_ref,
                   qkva_ref, beta_ref, g_ref, qkvb_ref, lg_ref, post_ref, csn_ref):
    x = x_ref[...]
    hb = _rms(x, nmix_ref[...]).astype(BF16)
    pre = jnp.dot(hb, w1_ref[:, O_QKVA:O_QKVA + C_A], preferred_element_type=F32)
    acc = cs_ref[:, 0:C_A] * wca_ref[0:1, :]
    for i in range(1, CONV_A - 1):
        acc = acc + cs_ref[:, i * C_A:(i + 1) * C_A] * wca_ref[i:i + 1, :]
    acc = acc + pre * wca_ref[CONV_A - 1:CONV_A, :]
    for i in range(CONV_A - 2):
        csn_ref[:, i * C_A:(i + 1) * C_A] = cs_ref[:, (i + 1) * C_A:(i + 2) * C_A]
    csn_ref[:, (CONV_A - 2) * C_A:] = pre
    qkva_ref[...] = _qkv_a_post(acc)
    small = jnp.dot(hb, w1_ref[:, O_SMALL:O_SMALL + LANES], preferred_element_type=F32)
    beta, g = _gates(small, alog_ref[...], dt_ref[...])
    beta_ref[...] = beta
    g_ref[...] = g
    lg_ref[...] = jax.nn.log_sigmoid(_dot(small, w2_ref[...]) + bgk_ref[...]) / GLA_GATE_NORM
    qkb = jnp.dot(hb, w1_ref[:, O_QB:O_QB + 2 * HB * DKB], preferred_element_type=F32)
    qkvb_ref[:, 0:HB * DKB] = qkb[:, 0:HB * DKB] * (DKB ** -0.5)
    qkvb_ref[:, HB * DKB:2 * HB * DKB] = qkb[:, HB * DKB:]
    qkvb_ref[:, 2 * HB * DKB:] = jnp.dot(hb, w1_ref[:, O_VB:O_VB + HB * DVB], preferred_element_type=F32)
    post_ref[...] = jnp.dot(hb, w1_ref[:, O_POST:O_POST + W_POST], preferred_element_type=F32)


def _dec_head_call(xs, cs, p):
    n = xs.shape[0]
    shapes = [(n, C_A), (n, LANES), (n, LANES), (n, 2 * HB * DKB + HB * DVB), (n, HB * DKB), (n, W_POST),
              (n, (CONV_A - 1) * C_A)]
    return pl.pallas_call(
        _dec_head_body,
        out_shape=[jax.ShapeDtypeStruct(s, F32) for s in shapes],
        compiler_params=pltpu.CompilerParams(vmem_limit_bytes=VMEM_LIMIT),
        name="decode_head",
    )(xs, cs, p["norm_mix"], p["w_conv_a"], p["alog_row"], p["dt_row"], p["b_gk"], p["w1"], p["w2"])


def _dec_rec_body(qkva_ref, beta_ref, g_ref, qkvb_ref, lg_ref, sa_ref, sb_ref,
                  oa_ref, ob_ref, san_ref, sbn_ref, *, tb):
    beta = beta_ref[...]
    eg = jnp.exp(g_ref[...])
    for h in range(HA):
        q_t = qkva_ref[:, h * DKA:(h + 1) * DKA].T
        k_t = qkva_ref[:, HA * DKA + h * DKA:HA * DKA + (h + 1) * DKA].T
        for j in range(tb):
            k_col = k_t[:, j:j + 1]
            s = sa_ref[j, h] * eg[j:j + 1, LANE_G + h:LANE_G + h + 1]
            v = qkva_ref[j:j + 1, 2 * HA * DKA + h * DVA:2 * HA * DKA + (h + 1) * DVA]
            err = (v - jnp.sum(k_col * s, axis=0, keepdims=True)) * beta[j:j + 1, LANE_BETA + h:LANE_BETA + h + 1]
            s = s + k_col * err
            san_ref[j, h] = s
            oa_ref[j:j + 1, h * DVA:(h + 1) * DVA] = jnp.sum(q_t[:, j:j + 1] * s, axis=0, keepdims=True)
    for h in range(HB):
        q_t = qkvb_ref[:, h * DKB:(h + 1) * DKB].T
        k_t = qkvb_ref[:, HB * DKB + h * DKB:HB * DKB + (h + 1) * DKB].T
        d_t = jnp.exp(lg_ref[:, h * DKB:(h + 1) * DKB]).T
        for j in range(tb):
            v = qkvb_ref[j:j + 1, 2 * HB * DKB + h * DVB:2 * HB * DKB + (h + 1) * DVB]
            s = sb_ref[j, h] * d_t[:, j:j + 1] + k_t[:, j:j + 1] * v
            sbn_ref[j, h] = s
            ob_ref[j:j + 1, h * DVB:(h + 1) * DVB] = jnp.sum(q_t[:, j:j + 1] * s, axis=0, keepdims=True)


def _dec_rec_call(qkva, beta, g, qkvb, lg, sa, sb, tb):
    n = qkva.shape[0]
    rows = lambda w: pl.BlockSpec((tb, w), lambda i: (i, 0))
    in_specs = [rows(C_A), rows(LANES), rows(LANES), rows(2 * HB * DKB + HB * DVB), rows(HB * DKB),
                pl.BlockSpec((tb, HA, DKA, DVA), lambda i: (i, 0, 0, 0)),
                pl.BlockSpec((tb, HB, DKB, DVB), lambda i: (i, 0, 0, 0))]
    out_specs = [rows(HA * DVA), rows(HB * DVB),
                 pl.BlockSpec((tb, HA, DKA, DVA), lambda i: (i, 0, 0, 0)),
                 pl.BlockSpec((tb, HB, DKB, DVB), lambda i: (i, 0, 0, 0))]
    out_shape = [jax.ShapeDtypeStruct((n, HA * DVA), F32), jax.ShapeDtypeStruct((n, HB * DVB), F32),
                 jax.ShapeDtypeStruct(sa.shape, F32), jax.ShapeDtypeStruct(sb.shape, F32)]
    return pl.pallas_call(
        functools.partial(_dec_rec_body, tb=tb),
        grid=(n // tb,), in_specs=in_specs, out_specs=out_specs, out_shape=out_shape,
        compiler_params=pltpu.CompilerParams(dimension_semantics=("arbitrary",), vmem_limit_bytes=VMEM_LIMIT),
        name="decode_recurrence",
    )(qkva, beta, g, qkvb, lg, sa, sb)


def _dec_tail_body(x_ref, oa_ref, ob_ref, post_ref, cf_ref, ona_ref, onb_ref, nffn_ref, wcf_ref, bcf_ref, nfin_ref,
                   wa_ref, wb_ref, wo_ref, wup_ref, wdn_ref, y_ref, cfn_ref):
    x1 = _mix_out(x_ref[...], oa_ref[...], ob_ref[...], post_ref[...], ona_ref[...], onb_ref[...],
                  wa_ref, wb_ref, wo_ref)
    h2 = _rms(x1, nffn_ref[...]).astype(BF16)
    u = jnp.dot(h2, wup_ref[:, 0:D_FF], preferred_element_type=F32)
    gf = jnp.dot(h2, wup_ref[:, D_FF:], preferred_element_type=F32)
    acc = cf_ref[:, 0:D_FF] * wcf_ref[0:1, :]
    for i in range(1, CONV_F - 1):
        acc = acc + cf_ref[:, i * D_FF:(i + 1) * D_FF] * wcf_ref[i:i + 1, :]
    acc = acc + u * wcf_ref[CONV_F - 1:CONV_F, :] + bcf_ref[...]
    for i in range(CONV_F - 2):
        cfn_ref[:, i * D_FF:(i + 1) * D_FF] = cf_ref[:, (i + 1) * D_FF:(i + 2) * D_FF]
    cfn_ref[:, (CONV_F - 2) * D_FF:] = u
    y_ref[...] = _ffn_core(x1, acc, gf, wdn_ref, nfin_ref[...])


def _dec_tail_call(xs, oa, ob, post, cf, p):
    n = xs.shape[0]
    return pl.pallas_call(
        _dec_tail_body,
        out_shape=[jax.ShapeDtypeStruct((n, D_MODEL), F32), jax.ShapeDtypeStruct((n, (CONV_F - 1) * D_FF), F32)],
        compiler_params=pltpu.CompilerParams(vmem_limit_bytes=VMEM_LIMIT),
        name="decode_tail",
    )(xs, oa, ob, post, cf, p["onorm_a"], p["onorm_b"], p["norm_ffn"], p["w_conv_f"], p["b_conv_f"],
      p["norm_final"], p["w_a_out"], p["w_b_out"], p["w_o"], p["w_ffn_in"], p["w_ffn_out"])


def _prep_params(l, norm_mix, w_in, w_conv_a, a_log, dt_bias, w_gk2, b_gk, onorm_a, onorm_b, w_a_out, w_b_out,
                 w_o, norm_ffn, w_ffn_in, w_conv_f, b_conv_f, w_ffn_out, norm_final):
    sizes = [HA * DKA, HA * DKA, HA * DVA, HA * DVA, HA, HA, HB * DKB, HB * DKB, HB * DVB, HB * DVB, GATE_RANK,
             D_MODEL, D_MODEL]
    offs = [0]
    for s in sizes:
        offs.append(offs[-1] + s)
    w = w_in[l]
    seg = lambda i: w[:, offs[i]:offs[i + 1]]
    n_small = 2 * HA + GATE_RANK
    pad = jnp.zeros((D_MODEL, LANES - n_small), w.dtype)
    w1 = jnp.concatenate([seg(0), seg(1), seg(2), seg(6), seg(7), seg(8), seg(4), seg(5), seg(10), pad,
                          seg(3), seg(9), seg(11), seg(12)], axis=1).astype(BF16)
    w2 = jnp.zeros((LANES, HB * DKB), F32).at[2 * HA:n_small].set(w_gk2[l]).astype(BF16)
    lane_row = lambda v: jnp.zeros((1, LANES), F32).at[0, LANE_G:LANE_G + HA].set(v)
    return dict(
        norm_mix=norm_mix[l][None], w_conv_a=w_conv_a[l], alog_row=lane_row(a_log[l]), dt_row=lane_row(dt_bias[l]),
        b_gk=b_gk[l][None], onorm_a=onorm_a[l][None], onorm_b=onorm_b[l][None], w1=w1, w2=w2,
        w_a_out=w_a_out[l].astype(BF16), w_b_out=w_b_out[l].astype(BF16), w_o=w_o[l].astype(BF16),
        norm_ffn=norm_ffn[l][None], w_ffn_in=w_ffn_in[l].astype(BF16), w_conv_f=w_conv_f[l],
        b_conv_f=b_conv_f[l][None], w_ffn_out=w_ffn_out[l].astype(BF16), norm_final=norm_final[None])


PROMPT_TILE = 256
DECODE_TILE = 8


def kernel(x_prompt, x_sample, state_delta, state_delta_conv, state_gla, state_ffn_conv, meta_tokens, norm_mix, w_in, w_conv_a, a_log, dt_bias, w_gk2, b_gk, onorm_a, onorm_b, w_a_out, w_b_out, w_o, norm_ffn, w_ffn_in, w_conv_f, b_conv_f, w_ffn_out, norm_final):
    assert w_in.shape[0] == 1, "single layer only"
    l = 0
    p = _prep_params(l, norm_mix, w_in, w_conv_a, a_log, dt_bias, w_gk2, b_gk, onorm_a, onorm_b, w_a_out, w_b_out,
                     w_o, norm_ffn, w_ffn_in, w_conv_f, b_conv_f, w_ffn_out, norm_final)
    bsz, seq, _ = x_prompt.shape
    n_dec = x_sample.shape[0]

    xm = jnp.concatenate([jnp.zeros((CHUNK - N_META, D_MODEL), F32), meta_tokens.astype(F32)], axis=0)[None]
    x1m, sa0, ca0, sb0 = _mixer_call(xm, jnp.zeros((HA, DKA, DVA), F32), jnp.zeros((HB, DKB, DVB), F32),
                                     jnp.zeros((CONV_A - 1, C_A), F32), p, CHUNK)
    _, cf0 = _ffn_call(x1m, jnp.zeros((CONV_F - 1, D_FF), F32), p, CHUNK)

    x1, sa_p, ca_p, sb_p = _mixer_call(x_prompt, sa0[0], sb0[0], ca0[0], p, PROMPT_TILE)
    y_prompt, cf_p = _ffn_call(x1, cf0[0], p, PROMPT_TILE)

    xs = x_sample.reshape(n_dec, D_MODEL)
    cs = state_delta_conv[l].reshape(n_dec, (CONV_A - 1) * C_A)
    cfs = state_ffn_conv[l].reshape(n_dec, (CONV_F - 1) * D_FF)
    qkva, beta, g, qkvb, lg, post, cs_new = _dec_head_call(xs, cs, p)
    oa, ob, sa_s, sb_s = _dec_rec_call(qkva, beta, g, qkvb, lg, state_delta[l], state_gla[l], DECODE_TILE)
    y_s, cf_s = _dec_tail_call(xs, oa, ob, post, cfs, p)

    return (y_prompt, y_s.reshape(n_dec, 1, D_MODEL),
            sa_p[None], ca_p[None], sb_p[None], cf_p[None],
            sa_s[None], cs_new.reshape(1, n_dec, CONV_A - 1, C_A), sb_s[None],
            cf_s.reshape(1, n_dec, CONV_F - 1, D_FF))
```

```python
import functools

import jax
import jax.numpy as jnp
from jax import lax
from jax.experimental import pallas as pl
from jax.experimental.pallas import tpu as pltpu

F32 = jnp.float32
BF16 = jnp.bfloat16

D_MODEL = 1024
N_META = 16
CHUNK = 64
HA, DKA, DVA = 4, 128, 128
HB, DKB, DVB = 4, 128, 256
CONV_A = 4
C_A = 2 * HA * DKA + HA * DVA
GATE_RANK = 16
GLA_GATE_NORM = 16.0
D_FF = 2816
CONV_F = 3
EPS = 1e-6
LANES = 128
SUBLANES = 8

O_QKVA = 0
O_QB = O_QKVA + C_A
O_KB = O_QB + HB * DKB
O_VB = O_KB + HB * DKB
O_SMALL = O_VB + HB * DVB
O_POST = O_SMALL + LANES
P_ZA = 0
P_RB = P_ZA + HA * DVA
P_GA = P_RB + HB * DVB
P_GB = P_GA + D_MODEL
W_POST = P_GB + D_MODEL
W_IN_COLS = O_POST + W_POST
LANE_BETA = 0
LANE_G = HA

VMEM_LIMIT = 56 * 1024 * 1024


def _dot(a, b):
    return jnp.dot(a.astype(BF16), b.astype(BF16), preferred_element_type=F32)


def _dot_nt(a, b):
    return lax.dot_general(a.astype(BF16), b.astype(BF16), (((1,), (1,)), ((), ())), preferred_element_type=F32)


def _dot_tn(a, b):
    return lax.dot_general(a.astype(BF16), b.astype(BF16), (((0,), (0,)), ((), ())), preferred_element_type=F32)


def _rms(x, w):
    return x * lax.rsqrt(jnp.mean(x * x, axis=-1, keepdims=True) + EPS) * w


def _silu(x):
    return x * jax.nn.sigmoid(x)


def _gelu_tanh(x):
    return 0.5 * x * (1.0 + jnp.tanh(0.7978845608028654 * (x + 0.044715 * (x * x * x))))


def _l2n(x):
    return x * lax.rsqrt(jnp.sum(x * x, axis=-1, keepdims=True) + EPS)


def _iota2(n, m, axis):
    return lax.broadcasted_iota(jnp.int32, (n, m), axis)


def _gates(small, alog_row, dt_row):
    beta = jax.nn.sigmoid(small)
    g = -jnp.exp(alog_row) * jax.nn.softplus(small + dt_row)
    return beta, g


def _qkv_a_post(conv_out):
    act = _silu(conv_out)
    parts = []
    for h in range(HA):
        parts.append(_l2n(act[:, h * DKA:(h + 1) * DKA]) * (DKA ** -0.5))
    for h in range(HA):
        o = HA * DKA + h * DKA
        parts.append(_l2n(act[:, o:o + DKA]))
    parts.append(act[:, 2 * HA * DKA:])
    return jnp.concatenate(parts, axis=1)


def _branch_a(oa, post, onorm_a, wa_ref):
    pa = [_rms(oa[:, h * DVA:(h + 1) * DVA], onorm_a) for h in range(HA)]
    return _dot(jnp.concatenate(pa, axis=1) * _silu(post[:, P_ZA:P_ZA + HA * DVA]), wa_ref[...])


def _branch_b(ob, post, onorm_b, wb_ref):
    pb = [_rms(ob[:, h * DVB:(h + 1) * DVB], onorm_b) for h in range(HB)]
    return _dot(jnp.concatenate(pb, axis=1) * _silu(post[:, P_RB:P_RB + HB * DVB]), wb_ref[...])


def _mix_residual(x, y_a, y_b, post, wo_ref):
    mix = (jax.nn.sigmoid(post[:, P_GA:P_GA + D_MODEL]) * y_a
           + jax.nn.sigmoid(post[:, P_GB:P_GB + D_MODEL]) * y_b)
    return x + _dot(mix, wo_ref[...])


def _cumsum_chunks(x, c):
    rowi = _iota2(x.shape[0], x.shape[1], 0) % c
    sh = 1
    while sh < c:
        x = x + jnp.where(rowi >= sh, pltpu.roll(x, sh, axis=0), 0.0)
        sh *= 2
    return x


def _delta_prepare(qkv_s, beta_s, gcum_s, gt_s, m_s, qk_s, rhs_s, qe_s, kdec_s, tt, c):
    ri = _iota2(c, c, 0)
    ci = _iota2(c, c, 1)
    causal = ri >= ci
    strict = ri > ci
    for ch in range(tt // c):
        rows = slice(ch * c, (ch + 1) * c)
        g_blk = gcum_s[rows, :]
        beta_blk = beta_s[rows, :]
        eg_blk = jnp.exp(g_blk)
        ed_blk = jnp.exp(g_blk[c - 1:c, :] - g_blk)
        for h in range(HA):
            n = ch * HA + h
            hs = slice(h * DKA, (h + 1) * DKA)
            q = qkv_s[rows, h * DKA:(h + 1) * DKA]
            k = qkv_s[rows, HA * DKA + h * DKA:HA * DKA + (h + 1) * DKA]
            v = qkv_s[rows, 2 * HA * DKA + h * DVA:2 * HA * DKA + (h + 1) * DVA]
            g_col = g_blk[:, LANE_G + h:LANE_G + h + 1]
            g_row = gt_s[LANE_G + h:LANE_G + h + 1, rows]
            decay = jnp.where(causal, jnp.exp(jnp.where(causal, g_col - g_row, 0.0)), 0.0)
            beta_col = beta_blk[:, LANE_BETA + h:LANE_BETA + h + 1]
            eg = eg_blk[:, LANE_G + h:LANE_G + h + 1]
            kb = k * beta_col
            kq = _dot_nt(jnp.concatenate([kb, q], axis=0), k)
            m_s[n] = jnp.where(strict, kq[0:c] * decay, 0.0)
            qk_s[n] = kq[c:] * decay
            rhs_s[rows, h * 2 * DVA:(h + 1) * 2 * DVA] = jnp.concatenate([v * beta_col, kb * eg], axis=1)
            qe_s[rows, hs] = q * eg
            kdec_s[rows, hs] = k * ed_blk[:, LANE_G + h:LANE_G + h + 1]


def _inverse_stages(m_s, p_s, pw_s, n_inst, c):
    ri = _iota2(c, c, 0)
    ci = _iota2(c, c, 1)
    eye = (ri == ci).astype(F32)
    base = min(16, c)
    same = (ri // base) == (ci // base)
    for n in range(n_inst):
        a = jnp.where(same, -m_s[n], 0.0)
        pw_s[n] = a
        p_s[n] = eye + a
    k = 2
    while k < base:
        for n in range(n_inst):
            pw = pw_s[n]
            pw_s[n] = _dot(pw, pw)
        for n in range(n_inst):
            p = p_s[n]
            p_s[n] = p + _dot(p, pw_s[n])
        k *= 2
    size = base
    while size < c:
        nxt = size * 2
        same_nxt = (ri // nxt) == (ci // nxt)
        off_mask = same_nxt & jnp.logical_not(same)
        for n in range(n_inst):
            pw_s[n] = _dot(p_s[n], jnp.where(off_mask, m_s[n], 0.0))
        for n in range(n_inst):
            p = p_s[n]
            p_s[n] = p - _dot(pw_s[n], p)
        same = same_nxt
        size = nxt


def _level_ref_rows(bc, half, c):
    w = bc.shape[1]
    parts = []
    if half >= SUBLANES // 2:
        for blk in range(c // (2 * half)):
            mrow = blk * 2 * half + half
            parts.append(jnp.broadcast_to(bc[mrow:mrow + 1, :], (2 * half, w)))
    else:
        sub = _iota2(SUBLANES, w, 0)
        for grp in range(c // SUBLANES):
            acc = None
            for blk in range(SUBLANES // (2 * half)):
                mrow = grp * SUBLANES + blk * 2 * half + half
                b = jnp.broadcast_to(bc[mrow:mrow + 1, :], (SUBLANES, w))
                acc = b if acc is None else jnp.where(sub >= blk * 2 * half, b, acc)
            parts.append(acc)
    return jnp.concatenate(parts, axis=0)


def _gla_prepare_steps(qkb_s, vb_s, bc_s, attn_s, ob_s, pkv_s, qeb_s, tt, c):
    ri = _iota2(c, c, 0)
    ci = _iota2(c, c, 1)
    n_ch = tt // c

    def operands(ch, h):
        rows = slice(ch * c, (ch + 1) * c)
        q = qkb_s[rows, h * DKB:(h + 1) * DKB]
        k = qkb_s[rows, HB * DKB + h * DKB:HB * DKB + (h + 1) * DKB]
        return rows, q, k

    def diag():
        for ch in range(n_ch):
            for h in range(HB):
                _, q, k = operands(ch, h)
                attn_s[ch * HB + h] = jnp.where(ri == ci, jnp.sum(q * k, axis=-1, keepdims=True), 0.0)

    def level(half):
        valid = (((ri // (2 * half)) == (ci // (2 * half))) & ((ri % (2 * half)) >= half)
                 & ((ci % (2 * half)) < half))
        for ch in range(n_ch):
            bc_all = bc_s[ch * c:(ch + 1) * c, :]
            d_all = bc_all - _level_ref_rows(bc_all, half, c)
            upper_all = (_iota2(c, HB * DKB, 0) % (2 * half)) >= half
            e_all = jnp.exp(jnp.where(upper_all, d_all, -d_all))
            upper = upper_all[:, 0:DKB]
            for h in range(HB):
                _, q, k = operands(ch, h)
                e = e_all[:, h * DKB:(h + 1) * DKB]
                sc = _dot_nt(jnp.where(upper, q * e, 0.0), jnp.where(upper, 0.0, k * e))
                attn_s[ch * HB + h] += jnp.where(valid, sc, 0.0)

    def finish():
        for ch in range(n_ch):
            for h in range(HB):
                rows, q, k = operands(ch, h)
                v = vb_s[rows, h * DVB:(h + 1) * DVB]
                bc = bc_s[rows, h * DKB:(h + 1) * DKB]
                ob_s[rows, h * DVB:(h + 1) * DVB] = _dot(attn_s[ch * HB + h], v)
                pkv_s[ch * HB + h] = _dot_tn(k * jnp.exp(bc[c - 1:c, :] - bc), v)
                qeb_s[rows, h * DKB:(h + 1) * DKB] = q * jnp.exp(bc)

    steps = [diag]
    half = c // 2
    while half >= 1:
        steps.append(functools.partial(level, half))
        half //= 2
    steps.append(finish)
    return steps


def _mixer_body(x_ref, sa0_ref, sb0_ref, ca0_ref, nmix_ref, wca_ref, alog_ref, dt_ref, bgk_ref, ona_ref, onb_ref,
                w1_ref, w2_ref, wa_ref, wb_ref, wo_ref,
                x1_ref, sa_ref, ca_ref, sb_ref,
                cbuf, hb_s, post_s, qkv_s, beta_s, gcum_s, gt_s, qkb_s, vb_s, bc_s, oa_s, ob_s,
                m_s, p_s, pw_s, qk_s, rhs_s, uw_s, qe_s, kdec_s, attn_s, pkv_s, qeb_s, gated_s, mix_s, *, tt, c):
    t_idx = pl.program_id(1)
    n_ch = tt // c

    @pl.when(t_idx == 0)
    def _():
        sa_ref[...] = sa0_ref[...]
        sb_ref[...] = sb0_ref[...]
        cbuf[SUBLANES - (CONV_A - 1):SUBLANES, :] = ca0_ref[...]

    rb = min(tt, 64)
    for r in range(tt // rb):
        rr = slice(r * rb, (r + 1) * rb)
        hb_s[rr, :] = _rms(x_ref[rr, :], nmix_ref[...]).astype(BF16)
    hb = hb_s[...]
    cbuf[SUBLANES:SUBLANES + tt, :] = jnp.dot(hb, w1_ref[:, O_QKVA:O_QKVA + C_A], preferred_element_type=F32)
    base = SUBLANES - (CONV_A - 1)
    n_blk = C_A // LANES
    post_w = 2 * LANES
    n_post = W_POST // post_w
    for j in range(max(n_blk, n_post)):
        if j < n_blk:
            cols = slice(j * LANES, (j + 1) * LANES)
            acc = cbuf[base:base + tt, cols] * wca_ref[0:1, cols]
            for i in range(1, CONV_A):
                acc = acc + cbuf[base + i:base + i + tt, cols] * wca_ref[i:i + 1, cols]
            act = _silu(acc)
            if j < HA:
                act = _l2n(act) * (DKA ** -0.5)
            elif j < 2 * HA:
                act = _l2n(act)
            qkv_s[:, cols] = act
        if j < n_post:
            pc = slice(j * post_w, (j + 1) * post_w)
            post_s[:, pc] = jnp.dot(hb, w1_ref[:, O_POST + j * post_w:O_POST + (j + 1) * post_w],
                                    preferred_element_type=F32)
    tail = cbuf[tt + base:tt + SUBLANES, :]
    cbuf[base:SUBLANES, :] = tail
    ca_ref[...] = tail
    small = jnp.dot(hb, w1_ref[:, O_SMALL:O_SMALL + LANES], preferred_element_type=F32)
    beta, g = _gates(small, alog_ref[...], dt_ref[...])
    beta_s[...] = beta
    g_cum = _cumsum_chunks(g, c)
    gcum_s[...] = g_cum
    gt_s[...] = g_cum.T

    _delta_prepare(qkv_s, beta_s, gcum_s, gt_s, m_s, qk_s, rhs_s, qe_s, kdec_s, tt, c)
    _inverse_stages(m_s, p_s, pw_s, n_ch * HA, c)
    for ch in range(n_ch):
        rows = slice(ch * c, (ch + 1) * c)
        for h in range(HA):
            cols = slice(h * 2 * DVA, (h + 1) * 2 * DVA)
            uw_s[rows, cols] = _dot(p_s[ch * HA + h], rhs_s[rows, cols])

    small_b = small.astype(BF16)
    for h in range(HB):
        hs = slice(h * DKB, (h + 1) * DKB)
        lg_pre = jnp.dot(small_b, w2_ref[:, hs], preferred_element_type=F32) + bgk_ref[:, hs]
        bc_s[:, hs] = _cumsum_chunks(jax.nn.log_sigmoid(lg_pre) / GLA_GATE_NORM, c)
    qkb_s[:, 0:HB * DKB] = jnp.dot(hb, w1_ref[:, O_QB:O_QB + HB * DKB], preferred_element_type=F32) * (DKB ** -0.5)
    qkb_s[:, HB * DKB:] = jnp.dot(hb, w1_ref[:, O_KB:O_KB + HB * DKB], preferred_element_type=F32)
    vb_s[...] = jnp.dot(hb, w1_ref[:, O_VB:O_VB + HB * DVB], preferred_element_type=F32)

    gla_steps = _gla_prepare_steps(qkb_s, vb_s, bc_s, attn_s, ob_s, pkv_s, qeb_s, tt, c)
    per_chunk = -(-len(gla_steps) // n_ch)
    for ch in range(n_ch):
        rows = slice(ch * c, (ch + 1) * c)
        last = slice(ch * c + c - 1, ch * c + c)
        for h in range(HA):
            hs = slice(h * DKA, (h + 1) * DKA)
            s = sa_ref[h]
            u = uw_s[rows, h * 2 * DVA:h * 2 * DVA + DVA]
            w = uw_s[rows, h * 2 * DVA + DVA:(h + 1) * 2 * DVA]
            wq = _dot(jnp.concatenate([w, qe_s[rows, hs]], axis=0), s)
            v_new = u - wq[0:c]
            oa_s[rows, h * DVA:(h + 1) * DVA] = wq[c:] + _dot(qk_s[ch * HA + h], v_new)
            g_last = gcum_s[last, LANE_G + h:LANE_G + h + 1]
            sa_ref[h] = s * jnp.exp(g_last) + _dot_tn(kdec_s[rows, hs], v_new)
        for step in gla_steps[ch * per_chunk:(ch + 1) * per_chunk]:
            step()

    for h in range(HA):
        hs = slice(h * DVA, (h + 1) * DVA)
        gated_s[:, hs] = (_rms(oa_s[:, hs], ona_ref[...]) * _silu(post_s[:, P_ZA + h * DVA:P_ZA + (h + 1) * DVA])
                          ).astype(BF16)

    for ch in range(n_ch):
        rows = slice(ch * c, (ch + 1) * c)
        last = slice(ch * c + c - 1, ch * c + c)
        for h in range(HB):
            s = sb_ref[h]
            ob_s[rows, h * DVB:(h + 1) * DVB] += _dot(qeb_s[rows, h * DKB:(h + 1) * DKB], s)
            e_col = jnp.exp(bc_s[last, h * DKB:(h + 1) * DKB]).T
            sb_ref[h] = s * e_col + pkv_s[ch * HB + h]

    for h in range(HB):
        hs = slice(h * DVB, (h + 1) * DVB)
        gated_s[:, HA * DVA + h * DVB:HA * DVA + (h + 1) * DVB] = (
            _rms(ob_s[:, hs], onb_ref[...]) * _silu(post_s[:, P_RB + h * DVB:P_RB + (h + 1) * DVB])).astype(BF16)
    out_w = 2 * LANES
    ga = gated_s[:, 0:HA * DVA]
    gb = gated_s[:, HA * DVA:]
    for n in range(D_MODEL // out_w):
        ns = slice(n * out_w, (n + 1) * out_w)
        y_a = jnp.dot(ga, wa_ref[:, ns], preferred_element_type=F32)
        y_b = jnp.dot(gb, wb_ref[:, ns], preferred_element_type=F32)
        mix_s[:, ns] = (jax.nn.sigmoid(post_s[:, P_GA + n * out_w:P_GA + (n + 1) * out_w]) * y_a
                        + jax.nn.sigmoid(post_s[:, P_GB + n * out_w:P_GB + (n + 1) * out_w]) * y_b).astype(BF16)
    mix = mix_s[...]
    for n in range(D_MODEL // out_w):
        ns = slice(n * out_w, (n + 1) * out_w)
        x1_ref[:, ns] = x_ref[:, ns] + jnp.dot(mix, wo_ref[:, ns], preferred_element_type=F32)


def _const_spec(shape):
    nd = len(shape)
    return pl.BlockSpec(shape, lambda *_: (0,) * nd, pipeline_mode=pl.Buffered(1))


def _mixer_call(x, sa0, sb0, ca0, p, tt):
    b, t, _ = x.shape
    c = CHUNK
    nt = t // tt
    n_inst = (tt // c) * HA
    row = lambda n: _const_spec((1, n))
    in_specs = [
        pl.BlockSpec((None, tt, D_MODEL), lambda i, j: (i, j, 0)),
        _const_spec((HA, DKA, DVA)), _const_spec((HB, DKB, DVB)), _const_spec((CONV_A - 1, C_A)),
        row(D_MODEL), _const_spec((CONV_A, C_A)), row(LANES), row(LANES), row(HB * DKB), row(DVA), row(DVB),
        _const_spec((D_MODEL, W_IN_COLS)), _const_spec((LANES, HB * DKB)),
        _const_spec((HA * DVA, D_MODEL)), _const_spec((HB * DVB, D_MODEL)), _const_spec((D_MODEL, D_MODEL)),
    ]
    out_specs = [
        pl.BlockSpec((None, tt, D_MODEL), lambda i, j: (i, j, 0)),
        pl.BlockSpec((None, HA, DKA, DVA), lambda i, j: (i, 0, 0, 0)),
        pl.BlockSpec((None, CONV_A - 1, C_A), lambda i, j: (i, 0, 0)),
        pl.BlockSpec((None, HB, DKB, DVB), lambda i, j: (i, 0, 0, 0)),
    ]
    out_shape = [
        jax.ShapeDtypeStruct((b, t, D_MODEL), F32),
        jax.ShapeDtypeStruct((b, HA, DKA, DVA), F32),
        jax.ShapeDtypeStruct((b, CONV_A - 1, C_A), F32),
        jax.ShapeDtypeStruct((b, HB, DKB, DVB), F32),
    ]
    mats = lambda: pltpu.VMEM((n_inst, c, c), F32)
    scratch = [
        pltpu.VMEM((tt + SUBLANES, C_A), F32),
        pltpu.VMEM((tt, D_MODEL), BF16),
        pltpu.VMEM((tt, W_POST), F32),
        pltpu.VMEM((tt, C_A), F32),
        pltpu.VMEM((tt, LANES), F32),
        pltpu.VMEM((tt, LANES), F32),
        pltpu.VMEM((LANES, tt), F32),
        pltpu.VMEM((tt, 2 * HB * DKB), F32),
        pltpu.VMEM((tt, HB * DVB), F32),
        pltpu.VMEM((tt, HB * DKB), F32),
        pltpu.VMEM((tt, HA * DVA), F32),
        pltpu.VMEM((tt, HB * DVB), F32),
        mats(), mats(), mats(), mats(),
        pltpu.VMEM((tt, HA * 2 * DVA), F32),
        pltpu.VMEM((tt, HA * 2 * DVA), F32),
        pltpu.VMEM((tt, HA * DKA), F32),
        pltpu.VMEM((tt, HA * DKA), F32),
        mats(),
        pltpu.VMEM((n_inst, DKB, DVB), F32),
        pltpu.VMEM((tt, HB * DKB), F32),
        pltpu.VMEM((tt, HA * DVA + HB * DVB), BF16),
        pltpu.VMEM((tt, D_MODEL), BF16),
    ]
    return pl.pallas_call(
        functools.partial(_mixer_body, tt=tt, c=c),
        grid=(b, nt), in_specs=in_specs, out_specs=out_specs, out_shape=out_shape, scratch_shapes=scratch,
        compiler_params=pltpu.CompilerParams(dimension_semantics=("arbitrary", "arbitrary"),
                                             vmem_limit_bytes=VMEM_LIMIT),
        name="mixer",
    )(x, sa0, sb0, ca0, p["norm_mix"], p["w_conv_a"], p["alog_row"], p["dt_row"], p["b_gk"], p["onorm_a"],
      p["onorm_b"], p["w1"], p["w2"], p["w_a_out"], p["w_b_out"], p["w_o"])


def _ffn_core(x1, u_conv, gf, wdn_ref, nfin):
    act = _gelu_tanh(u_conv) * gf
    x2 = x1 + _dot(act, wdn_ref[...])
    return _rms(x2, nfin)


def _ffn_body(x1_ref, cf0_ref, nffn_ref, wcf_ref, bcf_ref, nfin_ref, wup_ref, wdn_ref,
              y_ref, cf_ref, ubuf, *, tt):
    t_idx = pl.program_id(1)
    base = SUBLANES - (CONV_F - 1)

    @pl.when(t_idx == 0)
    def _():
        ubuf[base:SUBLANES, :] = cf0_ref[...]

    x1 = x1_ref[...]
    h2 = _rms(x1, nffn_ref[...]).astype(BF16)
    ubuf[SUBLANES:SUBLANES + tt, :] = jnp.dot(h2, wup_ref[:, 0:D_FF], preferred_element_type=F32)
    gf = jnp.dot(h2, wup_ref[:, D_FF:], preferred_element_type=F32)
    acc = ubuf[base:base + tt, :] * wcf_ref[0:1, :]
    for i in range(1, CONV_F):
        acc = acc + ubuf[base + i:base + i + tt, :] * wcf_ref[i:i + 1, :]
    acc = acc + bcf_ref[...]
    tail = ubuf[tt + base:tt + SUBLANES, :]
    ubuf[base:SUBLANES, :] = tail
    cf_ref[...] = tail
    y_ref[...] = _ffn_core(x1, acc, gf, wdn_ref, nfin_ref[...])


def _ffn_call(x1, cf0, p, tt):
    b, t, _ = x1.shape
    nt = t // tt
    row = lambda n: _const_spec((1, n))
    in_specs = [
        pl.BlockSpec((None, tt, D_MODEL), lambda i, j: (i, j, 0)),
        _const_spec((CONV_F - 1, D_FF)), row(D_MODEL), _const_spec((CONV_F, D_FF)), row(D_FF), row(D_MODEL),
        _const_spec((D_MODEL, 2 * D_FF)), _const_spec((D_FF, D_MODEL)),
    ]
    out_specs = [
        pl.BlockSpec((None, tt, D_MODEL), lambda i, j: (i, j, 0)),
        pl.BlockSpec((None, CONV_F - 1, D_FF), lambda i, j: (i, 0, 0)),
    ]
    out_shape = [jax.ShapeDtypeStruct((b, t, D_MODEL), F32), jax.ShapeDtypeStruct((b, CONV_F - 1, D_FF), F32)]
    return pl.pallas_call(
        functools.partial(_ffn_body, tt=tt),
        grid=(b, nt), in_specs=in_specs, out_specs=out_specs, out_shape=out_shape,
        scratch_shapes=[pltpu.VMEM((tt + SUBLANES, D_FF), F32)],
        compiler_params=pltpu.CompilerParams(dimension_semantics=("arbitrary", "arbitrary"),
                                             vmem_limit_bytes=VMEM_LIMIT),
        name="convffn",
    )(x1, cf0, p["norm_ffn"], p["w_conv_f"], p["b_conv_f"], p["norm_final"], p["w_ffn_in"], p["w_ffn_out"])


def _dec_head_body(x_ref, cs_ref, nmix_ref, wca_ref, alog_ref, dt_ref, bgk_ref, w1_ref, w2_ref,
                   qkva_ref, beta_ref, g_ref, qkvb_ref, lg_ref, post_ref, csn_ref):
    x = x_ref[...]
    hb = _rms(x, nmix_ref[...]).astype(BF16)
    pre = jnp.dot(hb, w1_ref[:, O_QKVA:O_QKVA + C_A], preferred_element_type=F32)
    acc = cs_ref[:, 0:C_A] * wca_ref[0:1, :]
    for i in range(1, CONV_A - 1):
        acc = acc + cs_ref[:, i * C_A:(i + 1) * C_A] * wca_ref[i:i + 1, :]
    acc = acc + pre * wca_ref[CONV_A - 1:CONV_A, :]
    for i in range(CONV_A - 2):
        csn_ref[:, i * C_A:(i + 1) * C_A] = cs_ref[:, (i + 1) * C_A:(i + 2) * C_A]
    csn_ref[:, (CONV_A - 2) * C_A:] = pre
    qkva_ref[...] = _qkv_a_post(acc)
    small = jnp.dot(hb, w1_ref[:, O_SMALL:O_SMALL + LANES], preferred_element_type=F32)
    beta, g = _gates(small, alog_ref[...], dt_ref[...])
    beta_ref[...] = beta
    g_ref[...] = g
    lg_ref[...] = jax.nn.log_sigmoid(_dot(small, w2_ref[...]) + bgk_ref[...]) / GLA_GATE_NORM
    qkb = jnp.dot(hb, w1_ref[:, O_QB:O_QB + 2 * HB * DKB], preferred_element_type=F32)
    qkvb_ref[:, 0:HB * DKB] = qkb[:, 0:HB * DKB] * (DKB ** -0.5)
    qkvb_ref[:, HB * DKB:2 * HB * DKB] = qkb[:, HB * DKB:]
    qkvb_ref[:, 2 * HB * DKB:] = jnp.dot(hb, w1_ref[:, O_VB:O_VB + HB * DVB], preferred_element_type=F32)
    post_ref[...] = jnp.dot(hb, w1_ref[:, O_POST:O_POST + W_POST], preferred_element_type=F32)


def _dec_head_call(xs, cs, p):
    n = xs.shape[0]
    shapes = [(n, C_A), (n, LANES), (n, LANES), (n, 2 * HB * DKB + HB * DVB), (n, HB * DKB), (n, W_POST),
              (n, (CONV_A - 1) * C_A)]
    return pl.pallas_call(
        _dec_head_body,
        out_shape=[jax.ShapeDtypeStruct(s, F32) for s in shapes],
        compiler_params=pltpu.CompilerParams(vmem_limit_bytes=VMEM_LIMIT),
        name="decode_head",
    )(xs, cs, p["norm_mix"], p["w_conv_a"], p["alog_row"], p["dt_row"], p["b_gk"], p["w1"], p["w2"])


def _dec_rec_body(qkva_ref, beta_ref, g_ref, qkvb_ref, lg_ref, sa_ref, sb_ref,
                  oa_ref, ob_ref, san_ref, sbn_ref, *, tb):
    beta = beta_ref[...]
    eg = jnp.exp(g_ref[...])
    for h in range(HA):
        q_t = qkva_ref[:, h * DKA:(h + 1) * DKA].T
        k_t = qkva_ref[:, HA * DKA + h * DKA:HA * DKA + (h + 1) * DKA].T
        for j in range(tb):
            k_col = k_t[:, j:j + 1]
            s = sa_ref[j, h] * eg[j:j + 1, LANE_G + h:LANE_G + h + 1]
            v = qkva_ref[j:j + 1, 2 * HA * DKA + h * DVA:2 * HA * DKA + (h + 1) * DVA]
            err = (v - jnp.sum(k_col * s, axis=0, keepdims=True)) * beta[j:j + 1, LANE_BETA + h:LANE_BETA + h + 1]
            s = s + k_col * err
            san_ref[j, h] = s
            oa_ref[j:j + 1, h * DVA:(h + 1) * DVA] = jnp.sum(q_t[:, j:j + 1] * s, axis=0, keepdims=True)
    for h in range(HB):
        q_t = qkvb_ref[:, h * DKB:(h + 1) * DKB].T
        k_t = qkvb_ref[:, HB * DKB + h * DKB:HB * DKB + (h + 1) * DKB].T
        d_t = jnp.exp(lg_ref[:, h * DKB:(h + 1) * DKB]).T
        for j in range(tb):
            v = qkvb_ref[j:j + 1, 2 * HB * DKB + h * DVB:2 * HB * DKB + (h + 1) * DVB]
            s = sb_ref[j, h] * d_t[:, j:j + 1] + k_t[:, j:j + 1] * v
            sbn_ref[j, h] = s
            ob_ref[j:j + 1, h * DVB:(h + 1) * DVB] = jnp.sum(q_t[:, j:j + 1] * s, axis=0, keepdims=True)


def _dec_rec_call(qkva, beta, g, qkvb, lg, sa, sb, tb):
    n = qkva.shape[0]
    rows = lambda w: pl.BlockSpec((tb, w), lambda i: (i, 0))
    in_specs = [rows(C_A), rows(LANES), rows(LANES), rows(2 * HB * DKB + HB * DVB), rows(HB * DKB),
                pl.BlockSpec((tb, HA, DKA, DVA), lambda i: (i, 0, 0, 0)),
                pl.BlockSpec((tb, HB, DKB, DVB), lambda i: (i, 0, 0, 0))]
    out_specs = [rows(HA * DVA), rows(HB * DVB),
                 pl.BlockSpec((tb, HA, DKA, DVA), lambda i: (i, 0, 0, 0)),
                 pl.BlockSpec((tb, HB, DKB, DVB), lambda i: (i, 0, 0, 0))]
    out_shape = [jax.ShapeDtypeStruct((n, HA * DVA), F32), jax.ShapeDtypeStruct((n, HB * DVB), F32),
                 jax.ShapeDtypeStruct(sa.shape, F32), jax.ShapeDtypeStruct(sb.shape, F32)]
    return pl.pallas_call(
        functools.partial(_dec_rec_body, tb=tb),
        grid=(n // tb,), in_specs=in_specs, out_specs=out_specs, out_shape=out_shape,
        compiler_params=pltpu.CompilerParams(dimension_semantics=("arbitrary",), vmem_limit_bytes=VMEM_LIMIT),
        name="decode_recurrence",
    )(qkva, beta, g, qkvb, lg, sa, sb)


def _dec_tail_body(x_ref, oa_ref, ob_ref, post_ref, cf_ref, ona_ref, onb_ref, nffn_ref, wcf_ref, bcf_ref, nfin_ref,
                   wa_ref, wb_ref, wo_ref, wup_ref, wdn_ref, y_ref, cfn_ref):
    post = post_ref[...]
    y_a = _branch_a(oa_ref[...], post, ona_ref[...], wa_ref)
    y_b = _branch_b(ob_ref[...], post, onb_ref[...], wb_ref)
    x1 = _mix_residual(x_ref[...], y_a, y_b, post, wo_ref)
    h2 = _rms(x1, nffn_ref[...]).astype(BF16)
    u = jnp.dot(h2, wup_ref[:, 0:D_FF], preferred_element_type=F32)
    gf = jnp.dot(h2, wup_ref[:, D_FF:], preferred_element_type=F32)
    acc = cf_ref[:, 0:D_FF] * wcf_ref[0:1, :]
    for i in range(1, CONV_F - 1):
        acc = acc + cf_ref[:, i * D_FF:(i + 1) * D_FF] * wcf_ref[i:i + 1, :]
    acc = acc + u * wcf_ref[CONV_F - 1:CONV_F, :] + bcf_ref[...]
    for i in range(CONV_F - 2):
        cfn_ref[:, i * D_FF:(i + 1) * D_FF] = cf_ref[:, (i + 1) * D_FF:(i + 2) * D_FF]
    cfn_ref[:, (CONV_F - 2) * D_FF:] = u
    y_ref[...] = _ffn_core(x1, acc, gf, wdn_ref, nfin_ref[...])


def _dec_tail_call(xs, oa, ob, post, cf, p):
    n = xs.shape[0]
    return pl.pallas_call(
        _dec_tail_body,
        out_shape=[jax.ShapeDtypeStruct((n, D_MODEL), F32), jax.ShapeDtypeStruct((n, (CONV_F - 1) * D_FF), F32)],
        compiler_params=pltpu.CompilerParams(vmem_limit_bytes=VMEM_LIMIT),
        name="decode_tail",
    )(xs, oa, ob, post, cf, p["onorm_a"], p["onorm_b"], p["norm_ffn"], p["w_conv_f"], p["b_conv_f"],
      p["norm_final"], p["w_a_out"], p["w_b_out"], p["w_o"], p["w_ffn_in"], p["w_ffn_out"])


def _prep_params(l, norm_mix, w_in, w_conv_a, a_log, dt_bias, w_gk2, b_gk, onorm_a, onorm_b, w_a_out, w_b_out,
                 w_o, norm_ffn, w_ffn_in, w_conv_f, b_conv_f, w_ffn_out, norm_final):
    sizes = [HA * DKA, HA * DKA, HA * DVA, HA * DVA, HA, HA, HB * DKB, HB * DKB, HB * DVB, HB * DVB, GATE_RANK,
             D_MODEL, D_MODEL]
    offs = [0]
    for s in sizes:
        offs.append(offs[-1] + s)
    w = w_in[l]
    seg = lambda i: w[:, offs[i]:offs[i + 1]]
    n_small = 2 * HA + GATE_RANK
    pad = jnp.zeros((D_MODEL, LANES - n_small), w.dtype)
    w1 = jnp.concatenate([seg(0), seg(1), seg(2), seg(6), seg(7), seg(8), seg(4), seg(5), seg(10), pad,
                          seg(3), seg(9), seg(11), seg(12)], axis=1).astype(BF16)
    w2 = jnp.zeros((LANES, HB * DKB), F32).at[2 * HA:n_small].set(w_gk2[l]).astype(BF16)
    lane_row = lambda v: jnp.zeros((1, LANES), F32).at[0, LANE_G:LANE_G + HA].set(v)
    return dict(
        norm_mix=norm_mix[l][None], w_conv_a=w_conv_a[l], alog_row=lane_row(a_log[l]), dt_row=lane_row(dt_bias[l]),
        b_gk=b_gk[l][None], onorm_a=onorm_a[l][None], onorm_b=onorm_b[l][None], w1=w1, w2=w2,
        w_a_out=w_a_out[l].astype(BF16), w_b_out=w_b_out[l].astype(BF16), w_o=w_o[l].astype(BF16),
        norm_ffn=norm_ffn[l][None], w_ffn_in=w_ffn_in[l].astype(BF16), w_conv_f=w_conv_f[l],
        b_conv_f=b_conv_f[l][None], w_ffn_out=w_ffn_out[l].astype(BF16), norm_final=norm_final[None])


PROMPT_TILE = 256
DECODE_TILE = 8


def kernel(x_prompt, x_sample, state_delta, state_delta_conv, state_gla, state_ffn_conv, meta_tokens, norm_mix, w_in, w_conv_a, a_log, dt_bias, w_gk2, b_gk, onorm_a, onorm_b, w_a_out, w_b_out, w_o, norm_ffn, w_ffn_in, w_conv_f, b_conv_f, w_ffn_out, norm_final):
    assert w_in.shape[0] == 1, "single layer only"
    l = 0
    p = _prep_params(l, norm_mix, w_in, w_conv_a, a_log, dt_bias, w_gk2, b_gk, onorm_a, onorm_b, w_a_out, w_b_out,
                     w_o, norm_ffn, w_ffn_in, w_conv_f, b_conv_f, w_ffn_out, norm_final)
    n_dec = x_sample.shape[0]

    xm = jnp.concatenate([jnp.zeros((CHUNK - N_META, D_MODEL), F32), meta_tokens.astype(F32)], axis=0)[None]
    x1m, sa0, ca0, sb0 = _mixer_call(xm, jnp.zeros((HA, DKA, DVA), F32), jnp.zeros((HB, DKB, DVB), F32),
                                     jnp.zeros((CONV_A - 1, C_A), F32), p, CHUNK)
    _, cf0 = _ffn_call(x1m, jnp.zeros((CONV_F - 1, D_FF), F32), p, CHUNK)

    x1, sa_p, ca_p, sb_p = _mixer_call(x_prompt, sa0[0], sb0[0], ca0[0], p, PROMPT_TILE)
    y_prompt, cf_p = _ffn_call(x1, cf0[0], p, PROMPT_TILE)

    xs = x_sample.reshape(n_dec, D_MODEL)
    cs = state_delta_conv[l].reshape(n_dec, (CONV_A - 1) * C_A)
    cfs = state_ffn_conv[l].reshape(n_dec, (CONV_F - 1) * D_FF)
    qkva, beta, g, qkvb, lg, post, cs_new = _dec_head_call(xs, cs, p)
    oa, ob, sa_s, sb_s = _dec_rec_call(qkva, beta, g, qkvb, lg, state_delta[l], state_gla[l], DECODE_TILE)
    y_s, cf_s = _dec_tail_call(xs, oa, ob, post, cfs, p)

    return (y_prompt, y_s.reshape(n_dec, 1, D_MODEL),
            sa_p[None], ca_p[None], sb_p[None], cf_p[None],
            sa_s[None], cs_new.reshape(1, n_dec, CONV_A - 1, C_A), sb_s[None],
            cf_s.reshape(1, n_dec, CONV_F - 1, D_FF))
```

```python
import functools

import jax
import jax.numpy as jnp
from jax import lax
from jax.experimental import pallas as pl
from jax.experimental.pallas import tpu as pltpu

F32 = jnp.float32
BF16 = jnp.bfloat16

D_MODEL = 1024
N_META = 16
CHUNK = 64
HA, DKA, DVA = 4, 128, 128
HB, DKB, DVB = 4, 128, 256
CONV_A = 4
C_A = 2 * HA * DKA + HA * DVA
GATE_RANK = 16
GLA_GATE_NORM = 16.0
D_FF = 2816
CONV_F = 3
EPS = 1e-6
LANES = 128
SUBLANES = 8

O_QKVA = 0
O_QB = O_QKVA + C_A
O_KB = O_QB + HB * DKB
O_VB = O_KB + HB * DKB
O_SMALL = O_VB + HB * DVB
O_POST = O_SMALL + LANES
P_ZA = 0
P_RB = P_ZA + HA * DVA
P_GA = P_RB + HB * DVB
P_GB = P_GA + D_MODEL
W_POST = P_GB + D_MODEL
W_IN_COLS = O_POST + W_POST
LANE_BETA = 0
LANE_G = HA

VMEM_LIMIT = 56 * 1024 * 1024


def _dot(a, b):
    return jnp.dot(a.astype(BF16), b.astype(BF16), preferred_element_type=F32)


def _dot_nt(a, b):
    return lax.dot_general(a.astype(BF16), b.astype(BF16), (((1,), (1,)), ((), ())), preferred_element_type=F32)


def _dot_tn(a, b):
    return lax.dot_general(a.astype(BF16), b.astype(BF16), (((0,), (0,)), ((), ())), preferred_element_type=F32)


def _rms(x, w):
    return x * lax.rsqrt(jnp.mean(x * x, axis=-1, keepdims=True) + EPS) * w


def _silu(x):
    return x * jax.nn.sigmoid(x)


def _gelu_tanh(x):
    return 0.5 * x * (1.0 + jnp.tanh(0.7978845608028654 * (x + 0.044715 * (x * x * x))))


def _l2n(x):
    return x * lax.rsqrt(jnp.sum(x * x, axis=-1, keepdims=True) + EPS)


def _iota2(n, m, axis):
    return lax.broadcasted_iota(jnp.int32, (n, m), axis)


def _gates(small, alog_row, dt_row):
    beta = jax.nn.sigmoid(small)
    g = -jnp.exp(alog_row) * jax.nn.softplus(small + dt_row)
    return beta, g


def _qkv_a_post(conv_out):
    act = _silu(conv_out)
    parts = []
    for h in range(HA):
        parts.append(_l2n(act[:, h * DKA:(h + 1) * DKA]) * (DKA ** -0.5))
    for h in range(HA):
        o = HA * DKA + h * DKA
        parts.append(_l2n(act[:, o:o + DKA]))
    parts.append(act[:, 2 * HA * DKA:])
    return jnp.concatenate(parts, axis=1)


def _branch_a(oa, post, onorm_a, wa_ref):
    pa = [_rms(oa[:, h * DVA:(h + 1) * DVA], onorm_a) for h in range(HA)]
    return _dot(jnp.concatenate(pa, axis=1) * _silu(post[:, P_ZA:P_ZA + HA * DVA]), wa_ref[...])


def _branch_b(ob, post, onorm_b, wb_ref):
    pb = [_rms(ob[:, h * DVB:(h + 1) * DVB], onorm_b) for h in range(HB)]
    return _dot(jnp.concatenate(pb, axis=1) * _silu(post[:, P_RB:P_RB + HB * DVB]), wb_ref[...])


def _mix_residual(x, y_a, y_b, post, wo_ref):
    mix = (jax.nn.sigmoid(post[:, P_GA:P_GA + D_MODEL]) * y_a
           + jax.nn.sigmoid(post[:, P_GB:P_GB + D_MODEL]) * y_b)
    return x + _dot(mix, wo_ref[...])


def _cumsum_chunks(x, c):
    rowi = _iota2(x.shape[0], x.shape[1], 0) % c
    sh = 1
    while sh < c:
        x = x + jnp.where(rowi >= sh, pltpu.roll(x, sh, axis=0), 0.0)
        sh *= 2
    return x


def _delta_prepare(qkv_s, beta_s, gcum_s, gt_s, m_s, qk_s, rhs_s, qe_s, kdec_s, tt, c):
    ri = _iota2(c, c, 0)
    ci = _iota2(c, c, 1)
    causal = ri >= ci
    strict = ri > ci
    for ch in range(tt // c):
        rows = slice(ch * c, (ch + 1) * c)
        g_blk = gcum_s[rows, :]
        beta_blk = beta_s[rows, :]
        eg_blk = jnp.exp(g_blk)
        ed_blk = jnp.exp(g_blk[c - 1:c, :] - g_blk)
        for h in range(HA):
            n = ch * HA + h
            hs = slice(h * DKA, (h + 1) * DKA)
            q = qkv_s[rows, h * DKA:(h + 1) * DKA]
            k = qkv_s[rows, HA * DKA + h * DKA:HA * DKA + (h + 1) * DKA]
            v = qkv_s[rows, 2 * HA * DKA + h * DVA:2 * HA * DKA + (h + 1) * DVA]
            g_col = g_blk[:, LANE_G + h:LANE_G + h + 1]
            g_row = gt_s[LANE_G + h:LANE_G + h + 1, rows]
            decay = jnp.where(causal, jnp.exp(jnp.where(causal, g_col - g_row, 0.0)), 0.0)
            beta_col = beta_blk[:, LANE_BETA + h:LANE_BETA + h + 1]
            eg = eg_blk[:, LANE_G + h:LANE_G + h + 1]
            kb = k * beta_col
            kq = _dot_nt(jnp.concatenate([kb, q], axis=0), k)
            m_s[n] = jnp.where(strict, kq[0:c] * decay, 0.0)
            qk_s[n] = kq[c:] * decay
            rhs_s[rows, h * 2 * DVA:(h + 1) * 2 * DVA] = jnp.concatenate([v * beta_col, kb * eg], axis=1)
            qe_s[rows, hs] = q * eg
            kdec_s[rows, hs] = k * ed_blk[:, LANE_G + h:LANE_G + h + 1]


def _inverse_stages(m_s, p_s, pw_s, n_inst, c):
    ri = _iota2(c, c, 0)
    ci = _iota2(c, c, 1)
    eye = (ri == ci).astype(F32)
    base = min(16, c)
    same = (ri // base) == (ci // base)
    for n in range(n_inst):
        a = jnp.where(same, -m_s[n], 0.0)
        pw_s[n] = a
        p_s[n] = eye + a
    yield
    k = 2
    while k < base:
        for n in range(n_inst):
            pw = pw_s[n]
            pw_s[n] = _dot(pw, pw)
        yield
        for n in range(n_inst):
            p = p_s[n]
            p_s[n] = p + _dot(p, pw_s[n])
        yield
        k *= 2
    size = base
    while size < c:
        nxt = size * 2
        same_nxt = (ri // nxt) == (ci // nxt)
        off_mask = same_nxt & jnp.logical_not(same)
        for n in range(n_inst):
            pw_s[n] = _dot(p_s[n], jnp.where(off_mask, m_s[n], 0.0))
        yield
        for n in range(n_inst):
            p = p_s[n]
            p_s[n] = p - _dot(pw_s[n], p)
        yield
        same = same_nxt
        size = nxt


def _level_ref_rows(bc, half, c):
    w = bc.shape[1]
    parts = []
    if half >= SUBLANES // 2:
        for blk in range(c // (2 * half)):
            mrow = blk * 2 * half + half
            parts.append(jnp.broadcast_to(bc[mrow:mrow + 1, :], (2 * half, w)))
    else:
        sub = _iota2(SUBLANES, w, 0)
        for grp in range(c // SUBLANES):
            acc = None
            for blk in range(SUBLANES // (2 * half)):
                mrow = grp * SUBLANES + blk * 2 * half + half
                b = jnp.broadcast_to(bc[mrow:mrow + 1, :], (SUBLANES, w))
                acc = b if acc is None else jnp.where(sub >= blk * 2 * half, b, acc)
            parts.append(acc)
    return jnp.concatenate(parts, axis=0)


def _gla_prepare_steps(qkb_s, vb_s, bc_s, attn_s, ob_s, pkv_s, qeb_s, tt, c):
    ri = _iota2(c, c, 0)
    ci = _iota2(c, c, 1)
    n_ch = tt // c

    def operands(ch, h):
        rows = slice(ch * c, (ch + 1) * c)
        q = qkb_s[rows, h * DKB:(h + 1) * DKB]
        k = qkb_s[rows, HB * DKB + h * DKB:HB * DKB + (h + 1) * DKB]
        return rows, q, k

    def diag():
        for ch in range(n_ch):
            for h in range(HB):
                _, q, k = operands(ch, h)
                attn_s[ch * HB + h] = jnp.where(ri == ci, jnp.sum(q * k, axis=-1, keepdims=True), 0.0)

    def level(half):
        valid = (((ri // (2 * half)) == (ci // (2 * half))) & ((ri % (2 * half)) >= half)
                 & ((ci % (2 * half)) < half))
        for ch in range(n_ch):
            bc_all = bc_s[ch * c:(ch + 1) * c, :]
            d_all = bc_all - _level_ref_rows(bc_all, half, c)
            upper_all = (_iota2(c, HB * DKB, 0) % (2 * half)) >= half
            e_all = jnp.exp(jnp.where(upper_all, d_all, -d_all))
            upper = upper_all[:, 0:DKB]
            for h in range(HB):
                _, q, k = operands(ch, h)
                e = e_all[:, h * DKB:(h + 1) * DKB]
                sc = _dot_nt(jnp.where(upper, q * e, 0.0), jnp.where(upper, 0.0, k * e))
                attn_s[ch * HB + h] += jnp.where(valid, sc, 0.0)

    def finish():
        for ch in range(n_ch):
            for h in range(HB):
                rows, q, k = operands(ch, h)
                v = vb_s[rows, h * DVB:(h + 1) * DVB]
                bc = bc_s[rows, h * DKB:(h + 1) * DKB]
                ob_s[rows, h * DVB:(h + 1) * DVB] = _dot(attn_s[ch * HB + h], v)
                pkv_s[ch * HB + h] = _dot_tn(k * jnp.exp(bc[c - 1:c, :] - bc), v)
                qeb_s[rows, h * DKB:(h + 1) * DKB] = q * jnp.exp(bc)

    steps = [diag]
    half = c // 2
    while half >= 1:
        steps.append(functools.partial(level, half))
        half //= 2
    steps.append(finish)
    return steps


def _mixer_body(x_ref, sa0_ref, sb0_ref, ca0_ref, nmix_ref, wca_ref, alog_ref, dt_ref, bgk_ref, ona_ref, onb_ref,
                w1_ref, w2_ref, wa_ref, wb_ref, wo_ref,
                x1_ref, sa_ref, ca_ref, sb_ref,
                cbuf, hb_s, post_s, qkv_s, beta_s, gcum_s, gt_s, qkb_s, vb_s, bc_s, oa_s, ob_s,
                m_s, p_s, pw_s, qk_s, rhs_s, uw_s, qe_s, kdec_s, attn_s, pkv_s, qeb_s, gated_s, mix_s, ktu_s, ktw_s, *, tt, c):
    t_idx = pl.program_id(1)
    n_ch = tt // c

    @pl.when(t_idx == 0)
    def _():
        sa_ref[...] = sa0_ref[...]
        sb_ref[...] = sb0_ref[...]
        cbuf[SUBLANES - (CONV_A - 1):SUBLANES, :] = ca0_ref[...]

    rb = min(tt, 64)
    for r in range(tt // rb):
        rr = slice(r * rb, (r + 1) * rb)
        hb_s[rr, :] = _rms(x_ref[rr, :], nmix_ref[...]).astype(BF16)
    hb = hb_s[...]
    cbuf[SUBLANES:SUBLANES + tt, :] = jnp.dot(hb, w1_ref[:, O_QKVA:O_QKVA + C_A], preferred_element_type=F32)
    base = SUBLANES - (CONV_A - 1)
    n_blk = C_A // LANES
    post_w = 2 * LANES
    n_post = W_POST // post_w
    for j in range(max(n_blk, n_post)):
        if j < n_post:
            pc = slice(j * post_w, (j + 1) * post_w)
            post_s[:, pc] = jnp.dot(hb, w1_ref[:, O_POST + j * post_w:O_POST + (j + 1) * post_w],
                                    preferred_element_type=F32)
        if j < n_blk:
            cols = slice(j * LANES, (j + 1) * LANES)
            acc = cbuf[base:base + tt, cols] * wca_ref[0:1, cols]
            for i in range(1, CONV_A):
                acc = acc + cbuf[base + i:base + i + tt, cols] * wca_ref[i:i + 1, cols]
            act = _silu(acc)
            if j < HA:
                act = _l2n(act) * (DKA ** -0.5)
            elif j < 2 * HA:
                act = _l2n(act)
            qkv_s[:, cols] = act
    tail = cbuf[tt + base:tt + SUBLANES, :]
    cbuf[base:SUBLANES, :] = tail
    ca_ref[...] = tail
    small = jnp.dot(hb, w1_ref[:, O_SMALL:O_SMALL + LANES], preferred_element_type=F32)
    beta, g = _gates(small, alog_ref[...], dt_ref[...])
    beta_s[...] = beta
    g_cum = _cumsum_chunks(g, c)
    gcum_s[...] = g_cum
    gt_s[...] = g_cum.T

    _delta_prepare(qkv_s, beta_s, gcum_s, gt_s, m_s, qk_s, rhs_s, qe_s, kdec_s, tt, c)
    for _ in _inverse_stages(m_s, p_s, pw_s, n_ch * HA, c):
        pass
    for ch in range(n_ch):
        rows = slice(ch * c, (ch + 1) * c)
        for h in range(HA):
            cols = slice(h * 2 * DVA, (h + 1) * 2 * DVA)
            uw_s[rows, cols] = _dot(p_s[ch * HA + h], rhs_s[rows, cols])
    for ch in range(n_ch):
        rows = slice(ch * c, (ch + 1) * c)
        for h in range(HA):
            n = ch * HA + h
            hs = slice(h * DKA, (h + 1) * DKA)
            uw = uw_s[rows, h * 2 * DVA:(h + 1) * 2 * DVA]
            att_uw = _dot(qk_s[n], uw)
            oa_s[rows, h * DVA:(h + 1) * DVA] = att_uw[:, 0:DVA]
            qe_s[rows, hs] = qe_s[rows, hs] - att_uw[:, DVA:]
            kt_uw = _dot_tn(kdec_s[rows, hs], uw)
            ktu_s[n] = kt_uw[:, 0:DVA]
            ktw_s[n] = kt_uw[:, DVA:]

    small_b = small.astype(BF16)
    for h in range(HB):
        hs = slice(h * DKB, (h + 1) * DKB)
        lg_pre = jnp.dot(small_b, w2_ref[:, hs], preferred_element_type=F32) + bgk_ref[:, hs]
        bc_s[:, hs] = _cumsum_chunks(jax.nn.log_sigmoid(lg_pre) / GLA_GATE_NORM, c)
    qkb_s[:, 0:HB * DKB] = jnp.dot(hb, w1_ref[:, O_QB:O_QB + HB * DKB], preferred_element_type=F32) * (DKB ** -0.5)
    qkb_s[:, HB * DKB:] = jnp.dot(hb, w1_ref[:, O_KB:O_KB + HB * DKB], preferred_element_type=F32)
    vb_s[...] = jnp.dot(hb, w1_ref[:, O_VB:O_VB + HB * DVB], preferred_element_type=F32)

    for step in _gla_prepare_steps(qkb_s, vb_s, bc_s, attn_s, ob_s, pkv_s, qeb_s, tt, c):
        step()

    for ch in range(n_ch):
        rows = slice(ch * c, (ch + 1) * c)
        last = slice(ch * c + c - 1, ch * c + c)
        for h in range(HA):
            n = ch * HA + h
            hs = slice(h * DKA, (h + 1) * DKA)
            s = sa_ref[h]
            prod = _dot(jnp.concatenate([ktw_s[n], qe_s[rows, hs]], axis=0), s)
            oa_s[rows, h * DVA:(h + 1) * DVA] += prod[DKA:]
            g_last = gcum_s[last, LANE_G + h:LANE_G + h + 1]
            sa_ref[h] = s * jnp.exp(g_last) + (ktu_s[n] - prod[0:DKA])

    for h in range(HA):
        hs = slice(h * DVA, (h + 1) * DVA)
        gated_s[:, hs] = (_rms(oa_s[:, hs], ona_ref[...]) * _silu(post_s[:, P_ZA + h * DVA:P_ZA + (h + 1) * DVA])
                          ).astype(BF16)

    for ch in range(n_ch):
        rows = slice(ch * c, (ch + 1) * c)
        last = slice(ch * c + c - 1, ch * c + c)
        for h in range(HB):
            s = sb_ref[h]
            ob_s[rows, h * DVB:(h + 1) * DVB] += _dot(qeb_s[rows, h * DKB:(h + 1) * DKB], s)
            e_col = jnp.exp(bc_s[last, h * DKB:(h + 1) * DKB]).T
            sb_ref[h] = s * e_col + pkv_s[ch * HB + h]

    for h in range(HB):
        hs = slice(h * DVB, (h + 1) * DVB)
        gated_s[:, HA * DVA + h * DVB:HA * DVA + (h + 1) * DVB] = (
            _rms(ob_s[:, hs], onb_ref[...]) * _silu(post_s[:, P_RB + h * DVB:P_RB + (h + 1) * DVB])).astype(BF16)
    out_w = 2 * LANES
    ga = gated_s[:, 0:HA * DVA]
    gb = gated_s[:, HA * DVA:]
    for n in range(D_MODEL // out_w):
        ns = slice(n * out_w, (n + 1) * out_w)
        y_a = jnp.dot(ga, wa_ref[:, ns], preferred_element_type=F32)
        y_b = jnp.dot(gb, wb_ref[:, ns], preferred_element_type=F32)
        mix_s[:, ns] = (jax.nn.sigmoid(post_s[:, P_GA + n * out_w:P_GA + (n + 1) * out_w]) * y_a
                        + jax.nn.sigmoid(post_s[:, P_GB + n * out_w:P_GB + (n + 1) * out_w]) * y_b).astype(BF16)
    mix = mix_s[...]
    for n in range(D_MODEL // out_w):
        ns = slice(n * out_w, (n + 1) * out_w)
        x1_ref[:, ns] = x_ref[:, ns] + jnp.dot(mix, wo_ref[:, ns], preferred_element_type=F32)


def _const_spec(shape):
    nd = len(shape)
    return pl.BlockSpec(shape, lambda *_: (0,) * nd, pipeline_mode=pl.Buffered(1))


def _mixer_call(x, sa0, sb0, ca0, p, tt):
    b, t, _ = x.shape
    c = CHUNK
    nt = t // tt
    n_inst = (tt // c) * HA
    row = lambda n: _const_spec((1, n))
    in_specs = [
        pl.BlockSpec((None, tt, D_MODEL), lambda i, j: (i, j, 0)),
        _const_spec((HA, DKA, DVA)), _const_spec((HB, DKB, DVB)), _const_spec((CONV_A - 1, C_A)),
        row(D_MODEL), _const_spec((CONV_A, C_A)), row(LANES), row(LANES), row(HB * DKB), row(DVA), row(DVB),
        _const_spec((D_MODEL, W_IN_COLS)), _const_spec((LANES, HB * DKB)),
        _const_spec((HA * DVA, D_MODEL)), _const_spec((HB * DVB, D_MODEL)), _const_spec((D_MODEL, D_MODEL)),
    ]
    out_specs = [
        pl.BlockSpec((None, tt, D_MODEL), lambda i, j: (i, j, 0)),
        pl.BlockSpec((None, HA, DKA, DVA), lambda i, j: (i, 0, 0, 0)),
        pl.BlockSpec((None, CONV_A - 1, C_A), lambda i, j: (i, 0, 0)),
        pl.BlockSpec((None, HB, DKB, DVB), lambda i, j: (i, 0, 0, 0)),
    ]
    out_shape = [
        jax.ShapeDtypeStruct((b, t, D_MODEL), F32),
        jax.ShapeDtypeStruct((b, HA, DKA, DVA), F32),
        jax.ShapeDtypeStruct((b, CONV_A - 1, C_A), F32),
        jax.ShapeDtypeStruct((b, HB, DKB, DVB), F32),
    ]
    mats = lambda: pltpu.VMEM((n_inst, c, c), F32)
    scratch = [
        pltpu.VMEM((tt + SUBLANES, C_A), F32),
        pltpu.VMEM((tt, D_MODEL), BF16),
        pltpu.VMEM((tt, W_POST), F32),
        pltpu.VMEM((tt, C_A), F32),
        pltpu.VMEM((tt, LANES), F32),
        pltpu.VMEM((tt, LANES), F32),
        pltpu.VMEM((LANES, tt), F32),
        pltpu.VMEM((tt, 2 * HB * DKB), F32),
        pltpu.VMEM((tt, HB * DVB), F32),
        pltpu.VMEM((tt, HB * DKB), F32),
        pltpu.VMEM((tt, HA * DVA), F32),
        pltpu.VMEM((tt, HB * DVB), F32),
        mats(), mats(), mats(), mats(),
        pltpu.VMEM((tt, HA * 2 * DVA), F32),
        pltpu.VMEM((tt, HA * 2 * DVA), F32),
        pltpu.VMEM((tt, HA * DKA), F32),
        pltpu.VMEM((tt, HA * DKA), F32),
        mats(),
        pltpu.VMEM((n_inst, DKB, DVB), F32),
        pltpu.VMEM((tt, HB * DKB), F32),
        pltpu.VMEM((tt, HA * DVA + HB * DVB), BF16),
        pltpu.VMEM((tt, D_MODEL), BF16),
        pltpu.VMEM((n_inst, DKA, DVA), F32),
        pltpu.VMEM((n_inst, DKA, DVA), F32),
    ]
    return pl.pallas_call(
        functools.partial(_mixer_body, tt=tt, c=c),
        grid=(b, nt), in_specs=in_specs, out_specs=out_specs, out_shape=out_shape, scratch_shapes=scratch,
        compiler_params=pltpu.CompilerParams(dimension_semantics=("arbitrary", "arbitrary"),
                                             vmem_limit_bytes=VMEM_LIMIT),
        name="mixer",
    )(x, sa0, sb0, ca0, p["norm_mix"], p["w_conv_a"], p["alog_row"], p["dt_row"], p["b_gk"], p["onorm_a"],
      p["onorm_b"], p["w1"], p["w2"], p["w_a_out"], p["w_b_out"], p["w_o"])


def _ffn_core(x1, u_conv, gf, wdn_ref, nfin):
    act = _gelu_tanh(u_conv) * gf
    x2 = x1 + _dot(act, wdn_ref[...])
    return _rms(x2, nfin)


def _ffn_body(x1_ref, cf0_ref, nffn_ref, wcf_ref, bcf_ref, nfin_ref, wup_ref, wdn_ref,
              y_ref, cf_ref, ubuf, *, tt):
    t_idx = pl.program_id(1)
    base = SUBLANES - (CONV_F - 1)

    @pl.when(t_idx == 0)
    def _():
        ubuf[base:SUBLANES, :] = cf0_ref[...]

    x1 = x1_ref[...]
    h2 = _rms(x1, nffn_ref[...]).astype(BF16)
    ubuf[SUBLANES:SUBLANES + tt, :] = jnp.dot(h2, wup_ref[:, 0:D_FF], preferred_element_type=F32)
    gf = jnp.dot(h2, wup_ref[:, D_FF:], preferred_element_type=F32)
    acc = ubuf[base:base + tt, :] * wcf_ref[0:1, :]
    for i in range(1, CONV_F):
        acc = acc + ubuf[base + i:base + i + tt, :] * wcf_ref[i:i + 1, :]
    acc = acc + bcf_ref[...]
    tail = ubuf[tt + base:tt + SUBLANES, :]
    ubuf[base:SUBLANES, :] = tail
    cf_ref[...] = tail
    y_ref[...] = _ffn_core(x1, acc, gf, wdn_ref, nfin_ref[...])


def _ffn_call(x1, cf0, p, tt):
    b, t, _ = x1.shape
    nt = t // tt
    row = lambda n: _const_spec((1, n))
    in_specs = [
        pl.BlockSpec((None, tt, D_MODEL), lambda i, j: (i, j, 0)),
        _const_spec((CONV_F - 1, D_FF)), row(D_MODEL), _const_spec((CONV_F, D_FF)), row(D_FF), row(D_MODEL),
        _const_spec((D_MODEL, 2 * D_FF)), _const_spec((D_FF, D_MODEL)),
    ]
    out_specs = [
        pl.BlockSpec((None, tt, D_MODEL), lambda i, j: (i, j, 0)),
        pl.BlockSpec((None, CONV_F - 1, D_FF), lambda i, j: (i, 0, 0)),
    ]
    out_shape = [jax.ShapeDtypeStruct((b, t, D_MODEL), F32), jax.ShapeDtypeStruct((b, CONV_F - 1, D_FF), F32)]
    return pl.pallas_call(
        functools.partial(_ffn_body, tt=tt),
        grid=(b, nt), in_specs=in_specs, out_specs=out_specs, out_shape=out_shape,
        scratch_shapes=[pltpu.VMEM((tt + SUBLANES, D_FF), F32)],
        compiler_params=pltpu.CompilerParams(dimension_semantics=("arbitrary", "arbitrary"),
                                             vmem_limit_bytes=VMEM_LIMIT),
        name="convffn",
    )(x1, cf0, p["norm_ffn"], p["w_conv_f"], p["b_conv_f"], p["norm_final"], p["w_ffn_in"], p["w_ffn_out"])


def _dec_head_body(x_ref, cs_ref, nmix_ref, wca_ref, alog_ref, dt_ref, bgk_ref, w1_ref, w2_ref,
                   qkva_ref, beta_ref, g_ref, qkvb_ref, lg_ref, post_ref, csn_ref):
    x = x_ref[...]
    hb = _rms(x, nmix_ref[...]).astype(BF16)
    pre = jnp.dot(hb, w1_ref[:, O_QKVA:O_QKVA + C_A], preferred_element_type=F32)
    acc = cs_ref[:, 0:C_A] * wca_ref[0:1, :]
    for i in range(1, CONV_A - 1):
        acc = acc + cs_ref[:, i * C_A:(i + 1) * C_A] * wca_ref[i:i + 1, :]
    acc = acc + pre * wca_ref[CONV_A - 1:CONV_A, :]
    for i in range(CONV_A - 2):
        csn_ref[:, i * C_A:(i + 1) * C_A] = cs_ref[:, (i + 1) * C_A:(i + 2) * C_A]
    csn_ref[:, (CONV_A - 2) * C_A:] = pre
    qkva_ref[...] = _qkv_a_post(acc)
    small = jnp.dot(hb, w1_ref[:, O_SMALL:O_SMALL + LANES], preferred_element_type=F32)
    beta, g = _gates(small, alog_ref[...], dt_ref[...])
    beta_ref[...] = beta
    g_ref[...] = g
    lg_ref[...] = jax.nn.log_sigmoid(_dot(small, w2_ref[...]) + bgk_ref[...]) / GLA_GATE_NORM
    qkb = jnp.dot(hb, w1_ref[:, O_QB:O_QB + 2 * HB * DKB], preferred_element_type=F32)
    qkvb_ref[:, 0:HB * DKB] = qkb[:, 0:HB * DKB] * (DKB ** -0.5)
    qkvb_ref[:, HB * DKB:2 * HB * DKB] = qkb[:, HB * DKB:]
    qkvb_ref[:, 2 * HB * DKB:] = jnp.dot(hb, w1_ref[:, O_VB:O_VB + HB * DVB], preferred_element_type=F32)
    post_ref[...] = jnp.dot(hb, w1_ref[:, O_POST:O_POST + W_POST], preferred_element_type=F32)


def _dec_head_call(xs, cs, p):
    n = xs.shape[0]
    shapes = [(n, C_A), (n, LANES), (n, LANES), (n, 2 * HB * DKB + HB * DVB), (n, HB * DKB), (n, W_POST),
              (n, (CONV_A - 1) * C_A)]
    return pl.pallas_call(
        _dec_head_body,
        out_shape=[jax.ShapeDtypeStruct(s, F32) for s in shapes],
        compiler_params=pltpu.CompilerParams(vmem_limit_bytes=VMEM_LIMIT),
        name="decode_head",
    )(xs, cs, p["norm_mix"], p["w_conv_a"], p["alog_row"], p["dt_row"], p["b_gk"], p["w1"], p["w2"])


def _dec_rec_body(qkva_ref, beta_ref, g_ref, qkvb_ref, lg_ref, sa_ref, sb_ref,
                  oa_ref, ob_ref, san_ref, sbn_ref, *, tb):
    beta = beta_ref[...]
    eg = jnp.exp(g_ref[...])
    for h in range(HA):
        q_t = qkva_ref[:, h * DKA:(h + 1) * DKA].T
        k_t = qkva_ref[:, HA * DKA + h * DKA:HA * DKA + (h + 1) * DKA].T
        for j in range(tb):
            k_col = k_t[:, j:j + 1]
            s = sa_ref[j, h] * eg[j:j + 1, LANE_G + h:LANE_G + h + 1]
            v = qkva_ref[j:j + 1, 2 * HA * DKA + h * DVA:2 * HA * DKA + (h + 1) * DVA]
            err = (v - jnp.sum(k_col * s, axis=0, keepdims=True)) * beta[j:j + 1, LANE_BETA + h:LANE_BETA + h + 1]
            s = s + k_col * err
            san_ref[j, h] = s
            oa_ref[j:j + 1, h * DVA:(h + 1) * DVA] = jnp.sum(q_t[:, j:j + 1] * s, axis=0, keepdims=True)
    for h in range(HB):
        q_t = qkvb_ref[:, h * DKB:(h + 1) * DKB].T
        k_t = qkvb_ref[:, HB * DKB + h * DKB:HB * DKB + (h + 1) * DKB].T
        d_t = jnp.exp(lg_ref[:, h * DKB:(h + 1) * DKB]).T
        for j in range(tb):
            v = qkvb_ref[j:j + 1, 2 * HB * DKB + h * DVB:2 * HB * DKB + (h + 1) * DVB]
            s = sb_ref[j, h] * d_t[:, j:j + 1] + k_t[:, j:j + 1] * v
            sbn_ref[j, h] = s
            ob_ref[j:j + 1, h * DVB:(h + 1) * DVB] = jnp.sum(q_t[:, j:j + 1] * s, axis=0, keepdims=True)


def _dec_rec_call(qkva, beta, g, qkvb, lg, sa, sb, tb):
    n = qkva.shape[0]
    rows = lambda w: pl.BlockSpec((tb, w), lambda i: (i, 0))
    in_specs = [rows(C_A), rows(LANES), rows(LANES), rows(2 * HB * DKB + HB * DVB), rows(HB * DKB),
                pl.BlockSpec((tb, HA, DKA, DVA), lambda i: (i, 0, 0, 0)),
                pl.BlockSpec((tb, HB, DKB, DVB), lambda i: (i, 0, 0, 0))]
    out_specs = [rows(HA * DVA), rows(HB * DVB),
                 pl.BlockSpec((tb, HA, DKA, DVA), lambda i: (i, 0, 0, 0)),
                 pl.BlockSpec((tb, HB, DKB, DVB), lambda i: (i, 0, 0, 0))]
    out_shape = [jax.ShapeDtypeStruct((n, HA * DVA), F32), jax.ShapeDtypeStruct((n, HB * DVB), F32),
                 jax.ShapeDtypeStruct(sa.shape, F32), jax.ShapeDtypeStruct(sb.shape, F32)]
    return pl.pallas_call(
        functools.partial(_dec_rec_body, tb=tb),
        grid=(n // tb,), in_specs=in_specs, out_specs=out_specs, out_shape=out_shape,
        compiler_params=pltpu.CompilerParams(dimension_semantics=("arbitrary",), vmem_limit_bytes=VMEM_LIMIT),
        name="decode_recurrence",
    )(qkva, beta, g, qkvb, lg, sa, sb)


def _dec_tail_body(x_ref, oa_ref, ob_ref, post_ref, cf_ref, ona_ref, onb_ref, nffn_ref, wcf_ref, bcf_ref, nfin_ref,
                   wa_ref, wb_ref, wo_ref, wup_ref, wdn_ref, y_ref, cfn_ref):
    post = post_ref[...]
    y_a = _branch_a(oa_ref[...], post, ona_ref[...], wa_ref)
    y_b = _branch_b(ob_ref[...], post, onb_ref[...], wb_ref)
    x1 = _mix_residual(x_ref[...], y_a, y_b, post, wo_ref)
    h2 = _rms(x1, nffn_ref[...]).astype(BF16)
    u = jnp.dot(h2, wup_ref[:, 0:D_FF], preferred_element_type=F32)
    gf = jnp.dot(h2, wup_ref[:, D_FF:], preferred_element_type=F32)
    acc = cf_ref[:, 0:D_FF] * wcf_ref[0:1, :]
    for i in range(1, CONV_F - 1):
        acc = acc + cf_ref[:, i * D_FF:(i + 1) * D_FF] * wcf_ref[i:i + 1, :]
    acc = acc + u * wcf_ref[CONV_F - 1:CONV_F, :] + bcf_ref[...]
    for i in range(CONV_F - 2):
        cfn_ref[:, i * D_FF:(i + 1) * D_FF] = cf_ref[:, (i + 1) * D_FF:(i + 2) * D_FF]
    cfn_ref[:, (CONV_F - 2) * D_FF:] = u
    y_ref[...] = _ffn_core(x1, acc, gf, wdn_ref, nfin_ref[...])


def _dec_tail_call(xs, oa, ob, post, cf, p):
    n = xs.shape[0]
    return pl.pallas_call(
        _dec_tail_body,
        out_shape=[jax.ShapeDtypeStruct((n, D_MODEL), F32), jax.ShapeDtypeStruct((n, (CONV_F - 1) * D_FF), F32)],
        compiler_params=pltpu.CompilerParams(vmem_limit_bytes=VMEM_LIMIT),
        name="decode_tail",
    )(xs, oa, ob, post, cf, p["onorm_a"], p["onorm_b"], p["norm_ffn"], p["w_conv_f"], p["b_conv_f"],
      p["norm_final"], p["w_a_out"], p["w_b_out"], p["w_o"], p["w_ffn_in"], p["w_ffn_out"])


def _prep_params(l, norm_mix, w_in, w_conv_a, a_log, dt_bias, w_gk2, b_gk, onorm_a, onorm_b, w_a_out, w_b_out,
                 w_o, norm_ffn, w_ffn_in, w_conv_f, b_conv_f, w_ffn_out, norm_final):
    sizes = [HA * DKA, HA * DKA, HA * DVA, HA * DVA, HA, HA, HB * DKB, HB * DKB, HB * DVB, HB * DVB, GATE_RANK,
             D_MODEL, D_MODEL]
    offs = [0]
    for s in sizes:
        offs.append(offs[-1] + s)
    w = w_in[l]
    seg = lambda i: w[:, offs[i]:offs[i + 1]]
    n_small = 2 * HA + GATE_RANK
    pad = jnp.zeros((D_MODEL, LANES - n_small), w.dtype)
    w1 = jnp.concatenate([seg(0), seg(1), seg(2), seg(6), seg(7), seg(8), seg(4), seg(5), seg(10), pad,
                          seg(3), seg(9), seg(11), seg(12)], axis=1).astype(BF16)
    w2 = jnp.zeros((LANES, HB * DKB), F32).at[2 * HA:n_small].set(w_gk2[l]).astype(BF16)
    lane_row = lambda v: jnp.zeros((1, LANES), F32).at[0, LANE_G:LANE_G + HA].set(v)
    return dict(
        norm_mix=norm_mix[l][None], w_conv_a=w_conv_a[l], alog_row=lane_row(a_log[l]), dt_row=lane_row(dt_bias[l]),
        b_gk=b_gk[l][None], onorm_a=onorm_a[l][None], onorm_b=onorm_b[l][None], w1=w1, w2=w2,
        w_a_out=w_a_out[l].astype(BF16), w_b_out=w_b_out[l].astype(BF16), w_o=w_o[l].astype(BF16),
        norm_ffn=norm_ffn[l][None], w_ffn_in=w_ffn_in[l].astype(BF16), w_conv_f=w_conv_f[l],
        b_conv_f=b_conv_f[l][None], w_ffn_out=w_ffn_out[l].astype(BF16), norm_final=norm_final[None])


PROMPT_TILE = 256
FFN_TILE = 512
DECODE_TILE = 8


def kernel(x_prompt, x_sample, state_delta, state_delta_conv, state_gla, state_ffn_conv, meta_tokens, norm_mix, w_in, w_conv_a, a_log, dt_bias, w_gk2, b_gk, onorm_a, onorm_b, w_a_out, w_b_out, w_o, norm_ffn, w_ffn_in, w_conv_f, b_conv_f, w_ffn_out, norm_final):
    assert w_in.shape[0] == 1, "single layer only"
    l = 0
    p = _prep_params(l, norm_mix, w_in, w_conv_a, a_log, dt_bias, w_gk2, b_gk, onorm_a, onorm_b, w_a_out, w_b_out,
                     w_o, norm_ffn, w_ffn_in, w_conv_f, b_conv_f, w_ffn_out, norm_final)
    n_dec = x_sample.shape[0]

    xm = jnp.concatenate([jnp.zeros((CHUNK - N_META, D_MODEL), F32), meta_tokens.astype(F32)], axis=0)[None]
    x1m, sa0, ca0, sb0 = _mixer_call(xm, jnp.zeros((HA, DKA, DVA), F32), jnp.zeros((HB, DKB, DVB), F32),
                                     jnp.zeros((CONV_A - 1, C_A), F32), p, CHUNK)
    _, cf0 = _ffn_call(x1m, jnp.zeros((CONV_F - 1, D_FF), F32), p, CHUNK)

    x1, sa_p, ca_p, sb_p = _mixer_call(x_prompt, sa0[0], sb0[0], ca0[0], p, PROMPT_TILE)
    y_prompt, cf_p = _ffn_call(x1, cf0[0], p, FFN_TILE)

    xs = x_sample.reshape(n_dec, D_MODEL)
    cs = state_delta_conv[l].reshape(n_dec, (CONV_A - 1) * C_A)
    cfs = state_ffn_conv[l].reshape(n_dec, (CONV_F - 1) * D_FF)
    qkva, beta, g, qkvb, lg, post, cs_new = _dec_head_call(xs, cs, p)
    oa, ob, sa_s, sb_s = _dec_rec_call(qkva, beta, g, qkvb, lg, state_delta[l], state_gla[l], DECODE_TILE)
    y_s, cf_s = _dec_tail_call(xs, oa, ob, post, cfs, p)

    return (y_prompt, y_s.reshape(n_dec, 1, D_MODEL),
            sa_p[None], ca_p[None], sb_p[None], cf_p[None],
            sa_s[None], cs_new.reshape(1, n_dec, CONV_A - 1, C_A), sb_s[None],
            cf_s.reshape(1, n_dec, CONV_F - 1, D_FF))
```

```python
import functools

import jax
import jax.numpy as jnp
from jax import lax
from jax.experimental import pallas as pl
from jax.experimental.pallas import tpu as pltpu

F32 = jnp.float32
BF16 = jnp.bfloat16

D_MODEL = 1024
N_META = 16
CHUNK = 64
HA, DKA, DVA = 4, 128, 128
HB, DKB, DVB = 4, 128, 256
CONV_A = 4
C_A = 2 * HA * DKA + HA * DVA
GATE_RANK = 16
GLA_GATE_NORM = 16.0
D_FF = 2816
CONV_F = 3
EPS = 1e-6
LANES = 128
SUBLANES = 8

O_QKVA = 0
O_QB = O_QKVA + C_A
O_KB = O_QB + HB * DKB
O_VB = O_KB + HB * DKB
O_SMALL = O_VB + HB * DVB
O_POST = O_SMALL + LANES
P_ZA = 0
P_RB = P_ZA + HA * DVA
P_GA = P_RB + HB * DVB
P_GB = P_GA + D_MODEL
W_POST = P_GB + D_MODEL
W_IN_COLS = O_POST + W_POST
LANE_BETA = 0
LANE_G = HA

VMEM_LIMIT = 56 * 1024 * 1024


def _dot(a, b):
    return jnp.dot(a.astype(BF16), b.astype(BF16), preferred_element_type=F32)


def _dot_nt(a, b):
    return lax.dot_general(a.astype(BF16), b.astype(BF16), (((1,), (1,)), ((), ())), preferred_element_type=F32)


def _dot_tn(a, b):
    return lax.dot_general(a.astype(BF16), b.astype(BF16), (((0,), (0,)), ((), ())), preferred_element_type=F32)


def _rms(x, w):
    return x * lax.rsqrt(jnp.mean(x * x, axis=-1, keepdims=True) + EPS) * w


def _silu(x):
    return x * jax.nn.sigmoid(x)


def _gelu_tanh(x):
    return 0.5 * x * (1.0 + jnp.tanh(0.7978845608028654 * (x + 0.044715 * (x * x * x))))


def _l2n(x):
    return x * lax.rsqrt(jnp.sum(x * x, axis=-1, keepdims=True) + EPS)


def _iota2(n, m, axis):
    return lax.broadcasted_iota(jnp.int32, (n, m), axis)


def _gates(small, alog_row, dt_row):
    beta = jax.nn.sigmoid(small)
    g = -jnp.exp(alog_row) * jax.nn.softplus(small + dt_row)
    return beta, g


def _qkv_a_post(conv_out):
    act = _silu(conv_out)
    parts = []
    for h in range(HA):
        parts.append(_l2n(act[:, h * DKA:(h + 1) * DKA]) * (DKA ** -0.5))
    for h in range(HA):
        o = HA * DKA + h * DKA
        parts.append(_l2n(act[:, o:o + DKA]))
    parts.append(act[:, 2 * HA * DKA:])
    return jnp.concatenate(parts, axis=1)


def _branch_a(oa, post, onorm_a, wa_ref):
    pa = [_rms(oa[:, h * DVA:(h + 1) * DVA], onorm_a) for h in range(HA)]
    return _dot(jnp.concatenate(pa, axis=1) * _silu(post[:, P_ZA:P_ZA + HA * DVA]), wa_ref[...])


def _branch_b(ob, post, onorm_b, wb_ref):
    pb = [_rms(ob[:, h * DVB:(h + 1) * DVB], onorm_b) for h in range(HB)]
    return _dot(jnp.concatenate(pb, axis=1) * _silu(post[:, P_RB:P_RB + HB * DVB]), wb_ref[...])


def _mix_residual(x, y_a, y_b, post, wo_ref):
    mix = (jax.nn.sigmoid(post[:, P_GA:P_GA + D_MODEL]) * y_a
           + jax.nn.sigmoid(post[:, P_GB:P_GB + D_MODEL]) * y_b)
    return x + _dot(mix, wo_ref[...])


def _cumsum_chunks(x, c):
    rowi = _iota2(x.shape[0], x.shape[1], 0) % c
    sh = 1
    while sh < c:
        x = x + jnp.where(rowi >= sh, pltpu.roll(x, sh, axis=0), 0.0)
        sh *= 2
    return x


def _delta_prepare(qkv_s, beta_s, gcum_s, gt_s, m_s, qk_s, rhs_s, qe_s, kdec_s, tt, c):
    ri = _iota2(c, c, 0)
    ci = _iota2(c, c, 1)
    causal = ri >= ci
    strict = ri > ci
    for ch in range(tt // c):
        rows = slice(ch * c, (ch + 1) * c)
        g_blk = gcum_s[rows, :]
        beta_blk = beta_s[rows, :]
        eg_blk = jnp.exp(g_blk)
        ed_blk = jnp.exp(g_blk[c - 1:c, :] - g_blk)
        for h in range(HA):
            n = ch * HA + h
            hs = slice(h * DKA, (h + 1) * DKA)
            q = qkv_s[rows, h * DKA:(h + 1) * DKA]
            k = qkv_s[rows, HA * DKA + h * DKA:HA * DKA + (h + 1) * DKA]
            v = qkv_s[rows, 2 * HA * DKA + h * DVA:2 * HA * DKA + (h + 1) * DVA]
            g_col = g_blk[:, LANE_G + h:LANE_G + h + 1]
            g_row = gt_s[LANE_G + h:LANE_G + h + 1, rows]
            decay = jnp.where(causal, jnp.exp(jnp.where(causal, g_col - g_row, 0.0)), 0.0)
            beta_col = beta_blk[:, LANE_BETA + h:LANE_BETA + h + 1]
            eg = eg_blk[:, LANE_G + h:LANE_G + h + 1]
            kb = k * beta_col
            kq = _dot_nt(jnp.concatenate([kb, q], axis=0), k)
            m_s[n] = jnp.where(strict, kq[0:c] * decay, 0.0)
            qk_s[n] = kq[c:] * decay
            rhs_s[rows, h * 2 * DVA:(h + 1) * 2 * DVA] = jnp.concatenate([v * beta_col, kb * eg], axis=1)
            qe_s[rows, hs] = q * eg
            kdec_s[rows, hs] = k * ed_blk[:, LANE_G + h:LANE_G + h + 1]


def _inverse_stages(m_s, p_s, pw_s, n_inst, c):
    ri = _iota2(c, c, 0)
    ci = _iota2(c, c, 1)
    eye = (ri == ci).astype(F32)
    base = min(16, c)
    same = (ri // base) == (ci // base)
    for n in range(n_inst):
        a = jnp.where(same, -m_s[n], 0.0)
        pw_s[n] = _dot(a, a)
        p_s[n] = eye + a
    yield
    k = 2
    while k < base:
        for n in range(n_inst):
            p = p_s[n]
            pw = pw_s[n]
            p_s[n] = p + _dot(p, pw)
            if 2 * k < base:
                pw_s[n] = _dot(pw, pw)
        yield
        k *= 2
    size = base
    while size < c:
        nxt = size * 2
        same_nxt = (ri // nxt) == (ci // nxt)
        off_mask = same_nxt & jnp.logical_not(same)
        for n in range(n_inst):
            pw_s[n] = _dot(p_s[n], jnp.where(off_mask, m_s[n], 0.0))
        yield
        for n in range(n_inst):
            p = p_s[n]
            p_s[n] = p - _dot(pw_s[n], p)
        yield
        same = same_nxt
        size = nxt


def _level_ref_rows(bc, half, c):
    w = bc.shape[1]
    parts = []
    if half >= SUBLANES // 2:
        for blk in range(c // (2 * half)):
            mrow = blk * 2 * half + half
            parts.append(jnp.broadcast_to(bc[mrow:mrow + 1, :], (2 * half, w)))
    else:
        sub = _iota2(SUBLANES, w, 0)
        for grp in range(c // SUBLANES):
            acc = None
            for blk in range(SUBLANES // (2 * half)):
                mrow = grp * SUBLANES + blk * 2 * half + half
                b = jnp.broadcast_to(bc[mrow:mrow + 1, :], (SUBLANES, w))
                acc = b if acc is None else jnp.where(sub >= blk * 2 * half, b, acc)
            parts.append(acc)
    return jnp.concatenate(parts, axis=0)


def _gla_prepare_steps(qkb_s, vb_s, bc_s, attn_s, ob_s, pkv_s, qeb_s, tt, c):
    ri = _iota2(c, c, 0)
    ci = _iota2(c, c, 1)
    n_ch = tt // c

    def operands(ch, h):
        rows = slice(ch * c, (ch + 1) * c)
        q = qkb_s[rows, h * DKB:(h + 1) * DKB]
        k = qkb_s[rows, HB * DKB + h * DKB:HB * DKB + (h + 1) * DKB]
        return rows, q, k

    def diag():
        for ch in range(n_ch):
            for h in range(HB):
                _, q, k = operands(ch, h)
                attn_s[ch * HB + h] = jnp.where(ri == ci, jnp.sum(q * k, axis=-1, keepdims=True), 0.0)

    def level(half):
        valid = (((ri // (2 * half)) == (ci // (2 * half))) & ((ri % (2 * half)) >= half)
                 & ((ci % (2 * half)) < half))
        for ch in range(n_ch):
            bc_all = bc_s[ch * c:(ch + 1) * c, :]
            d_all = bc_all - _level_ref_rows(bc_all, half, c)
            upper_all = (_iota2(c, HB * DKB, 0) % (2 * half)) >= half
            e_all = jnp.exp(jnp.where(upper_all, d_all, -d_all))
            upper = upper_all[:, 0:DKB]
            for h in range(HB):
                _, q, k = operands(ch, h)
                e = e_all[:, h * DKB:(h + 1) * DKB]
                sc = _dot_nt(jnp.where(upper, q * e, 0.0), jnp.where(upper, 0.0, k * e))
                attn_s[ch * HB + h] += jnp.where(valid, sc, 0.0)

    def finish():
        for ch in range(n_ch):
            for h in range(HB):
                rows, q, k = operands(ch, h)
                v = vb_s[rows, h * DVB:(h + 1) * DVB]
                bc = bc_s[rows, h * DKB:(h + 1) * DKB]
                ob_s[rows, h * DVB:(h + 1) * DVB] = _dot(attn_s[ch * HB + h], v)
                pkv_s[ch * HB + h] = _dot_tn(k * jnp.exp(bc[c - 1:c, :] - bc), v)
                qeb_s[rows, h * DKB:(h + 1) * DKB] = q * jnp.exp(bc)

    steps = [diag]
    half = c // 2
    while half >= 1:
        steps.append(functools.partial(level, half))
        half //= 2
    steps.append(finish)
    return steps


def _mixer_body(x_ref, sa0_ref, sb0_ref, ca0_ref, nmix_ref, wca_ref, alog_ref, dt_ref, bgk_ref, ona_ref, onb_ref,
                w1_ref, w2_ref, wa_ref, wb_ref, wo_ref,
                x1_ref, sa_ref, ca_ref, sb_ref,
                cbuf, hb_s, post_s, qkv_s, beta_s, gcum_s, gt_s, qkb_s, vb_s, bc_s, oa_s, ob_s,
                m_s, p_s, pw_s, qk_s, rhs_s, uw_s, qe_s, kdec_s, attn_s, pkv_s, qeb_s, gated_s, mix_s, ktu_s, ktw_s, *, tt, c):
    t_idx = pl.program_id(1)
    n_ch = tt // c

    @pl.when(t_idx == 0)
    def _():
        sa_ref[...] = sa0_ref[...]
        sb_ref[...] = sb0_ref[...]
        cbuf[SUBLANES - (CONV_A - 1):SUBLANES, :] = ca0_ref[...]

    rb = min(tt, 64)
    for r in range(tt // rb):
        rr = slice(r * rb, (r + 1) * rb)
        hb_s[rr, :] = _rms(x_ref[rr, :], nmix_ref[...]).astype(BF16)
    hb = hb_s[...]
    cbuf[SUBLANES:SUBLANES + tt, :] = jnp.dot(hb, w1_ref[:, O_QKVA:O_QKVA + C_A], preferred_element_type=F32)
    base = SUBLANES - (CONV_A - 1)
    n_blk = C_A // LANES
    post_w = 2 * LANES
    n_post = W_POST // post_w
    for j in range(max(n_blk, n_post)):
        if j < n_post:
            pc = slice(j * post_w, (j + 1) * post_w)
            post_s[:, pc] = jnp.dot(hb, w1_ref[:, O_POST + j * post_w:O_POST + (j + 1) * post_w],
                                    preferred_element_type=F32)
        if j < n_blk:
            cols = slice(j * LANES, (j + 1) * LANES)
            acc = cbuf[base:base + tt, cols] * wca_ref[0:1, cols]
            for i in range(1, CONV_A):
                acc = acc + cbuf[base + i:base + i + tt, cols] * wca_ref[i:i + 1, cols]
            act = _silu(acc)
            if j < HA:
                act = _l2n(act) * (DKA ** -0.5)
            elif j < 2 * HA:
                act = _l2n(act)
            qkv_s[:, cols] = act
    tail = cbuf[tt + base:tt + SUBLANES, :]
    cbuf[base:SUBLANES, :] = tail
    ca_ref[...] = tail
    small = jnp.dot(hb, w1_ref[:, O_SMALL:O_SMALL + LANES], preferred_element_type=F32)
    beta, g = _gates(small, alog_ref[...], dt_ref[...])
    beta_s[...] = beta
    g_cum = _cumsum_chunks(g, c)
    gcum_s[...] = g_cum
    gt_s[...] = g_cum.T

    _delta_prepare(qkv_s, beta_s, gcum_s, gt_s, m_s, qk_s, rhs_s, qe_s, kdec_s, tt, c)
    for _ in _inverse_stages(m_s, p_s, pw_s, n_ch * HA, c):
        pass
    for ch in range(n_ch):
        rows = slice(ch * c, (ch + 1) * c)
        for h in range(HA):
            cols = slice(h * 2 * DVA, (h + 1) * 2 * DVA)
            uw_s[rows, cols] = _dot(p_s[ch * HA + h], rhs_s[rows, cols])
    for ch in range(n_ch):
        rows = slice(ch * c, (ch + 1) * c)
        for h in range(HA):
            n = ch * HA + h
            hs = slice(h * DKA, (h + 1) * DKA)
            uw = uw_s[rows, h * 2 * DVA:(h + 1) * 2 * DVA]
            att_uw = _dot(qk_s[n], uw)
            oa_s[rows, h * DVA:(h + 1) * DVA] = att_uw[:, 0:DVA]
            qe_s[rows, hs] = qe_s[rows, hs] - att_uw[:, DVA:]
            kt_uw = _dot_tn(kdec_s[rows, hs], uw)
            ktu_s[n] = kt_uw[:, 0:DVA]
            ktw_s[n] = kt_uw[:, DVA:]

    small_b = small.astype(BF16)
    for h in range(HB):
        hs = slice(h * DKB, (h + 1) * DKB)
        lg_pre = jnp.dot(small_b, w2_ref[:, hs], preferred_element_type=F32) + bgk_ref[:, hs]
        bc_s[:, hs] = _cumsum_chunks(jax.nn.log_sigmoid(lg_pre) / GLA_GATE_NORM, c)
    qkb_s[:, 0:HB * DKB] = jnp.dot(hb, w1_ref[:, O_QB:O_QB + HB * DKB], preferred_element_type=F32) * (DKB ** -0.5)
    qkb_s[:, HB * DKB:] = jnp.dot(hb, w1_ref[:, O_KB:O_KB + HB * DKB], preferred_element_type=F32)
    vb_s[...] = jnp.dot(hb, w1_ref[:, O_VB:O_VB + HB * DVB], preferred_element_type=F32)

    for step in _gla_prepare_steps(qkb_s, vb_s, bc_s, attn_s, ob_s, pkv_s, qeb_s, tt, c):
        step()

    for ch in range(n_ch):
        rows = slice(ch * c, (ch + 1) * c)
        last = slice(ch * c + c - 1, ch * c + c)
        for h in range(HA):
            n = ch * HA + h
            hs = slice(h * DKA, (h + 1) * DKA)
            s = sa_ref[h]
            prod = _dot(jnp.concatenate([ktw_s[n], qe_s[rows, hs]], axis=0), s)
            oa_s[rows, h * DVA:(h + 1) * DVA] += prod[DKA:]
            g_last = gcum_s[last, LANE_G + h:LANE_G + h + 1]
            sa_ref[h] = s * jnp.exp(g_last) + (ktu_s[n] - prod[0:DKA])

    for h in range(HA):
        hs = slice(h * DVA, (h + 1) * DVA)
        gated_s[:, hs] = (_rms(oa_s[:, hs], ona_ref[...]) * _silu(post_s[:, P_ZA + h * DVA:P_ZA + (h + 1) * DVA])
                          ).astype(BF16)

    for ch in range(n_ch):
        rows = slice(ch * c, (ch + 1) * c)
        last = slice(ch * c + c - 1, ch * c + c)
        for h in range(HB):
            s = sb_ref[h]
            ob_s[rows, h * DVB:(h + 1) * DVB] += _dot(qeb_s[rows, h * DKB:(h + 1) * DKB], s)
            e_col = jnp.exp(bc_s[last, h * DKB:(h + 1) * DKB]).T
            sb_ref[h] = s * e_col + pkv_s[ch * HB + h]

    for h in range(HB):
        hs = slice(h * DVB, (h + 1) * DVB)
        gated_s[:, HA * DVA + h * DVB:HA * DVA + (h + 1) * DVB] = (
            _rms(ob_s[:, hs], onb_ref[...]) * _silu(post_s[:, P_RB + h * DVB:P_RB + (h + 1) * DVB])).astype(BF16)
    out_w = 2 * LANES
    ga = gated_s[:, 0:HA * DVA]
    gb = gated_s[:, HA * DVA:]
    for n in range(D_MODEL // out_w):
        ns = slice(n * out_w, (n + 1) * out_w)
        y_a = jnp.dot(ga, wa_ref[:, ns], preferred_element_type=F32)
        y_b = jnp.dot(gb, wb_ref[:, ns], preferred_element_type=F32)
        mix_s[:, ns] = (jax.nn.sigmoid(post_s[:, P_GA + n * out_w:P_GA + (n + 1) * out_w]) * y_a
                        + jax.nn.sigmoid(post_s[:, P_GB + n * out_w:P_GB + (n + 1) * out_w]) * y_b).astype(BF16)
    mix = mix_s[...]
    for n in range(D_MODEL // out_w):
        ns = slice(n * out_w, (n + 1) * out_w)
        x1_ref[:, ns] = x_ref[:, ns] + jnp.dot(mix, wo_ref[:, ns], preferred_element_type=F32)


def _const_spec(shape):
    nd = len(shape)
    return pl.BlockSpec(shape, lambda *_: (0,) * nd, pipeline_mode=pl.Buffered(1))


def _mixer_call(x, sa0, sb0, ca0, p, tt):
    b, t, _ = x.shape
    c = CHUNK
    nt = t // tt
    n_inst = (tt // c) * HA
    row = lambda n: _const_spec((1, n))
    in_specs = [
        pl.BlockSpec((None, tt, D_MODEL), lambda i, j: (i, j, 0)),
        _const_spec((HA, DKA, DVA)), _const_spec((HB, DKB, DVB)), _const_spec((CONV_A - 1, C_A)),
        row(D_MODEL), _const_spec((CONV_A, C_A)), row(LANES), row(LANES), row(HB * DKB), row(DVA), row(DVB),
        _const_spec((D_MODEL, W_IN_COLS)), _const_spec((LANES, HB * DKB)),
        _const_spec((HA * DVA, D_MODEL)), _const_spec((HB * DVB, D_MODEL)), _const_spec((D_MODEL, D_MODEL)),
    ]
    out_specs = [
        pl.BlockSpec((None, tt, D_MODEL), lambda i, j: (i, j, 0)),
        pl.BlockSpec((None, HA, DKA, DVA), lambda i, j: (i, 0, 0, 0)),
        pl.BlockSpec((None, CONV_A - 1, C_A), lambda i, j: (i, 0, 0)),
        pl.BlockSpec((None, HB, DKB, DVB), lambda i, j: (i, 0, 0, 0)),
    ]
    out_shape = [
        jax.ShapeDtypeStruct((b, t, D_MODEL), F32),
        jax.ShapeDtypeStruct((b, HA, DKA, DVA), F32),
        jax.ShapeDtypeStruct((b, CONV_A - 1, C_A), F32),
        jax.ShapeDtypeStruct((b, HB, DKB, DVB), F32),
    ]
    mats = lambda: pltpu.VMEM((n_inst, c, c), F32)
    scratch = [
        pltpu.VMEM((tt + SUBLANES, C_A), F32),
        pltpu.VMEM((tt, D_MODEL), BF16),
        pltpu.VMEM((tt, W_POST), F32),
        pltpu.VMEM((tt, C_A), F32),
        pltpu.VMEM((tt, LANES), F32),
        pltpu.VMEM((tt, LANES), F32),
        pltpu.VMEM((LANES, tt), F32),
        pltpu.VMEM((tt, 2 * HB * DKB), F32),
        pltpu.VMEM((tt, HB * DVB), F32),
        pltpu.VMEM((tt, HB * DKB), F32),
        pltpu.VMEM((tt, HA * DVA), F32),
        pltpu.VMEM((tt, HB * DVB), F32),
        mats(), mats(), mats(), mats(),
        pltpu.VMEM((tt, HA * 2 * DVA), F32),
        pltpu.VMEM((tt, HA * 2 * DVA), F32),
        pltpu.VMEM((tt, HA * DKA), F32),
        pltpu.VMEM((tt, HA * DKA), F32),
        mats(),
        pltpu.VMEM((n_inst, DKB, DVB), F32),
        pltpu.VMEM((tt, HB * DKB), F32),
        pltpu.VMEM((tt, HA * DVA + HB * DVB), BF16),
        pltpu.VMEM((tt, D_MODEL), BF16),
        pltpu.VMEM((n_inst, DKA, DVA), F32),
        pltpu.VMEM((n_inst, DKA, DVA), F32),
    ]
    return pl.pallas_call(
        functools.partial(_mixer_body, tt=tt, c=c),
        grid=(b, nt), in_specs=in_specs, out_specs=out_specs, out_shape=out_shape, scratch_shapes=scratch,
        compiler_params=pltpu.CompilerParams(dimension_semantics=("arbitrary", "arbitrary"),
                                             vmem_limit_bytes=VMEM_LIMIT),
        name="mixer",
    )(x, sa0, sb0, ca0, p["norm_mix"], p["w_conv_a"], p["alog_row"], p["dt_row"], p["b_gk"], p["onorm_a"],
      p["onorm_b"], p["w1"], p["w2"], p["w_a_out"], p["w_b_out"], p["w_o"])


def _ffn_core(x1, u_conv, gf, wdn_ref, nfin):
    act = _gelu_tanh(u_conv) * gf
    x2 = x1 + _dot(act, wdn_ref[...])
    return _rms(x2, nfin)


def _ffn_body(x1_ref, cf0_ref, nffn_ref, wcf_ref, bcf_ref, nfin_ref, wup_ref, wdn_ref,
              y_ref, cf_ref, ubuf, *, tt):
    t_idx = pl.program_id(1)
    base = SUBLANES - (CONV_F - 1)

    @pl.when(t_idx == 0)
    def _():
        ubuf[base:SUBLANES, :] = cf0_ref[...]

    x1 = x1_ref[...]
    h2 = _rms(x1, nffn_ref[...]).astype(BF16)
    ubuf[SUBLANES:SUBLANES + tt, :] = jnp.dot(h2, wup_ref[:, 0:D_FF], preferred_element_type=F32)
    gf = jnp.dot(h2, wup_ref[:, D_FF:], preferred_element_type=F32)
    acc = ubuf[base:base + tt, :] * wcf_ref[0:1, :]
    for i in range(1, CONV_F):
        acc = acc + ubuf[base + i:base + i + tt, :] * wcf_ref[i:i + 1, :]
    acc = acc + bcf_ref[...]
    tail = ubuf[tt + base:tt + SUBLANES, :]
    ubuf[base:SUBLANES, :] = tail
    cf_ref[...] = tail
    y_ref[...] = _ffn_core(x1, acc, gf, wdn_ref, nfin_ref[...])


def _ffn_call(x1, cf0, p, tt):
    b, t, _ = x1.shape
    nt = t // tt
    row = lambda n: _const_spec((1, n))
    in_specs = [
        pl.BlockSpec((None, tt, D_MODEL), lambda i, j: (i, j, 0)),
        _const_spec((CONV_F - 1, D_FF)), row(D_MODEL), _const_spec((CONV_F, D_FF)), row(D_FF), row(D_MODEL),
        _const_spec((D_MODEL, 2 * D_FF)), _const_spec((D_FF, D_MODEL)),
    ]
    out_specs = [
        pl.BlockSpec((None, tt, D_MODEL), lambda i, j: (i, j, 0)),
        pl.BlockSpec((None, CONV_F - 1, D_FF), lambda i, j: (i, 0, 0)),
    ]
    out_shape = [jax.ShapeDtypeStruct((b, t, D_MODEL), F32), jax.ShapeDtypeStruct((b, CONV_F - 1, D_FF), F32)]
    return pl.pallas_call(
        functools.partial(_ffn_body, tt=tt),
        grid=(b, nt), in_specs=in_specs, out_specs=out_specs, out_shape=out_shape,
        scratch_shapes=[pltpu.VMEM((tt + SUBLANES, D_FF), F32)],
        compiler_params=pltpu.CompilerParams(dimension_semantics=("arbitrary", "arbitrary"),
                                             vmem_limit_bytes=VMEM_LIMIT),
        name="convffn",
    )(x1, cf0, p["norm_ffn"], p["w_conv_f"], p["b_conv_f"], p["norm_final"], p["w_ffn_in"], p["w_ffn_out"])


def _dec_head_body(x_ref, cs_ref, nmix_ref, wca_ref, alog_ref, dt_ref, bgk_ref, w1_ref, w2_ref,
                   qkva_ref, beta_ref, g_ref, qkvb_ref, lg_ref, post_ref, csn_ref):
    x = x_ref[...]
    hb = _rms(x, nmix_ref[...]).astype(BF16)
    pre = jnp.dot(hb, w1_ref[:, O_QKVA:O_QKVA + C_A], preferred_element_type=F32)
    acc = cs_ref[0] * wca_ref[0:1, :]
    for i in range(1, CONV_A - 1):
        acc = acc + cs_ref[i] * wca_ref[i:i + 1, :]
    acc = acc + pre * wca_ref[CONV_A - 1:CONV_A, :]
    for i in range(CONV_A - 2):
        csn_ref[i] = cs_ref[i + 1]
    csn_ref[CONV_A - 2] = pre
    qkva_ref[...] = _qkv_a_post(acc)
    small = jnp.dot(hb, w1_ref[:, O_SMALL:O_SMALL + LANES], preferred_element_type=F32)
    beta, g = _gates(small, alog_ref[...], dt_ref[...])
    beta_ref[...] = beta
    g_ref[...] = g
    lg_ref[...] = jax.nn.log_sigmoid(_dot(small, w2_ref[...]) + bgk_ref[...]) / GLA_GATE_NORM
    qkb = jnp.dot(hb, w1_ref[:, O_QB:O_QB + 2 * HB * DKB], preferred_element_type=F32)
    qkvb_ref[:, 0:HB * DKB] = qkb[:, 0:HB * DKB] * (DKB ** -0.5)
    qkvb_ref[:, HB * DKB:2 * HB * DKB] = qkb[:, HB * DKB:]
    qkvb_ref[:, 2 * HB * DKB:] = jnp.dot(hb, w1_ref[:, O_VB:O_VB + HB * DVB], preferred_element_type=F32)
    post_ref[...] = jnp.dot(hb, w1_ref[:, O_POST:O_POST + W_POST], preferred_element_type=F32)


def _dec_head_call(xs, cs, p):
    n = xs.shape[0]
    shapes = [(n, C_A), (n, LANES), (n, LANES), (n, 2 * HB * DKB + HB * DVB), (n, HB * DKB), (n, W_POST),
              (CONV_A - 1, n, C_A)]
    return pl.pallas_call(
        _dec_head_body,
        out_shape=[jax.ShapeDtypeStruct(s, F32) for s in shapes],
        compiler_params=pltpu.CompilerParams(vmem_limit_bytes=VMEM_LIMIT),
        name="decode_head",
    )(xs, cs, p["norm_mix"], p["w_conv_a"], p["alog_row"], p["dt_row"], p["b_gk"], p["w1"], p["w2"])


def _dec_rec_body(qkva_ref, beta_ref, g_ref, qkvb_ref, lg_ref, sa_ref, sb_ref,
                  oa_ref, ob_ref, san_ref, sbn_ref, *, tb):
    beta = beta_ref[...]
    eg = jnp.exp(g_ref[...])
    for h in range(HA):
        q_t = qkva_ref[:, h * DKA:(h + 1) * DKA].T
        k_t = qkva_ref[:, HA * DKA + h * DKA:HA * DKA + (h + 1) * DKA].T
        for j in range(tb):
            k_col = k_t[:, j:j + 1]
            s = sa_ref[j, h] * eg[j:j + 1, LANE_G + h:LANE_G + h + 1]
            v = qkva_ref[j:j + 1, 2 * HA * DKA + h * DVA:2 * HA * DKA + (h + 1) * DVA]
            err = (v - jnp.sum(k_col * s, axis=0, keepdims=True)) * beta[j:j + 1, LANE_BETA + h:LANE_BETA + h + 1]
            s = s + k_col * err
            san_ref[j, h] = s
            oa_ref[j:j + 1, h * DVA:(h + 1) * DVA] = jnp.sum(q_t[:, j:j + 1] * s, axis=0, keepdims=True)
    for h in range(HB):
        q_t = qkvb_ref[:, h * DKB:(h + 1) * DKB].T
        k_t = qkvb_ref[:, HB * DKB + h * DKB:HB * DKB + (h + 1) * DKB].T
        d_t = jnp.exp(lg_ref[:, h * DKB:(h + 1) * DKB]).T
        for j in range(tb):
            v = qkvb_ref[j:j + 1, 2 * HB * DKB + h * DVB:2 * HB * DKB + (h + 1) * DVB]
            s = sb_ref[j, h] * d_t[:, j:j + 1] + k_t[:, j:j + 1] * v
            sbn_ref[j, h] = s
            ob_ref[j:j + 1, h * DVB:(h + 1) * DVB] = jnp.sum(q_t[:, j:j + 1] * s, axis=0, keepdims=True)


def _dec_rec_call(qkva, beta, g, qkvb, lg, sa, sb, tb):
    n = qkva.shape[0]
    rows = lambda w: pl.BlockSpec((tb, w), lambda i: (i, 0))
    in_specs = [rows(C_A), rows(LANES), rows(LANES), rows(2 * HB * DKB + HB * DVB), rows(HB * DKB),
                pl.BlockSpec((tb, HA, DKA, DVA), lambda i: (i, 0, 0, 0)),
                pl.BlockSpec((tb, HB, DKB, DVB), lambda i: (i, 0, 0, 0))]
    out_specs = [rows(HA * DVA), rows(HB * DVB),
                 pl.BlockSpec((tb, HA, DKA, DVA), lambda i: (i, 0, 0, 0)),
                 pl.BlockSpec((tb, HB, DKB, DVB), lambda i: (i, 0, 0, 0))]
    out_shape = [jax.ShapeDtypeStruct((n, HA * DVA), F32), jax.ShapeDtypeStruct((n, HB * DVB), F32),
                 jax.ShapeDtypeStruct(sa.shape, F32), jax.ShapeDtypeStruct(sb.shape, F32)]
    return pl.pallas_call(
        functools.partial(_dec_rec_body, tb=tb),
        grid=(n // tb,), in_specs=in_specs, out_specs=out_specs, out_shape=out_shape,
        compiler_params=pltpu.CompilerParams(dimension_semantics=("arbitrary",), vmem_limit_bytes=VMEM_LIMIT),
        name="decode_recurrence",
    )(qkva, beta, g, qkvb, lg, sa, sb)


def _dec_tail_body(x_ref, oa_ref, ob_ref, post_ref, cf_ref, ona_ref, onb_ref, nffn_ref, wcf_ref, bcf_ref, nfin_ref,
                   wa_ref, wb_ref, wo_ref, wup_ref, wdn_ref, y_ref, cfn_ref):
    post = post_ref[...]
    y_a = _branch_a(oa_ref[...], post, ona_ref[...], wa_ref)
    y_b = _branch_b(ob_ref[...], post, onb_ref[...], wb_ref)
    x1 = _mix_residual(x_ref[...], y_a, y_b, post, wo_ref)
    h2 = _rms(x1, nffn_ref[...]).astype(BF16)
    u = jnp.dot(h2, wup_ref[:, 0:D_FF], preferred_element_type=F32)
    gf = jnp.dot(h2, wup_ref[:, D_FF:], preferred_element_type=F32)
    acc = cf_ref[:, 0:D_FF] * wcf_ref[0:1, :]
    for i in range(1, CONV_F - 1):
        acc = acc + cf_ref[:, i * D_FF:(i + 1) * D_FF] * wcf_ref[i:i + 1, :]
    acc = acc + u * wcf_ref[CONV_F - 1:CONV_F, :] + bcf_ref[...]
    for i in range(CONV_F - 2):
        cfn_ref[:, i * D_FF:(i + 1) * D_FF] = cf_ref[:, (i + 1) * D_FF:(i + 2) * D_FF]
    cfn_ref[:, (CONV_F - 2) * D_FF:] = u
    y_ref[...] = _ffn_core(x1, acc, gf, wdn_ref, nfin_ref[...])


def _dec_tail_call(xs, oa, ob, post, cf, p):
    n = xs.shape[0]
    return pl.pallas_call(
        _dec_tail_body,
        out_shape=[jax.ShapeDtypeStruct((n, D_MODEL), F32), jax.ShapeDtypeStruct((n, (CONV_F - 1) * D_FF), F32)],
        compiler_params=pltpu.CompilerParams(vmem_limit_bytes=VMEM_LIMIT),
        name="decode_tail",
    )(xs, oa, ob, post, cf, p["onorm_a"], p["onorm_b"], p["norm_ffn"], p["w_conv_f"], p["b_conv_f"],
      p["norm_final"], p["w_a_out"], p["w_b_out"], p["w_o"], p["w_ffn_in"], p["w_ffn_out"])


PROJ_SIZES = (HA * DKA, HA * DKA, HA * DVA, HA * DVA, HA, HA, HB * DKB, HB * DKB, HB * DVB, HB * DVB, GATE_RANK,
              D_MODEL, D_MODEL)
PROJ_ORDER = (0, 1, 2, 6, 7, 8, 4, 5, 10, None, 3, 9, 11, 12)
RELAYOUT_ROWS = 128


def _relayout_body(w_ref, o_ref):
    offs = [0]
    for s in PROJ_SIZES:
        offs.append(offs[-1] + s)
    dst = 0
    for seg in PROJ_ORDER:
        if seg is None:
            width = (-dst) % LANES
            o_ref[:, dst:dst + width] = jnp.zeros((o_ref.shape[0], width), BF16)
        else:
            width = PROJ_SIZES[seg]
            o_ref[:, dst:dst + width] = w_ref[:, offs[seg]:offs[seg] + width].astype(BF16)
        dst += width
    assert dst == W_IN_COLS


def _relayout_w_in(w):
    rows, cols = w.shape
    return pl.pallas_call(
        _relayout_body,
        grid=(rows // RELAYOUT_ROWS,),
        in_specs=[pl.BlockSpec((RELAYOUT_ROWS, cols), lambda i: (i, 0))],
        out_specs=pl.BlockSpec((RELAYOUT_ROWS, W_IN_COLS), lambda i: (i, 0)),
        out_shape=jax.ShapeDtypeStruct((rows, W_IN_COLS), BF16),
        compiler_params=pltpu.CompilerParams(dimension_semantics=("arbitrary",), vmem_limit_bytes=VMEM_LIMIT),
        name="relayout_w_in",
    )(w)


def _prep_params(l, norm_mix, w_in, w_conv_a, a_log, dt_bias, w_gk2, b_gk, onorm_a, onorm_b, w_a_out, w_b_out,
                 w_o, norm_ffn, w_ffn_in, w_conv_f, b_conv_f, w_ffn_out, norm_final):
    n_small = 2 * HA + GATE_RANK
    w1 = _relayout_w_in(w_in[l])
    w2 = jnp.zeros((LANES, HB * DKB), F32).at[2 * HA:n_small].set(w_gk2[l]).astype(BF16)
    lane_row = lambda v: jnp.zeros((1, LANES), F32).at[0, LANE_G:LANE_G + HA].set(v)
    return dict(
        norm_mix=norm_mix[l][None], w_conv_a=w_conv_a[l], alog_row=lane_row(a_log[l]), dt_row=lane_row(dt_bias[l]),
        b_gk=b_gk[l][None], onorm_a=onorm_a[l][None], onorm_b=onorm_b[l][None], w1=w1, w2=w2,
        w_a_out=w_a_out[l].astype(BF16), w_b_out=w_b_out[l].astype(BF16), w_o=w_o[l].astype(BF16),
        norm_ffn=norm_ffn[l][None], w_ffn_in=w_ffn_in[l].astype(BF16), w_conv_f=w_conv_f[l],
        b_conv_f=b_conv_f[l][None], w_ffn_out=w_ffn_out[l].astype(BF16), norm_final=norm_final[None])


PROMPT_TILE = 256
FFN_TILE = 512
DECODE_TILE = 8


def kernel(x_prompt, x_sample, state_delta, state_delta_conv, state_gla, state_ffn_conv, meta_tokens, norm_mix, w_in, w_conv_a, a_log, dt_bias, w_gk2, b_gk, onorm_a, onorm_b, w_a_out, w_b_out, w_o, norm_ffn, w_ffn_in, w_conv_f, b_conv_f, w_ffn_out, norm_final):
    assert w_in.shape[0] == 1, "single layer only"
    l = 0
    p = _prep_params(l, norm_mix, w_in, w_conv_a, a_log, dt_bias, w_gk2, b_gk, onorm_a, onorm_b, w_a_out, w_b_out,
                     w_o, norm_ffn, w_ffn_in, w_conv_f, b_conv_f, w_ffn_out, norm_final)
    n_dec = x_sample.shape[0]

    xm = jnp.concatenate([jnp.zeros((CHUNK - N_META, D_MODEL), F32), meta_tokens.astype(F32)], axis=0)[None]
    x1m, sa0, ca0, sb0 = _mixer_call(xm, jnp.zeros((HA, DKA, DVA), F32), jnp.zeros((HB, DKB, DVB), F32),
                                     jnp.zeros((CONV_A - 1, C_A), F32), p, CHUNK)
    _, cf0 = _ffn_call(x1m, jnp.zeros((CONV_F - 1, D_FF), F32), p, CHUNK)

    x1, sa_p, ca_p, sb_p = _mixer_call(x_prompt, sa0[0], sb0[0], ca0[0], p, PROMPT_TILE)
    y_prompt, cf_p = _ffn_call(x1, cf0[0], p, FFN_TILE)

    xs = x_sample.reshape(n_dec, D_MODEL)
    cs = jnp.transpose(state_delta_conv[l], (1, 0, 2))
    cfs = state_ffn_conv[l].reshape(n_dec, (CONV_F - 1) * D_FF)
    qkva, beta, g, qkvb, lg, post, cs_new = _dec_head_call(xs, cs, p)
    oa, ob, sa_s, sb_s = _dec_rec_call(qkva, beta, g, qkvb, lg, state_delta[l], state_gla[l], DECODE_TILE)
    y_s, cf_s = _dec_tail_call(xs, oa, ob, post, cfs, p)

    return (y_prompt, y_s.reshape(n_dec, 1, D_MODEL),
            sa_p[None], ca_p[None], sb_p[None], cf_p[None],
            sa_s[None], jnp.transpose(cs_new, (1, 0, 2))[None], sb_s[None],
            cf_s.reshape(1, n_dec, CONV_F - 1, D_FF))
```

```python
import functools

import jax
import jax.numpy as jnp
from jax import lax
from jax.experimental import pallas as pl
from jax.experimental.pallas import tpu as pltpu

F32 = jnp.float32
BF16 = jnp.bfloat16

D_MODEL = 1024
N_META = 16
CHUNK = 64
HA, DKA, DVA = 4, 128, 128
HB, DKB, DVB = 4, 128, 256
CONV_A = 4
C_A = 2 * HA * DKA + HA * DVA
GATE_RANK = 16
GLA_GATE_NORM = 16.0
D_FF = 2816
CONV_F = 3
EPS = 1e-6
LANES = 128
SUBLANES = 8

O_QKVA = 0
O_QB = O_QKVA + C_A
O_KB = O_QB + HB * DKB
O_VB = O_KB + HB * DKB
O_SMALL = O_VB + HB * DVB
O_POST = O_SMALL + LANES
P_ZA = 0
P_RB = P_ZA + HA * DVA
P_GA = P_RB + HB * DVB
P_GB = P_GA + D_MODEL
W_POST = P_GB + D_MODEL
W_IN_COLS = O_POST + W_POST
LANE_BETA = 0
LANE_G = HA

VMEM_LIMIT = 56 * 1024 * 1024


def _dot(a, b):
    return jnp.dot(a.astype(BF16), b.astype(BF16), preferred_element_type=F32)


def _dot_nt(a, b):
    return lax.dot_general(a.astype(BF16), b.astype(BF16), (((1,), (1,)), ((), ())), preferred_element_type=F32)


def _dot_tn(a, b):
    return lax.dot_general(a.astype(BF16), b.astype(BF16), (((0,), (0,)), ((), ())), preferred_element_type=F32)


def _rms(x, w):
    return x * lax.rsqrt(jnp.mean(x * x, axis=-1, keepdims=True) + EPS) * w


def _silu(x):
    return x * jax.nn.sigmoid(x)


def _gelu_tanh(x):
    return 0.5 * x * (1.0 + jnp.tanh(0.7978845608028654 * (x + 0.044715 * (x * x * x))))


def _l2n(x):
    return x * lax.rsqrt(jnp.sum(x * x, axis=-1, keepdims=True) + EPS)


def _iota2(n, m, axis):
    return lax.broadcasted_iota(jnp.int32, (n, m), axis)


def _gates(small, alog_row, dt_row):
    beta = jax.nn.sigmoid(small)
    g = -jnp.exp(alog_row) * jax.nn.softplus(small + dt_row)
    return beta, g


def _qkv_a_post(conv_out):
    act = _silu(conv_out)
    parts = []
    for h in range(HA):
        parts.append(_l2n(act[:, h * DKA:(h + 1) * DKA]) * (DKA ** -0.5))
    for h in range(HA):
        o = HA * DKA + h * DKA
        parts.append(_l2n(act[:, o:o + DKA]))
    parts.append(act[:, 2 * HA * DKA:])
    return jnp.concatenate(parts, axis=1)


def _branch_a(oa, post, onorm_a, wa_ref):
    pa = [_rms(oa[:, h * DVA:(h + 1) * DVA], onorm_a) for h in range(HA)]
    return _dot(jnp.concatenate(pa, axis=1) * _silu(post[:, P_ZA:P_ZA + HA * DVA]), wa_ref[...])


def _branch_b(ob, post, onorm_b, wb_ref):
    pb = [_rms(ob[:, h * DVB:(h + 1) * DVB], onorm_b) for h in range(HB)]
    return _dot(jnp.concatenate(pb, axis=1) * _silu(post[:, P_RB:P_RB + HB * DVB]), wb_ref[...])


def _mix_residual(x, y_a, y_b, post, wo_ref):
    mix = (jax.nn.sigmoid(post[:, P_GA:P_GA + D_MODEL]) * y_a
           + jax.nn.sigmoid(post[:, P_GB:P_GB + D_MODEL]) * y_b)
    return x + _dot(mix, wo_ref[...])


def _cumsum_chunks(x, c):
    rowi = _iota2(x.shape[0], x.shape[1], 0) % c
    sh = 1
    while sh < c:
        x = x + jnp.where(rowi >= sh, pltpu.roll(x, sh, axis=0), 0.0)
        sh *= 2
    return x


def _delta_prepare(qkv_s, beta_s, gcum_s, gt_s, m_s, qk_s, rhs_s, qe_s, kdec_s, tt, c):
    ri = _iota2(c, c, 0)
    ci = _iota2(c, c, 1)
    causal = (ri >= ci).astype(F32)
    strict = (ri > ci).astype(F32)
    for ch in range(tt // c):
        rows = slice(ch * c, (ch + 1) * c)
        g_blk = gcum_s[rows, :]
        beta_blk = beta_s[rows, :]
        eg_blk = jnp.exp(g_blk)
        ed_blk = jnp.exp(g_blk[c - 1:c, :] - g_blk)
        for h in range(HA):
            n = ch * HA + h
            hs = slice(h * DKA, (h + 1) * DKA)
            q = qkv_s[rows, h * DKA:(h + 1) * DKA]
            k = qkv_s[rows, HA * DKA + h * DKA:HA * DKA + (h + 1) * DKA]
            v = qkv_s[rows, 2 * HA * DKA + h * DVA:2 * HA * DKA + (h + 1) * DVA]
            g_col = g_blk[:, LANE_G + h:LANE_G + h + 1]
            g_row = gt_s[LANE_G + h:LANE_G + h + 1, rows]
            decay = jnp.exp(jnp.minimum(g_col - g_row, 0.0))
            beta_col = beta_blk[:, LANE_BETA + h:LANE_BETA + h + 1]
            eg = eg_blk[:, LANE_G + h:LANE_G + h + 1]
            kb = k * beta_col
            kq = _dot_nt(jnp.concatenate([kb, q], axis=0), k)
            m_s[n] = kq[0:c] * (decay * strict)
            qk_s[n] = kq[c:] * (decay * causal)
            rhs_s[rows, h * 2 * DVA:(h + 1) * 2 * DVA] = jnp.concatenate([v * beta_col, kb * eg], axis=1)
            qe_s[rows, hs] = q * eg
            kdec_s[rows, hs] = k * ed_blk[:, LANE_G + h:LANE_G + h + 1]


def _inverse_stages(m_s, p_s, pw_s, n_inst, c):
    ri = _iota2(c, c, 0)
    ci = _iota2(c, c, 1)
    eye = (ri == ci).astype(F32)
    base = min(16, c)
    same = (ri // base) == (ci // base)
    neg_same = jnp.where(same, -1.0, 0.0)
    for n in range(n_inst):
        a = m_s[n] * neg_same
        pw_s[n] = _dot(a, a)
        p_s[n] = eye + a
    k = 2
    while k < base:
        for n in range(n_inst):
            p = p_s[n]
            pw = pw_s[n]
            p_s[n] = p + _dot(p, pw)
            if 2 * k < base:
                pw_s[n] = _dot(pw, pw)
        k *= 2
    size = base
    while size < c:
        nxt = size * 2
        same_nxt = (ri // nxt) == (ci // nxt)
        off = (same_nxt & jnp.logical_not(same)).astype(F32)
        for n in range(n_inst):
            pw_s[n] = _dot(p_s[n], m_s[n] * off)
        for n in range(n_inst):
            p = p_s[n]
            p_s[n] = p - _dot(pw_s[n], p)
        same = same_nxt
        size = nxt


def _level_ref_rows(bc, half, c):
    w = bc.shape[1]
    parts = []
    if half >= SUBLANES // 2:
        for blk in range(c // (2 * half)):
            mrow = blk * 2 * half + half
            parts.append(jnp.broadcast_to(bc[mrow:mrow + 1, :], (2 * half, w)))
    else:
        sub = _iota2(SUBLANES, w, 0)
        for grp in range(c // SUBLANES):
            acc = None
            for blk in range(SUBLANES // (2 * half)):
                mrow = grp * SUBLANES + blk * 2 * half + half
                b = jnp.broadcast_to(bc[mrow:mrow + 1, :], (SUBLANES, w))
                acc = b if acc is None else jnp.where(sub >= blk * 2 * half, b, acc)
            parts.append(acc)
    return jnp.concatenate(parts, axis=0)


def _gla_prepare_steps(qkb_s, vb_s, bc_s, attn_s, ob_s, pkv_s, qeb_s, tt, c):
    ri = _iota2(c, c, 0)
    ci = _iota2(c, c, 1)
    n_ch = tt // c

    def operands(ch, h):
        rows = slice(ch * c, (ch + 1) * c)
        q = qkb_s[rows, h * DKB:(h + 1) * DKB]
        k = qkb_s[rows, HB * DKB + h * DKB:HB * DKB + (h + 1) * DKB]
        return rows, q, k

    def diag():
        for ch in range(n_ch):
            for h in range(HB):
                _, q, k = operands(ch, h)
                attn_s[ch * HB + h] = jnp.where(ri == ci, jnp.sum(q * k, axis=-1, keepdims=True), 0.0)

    def level(half):
        valid = (((ri // (2 * half)) == (ci // (2 * half))) & ((ri % (2 * half)) >= half)
                 & ((ci % (2 * half)) < half)).astype(F32)
        for ch in range(n_ch):
            bc_all = bc_s[ch * c:(ch + 1) * c, :]
            e_all = jnp.exp(-jnp.abs(bc_all - _level_ref_rows(bc_all, half, c)))
            for h in range(HB):
                _, q, k = operands(ch, h)
                e = e_all[:, h * DKB:(h + 1) * DKB]
                attn_s[ch * HB + h] += _dot_nt(q * e, k * e) * valid

    def finish():
        for ch in range(n_ch):
            for h in range(HB):
                rows, q, k = operands(ch, h)
                v = vb_s[rows, h * DVB:(h + 1) * DVB]
                bc = bc_s[rows, h * DKB:(h + 1) * DKB]
                ob_s[rows, h * DVB:(h + 1) * DVB] = _dot(attn_s[ch * HB + h], v)
                pkv_s[ch * HB + h] = _dot_tn(k * jnp.exp(bc[c - 1:c, :] - bc), v)
                qeb_s[rows, h * DKB:(h + 1) * DKB] = q * jnp.exp(bc)

    steps = [diag]
    half = c // 2
    while half >= 1:
        steps.append(functools.partial(level, half))
        half //= 2
    steps.append(finish)
    return steps


def _mixer_body(x_ref, sa0_ref, sb0_ref, ca0_ref, nmix_ref, wca_ref, alog_ref, dt_ref, bgk_ref, ona_ref, onb_ref,
                w1_ref, w2_ref, wa_ref, wb_ref, wo_ref,
                x1_ref, sa_ref, ca_ref, sb_ref,
                cbuf, hb_s, post_s, qkv_s, beta_s, gcum_s, gt_s, qkb_s, vb_s, bc_s, oa_s, ob_s,
                m_s, p_s, pw_s, qk_s, rhs_s, uw_s, qe_s, kdec_s, attn_s, pkv_s, qeb_s, gated_s, mix_s, ktu_s, ktw_s, *, tt, c):
    t_idx = pl.program_id(1)
    n_ch = tt // c

    @pl.when(t_idx == 0)
    def _():
        sa_ref[...] = sa0_ref[...]
        sb_ref[...] = sb0_ref[...]
        cbuf[SUBLANES - (CONV_A - 1):SUBLANES, :] = ca0_ref[...]

    rb = min(tt, 64)
    for r in range(tt // rb):
        rr = slice(r * rb, (r + 1) * rb)
        hb_s[rr, :] = _rms(x_ref[rr, :], nmix_ref[...]).astype(BF16)
    hb = hb_s[...]
    cbuf[SUBLANES:SUBLANES + tt, :] = jnp.dot(hb, w1_ref[:, O_QKVA:O_QKVA + C_A], preferred_element_type=F32)
    base = SUBLANES - (CONV_A - 1)
    n_blk = C_A // LANES
    post_w = 2 * LANES
    n_post = W_POST // post_w
    for j in range(max(n_blk, n_post)):
        if j < n_post:
            pc = slice(j * post_w, (j + 1) * post_w)
            post_s[:, pc] = jnp.dot(hb, w1_ref[:, O_POST + j * post_w:O_POST + (j + 1) * post_w],
                                    preferred_element_type=F32)
        if j < n_blk:
            cols = slice(j * LANES, (j + 1) * LANES)
            full = cbuf[:, cols]
            shifted = lambda k: full[SUBLANES:] if k == 0 else pltpu.roll(full, k, axis=0)[SUBLANES:]
            acc = shifted(CONV_A - 1) * wca_ref[0:1, cols]
            for i in range(1, CONV_A):
                acc = acc + shifted(CONV_A - 1 - i) * wca_ref[i:i + 1, cols]
            act = _silu(acc)
            if j < HA:
                act = _l2n(act) * (DKA ** -0.5)
            elif j < 2 * HA:
                act = _l2n(act)
            qkv_s[:, cols] = act
    tail = cbuf[tt + base:tt + SUBLANES, :]
    cbuf[base:SUBLANES, :] = tail
    ca_ref[...] = tail
    small = jnp.dot(hb, w1_ref[:, O_SMALL:O_SMALL + LANES], preferred_element_type=F32)
    beta, g = _gates(small, alog_ref[...], dt_ref[...])
    beta_s[...] = beta
    g_cum = _cumsum_chunks(g, c)
    gcum_s[...] = g_cum
    gt_s[...] = g_cum.T

    _delta_prepare(qkv_s, beta_s, gcum_s, gt_s, m_s, qk_s, rhs_s, qe_s, kdec_s, tt, c)
    _inverse_stages(m_s, p_s, pw_s, n_ch * HA, c)
    for ch in range(n_ch):
        rows = slice(ch * c, (ch + 1) * c)
        for h in range(HA):
            cols = slice(h * 2 * DVA, (h + 1) * 2 * DVA)
            uw_s[rows, cols] = _dot(p_s[ch * HA + h], rhs_s[rows, cols])
    for ch in range(n_ch):
        rows = slice(ch * c, (ch + 1) * c)
        for h in range(HA):
            n = ch * HA + h
            hs = slice(h * DKA, (h + 1) * DKA)
            uw = uw_s[rows, h * 2 * DVA:(h + 1) * 2 * DVA]
            att_uw = _dot(qk_s[n], uw)
            oa_s[rows, h * DVA:(h + 1) * DVA] = att_uw[:, 0:DVA]
            qe_s[rows, hs] = qe_s[rows, hs] - att_uw[:, DVA:]
            kt_uw = _dot_tn(kdec_s[rows, hs], uw)
            ktu_s[n] = kt_uw[:, 0:DVA]
            ktw_s[n] = kt_uw[:, DVA:]

    small_b = small.astype(BF16)
    for h in range(HB):
        hs = slice(h * DKB, (h + 1) * DKB)
        lg_pre = jnp.dot(small_b, w2_ref[:, hs], preferred_element_type=F32) + bgk_ref[:, hs]
        bc_s[:, hs] = _cumsum_chunks(jax.nn.log_sigmoid(lg_pre) / GLA_GATE_NORM, c)
    qkb_s[:, 0:HB * DKB] = jnp.dot(hb, w1_ref[:, O_QB:O_QB + HB * DKB], preferred_element_type=F32) * (DKB ** -0.5)
    qkb_s[:, HB * DKB:] = jnp.dot(hb, w1_ref[:, O_KB:O_KB + HB * DKB], preferred_element_type=F32)
    vb_s[...] = jnp.dot(hb, w1_ref[:, O_VB:O_VB + HB * DVB], preferred_element_type=F32)

    for step in _gla_prepare_steps(qkb_s, vb_s, bc_s, attn_s, ob_s, pkv_s, qeb_s, tt, c):
        step()

    for ch in range(n_ch):
        rows = slice(ch * c, (ch + 1) * c)
        last = slice(ch * c + c - 1, ch * c + c)
        for h in range(HA):
            n = ch * HA + h
            hs = slice(h * DKA, (h + 1) * DKA)
            s = sa_ref[h]
            prod = _dot(jnp.concatenate([ktw_s[n], qe_s[rows, hs]], axis=0), s)
            oa_s[rows, h * DVA:(h + 1) * DVA] += prod[DKA:]
            g_last = gcum_s[last, LANE_G + h:LANE_G + h + 1]
            sa_ref[h] = s * jnp.exp(g_last) + (ktu_s[n] - prod[0:DKA])

    for h in range(HA):
        hs = slice(h * DVA, (h + 1) * DVA)
        gated_s[:, hs] = (_rms(oa_s[:, hs], ona_ref[...]) * _silu(post_s[:, P_ZA + h * DVA:P_ZA + (h + 1) * DVA])
                          ).astype(BF16)

    for ch in range(n_ch):
        rows = slice(ch * c, (ch + 1) * c)
        last = slice(ch * c + c - 1, ch * c + c)
        for h in range(HB):
            s = sb_ref[h]
            ob_s[rows, h * DVB:(h + 1) * DVB] += _dot(qeb_s[rows, h * DKB:(h + 1) * DKB], s)
            e_col = jnp.exp(bc_s[last, h * DKB:(h + 1) * DKB]).T
            sb_ref[h] = s * e_col + pkv_s[ch * HB + h]

    for h in range(HB):
        hs = slice(h * DVB, (h + 1) * DVB)
        gated_s[:, HA * DVA + h * DVB:HA * DVA + (h + 1) * DVB] = (
            _rms(ob_s[:, hs], onb_ref[...]) * _silu(post_s[:, P_RB + h * DVB:P_RB + (h + 1) * DVB])).astype(BF16)
    out_w = 2 * LANES
    ga = gated_s[:, 0:HA * DVA]
    gb = gated_s[:, HA * DVA:]
    for n in range(D_MODEL // out_w):
        ns = slice(n * out_w, (n + 1) * out_w)
        y_a = jnp.dot(ga, wa_ref[:, ns], preferred_element_type=F32)
        y_b = jnp.dot(gb, wb_ref[:, ns], preferred_element_type=F32)
        mix_s[:, ns] = (jax.nn.sigmoid(post_s[:, P_GA + n * out_w:P_GA + (n + 1) * out_w]) * y_a
                        + jax.nn.sigmoid(post_s[:, P_GB + n * out_w:P_GB + (n + 1) * out_w]) * y_b).astype(BF16)
    mix = mix_s[...]
    for n in range(D_MODEL // out_w):
        ns = slice(n * out_w, (n + 1) * out_w)
        x1_ref[:, ns] = x_ref[:, ns] + jnp.dot(mix, wo_ref[:, ns], preferred_element_type=F32)


def _const_spec(shape):
    nd = len(shape)
    return pl.BlockSpec(shape, lambda *_: (0,) * nd, pipeline_mode=pl.Buffered(1))


def _mixer_call(x, sa0, sb0, ca0, p, tt):
    b, t, _ = x.shape
    c = CHUNK
    nt = t // tt
    n_inst = (tt // c) * HA
    row = lambda n: _const_spec((1, n))
    in_specs = [
        pl.BlockSpec((None, tt, D_MODEL), lambda i, j: (i, j, 0)),
        _const_spec((HA, DKA, DVA)), _const_spec((HB, DKB, DVB)), _const_spec((CONV_A - 1, C_A)),
        row(D_MODEL), _const_spec((CONV_A, C_A)), row(LANES), row(LANES), row(HB * DKB), row(DVA), row(DVB),
        _const_spec((D_MODEL, W_IN_COLS)), _const_spec((LANES, HB * DKB)),
        _const_spec((HA * DVA, D_MODEL)), _const_spec((HB * DVB, D_MODEL)), _const_spec((D_MODEL, D_MODEL)),
    ]
    out_specs = [
        pl.BlockSpec((None, tt, D_MODEL), lambda i, j: (i, j, 0)),
        pl.BlockSpec((None, HA, DKA, DVA), lambda i, j: (i, 0, 0, 0)),
        pl.BlockSpec((None, CONV_A - 1, C_A), lambda i, j: (i, 0, 0)),
        pl.BlockSpec((None, HB, DKB, DVB), lambda i, j: (i, 0, 0, 0)),
    ]
    out_shape = [
        jax.ShapeDtypeStruct((b, t, D_MODEL), F32),
        jax.ShapeDtypeStruct((b, HA, DKA, DVA), F32),
        jax.ShapeDtypeStruct((b, CONV_A - 1, C_A), F32),
        jax.ShapeDtypeStruct((b, HB, DKB, DVB), F32),
    ]
    mats = lambda: pltpu.VMEM((n_inst, c, c), F32)
    scratch = [
        pltpu.VMEM((tt + SUBLANES, C_A), F32),
        pltpu.VMEM((tt, D_MODEL), BF16),
        pltpu.VMEM((tt, W_POST), F32),
        pltpu.VMEM((tt, C_A), F32),
        pltpu.VMEM((tt, LANES), F32),
        pltpu.VMEM((tt, LANES), F32),
        pltpu.VMEM((LANES, tt), F32),
        pltpu.VMEM((tt, 2 * HB * DKB), F32),
        pltpu.VMEM((tt, HB * DVB), F32),
        pltpu.VMEM((tt, HB * DKB), F32),
        pltpu.VMEM((tt, HA * DVA), F32),
        pltpu.VMEM((tt, HB * DVB), F32),
        mats(), mats(), mats(), mats(),
        pltpu.VMEM((tt, HA * 2 * DVA), F32),
        pltpu.VMEM((tt, HA * 2 * DVA), F32),
        pltpu.VMEM((tt, HA * DKA), F32),
        pltpu.VMEM((tt, HA * DKA), F32),
        mats(),
        pltpu.VMEM((n_inst, DKB, DVB), F32),
        pltpu.VMEM((tt, HB * DKB), F32),
        pltpu.VMEM((tt, HA * DVA + HB * DVB), BF16),
        pltpu.VMEM((tt, D_MODEL), BF16),
        pltpu.VMEM((n_inst, DKA, DVA), F32),
        pltpu.VMEM((n_inst, DKA, DVA), F32),
    ]
    return pl.pallas_call(
        functools.partial(_mixer_body, tt=tt, c=c),
        grid=(b, nt), in_specs=in_specs, out_specs=out_specs, out_shape=out_shape, scratch_shapes=scratch,
        compiler_params=pltpu.CompilerParams(dimension_semantics=("arbitrary", "arbitrary"),
                                             vmem_limit_bytes=VMEM_LIMIT),
        name="mixer",
    )(x, sa0, sb0, ca0, p["norm_mix"], p["w_conv_a"], p["alog_row"], p["dt_row"], p["b_gk"], p["onorm_a"],
      p["onorm_b"], p["w1"], p["w2"], p["w_a_out"], p["w_b_out"], p["w_o"])


def _ffn_core(x1, u_conv, gf, wdn_ref, nfin):
    act = _gelu_tanh(u_conv) * gf
    x2 = x1 + _dot(act, wdn_ref[...])
    return _rms(x2, nfin)


def _ffn_body(x1_ref, cf0_ref, nffn_ref, wcf_ref, bcf_ref, nfin_ref, wup_ref, wdn_ref,
              y_ref, cf_ref, ubuf, *, tt):
    t_idx = pl.program_id(1)
    base = SUBLANES - (CONV_F - 1)

    @pl.when(t_idx == 0)
    def _():
        ubuf[base:SUBLANES, :] = cf0_ref[...]

    x1 = x1_ref[...]
    h2 = _rms(x1, nffn_ref[...]).astype(BF16)
    ubuf[SUBLANES:SUBLANES + tt, :] = jnp.dot(h2, wup_ref[:, 0:D_FF], preferred_element_type=F32)
    gf = jnp.dot(h2, wup_ref[:, D_FF:], preferred_element_type=F32)
    acc = ubuf[base:base + tt, :] * wcf_ref[0:1, :]
    for i in range(1, CONV_F):
        acc = acc + ubuf[base + i:base + i + tt, :] * wcf_ref[i:i + 1, :]
    acc = acc + bcf_ref[...]
    tail = ubuf[tt + base:tt + SUBLANES, :]
    ubuf[base:SUBLANES, :] = tail
    cf_ref[...] = tail
    y_ref[...] = _ffn_core(x1, acc, gf, wdn_ref, nfin_ref[...])


def _ffn_call(x1, cf0, p, tt):
    b, t, _ = x1.shape
    nt = t // tt
    row = lambda n: _const_spec((1, n))
    in_specs = [
        pl.BlockSpec((None, tt, D_MODEL), lambda i, j: (i, j, 0)),
        _const_spec((CONV_F - 1, D_FF)), row(D_MODEL), _const_spec((CONV_F, D_FF)), row(D_FF), row(D_MODEL),
        _const_spec((D_MODEL, 2 * D_FF)), _const_spec((D_FF, D_MODEL)),
    ]
    out_specs = [
        pl.BlockSpec((None, tt, D_MODEL), lambda i, j: (i, j, 0)),
        pl.BlockSpec((None, CONV_F - 1, D_FF), lambda i, j: (i, 0, 0)),
    ]
    out_shape = [jax.ShapeDtypeStruct((b, t, D_MODEL), F32), jax.ShapeDtypeStruct((b, CONV_F - 1, D_FF), F32)]
    return pl.pallas_call(
        functools.partial(_ffn_body, tt=tt),
        grid=(b, nt), in_specs=in_specs, out_specs=out_specs, out_shape=out_shape,
        scratch_shapes=[pltpu.VMEM((tt + SUBLANES, D_FF), F32)],
        compiler_params=pltpu.CompilerParams(dimension_semantics=("arbitrary", "arbitrary"),
                                             vmem_limit_bytes=VMEM_LIMIT),
        name="convffn",
    )(x1, cf0, p["norm_ffn"], p["w_conv_f"], p["b_conv_f"], p["norm_final"], p["w_ffn_in"], p["w_ffn_out"])


def _dec_head_body(x_ref, cs_ref, nmix_ref, wca_ref, alog_ref, dt_ref, bgk_ref, w1_ref, w2_ref,
                   qkva_ref, beta_ref, g_ref, qkvb_ref, lg_ref, post_ref, csn_ref):
    x = x_ref[...]
    hb = _rms(x, nmix_ref[...]).astype(BF16)
    pre = jnp.dot(hb, w1_ref[:, O_QKVA:O_QKVA + C_A], preferred_element_type=F32)
    acc = cs_ref[0] * wca_ref[0:1, :]
    for i in range(1, CONV_A - 1):
        acc = acc + cs_ref[i] * wca_ref[i:i + 1, :]
    acc = acc + pre * wca_ref[CONV_A - 1:CONV_A, :]
    for i in range(CONV_A - 2):
        csn_ref[i] = cs_ref[i + 1]
    csn_ref[CONV_A - 2] = pre
    qkva_ref[...] = _qkv_a_post(acc)
    small = jnp.dot(hb, w1_ref[:, O_SMALL:O_SMALL + LANES], preferred_element_type=F32)
    beta, g = _gates(small, alog_ref[...], dt_ref[...])
    beta_ref[...] = beta
    g_ref[...] = g
    lg_ref[...] = jax.nn.log_sigmoid(_dot(small, w2_ref[...]) + bgk_ref[...]) / GLA_GATE_NORM
    qkb = jnp.dot(hb, w1_ref[:, O_QB:O_QB + 2 * HB * DKB], preferred_element_type=F32)
    qkvb_ref[:, 0:HB * DKB] = qkb[:, 0:HB * DKB] * (DKB ** -0.5)
    qkvb_ref[:, HB * DKB:2 * HB * DKB] = qkb[:, HB * DKB:]
    qkvb_ref[:, 2 * HB * DKB:] = jnp.dot(hb, w1_ref[:, O_VB:O_VB + HB * DVB], preferred_element_type=F32)
    post_ref[...] = jnp.dot(hb, w1_ref[:, O_POST:O_POST + W_POST], preferred_element_type=F32)


def _dec_head_call(xs, cs, p):
    n = xs.shape[0]
    shapes = [(n, C_A), (n, LANES), (n, LANES), (n, 2 * HB * DKB + HB * DVB), (n, HB * DKB), (n, W_POST),
              (CONV_A - 1, n, C_A)]
    return pl.pallas_call(
        _dec_head_body,
        out_shape=[jax.ShapeDtypeStruct(s, F32) for s in shapes],
        compiler_params=pltpu.CompilerParams(vmem_limit_bytes=VMEM_LIMIT),
        name="decode_head",
    )(xs, cs, p["norm_mix"], p["w_conv_a"], p["alog_row"], p["dt_row"], p["b_gk"], p["w1"], p["w2"])


def _dec_rec_body(qkva_ref, beta_ref, g_ref, qkvb_ref, lg_ref, sa_ref, sb_ref,
                  oa_ref, ob_ref, san_ref, sbn_ref, *, tb):
    beta = beta_ref[...]
    eg = jnp.exp(g_ref[...])
    for h in range(HA):
        q_t = qkva_ref[:, h * DKA:(h + 1) * DKA].T
        k_t = qkva_ref[:, HA * DKA + h * DKA:HA * DKA + (h + 1) * DKA].T
        for j in range(tb):
            k_col = k_t[:, j:j + 1]
            s = sa_ref[j, h] * eg[j:j + 1, LANE_G + h:LANE_G + h + 1]
            v = qkva_ref[j:j + 1, 2 * HA * DKA + h * DVA:2 * HA * DKA + (h + 1) * DVA]
            err = (v - jnp.sum(k_col * s, axis=0, keepdims=True)) * beta[j:j + 1, LANE_BETA + h:LANE_BETA + h + 1]
            s = s + k_col * err
            san_ref[j, h] = s
            oa_ref[j:j + 1, h * DVA:(h + 1) * DVA] = jnp.sum(q_t[:, j:j + 1] * s, axis=0, keepdims=True)
    for h in range(HB):
        q_t = qkvb_ref[:, h * DKB:(h + 1) * DKB].T
        k_t = qkvb_ref[:, HB * DKB + h * DKB:HB * DKB + (h + 1) * DKB].T
        d_t = jnp.exp(lg_ref[:, h * DKB:(h + 1) * DKB]).T
        for j in range(tb):
            v = qkvb_ref[j:j + 1, 2 * HB * DKB + h * DVB:2 * HB * DKB + (h + 1) * DVB]
            s = sb_ref[j, h] * d_t[:, j:j + 1] + k_t[:, j:j + 1] * v
            sbn_ref[j, h] = s
            ob_ref[j:j + 1, h * DVB:(h + 1) * DVB] = jnp.sum(q_t[:, j:j + 1] * s, axis=0, keepdims=True)


def _dec_rec_call(qkva, beta, g, qkvb, lg, sa, sb, tb):
    n = qkva.shape[0]
    rows = lambda w: pl.BlockSpec((tb, w), lambda i: (i, 0))
    in_specs = [rows(C_A), rows(LANES), rows(LANES), rows(2 * HB * DKB + HB * DVB), rows(HB * DKB),
                pl.BlockSpec((tb, HA, DKA, DVA), lambda i: (i, 0, 0, 0)),
                pl.BlockSpec((tb, HB, DKB, DVB), lambda i: (i, 0, 0, 0))]
    out_specs = [rows(HA * DVA), rows(HB * DVB),
                 pl.BlockSpec((tb, HA, DKA, DVA), lambda i: (i, 0, 0, 0)),
                 pl.BlockSpec((tb, HB, DKB, DVB), lambda i: (i, 0, 0, 0))]
    out_shape = [jax.ShapeDtypeStruct((n, HA * DVA), F32), jax.ShapeDtypeStruct((n, HB * DVB), F32),
                 jax.ShapeDtypeStruct(sa.shape, F32), jax.ShapeDtypeStruct(sb.shape, F32)]
    return pl.pallas_call(
        functools.partial(_dec_rec_body, tb=tb),
        grid=(n // tb,), in_specs=in_specs, out_specs=out_specs, out_shape=out_shape,
        compiler_params=pltpu.CompilerParams(dimension_semantics=("arbitrary",), vmem_limit_bytes=VMEM_LIMIT),
        name="decode_recurrence",
    )(qkva, beta, g, qkvb, lg, sa, sb)


def _dec_tail_body(x_ref, oa_ref, ob_ref, post_ref, cf_ref, ona_ref, onb_ref, nffn_ref, wcf_ref, bcf_ref, nfin_ref,
                   wa_ref, wb_ref, wo_ref, wup_ref, wdn_ref, y_ref, cfn_ref):
    post = post_ref[...]
    y_a = _branch_a(oa_ref[...], post, ona_ref[...], wa_ref)
    y_b = _branch_b(ob_ref[...], post, onb_ref[...], wb_ref)
    x1 = _mix_residual(x_ref[...], y_a, y_b, post, wo_ref)
    h2 = _rms(x1, nffn_ref[...]).astype(BF16)
    u = jnp.dot(h2, wup_ref[:, 0:D_FF], preferred_element_type=F32)
    gf = jnp.dot(h2, wup_ref[:, D_FF:], preferred_element_type=F32)
    acc = cf_ref[:, 0:D_FF] * wcf_ref[0:1, :]
    for i in range(1, CONV_F - 1):
        acc = acc + cf_ref[:, i * D_FF:(i + 1) * D_FF] * wcf_ref[i:i + 1, :]
    acc = acc + u * wcf_ref[CONV_F - 1:CONV_F, :] + bcf_ref[...]
    for i in range(CONV_F - 2):
        cfn_ref[:, i * D_FF:(i + 1) * D_FF] = cf_ref[:, (i + 1) * D_FF:(i + 2) * D_FF]
    cfn_ref[:, (CONV_F - 2) * D_FF:] = u
    y_ref[...] = _ffn_core(x1, acc, gf, wdn_ref, nfin_ref[...])


def _dec_tail_call(xs, oa, ob, post, cf, p):
    n = xs.shape[0]
    return pl.pallas_call(
        _dec_tail_body,
        out_shape=[jax.ShapeDtypeStruct((n, D_MODEL), F32), jax.ShapeDtypeStruct((n, (CONV_F - 1) * D_FF), F32)],
        compiler_params=pltpu.CompilerParams(vmem_limit_bytes=VMEM_LIMIT),
        name="decode_tail",
    )(xs, oa, ob, post, cf, p["onorm_a"], p["onorm_b"], p["norm_ffn"], p["w_conv_f"], p["b_conv_f"],
      p["norm_final"], p["w_a_out"], p["w_b_out"], p["w_o"], p["w_ffn_in"], p["w_ffn_out"])


PROJ_SIZES = (HA * DKA, HA * DKA, HA * DVA, HA * DVA, HA, HA, HB * DKB, HB * DKB, HB * DVB, HB * DVB, GATE_RANK,
              D_MODEL, D_MODEL)
PROJ_ORDER = (0, 1, 2, 6, 7, 8, 4, 5, 10, None, 3, 9, 11, 12)
RELAYOUT_ROWS = 128


def _relayout_body(w_ref, o_ref):
    offs = [0]
    for s in PROJ_SIZES:
        offs.append(offs[-1] + s)
    dst = 0
    for seg in PROJ_ORDER:
        if seg is None:
            width = (-dst) % LANES
            o_ref[:, dst:dst + width] = jnp.zeros((o_ref.shape[0], width), BF16)
        else:
            width = PROJ_SIZES[seg]
            o_ref[:, dst:dst + width] = w_ref[:, offs[seg]:offs[seg] + width].astype(BF16)
        dst += width
    assert dst == W_IN_COLS


def _relayout_w_in(w):
    rows, cols = w.shape
    return pl.pallas_call(
        _relayout_body,
        grid=(rows // RELAYOUT_ROWS,),
        in_specs=[pl.BlockSpec((RELAYOUT_ROWS, cols), lambda i: (i, 0))],
        out_specs=pl.BlockSpec((RELAYOUT_ROWS, W_IN_COLS), lambda i: (i, 0)),
        out_shape=jax.ShapeDtypeStruct((rows, W_IN_COLS), BF16),
        compiler_params=pltpu.CompilerParams(dimension_semantics=("arbitrary",), vmem_limit_bytes=VMEM_LIMIT),
        name="relayout_w_in",
    )(w)


def _prep_params(l, norm_mix, w_in, w_conv_a, a_log, dt_bias, w_gk2, b_gk, onorm_a, onorm_b, w_a_out, w_b_out,
                 w_o, norm_ffn, w_ffn_in, w_conv_f, b_conv_f, w_ffn_out, norm_final):
    n_small = 2 * HA + GATE_RANK
    w1 = _relayout_w_in(w_in[l])
    w2 = jnp.zeros((LANES, HB * DKB), F32).at[2 * HA:n_small].set(w_gk2[l]).astype(BF16)
    lane_row = lambda v: jnp.zeros((1, LANES), F32).at[0, LANE_G:LANE_G + HA].set(v)
    return dict(
        norm_mix=norm_mix[l][None], w_conv_a=w_conv_a[l], alog_row=lane_row(a_log[l]), dt_row=lane_row(dt_bias[l]),
        b_gk=b_gk[l][None], onorm_a=onorm_a[l][None], onorm_b=onorm_b[l][None], w1=w1, w2=w2,
        w_a_out=w_a_out[l].astype(BF16), w_b_out=w_b_out[l].astype(BF16), w_o=w_o[l].astype(BF16),
        norm_ffn=norm_ffn[l][None], w_ffn_in=w_ffn_in[l].astype(BF16), w_conv_f=w_conv_f[l],
        b_conv_f=b_conv_f[l][None], w_ffn_out=w_ffn_out[l].astype(BF16), norm_final=norm_final[None])


PROMPT_TILE = 256
FFN_TILE = 512
DECODE_TILE = 8


def kernel(x_prompt, x_sample, state_delta, state_delta_conv, state_gla, state_ffn_conv, meta_tokens, norm_mix, w_in, w_conv_a, a_log, dt_bias, w_gk2, b_gk, onorm_a, onorm_b, w_a_out, w_b_out, w_o, norm_ffn, w_ffn_in, w_conv_f, b_conv_f, w_ffn_out, norm_final):
    assert w_in.shape[0] == 1, "single layer only"
    l = 0
    p = _prep_params(l, norm_mix, w_in, w_conv_a, a_log, dt_bias, w_gk2, b_gk, onorm_a, onorm_b, w_a_out, w_b_out,
                     w_o, norm_ffn, w_ffn_in, w_conv_f, b_conv_f, w_ffn_out, norm_final)
    n_dec = x_sample.shape[0]

    xm = jnp.concatenate([jnp.zeros((CHUNK - N_META, D_MODEL), F32), meta_tokens.astype(F32)], axis=0)[None]
    x1m, sa0, ca0, sb0 = _mixer_call(xm, jnp.zeros((HA, DKA, DVA), F32), jnp.zeros((HB, DKB, DVB), F32),
                                     jnp.zeros((CONV_A - 1, C_A), F32), p, CHUNK)
    _, cf0 = _ffn_call(x1m, jnp.zeros((CONV_F - 1, D_FF), F32), p, CHUNK)

    x1, sa_p, ca_p, sb_p = _mixer_call(x_prompt, sa0[0], sb0[0], ca0[0], p, PROMPT_TILE)
    y_prompt, cf_p = _ffn_call(x1, cf0[0], p, FFN_TILE)

    xs = x_sample.reshape(n_dec, D_MODEL)
    cs = jnp.transpose(state_delta_conv[l], (1, 0, 2))
    cfs = state_ffn_conv[l].reshape(n_dec, (CONV_F - 1) * D_FF)
    qkva, beta, g, qkvb, lg, post, cs_new = _dec_head_call(xs, cs, p)
    oa, ob, sa_s, sb_s = _dec_rec_call(qkva, beta, g, qkvb, lg, state_delta[l], state_gla[l], DECODE_TILE)
    y_s, cf_s = _dec_tail_call(xs, oa, ob, post, cfs, p)

    return (y_prompt, y_s.reshape(n_dec, 1, D_MODEL),
            sa_p[None], ca_p[None], sb_p[None], cf_p[None],
            sa_s[None], jnp.transpose(cs_new, (1, 0, 2))[None], sb_s[None],
            cf_s.reshape(1, n_dec, CONV_F - 1, D_FF))
```

```python
import functools

import jax
import jax.numpy as jnp
from jax import lax
from jax.experimental import pallas as pl
from jax.experimental.pallas import tpu as pltpu

F32 = jnp.float32
BF16 = jnp.bfloat16

D_MODEL = 1024
N_META = 16
CHUNK = 64
HA, DKA, DVA = 4, 128, 128
HB, DKB, DVB = 4, 128, 256
CONV_A = 4
C_A = 2 * HA * DKA + HA * DVA
GATE_RANK = 16
GLA_GATE_NORM = 16.0
D_FF = 2816
CONV_F = 3
EPS = 1e-6
LANES = 128
SUBLANES = 8

O_QKVA = 0
O_QB = O_QKVA + C_A
O_KB = O_QB + HB * DKB
O_VB = O_KB + HB * DKB
O_SMALL = O_VB + HB * DVB
O_POST = O_SMALL + LANES
P_ZA = 0
P_RB = P_ZA + HA * DVA
P_GA = P_RB + HB * DVB
P_GB = P_GA + D_MODEL
W_POST = P_GB + D_MODEL
W_IN_COLS = O_POST + W_POST
LANE_BETA = 0
LANE_G = HA

VMEM_LIMIT = 56 * 1024 * 1024


def _dot(a, b):
    return jnp.dot(a.astype(BF16), b.astype(BF16), preferred_element_type=F32)


def _dot_nt(a, b):
    return lax.dot_general(a.astype(BF16), b.astype(BF16), (((1,), (1,)), ((), ())), preferred_element_type=F32)


def _dot_tn(a, b):
    return lax.dot_general(a.astype(BF16), b.astype(BF16), (((0,), (0,)), ((), ())), preferred_element_type=F32)


def _rms(x, w):
    return x * lax.rsqrt(jnp.mean(x * x, axis=-1, keepdims=True) + EPS) * w


def _silu(x):
    return x * jax.nn.sigmoid(x)


def _gelu_tanh(x):
    return 0.5 * x * (1.0 + jnp.tanh(0.7978845608028654 * (x + 0.044715 * (x * x * x))))


def _l2n(x):
    return x * lax.rsqrt(jnp.sum(x * x, axis=-1, keepdims=True) + EPS)


def _iota2(n, m, axis):
    return lax.broadcasted_iota(jnp.int32, (n, m), axis)


def _gates(small, alog_row, dt_row):
    beta = jax.nn.sigmoid(small)
    g = -jnp.exp(alog_row) * jax.nn.softplus(small + dt_row)
    return beta, g


def _qkv_a_post(conv_out):
    act = _silu(conv_out)
    parts = []
    for h in range(HA):
        parts.append(_l2n(act[:, h * DKA:(h + 1) * DKA]) * (DKA ** -0.5))
    for h in range(HA):
        o = HA * DKA + h * DKA
        parts.append(_l2n(act[:, o:o + DKA]))
    parts.append(act[:, 2 * HA * DKA:])
    return jnp.concatenate(parts, axis=1)


def _branch_a(oa, post, onorm_a, wa_ref):
    pa = [_rms(oa[:, h * DVA:(h + 1) * DVA], onorm_a) for h in range(HA)]
    return _dot(jnp.concatenate(pa, axis=1) * _silu(post[:, P_ZA:P_ZA + HA * DVA]), wa_ref[...])


def _branch_b(ob, post, onorm_b, wb_ref):
    pb = [_rms(ob[:, h * DVB:(h + 1) * DVB], onorm_b) for h in range(HB)]
    return _dot(jnp.concatenate(pb, axis=1) * _silu(post[:, P_RB:P_RB + HB * DVB]), wb_ref[...])


def _mix_residual(x, y_a, y_b, post, wo_ref):
    mix = (jax.nn.sigmoid(post[:, P_GA:P_GA + D_MODEL]) * y_a
           + jax.nn.sigmoid(post[:, P_GB:P_GB + D_MODEL]) * y_b)
    return x + _dot(mix, wo_ref[...])


def _cumsum_chunks(x, c):
    rowi = _iota2(x.shape[0], x.shape[1], 0) % c
    sh = 1
    while sh < c:
        x = x + jnp.where(rowi >= sh, pltpu.roll(x, sh, axis=0), 0.0)
        sh *= 2
    return x


def _delta_prepare(qkv_s, beta_s, gcum_s, gt_s, m_s, qk_s, rhs_s, qe_s, kdec_s, tt, c):
    ri = _iota2(c, c, 0)
    ci = _iota2(c, c, 1)
    causal = (ri >= ci).astype(F32)
    strict = (ri > ci).astype(F32)
    for ch in range(tt // c):
        rows = slice(ch * c, (ch + 1) * c)
        g_blk = gcum_s[rows, :]
        beta_blk = beta_s[rows, :]
        eg_blk = jnp.exp(g_blk)
        ed_blk = jnp.exp(g_blk[c - 1:c, :] - g_blk)
        for h in range(HA):
            n = ch * HA + h
            hs = slice(h * DKA, (h + 1) * DKA)
            q = qkv_s[rows, h * DKA:(h + 1) * DKA]
            k = qkv_s[rows, HA * DKA + h * DKA:HA * DKA + (h + 1) * DKA]
            v = qkv_s[rows, 2 * HA * DKA + h * DVA:2 * HA * DKA + (h + 1) * DVA]
            g_col = g_blk[:, LANE_G + h:LANE_G + h + 1]
            g_row = gt_s[LANE_G + h:LANE_G + h + 1, rows]
            decay = jnp.exp(jnp.minimum(g_col - g_row, 0.0))
            beta_col = beta_blk[:, LANE_BETA + h:LANE_BETA + h + 1]
            eg = eg_blk[:, LANE_G + h:LANE_G + h + 1]
            kb = k * beta_col
            kq = _dot_nt(jnp.concatenate([kb, q], axis=0), k)
            m_s[n] = kq[0:c] * (decay * strict)
            qk_s[n] = kq[c:] * (decay * causal)
            rhs_s[rows, h * 2 * DVA:(h + 1) * 2 * DVA] = jnp.concatenate([v * beta_col, kb * eg], axis=1)
            qe_s[rows, hs] = q * eg
            kdec_s[rows, hs] = k * ed_blk[:, LANE_G + h:LANE_G + h + 1]


def _inverse_stages(m_s, p_s, pw_s, n_inst, c):
    ri = _iota2(c, c, 0)
    ci = _iota2(c, c, 1)
    eye = (ri == ci).astype(F32)
    base = min(16, c)
    same = (ri // base) == (ci // base)
    neg_same = jnp.where(same, -1.0, 0.0)
    for n in range(n_inst):
        a = m_s[n] * neg_same
        pw_s[n] = _dot(a, a)
        p_s[n] = eye + a
    k = 2
    while k < base:
        for n in range(n_inst):
            p = p_s[n]
            pw = pw_s[n]
            p_s[n] = p + _dot(p, pw)
            if 2 * k < base:
                pw_s[n] = _dot(pw, pw)
        k *= 2
    size = base
    while size < c:
        nxt = size * 2
        same_nxt = (ri // nxt) == (ci // nxt)
        off = (same_nxt & jnp.logical_not(same)).astype(F32)
        for n in range(n_inst):
            pw_s[n] = _dot(p_s[n], m_s[n] * off)
        for n in range(n_inst):
            p = p_s[n]
            p_s[n] = p - _dot(pw_s[n], p)
        same = same_nxt
        size = nxt


def _level_ref_rows(bc, half, c):
    w = bc.shape[1]
    parts = []
    if half >= SUBLANES // 2:
        for blk in range(c // (2 * half)):
            mrow = blk * 2 * half + half
            parts.append(jnp.broadcast_to(bc[mrow:mrow + 1, :], (2 * half, w)))
    else:
        sub = _iota2(SUBLANES, w, 0)
        for grp in range(c // SUBLANES):
            acc = None
            for blk in range(SUBLANES // (2 * half)):
                mrow = grp * SUBLANES + blk * 2 * half + half
                b = jnp.broadcast_to(bc[mrow:mrow + 1, :], (SUBLANES, w))
                acc = b if acc is None else jnp.where(sub >= blk * 2 * half, b, acc)
            parts.append(acc)
    return jnp.concatenate(parts, axis=0)


def _gla_prepare_steps(qkb_s, vb_s, bc_s, attn_s, ob_s, pkv_s, qeb_s, tt, c):
    ri = _iota2(c, c, 0)
    ci = _iota2(c, c, 1)
    n_ch = tt // c

    def operands(ch, h):
        rows = slice(ch * c, (ch + 1) * c)
        q = qkb_s[rows, h * DKB:(h + 1) * DKB]
        k = qkb_s[rows, HB * DKB + h * DKB:HB * DKB + (h + 1) * DKB]
        return rows, q, k

    def diag():
        for ch in range(n_ch):
            for h in range(HB):
                _, q, k = operands(ch, h)
                attn_s[ch * HB + h] = jnp.where(ri == ci, jnp.sum(q * k, axis=-1, keepdims=True), 0.0)

    def level(half):
        valid = (((ri // (2 * half)) == (ci // (2 * half))) & ((ri % (2 * half)) >= half)
                 & ((ci % (2 * half)) < half)).astype(F32)
        for ch in range(n_ch):
            bc_all = bc_s[ch * c:(ch + 1) * c, :]
            e_all = jnp.exp(-jnp.abs(bc_all - _level_ref_rows(bc_all, half, c)))
            for h in range(HB):
                _, q, k = operands(ch, h)
                e = e_all[:, h * DKB:(h + 1) * DKB]
                attn_s[ch * HB + h] += _dot_nt(q * e, k * e) * valid

    def finish():
        for ch in range(n_ch):
            for h in range(HB):
                rows, q, k = operands(ch, h)
                v = vb_s[rows, h * DVB:(h + 1) * DVB]
                bc = bc_s[rows, h * DKB:(h + 1) * DKB]
                ob_s[rows, h * DVB:(h + 1) * DVB] = _dot(attn_s[ch * HB + h], v)
                pkv_s[ch * HB + h] = _dot_tn(k * jnp.exp(bc[c - 1:c, :] - bc), v)
                qeb_s[rows, h * DKB:(h + 1) * DKB] = q * jnp.exp(bc)

    steps = [diag]
    half = c // 2
    while half >= 1:
        steps.append(functools.partial(level, half))
        half //= 2
    steps.append(finish)
    return steps


def _mixer_body(x_ref, sa0_ref, sb0_ref, ca0_ref, nmix_ref, wca_ref, alog_ref, dt_ref, bgk_ref, ona_ref, onb_ref,
                w1_ref, w2_ref, wa_ref, wb_ref, wo_ref,
                x1_ref, sa_ref, ca_ref, sb_ref,
                cbuf, hb_s, post_s, qkv_s, beta_s, gcum_s, gt_s, qkb_s, vb_s, bc_s, oa_s, ob_s,
                m_s, p_s, pw_s, qk_s, rhs_s, uw_s, qe_s, kdec_s, attn_s, pkv_s, qeb_s, gated_s, mix_s, ktu_s, ktw_s, *, tt, c):
    t_idx = pl.program_id(1)
    n_ch = tt // c

    @pl.when(t_idx == 0)
    def _():
        sa_ref[...] = sa0_ref[...]
        sb_ref[...] = sb0_ref[...]
        cbuf[SUBLANES - (CONV_A - 1):SUBLANES, :] = ca0_ref[...]

    rb = min(tt, 64)
    for r in range(tt // rb):
        rr = slice(r * rb, (r + 1) * rb)
        hb_s[rr, :] = _rms(x_ref[rr, :], nmix_ref[...]).astype(BF16)
    hb = hb_s[...]
    cbuf[SUBLANES:SUBLANES + tt, :] = jnp.dot(hb, w1_ref[:, O_QKVA:O_QKVA + C_A], preferred_element_type=F32)
    base = SUBLANES - (CONV_A - 1)
    n_blk = C_A // LANES
    post_w = 2 * LANES
    n_post = W_POST // post_w
    for j in range(max(n_blk, n_post)):
        if j < n_post:
            pc = slice(j * post_w, (j + 1) * post_w)
            post_s[:, pc] = jnp.dot(hb, w1_ref[:, O_POST + j * post_w:O_POST + (j + 1) * post_w],
                                    preferred_element_type=F32)
        if j < n_blk:
            cols = slice(j * LANES, (j + 1) * LANES)
            full = cbuf[:, cols]
            shifted = lambda k: full[SUBLANES:] if k == 0 else pltpu.roll(full, k, axis=0)[SUBLANES:]
            acc = shifted(CONV_A - 1) * wca_ref[0:1, cols]
            for i in range(1, CONV_A):
                acc = acc + shifted(CONV_A - 1 - i) * wca_ref[i:i + 1, cols]
            act = _silu(acc)
            if j < HA:
                act = _l2n(act) * (DKA ** -0.5)
            elif j < 2 * HA:
                act = _l2n(act)
            qkv_s[:, cols] = act
    tail = cbuf[tt + base:tt + SUBLANES, :]
    cbuf[base:SUBLANES, :] = tail
    ca_ref[...] = tail
    small = jnp.dot(hb, w1_ref[:, O_SMALL:O_SMALL + LANES], preferred_element_type=F32)
    beta, g = _gates(small, alog_ref[...], dt_ref[...])
    beta_s[...] = beta
    g_cum = _cumsum_chunks(g, c)
    gcum_s[...] = g_cum
    gt_s[...] = g_cum.T

    _delta_prepare(qkv_s, beta_s, gcum_s, gt_s, m_s, qk_s, rhs_s, qe_s, kdec_s, tt, c)
    _inverse_stages(m_s, p_s, pw_s, n_ch * HA, c)
    for ch in range(n_ch):
        rows = slice(ch * c, (ch + 1) * c)
        for h in range(HA):
            cols = slice(h * 2 * DVA, (h + 1) * 2 * DVA)
            uw_s[rows, cols] = _dot(p_s[ch * HA + h], rhs_s[rows, cols])
    for ch in range(n_ch):
        rows = slice(ch * c, (ch + 1) * c)
        for h in range(HA):
            n = ch * HA + h
            hs = slice(h * DKA, (h + 1) * DKA)
            uw = uw_s[rows, h * 2 * DVA:(h + 1) * 2 * DVA]
            att_uw = _dot(qk_s[n], uw)
            oa_s[rows, h * DVA:(h + 1) * DVA] = att_uw[:, 0:DVA]
            qe_s[rows, hs] = qe_s[rows, hs] - att_uw[:, DVA:]
            kt_uw = _dot_tn(kdec_s[rows, hs], uw)
            ktu_s[n] = kt_uw[:, 0:DVA]
            ktw_s[n] = kt_uw[:, DVA:]

    small_b = small.astype(BF16)
    for h in range(HB):
        hs = slice(h * DKB, (h + 1) * DKB)
        lg_pre = jnp.dot(small_b, w2_ref[:, hs], preferred_element_type=F32) + bgk_ref[:, hs]
        bc_s[:, hs] = _cumsum_chunks(jax.nn.log_sigmoid(lg_pre) / GLA_GATE_NORM, c)
    qkb_s[:, 0:HB * DKB] = jnp.dot(hb, w1_ref[:, O_QB:O_QB + HB * DKB], preferred_element_type=F32) * (DKB ** -0.5)
    qkb_s[:, HB * DKB:] = jnp.dot(hb, w1_ref[:, O_KB:O_KB + HB * DKB], preferred_element_type=F32)
    vb_s[...] = jnp.dot(hb, w1_ref[:, O_VB:O_VB + HB * DVB], preferred_element_type=F32)

    for step in _gla_prepare_steps(qkb_s, vb_s, bc_s, attn_s, ob_s, pkv_s, qeb_s, tt, c):
        step()

    for ch in range(n_ch):
        rows = slice(ch * c, (ch + 1) * c)
        last = slice(ch * c + c - 1, ch * c + c)
        for h in range(HA):
            n = ch * HA + h
            hs = slice(h * DKA, (h + 1) * DKA)
            s = sa_ref[h]
            prod = _dot(jnp.concatenate([ktw_s[n], qe_s[rows, hs]], axis=0), s)
            oa_s[rows, h * DVA:(h + 1) * DVA] += prod[DKA:]
            g_last = gcum_s[last, LANE_G + h:LANE_G + h + 1]
            sa_ref[h] = s * jnp.exp(g_last) + (ktu_s[n] - prod[0:DKA])

    for h in range(HA):
        hs = slice(h * DVA, (h + 1) * DVA)
        gated_s[:, hs] = (_rms(oa_s[:, hs], ona_ref[...]) * _silu(post_s[:, P_ZA + h * DVA:P_ZA + (h + 1) * DVA])
                          ).astype(BF16)

    for ch in range(n_ch):
        rows = slice(ch * c, (ch + 1) * c)
        last = slice(ch * c + c - 1, ch * c + c)
        for h in range(HB):
            s = sb_ref[h]
            ob_s[rows, h * DVB:(h + 1) * DVB] += _dot(qeb_s[rows, h * DKB:(h + 1) * DKB], s)
            e_col = jnp.exp(bc_s[last, h * DKB:(h + 1) * DKB]).T
            sb_ref[h] = s * e_col + pkv_s[ch * HB + h]

    for h in range(HB):
        hs = slice(h * DVB, (h + 1) * DVB)
        gated_s[:, HA * DVA + h * DVB:HA * DVA + (h + 1) * DVB] = (
            _rms(ob_s[:, hs], onb_ref[...]) * _silu(post_s[:, P_RB + h * DVB:P_RB + (h + 1) * DVB])).astype(BF16)
    out_w = 2 * LANES
    ga = gated_s[:, 0:HA * DVA]
    gb = gated_s[:, HA * DVA:]
    for n in range(D_MODEL // out_w):
        ns = slice(n * out_w, (n + 1) * out_w)
        y_a = jnp.dot(ga, wa_ref[:, ns], preferred_element_type=F32)
        y_b = jnp.dot(gb, wb_ref[:, ns], preferred_element_type=F32)
        mix_s[:, ns] = (jax.nn.sigmoid(post_s[:, P_GA + n * out_w:P_GA + (n + 1) * out_w]) * y_a
                        + jax.nn.sigmoid(post_s[:, P_GB + n * out_w:P_GB + (n + 1) * out_w]) * y_b).astype(BF16)
    mix = mix_s[...]
    for n in range(D_MODEL // out_w):
        ns = slice(n * out_w, (n + 1) * out_w)
        x1_ref[:, ns] = x_ref[:, ns] + jnp.dot(mix, wo_ref[:, ns], preferred_element_type=F32)


def _const_spec(shape):
    nd = len(shape)
    return pl.BlockSpec(shape, lambda *_: (0,) * nd, pipeline_mode=pl.Buffered(1))


def _mixer_call(x, sa0, sb0, ca0, p, tt):
    b, t, _ = x.shape
    c = CHUNK
    nt = t // tt
    n_inst = (tt // c) * HA
    row = lambda n: _const_spec((1, n))
    in_specs = [
        pl.BlockSpec((None, tt, D_MODEL), lambda i, j: (i, j, 0)),
        _const_spec((HA, DKA, DVA)), _const_spec((HB, DKB, DVB)), _const_spec((CONV_A - 1, C_A)),
        row(D_MODEL), _const_spec((CONV_A, C_A)), row(LANES), row(LANES), row(HB * DKB), row(DVA), row(DVB),
        _const_spec((D_MODEL, W_IN_COLS)), _const_spec((LANES, HB * DKB)),
        _const_spec((HA * DVA, D_MODEL)), _const_spec((HB * DVB, D_MODEL)), _const_spec((D_MODEL, D_MODEL)),
    ]
    out_specs = [
        pl.BlockSpec((None, tt, D_MODEL), lambda i, j: (i, j, 0)),
        pl.BlockSpec((None, HA, DKA, DVA), lambda i, j: (i, 0, 0, 0)),
        pl.BlockSpec((None, CONV_A - 1, C_A), lambda i, j: (i, 0, 0)),
        pl.BlockSpec((None, HB, DKB, DVB), lambda i, j: (i, 0, 0, 0)),
    ]
    out_shape = [
        jax.ShapeDtypeStruct((b, t, D_MODEL), F32),
        jax.ShapeDtypeStruct((b, HA, DKA, DVA), F32),
        jax.ShapeDtypeStruct((b, CONV_A - 1, C_A), F32),
        jax.ShapeDtypeStruct((b, HB, DKB, DVB), F32),
    ]
    mats = lambda: pltpu.VMEM((n_inst, c, c), F32)
    scratch = [
        pltpu.VMEM((tt + SUBLANES, C_A), F32),
        pltpu.VMEM((tt, D_MODEL), BF16),
        pltpu.VMEM((tt, W_POST), F32),
        pltpu.VMEM((tt, C_A), F32),
        pltpu.VMEM((tt, LANES), F32),
        pltpu.VMEM((tt, LANES), F32),
        pltpu.VMEM((LANES, tt), F32),
        pltpu.VMEM((tt, 2 * HB * DKB), F32),
        pltpu.VMEM((tt, HB * DVB), F32),
        pltpu.VMEM((tt, HB * DKB), F32),
        pltpu.VMEM((tt, HA * DVA), F32),
        pltpu.VMEM((tt, HB * DVB), F32),
        mats(), mats(), mats(), mats(),
        pltpu.VMEM((tt, HA * 2 * DVA), F32),
        pltpu.VMEM((tt, HA * 2 * DVA), F32),
        pltpu.VMEM((tt, HA * DKA), F32),
        pltpu.VMEM((tt, HA * DKA), F32),
        mats(),
        pltpu.VMEM((n_inst, DKB, DVB), F32),
        pltpu.VMEM((tt, HB * DKB), F32),
        pltpu.VMEM((tt, HA * DVA + HB * DVB), BF16),
        pltpu.VMEM((tt, D_MODEL), BF16),
        pltpu.VMEM((n_inst, DKA, DVA), F32),
        pltpu.VMEM((n_inst, DKA, DVA), F32),
    ]
    return pl.pallas_call(
        functools.partial(_mixer_body, tt=tt, c=c),
        grid=(b, nt), in_specs=in_specs, out_specs=out_specs, out_shape=out_shape, scratch_shapes=scratch,
        compiler_params=pltpu.CompilerParams(dimension_semantics=("arbitrary", "arbitrary"),
                                             vmem_limit_bytes=VMEM_LIMIT),
        name="mixer",
    )(x, sa0, sb0, ca0, p["norm_mix"], p["w_conv_a"], p["alog_row"], p["dt_row"], p["b_gk"], p["onorm_a"],
      p["onorm_b"], p["w1"], p["w2"], p["w_a_out"], p["w_b_out"], p["w_o"])


def _ffn_core(x1, u_conv, gf, wdn_ref, nfin):
    act = _gelu_tanh(u_conv) * gf
    x2 = x1 + _dot(act, wdn_ref[...])
    return _rms(x2, nfin)


def _ffn_body(x1_ref, cf0_ref, nffn_ref, wcf_ref, bcf_ref, nfin_ref, wup_ref, wdn_ref,
              y_ref, cf_ref, ubuf, *, tt):
    t_idx = pl.program_id(1)
    base = SUBLANES - (CONV_F - 1)

    @pl.when(t_idx == 0)
    def _():
        ubuf[base:SUBLANES, :] = cf0_ref[...]

    x1 = x1_ref[...]
    h2 = _rms(x1, nffn_ref[...]).astype(BF16)
    ubuf[SUBLANES:SUBLANES + tt, :] = jnp.dot(h2, wup_ref[:, 0:D_FF], preferred_element_type=F32)
    gf = jnp.dot(h2, wup_ref[:, D_FF:], preferred_element_type=F32)
    acc = ubuf[base:base + tt, :] * wcf_ref[0:1, :]
    for i in range(1, CONV_F):
        acc = acc + ubuf[base + i:base + i + tt, :] * wcf_ref[i:i + 1, :]
    acc = acc + bcf_ref[...]
    tail = ubuf[tt + base:tt + SUBLANES, :]
    ubuf[base:SUBLANES, :] = tail
    cf_ref[...] = tail
    y_ref[...] = _ffn_core(x1, acc, gf, wdn_ref, nfin_ref[...])


def _ffn_call(x1, cf0, p, tt):
    b, t, _ = x1.shape
    nt = t // tt
    row = lambda n: _const_spec((1, n))
    in_specs = [
        pl.BlockSpec((None, tt, D_MODEL), lambda i, j: (i, j, 0)),
        _const_spec((CONV_F - 1, D_FF)), row(D_MODEL), _const_spec((CONV_F, D_FF)), row(D_FF), row(D_MODEL),
        _const_spec((D_MODEL, 2 * D_FF)), _const_spec((D_FF, D_MODEL)),
    ]
    out_specs = [
        pl.BlockSpec((None, tt, D_MODEL), lambda i, j: (i, j, 0)),
        pl.BlockSpec((None, CONV_F - 1, D_FF), lambda i, j: (i, 0, 0)),
    ]
    out_shape = [jax.ShapeDtypeStruct((b, t, D_MODEL), F32), jax.ShapeDtypeStruct((b, CONV_F - 1, D_FF), F32)]
    return pl.pallas_call(
        functools.partial(_ffn_body, tt=tt),
        grid=(b, nt), in_specs=in_specs, out_specs=out_specs, out_shape=out_shape,
        scratch_shapes=[pltpu.VMEM((tt + SUBLANES, D_FF), F32)],
        compiler_params=pltpu.CompilerParams(dimension_semantics=("arbitrary", "arbitrary"),
                                             vmem_limit_bytes=VMEM_LIMIT),
        name="convffn",
    )(x1, cf0, p["norm_ffn"], p["w_conv_f"], p["b_conv_f"], p["norm_final"], p["w_ffn_in"], p["w_ffn_out"])


def _dec_head_body(x_ref, cs_ref, nmix_ref, wca_ref, alog_ref, dt_ref, bgk_ref, w1_ref, w2_ref,
                   qkva_ref, beta_ref, g_ref, qkvb_ref, lg_ref, post_ref, csn_ref):
    x = x_ref[...]
    hb = _rms(x, nmix_ref[...]).astype(BF16)
    pre = jnp.dot(hb, w1_ref[:, O_QKVA:O_QKVA + C_A], preferred_element_type=F32)
    acc = cs_ref[0] * wca_ref[0:1, :]
    for i in range(1, CONV_A - 1):
        acc = acc + cs_ref[i] * wca_ref[i:i + 1, :]
    acc = acc + pre * wca_ref[CONV_A - 1:CONV_A, :]
    for i in range(CONV_A - 2):
        csn_ref[i] = cs_ref[i + 1]
    csn_ref[CONV_A - 2] = pre
    qkva_ref[...] = _qkv_a_post(acc)
    small = jnp.dot(hb, w1_ref[:, O_SMALL:O_SMALL + LANES], preferred_element_type=F32)
    beta, g = _gates(small, alog_ref[...], dt_ref[...])
    beta_ref[...] = beta
    g_ref[...] = g
    lg_ref[...] = jax.nn.log_sigmoid(_dot(small, w2_ref[...]) + bgk_ref[...]) / GLA_GATE_NORM
    qkb = jnp.dot(hb, w1_ref[:, O_QB:O_QB + 2 * HB * DKB], preferred_element_type=F32)
    qkvb_ref[:, 0:HB * DKB] = qkb[:, 0:HB * DKB] * (DKB ** -0.5)
    qkvb_ref[:, HB * DKB:2 * HB * DKB] = qkb[:, HB * DKB:]
    qkvb_ref[:, 2 * HB * DKB:] = jnp.dot(hb, w1_ref[:, O_VB:O_VB + HB * DVB], preferred_element_type=F32)
    post_ref[...] = jnp.dot(hb, w1_ref[:, O_POST:O_POST + W_POST], preferred_element_type=F32)


def _dec_head_call(xs, cs, p):
    n = xs.shape[0]
    shapes = [(n, C_A), (n, LANES), (n, LANES), (n, 2 * HB * DKB + HB * DVB), (n, HB * DKB), (n, W_POST),
              (CONV_A - 1, n, C_A)]
    return pl.pallas_call(
        _dec_head_body,
        out_shape=[jax.ShapeDtypeStruct(s, F32) for s in shapes],
        compiler_params=pltpu.CompilerParams(vmem_limit_bytes=VMEM_LIMIT),
        name="decode_head",
    )(xs, cs, p["norm_mix"], p["w_conv_a"], p["alog_row"], p["dt_row"], p["b_gk"], p["w1"], p["w2"])


def _dec_rec_body(qkva_ref, beta_ref, g_ref, qkvb_ref, lg_ref, sa_ref, sb_ref,
                  oa_ref, ob_ref, san_ref, sbn_ref, *, tb):
    beta = beta_ref[...]
    eg = jnp.exp(g_ref[...])
    for h in range(HA):
        q_rows = qkva_ref[:, h * DKA:(h + 1) * DKA].astype(BF16)
        k_t = qkva_ref[:, HA * DKA + h * DKA:HA * DKA + (h + 1) * DKA].T
        for j in range(tb):
            k_col = k_t[:, j:j + 1]
            s = sa_ref[j, h] * eg[j:j + 1, LANE_G + h:LANE_G + h + 1]
            v = qkva_ref[j:j + 1, 2 * HA * DKA + h * DVA:2 * HA * DKA + (h + 1) * DVA]
            err = (v - jnp.sum(k_col * s, axis=0, keepdims=True)) * beta[j:j + 1, LANE_BETA + h:LANE_BETA + h + 1]
            s = s + k_col * err
            san_ref[j, h] = s
            o_all = jnp.dot(q_rows, s.astype(BF16), preferred_element_type=F32)
            oa_ref[j:j + 1, h * DVA:(h + 1) * DVA] = o_all[j:j + 1, :]
    for h in range(HB):
        q_rows = qkvb_ref[:, h * DKB:(h + 1) * DKB].astype(BF16)
        k_t = qkvb_ref[:, HB * DKB + h * DKB:HB * DKB + (h + 1) * DKB].T
        d_t = jnp.exp(lg_ref[:, h * DKB:(h + 1) * DKB]).T
        for j in range(tb):
            v = qkvb_ref[j:j + 1, 2 * HB * DKB + h * DVB:2 * HB * DKB + (h + 1) * DVB]
            s = sb_ref[j, h] * d_t[:, j:j + 1] + k_t[:, j:j + 1] * v
            sbn_ref[j, h] = s
            o_all = jnp.dot(q_rows, s.astype(BF16), preferred_element_type=F32)
            ob_ref[j:j + 1, h * DVB:(h + 1) * DVB] = o_all[j:j + 1, :]


def _dec_rec_call(qkva, beta, g, qkvb, lg, sa, sb, tb):
    n = qkva.shape[0]
    rows = lambda w: pl.BlockSpec((tb, w), lambda i: (i, 0))
    in_specs = [rows(C_A), rows(LANES), rows(LANES), rows(2 * HB * DKB + HB * DVB), rows(HB * DKB),
                pl.BlockSpec((tb, HA, DKA, DVA), lambda i: (i, 0, 0, 0)),
                pl.BlockSpec((tb, HB, DKB, DVB), lambda i: (i, 0, 0, 0))]
    out_specs = [rows(HA * DVA), rows(HB * DVB),
                 pl.BlockSpec((tb, HA, DKA, DVA), lambda i: (i, 0, 0, 0)),
                 pl.BlockSpec((tb, HB, DKB, DVB), lambda i: (i, 0, 0, 0))]
    out_shape = [jax.ShapeDtypeStruct((n, HA * DVA), F32), jax.ShapeDtypeStruct((n, HB * DVB), F32),
                 jax.ShapeDtypeStruct(sa.shape, F32), jax.ShapeDtypeStruct(sb.shape, F32)]
    return pl.pallas_call(
        functools.partial(_dec_rec_body, tb=tb),
        grid=(n // tb,), in_specs=in_specs, out_specs=out_specs, out_shape=out_shape,
        compiler_params=pltpu.CompilerParams(dimension_semantics=("arbitrary",), vmem_limit_bytes=VMEM_LIMIT),
        name="decode_recurrence",
    )(qkva, beta, g, qkvb, lg, sa, sb)


def _dec_tail_body(x_ref, oa_ref, ob_ref, post_ref, cf_ref, ona_ref, onb_ref, nffn_ref, wcf_ref, bcf_ref, nfin_ref,
                   wa_ref, wb_ref, wo_ref, wup_ref, wdn_ref, y_ref, cfn_ref):
    post = post_ref[...]
    y_a = _branch_a(oa_ref[...], post, ona_ref[...], wa_ref)
    y_b = _branch_b(ob_ref[...], post, onb_ref[...], wb_ref)
    x1 = _mix_residual(x_ref[...], y_a, y_b, post, wo_ref)
    h2 = _rms(x1, nffn_ref[...]).astype(BF16)
    u = jnp.dot(h2, wup_ref[:, 0:D_FF], preferred_element_type=F32)
    gf = jnp.dot(h2, wup_ref[:, D_FF:], preferred_element_type=F32)
    acc = cf_ref[:, 0:D_FF] * wcf_ref[0:1, :]
    for i in range(1, CONV_F - 1):
        acc = acc + cf_ref[:, i * D_FF:(i + 1) * D_FF] * wcf_ref[i:i + 1, :]
    acc = acc + u * wcf_ref[CONV_F - 1:CONV_F, :] + bcf_ref[...]
    for i in range(CONV_F - 2):
        cfn_ref[:, i * D_FF:(i + 1) * D_FF] = cf_ref[:, (i + 1) * D_FF:(i + 2) * D_FF]
    cfn_ref[:, (CONV_F - 2) * D_FF:] = u
    y_ref[...] = _ffn_core(x1, acc, gf, wdn_ref, nfin_ref[...])


def _dec_tail_call(xs, oa, ob, post, cf, p):
    n = xs.shape[0]
    return pl.pallas_call(
        _dec_tail_body,
        out_shape=[jax.ShapeDtypeStruct((n, D_MODEL), F32), jax.ShapeDtypeStruct((n, (CONV_F - 1) * D_FF), F32)],
        compiler_params=pltpu.CompilerParams(vmem_limit_bytes=VMEM_LIMIT),
        name="decode_tail",
    )(xs, oa, ob, post, cf, p["onorm_a"], p["onorm_b"], p["norm_ffn"], p["w_conv_f"], p["b_conv_f"],
      p["norm_final"], p["w_a_out"], p["w_b_out"], p["w_o"], p["w_ffn_in"], p["w_ffn_out"])


PROJ_SIZES = (HA * DKA, HA * DKA, HA * DVA, HA * DVA, HA, HA, HB * DKB, HB * DKB, HB * DVB, HB * DVB, GATE_RANK,
              D_MODEL, D_MODEL)
PROJ_ORDER = (0, 1, 2, 6, 7, 8, 4, 5, 10, None, 3, 9, 11, 12)
RELAYOUT_ROWS = 128


def _relayout_body(w_ref, o_ref):
    offs = [0]
    for s in PROJ_SIZES:
        offs.append(offs[-1] + s)
    dst = 0
    for seg in PROJ_ORDER:
        if seg is None:
            width = (-dst) % LANES
            o_ref[:, dst:dst + width] = jnp.zeros((o_ref.shape[0], width), BF16)
        else:
            width = PROJ_SIZES[seg]
            o_ref[:, dst:dst + width] = w_ref[:, offs[seg]:offs[seg] + width].astype(BF16)
        dst += width
    assert dst == W_IN_COLS


def _relayout_w_in(w):
    rows, cols = w.shape
    return pl.pallas_call(
        _relayout_body,
        grid=(rows // RELAYOUT_ROWS,),
        in_specs=[pl.BlockSpec((RELAYOUT_ROWS, cols), lambda i: (i, 0))],
        out_specs=pl.BlockSpec((RELAYOUT_ROWS, W_IN_COLS), lambda i: (i, 0)),
        out_shape=jax.ShapeDtypeStruct((rows, W_IN_COLS), BF16),
        compiler_params=pltpu.CompilerParams(dimension_semantics=("arbitrary",), vmem_limit_bytes=VMEM_LIMIT),
        name="relayout_w_in",
    )(w)


def _prep_params(l, norm_mix, w_in, w_conv_a, a_log, dt_bias, w_gk2, b_gk, onorm_a, onorm_b, w_a_out, w_b_out,
                 w_o, norm_ffn, w_ffn_in, w_conv_f, b_conv_f, w_ffn_out, norm_final):
    n_small = 2 * HA + GATE_RANK
    w1 = _relayout_w_in(w_in[l])
    w2 = jnp.zeros((LANES, HB * DKB), F32).at[2 * HA:n_small].set(w_gk2[l]).astype(BF16)
    lane_row = lambda v: jnp.zeros((1, LANES), F32).at[0, LANE_G:LANE_G + HA].set(v)
    return dict(
        norm_mix=norm_mix[l][None], w_conv_a=w_conv_a[l], alog_row=lane_row(a_log[l]), dt_row=lane_row(dt_bias[l]),
        b_gk=b_gk[l][None], onorm_a=onorm_a[l][None], onorm_b=onorm_b[l][None], w1=w1, w2=w2,
        w_a_out=w_a_out[l].astype(BF16), w_b_out=w_b_out[l].astype(BF16), w_o=w_o[l].astype(BF16),
        norm_ffn=norm_ffn[l][None], w_ffn_in=w_ffn_in[l].astype(BF16), w_conv_f=w_conv_f[l],
        b_conv_f=b_conv_f[l][None], w_ffn_out=w_ffn_out[l].astype(BF16), norm_final=norm_final[None])


PROMPT_TILE = 256
FFN_TILE = 512
DECODE_TILE = 8


def kernel(x_prompt, x_sample, state_delta, state_delta_conv, state_gla, state_ffn_conv, meta_tokens, norm_mix, w_in, w_conv_a, a_log, dt_bias, w_gk2, b_gk, onorm_a, onorm_b, w_a_out, w_b_out, w_o, norm_ffn, w_ffn_in, w_conv_f, b_conv_f, w_ffn_out, norm_final):
    assert w_in.shape[0] == 1, "single layer only"
    l = 0
    p = _prep_params(l, norm_mix, w_in, w_conv_a, a_log, dt_bias, w_gk2, b_gk, onorm_a, onorm_b, w_a_out, w_b_out,
                     w_o, norm_ffn, w_ffn_in, w_conv_f, b_conv_f, w_ffn_out, norm_final)
    n_dec = x_sample.shape[0]

    xm = jnp.concatenate([jnp.zeros((CHUNK - N_META, D_MODEL), F32), meta_tokens.astype(F32)], axis=0)[None]
    x1m, sa0, ca0, sb0 = _mixer_call(xm, jnp.zeros((HA, DKA, DVA), F32), jnp.zeros((HB, DKB, DVB), F32),
                                     jnp.zeros((CONV_A - 1, C_A), F32), p, CHUNK)
    _, cf0 = _ffn_call(x1m, jnp.zeros((CONV_F - 1, D_FF), F32), p, CHUNK)

    x1, sa_p, ca_p, sb_p = _mixer_call(x_prompt, sa0[0], sb0[0], ca0[0], p, PROMPT_TILE)
    y_prompt, cf_p = _ffn_call(x1, cf0[0], p, FFN_TILE)

    xs = x_sample.reshape(n_dec, D_MODEL)
    cs = jnp.transpose(state_delta_conv[l], (1, 0, 2))
    cfs = state_ffn_conv[l].reshape(n_dec, (CONV_F - 1) * D_FF)
    qkva, beta, g, qkvb, lg, post, cs_new = _dec_head_call(xs, cs, p)
    oa, ob, sa_s, sb_s = _dec_rec_call(qkva, beta, g, qkvb, lg, state_delta[l], state_gla[l], DECODE_TILE)
    y_s, cf_s = _dec_tail_call(xs, oa, ob, post, cfs, p)

    return (y_prompt, y_s.reshape(n_dec, 1, D_MODEL),
            sa_p[None], ca_p[None], sb_p[None], cf_p[None],
            sa_s[None], jnp.transpose(cs_new, (1, 0, 2))[None], sb_s[None],
            cf_s.reshape(1, n_dec, CONV_F - 1, D_FF))
```

```python
import functools

import jax
import jax.numpy as jnp
from jax import lax
from jax.experimental import pallas as pl
from jax.experimental.pallas import tpu as pltpu

F32 = jnp.float32
BF16 = jnp.bfloat16

D_MODEL = 1024
N_META = 16
CHUNK = 64
HA, DKA, DVA = 4, 128, 128
HB, DKB, DVB = 4, 128, 256
CONV_A = 4
C_A = 2 * HA * DKA + HA * DVA
GATE_RANK = 16
GLA_GATE_NORM = 16.0
D_FF = 2816
CONV_F = 3
EPS = 1e-6
LANES = 128
SUBLANES = 8

O_QKVA = 0
O_QB = O_QKVA + C_A
O_KB = O_QB + HB * DKB
O_VB = O_KB + HB * DKB
O_SMALL = O_VB + HB * DVB
O_POST = O_SMALL + LANES
P_ZA = 0
P_RB = P_ZA + HA * DVA
P_GA = P_RB + HB * DVB
P_GB = P_GA + D_MODEL
W_POST = P_GB + D_MODEL
W_IN_COLS = O_POST + W_POST
LANE_BETA = 0
LANE_G = HA

VMEM_LIMIT = 56 * 1024 * 1024


def _dot(a, b):
    return jnp.dot(a.astype(BF16), b.astype(BF16), preferred_element_type=F32)


def _dot_nt(a, b):
    return lax.dot_general(a.astype(BF16), b.astype(BF16), (((1,), (1,)), ((), ())), preferred_element_type=F32)


def _dot_tn(a, b):
    return lax.dot_general(a.astype(BF16), b.astype(BF16), (((0,), (0,)), ((), ())), preferred_element_type=F32)


def _rms(x, w):
    return x * lax.rsqrt(jnp.mean(x * x, axis=-1, keepdims=True) + EPS) * w


def _silu(x):
    return x * jax.nn.sigmoid(x)


def _gelu_tanh(x):
    return 0.5 * x * (1.0 + jnp.tanh(0.7978845608028654 * (x + 0.044715 * (x * x * x))))


def _l2n(x):
    return x * lax.rsqrt(jnp.sum(x * x, axis=-1, keepdims=True) + EPS)


def _iota2(n, m, axis):
    return lax.broadcasted_iota(jnp.int32, (n, m), axis)


def _gates(small, alog_row, dt_row):
    beta = jax.nn.sigmoid(small)
    g = -jnp.exp(alog_row) * jax.nn.softplus(small + dt_row)
    return beta, g


def _qkv_a_post(conv_out):
    act = _silu(conv_out)
    parts = []
    for h in range(HA):
        parts.append(_l2n(act[:, h * DKA:(h + 1) * DKA]) * (DKA ** -0.5))
    for h in range(HA):
        o = HA * DKA + h * DKA
        parts.append(_l2n(act[:, o:o + DKA]))
    parts.append(act[:, 2 * HA * DKA:])
    return jnp.concatenate(parts, axis=1)


def _branch_a(oa, post, onorm_a, wa_ref):
    pa = [_rms(oa[:, h * DVA:(h + 1) * DVA], onorm_a) for h in range(HA)]
    return _dot(jnp.concatenate(pa, axis=1) * _silu(post[:, P_ZA:P_ZA + HA * DVA]), wa_ref[...])


def _branch_b(ob, post, onorm_b, wb_ref):
    pb = [_rms(ob[:, h * DVB:(h + 1) * DVB], onorm_b) for h in range(HB)]
    return _dot(jnp.concatenate(pb, axis=1) * _silu(post[:, P_RB:P_RB + HB * DVB]), wb_ref[...])


def _mix_residual(x, y_a, y_b, post, wo_ref):
    mix = (jax.nn.sigmoid(post[:, P_GA:P_GA + D_MODEL]) * y_a
           + jax.nn.sigmoid(post[:, P_GB:P_GB + D_MODEL]) * y_b)
    return x + _dot(mix, wo_ref[...])


def _cumsum_chunks(x, c):
    rowi = _iota2(x.shape[0], x.shape[1], 0) % c
    sh = 1
    while sh < c:
        x = x + jnp.where(rowi >= sh, pltpu.roll(x, sh, axis=0), 0.0)
        sh *= 2
    return x


def _delta_prepare(qkv_s, beta_s, gcum_s, gt_s, m_s, qk_s, rhs_s, qe_s, kdec_s, tt, c):
    ri = _iota2(c, c, 0)
    ci = _iota2(c, c, 1)
    causal = (ri >= ci).astype(F32)
    strict = (ri > ci).astype(F32)
    for ch in range(tt // c):
        rows = slice(ch * c, (ch + 1) * c)
        g_blk = gcum_s[rows, :]
        beta_blk = beta_s[rows, :]
        eg_blk = jnp.exp(g_blk)
        ed_blk = jnp.exp(g_blk[c - 1:c, :] - g_blk)
        for h in range(HA):
            n = ch * HA + h
            hs = slice(h * DKA, (h + 1) * DKA)
            q = qkv_s[rows, h * DKA:(h + 1) * DKA]
            k = qkv_s[rows, HA * DKA + h * DKA:HA * DKA + (h + 1) * DKA]
            v = qkv_s[rows, 2 * HA * DKA + h * DVA:2 * HA * DKA + (h + 1) * DVA]
            g_col = g_blk[:, LANE_G + h:LANE_G + h + 1]
            g_row = gt_s[LANE_G + h:LANE_G + h + 1, rows]
            decay = jnp.exp(jnp.minimum(g_col - g_row, 0.0))
            beta_col = beta_blk[:, LANE_BETA + h:LANE_BETA + h + 1]
            eg = eg_blk[:, LANE_G + h:LANE_G + h + 1]
            kb = k * beta_col
            kq = _dot_nt(jnp.concatenate([kb, q], axis=0), k)
            m_s[n] = kq[0:c] * (decay * strict)
            qk_s[n] = kq[c:] * (decay * causal)
            rhs_s[rows, h * 2 * DVA:(h + 1) * 2 * DVA] = jnp.concatenate([v * beta_col, kb * eg], axis=1)
            qe_s[rows, hs] = q * eg
            kdec_s[rows, hs] = k * ed_blk[:, LANE_G + h:LANE_G + h + 1]


def _inverse_stages(m_s, p_s, pw_s, n_inst, c):
    ri = _iota2(c, c, 0)
    ci = _iota2(c, c, 1)
    eye = (ri == ci).astype(F32)
    base = min(16, c)
    same = (ri // base) == (ci // base)
    neg_same = jnp.where(same, -1.0, 0.0)
    for n in range(n_inst):
        a = m_s[n] * neg_same
        pw_s[n] = _dot(a, a)
        p_s[n] = eye + a
    k = 2
    while k < base:
        for n in range(n_inst):
            p = p_s[n]
            pw = pw_s[n]
            p_s[n] = p + _dot(p, pw)
            if 2 * k < base:
                pw_s[n] = _dot(pw, pw)
        k *= 2
    size = base
    while size < c:
        nxt = size * 2
        same_nxt = (ri // nxt) == (ci // nxt)
        off = (same_nxt & jnp.logical_not(same)).astype(F32)
        for n in range(n_inst):
            pw_s[n] = _dot(p_s[n], m_s[n] * off)
        for n in range(n_inst):
            p = p_s[n]
            p_s[n] = p - _dot(pw_s[n], p)
        same = same_nxt
        size = nxt


def _level_ref_rows(bc, half, c):
    w = bc.shape[1]
    parts = []
    if half >= SUBLANES // 2:
        for blk in range(c // (2 * half)):
            mrow = blk * 2 * half + half
            parts.append(jnp.broadcast_to(bc[mrow:mrow + 1, :], (2 * half, w)))
    else:
        sub = _iota2(SUBLANES, w, 0)
        for grp in range(c // SUBLANES):
            acc = None
            for blk in range(SUBLANES // (2 * half)):
                mrow = grp * SUBLANES + blk * 2 * half + half
                b = jnp.broadcast_to(bc[mrow:mrow + 1, :], (SUBLANES, w))
                acc = b if acc is None else jnp.where(sub >= blk * 2 * half, b, acc)
            parts.append(acc)
    return jnp.concatenate(parts, axis=0)


def _gla_prepare_steps(qkb_s, vb_s, bc_s, attn_s, ob_s, pkv_s, qeb_s, tt, c):
    ri = _iota2(c, c, 0)
    ci = _iota2(c, c, 1)
    n_ch = tt // c

    def operands(ch, h):
        rows = slice(ch * c, (ch + 1) * c)
        q = qkb_s[rows, h * DKB:(h + 1) * DKB]
        k = qkb_s[rows, HB * DKB + h * DKB:HB * DKB + (h + 1) * DKB]
        return rows, q, k

    def diag():
        for ch in range(n_ch):
            for h in range(HB):
                _, q, k = operands(ch, h)
                attn_s[ch * HB + h] = jnp.where(ri == ci, jnp.sum(q * k, axis=-1, keepdims=True), 0.0)

    def level(half):
        valid = (((ri // (2 * half)) == (ci // (2 * half))) & ((ri % (2 * half)) >= half)
                 & ((ci % (2 * half)) < half)).astype(F32)
        for ch in range(n_ch):
            bc_all = bc_s[ch * c:(ch + 1) * c, :]
            e_all = jnp.exp(-jnp.abs(bc_all - _level_ref_rows(bc_all, half, c)))
            for h in range(HB):
                _, q, k = operands(ch, h)
                e = e_all[:, h * DKB:(h + 1) * DKB]
                attn_s[ch * HB + h] += _dot_nt(q * e, k * e) * valid

    def finish():
        for ch in range(n_ch):
            for h in range(HB):
                rows, q, k = operands(ch, h)
                v = vb_s[rows, h * DVB:(h + 1) * DVB]
                bc = bc_s[rows, h * DKB:(h + 1) * DKB]
                ob_s[rows, h * DVB:(h + 1) * DVB] = _dot(attn_s[ch * HB + h], v)
                pkv_s[ch * HB + h] = _dot_tn(k * jnp.exp(bc[c - 1:c, :] - bc), v)
                qeb_s[rows, h * DKB:(h + 1) * DKB] = q * jnp.exp(bc)

    steps = [diag]
    half = c // 2
    while half >= 1:
        steps.append(functools.partial(level, half))
        half //= 2
    steps.append(finish)
    return steps


def _mixer_body(x_ref, sa0_ref, sb0_ref, ca0_ref, nmix_ref, wca_ref, alog_ref, dt_ref, bgk_ref, ona_ref, onb_ref,
                w1_ref, w2_ref, wa_ref, wb_ref, wo_ref,
                x1_ref, sa_ref, ca_ref, sb_ref,
                cbuf, hb_s, post_s, qkv_s, beta_s, gcum_s, gt_s, qkb_s, vb_s, bc_s, oa_s, ob_s,
                m_s, p_s, pw_s, qk_s, rhs_s, uw_s, qe_s, kdec_s, attn_s, pkv_s, qeb_s, gated_s, mix_s, ktu_s, ktw_s, *, tt, c):
    t_idx = pl.program_id(1)
    n_ch = tt // c

    @pl.when(t_idx == 0)
    def _():
        sa_ref[...] = sa0_ref[...]
        sb_ref[...] = sb0_ref[...]
        cbuf[SUBLANES - (CONV_A - 1):SUBLANES, :] = ca0_ref[...]

    rb = min(tt, 64)
    for r in range(tt // rb):
        rr = slice(r * rb, (r + 1) * rb)
        hb_s[rr, :] = _rms(x_ref[rr, :], nmix_ref[...]).astype(BF16)
    hb = hb_s[...]
    cbuf[SUBLANES:SUBLANES + tt, :] = jnp.dot(hb, w1_ref[:, O_QKVA:O_QKVA + C_A], preferred_element_type=F32)
    base = SUBLANES - (CONV_A - 1)
    n_blk = C_A // LANES
    post_w = 2 * LANES
    n_post = W_POST // post_w
    for j in range(max(n_blk, n_post)):
        if j < n_post:
            pc = slice(j * post_w, (j + 1) * post_w)
            post_s[:, pc] = jnp.dot(hb, w1_ref[:, O_POST + j * post_w:O_POST + (j + 1) * post_w],
                                    preferred_element_type=F32)
        if j < n_blk:
            cols = slice(j * LANES, (j + 1) * LANES)
            full = cbuf[:, cols]
            shifted = lambda k: full[SUBLANES:] if k == 0 else pltpu.roll(full, k, axis=0)[SUBLANES:]
            acc = shifted(CONV_A - 1) * wca_ref[0:1, cols]
            for i in range(1, CONV_A):
                acc = acc + shifted(CONV_A - 1 - i) * wca_ref[i:i + 1, cols]
            act = _silu(acc)
            if j < HA:
                act = _l2n(act) * (DKA ** -0.5)
            elif j < 2 * HA:
                act = _l2n(act)
            qkv_s[:, cols] = act
    tail = cbuf[tt + base:tt + SUBLANES, :]
    cbuf[base:SUBLANES, :] = tail
    ca_ref[...] = tail
    small = jnp.dot(hb, w1_ref[:, O_SMALL:O_SMALL + LANES], preferred_element_type=F32)
    beta, g = _gates(small, alog_ref[...], dt_ref[...])
    beta_s[...] = beta
    g_cum = _cumsum_chunks(g, c)
    gcum_s[...] = g_cum
    gt_s[...] = g_cum.T

    _delta_prepare(qkv_s, beta_s, gcum_s, gt_s, m_s, qk_s, rhs_s, qe_s, kdec_s, tt, c)
    _inverse_stages(m_s, p_s, pw_s, n_ch * HA, c)
    for ch in range(n_ch):
        rows = slice(ch * c, (ch + 1) * c)
        for h in range(HA):
            cols = slice(h * 2 * DVA, (h + 1) * 2 * DVA)
            uw_s[rows, cols] = _dot(p_s[ch * HA + h], rhs_s[rows, cols])
    for ch in range(n_ch):
        rows = slice(ch * c, (ch + 1) * c)
        for h in range(HA):
            n = ch * HA + h
            hs = slice(h * DKA, (h + 1) * DKA)
            uw = uw_s[rows, h * 2 * DVA:(h + 1) * 2 * DVA]
            att_uw = _dot(qk_s[n], uw)
            oa_s[rows, h * DVA:(h + 1) * DVA] = att_uw[:, 0:DVA]
            qe_s[rows, hs] = qe_s[rows, hs] - att_uw[:, DVA:]
            kt_uw = _dot_tn(kdec_s[rows, hs], uw)
            ktu_s[n] = kt_uw[:, 0:DVA]
            ktw_s[n] = kt_uw[:, DVA:]

    small_b = small.astype(BF16)
    for h in range(HB):
        hs = slice(h * DKB, (h + 1) * DKB)
        lg_pre = jnp.dot(small_b, w2_ref[:, hs], preferred_element_type=F32) + bgk_ref[:, hs]
        bc_s[:, hs] = _cumsum_chunks(jax.nn.log_sigmoid(lg_pre) / GLA_GATE_NORM, c)
    qkb_s[:, 0:HB * DKB] = jnp.dot(hb, w1_ref[:, O_QB:O_QB + HB * DKB], preferred_element_type=F32) * (DKB ** -0.5)
    qkb_s[:, HB * DKB:] = jnp.dot(hb, w1_ref[:, O_KB:O_KB + HB * DKB], preferred_element_type=F32)
    vb_s[...] = jnp.dot(hb, w1_ref[:, O_VB:O_VB + HB * DVB], preferred_element_type=F32)

    for step in _gla_prepare_steps(qkb_s, vb_s, bc_s, attn_s, ob_s, pkv_s, qeb_s, tt, c):
        step()

    for ch in range(n_ch):
        rows = slice(ch * c, (ch + 1) * c)
        last = slice(ch * c + c - 1, ch * c + c)
        for h in range(HA):
            n = ch * HA + h
            hs = slice(h * DKA, (h + 1) * DKA)
            s = sa_ref[h]
            prod = _dot(jnp.concatenate([ktw_s[n], qe_s[rows, hs]], axis=0), s)
            oa_s[rows, h * DVA:(h + 1) * DVA] += prod[DKA:]
            g_last = gcum_s[last, LANE_G + h:LANE_G + h + 1]
            sa_ref[h] = s * jnp.exp(g_last) + (ktu_s[n] - prod[0:DKA])

    for h in range(HA):
        hs = slice(h * DVA, (h + 1) * DVA)
        gated_s[:, hs] = (_rms(oa_s[:, hs], ona_ref[...]) * _silu(post_s[:, P_ZA + h * DVA:P_ZA + (h + 1) * DVA])
                          ).astype(BF16)

    for ch in range(n_ch):
        rows = slice(ch * c, (ch + 1) * c)
        last = slice(ch * c + c - 1, ch * c + c)
        for h in range(HB):
            s = sb_ref[h]
            ob_s[rows, h * DVB:(h + 1) * DVB] += _dot(qeb_s[rows, h * DKB:(h + 1) * DKB], s)
            e_col = jnp.exp(bc_s[last, h * DKB:(h + 1) * DKB]).T
            sb_ref[h] = s * e_col + pkv_s[ch * HB + h]

    for h in range(HB):
        hs = slice(h * DVB, (h + 1) * DVB)
        gated_s[:, HA * DVA + h * DVB:HA * DVA + (h + 1) * DVB] = (
            _rms(ob_s[:, hs], onb_ref[...]) * _silu(post_s[:, P_RB + h * DVB:P_RB + (h + 1) * DVB])).astype(BF16)
    out_w = 2 * LANES
    ga = gated_s[:, 0:HA * DVA]
    gb = gated_s[:, HA * DVA:]
    for n in range(D_MODEL // out_w):
        ns = slice(n * out_w, (n + 1) * out_w)
        y_a = jnp.dot(ga, wa_ref[:, ns], preferred_element_type=F32)
        y_b = jnp.dot(gb, wb_ref[:, ns], preferred_element_type=F32)
        mix_s[:, ns] = (jax.nn.sigmoid(post_s[:, P_GA + n * out_w:P_GA + (n + 1) * out_w]) * y_a
                        + jax.nn.sigmoid(post_s[:, P_GB + n * out_w:P_GB + (n + 1) * out_w]) * y_b).astype(BF16)
    mix = mix_s[...]
    for n in range(D_MODEL // out_w):
        ns = slice(n * out_w, (n + 1) * out_w)
        x1_ref[:, ns] = x_ref[:, ns] + jnp.dot(mix, wo_ref[:, ns], preferred_element_type=F32)


def _const_spec(shape):
    nd = len(shape)
    return pl.BlockSpec(shape, lambda *_: (0,) * nd, pipeline_mode=pl.Buffered(1))


def _mixer_call(x, sa0, sb0, ca0, p, tt):
    b, t, _ = x.shape
    c = CHUNK
    nt = t // tt
    n_inst = (tt // c) * HA
    row = lambda n: _const_spec((1, n))
    in_specs = [
        pl.BlockSpec((None, tt, D_MODEL), lambda i, j: (i, j, 0)),
        _const_spec((HA, DKA, DVA)), _const_spec((HB, DKB, DVB)), _const_spec((CONV_A - 1, C_A)),
        row(D_MODEL), _const_spec((CONV_A, C_A)), row(LANES), row(LANES), row(HB * DKB), row(DVA), row(DVB),
        _const_spec((D_MODEL, W_IN_COLS)), _const_spec((LANES, HB * DKB)),
        _const_spec((HA * DVA, D_MODEL)), _const_spec((HB * DVB, D_MODEL)), _const_spec((D_MODEL, D_MODEL)),
    ]
    out_specs = [
        pl.BlockSpec((None, tt, D_MODEL), lambda i, j: (i, j, 0)),
        pl.BlockSpec((None, HA, DKA, DVA), lambda i, j: (i, 0, 0, 0)),
        pl.BlockSpec((None, CONV_A - 1, C_A), lambda i, j: (i, 0, 0)),
        pl.BlockSpec((None, HB, DKB, DVB), lambda i, j: (i, 0, 0, 0)),
    ]
    out_shape = [
        jax.ShapeDtypeStruct((b, t, D_MODEL), F32),
        jax.ShapeDtypeStruct((b, HA, DKA, DVA), F32),
        jax.ShapeDtypeStruct((b, CONV_A - 1, C_A), F32),
        jax.ShapeDtypeStruct((b, HB, DKB, DVB), F32),
    ]
    mats = lambda: pltpu.VMEM((n_inst, c, c), F32)
    scratch = [
        pltpu.VMEM((tt + SUBLANES, C_A), F32),
        pltpu.VMEM((tt, D_MODEL), BF16),
        pltpu.VMEM((tt, W_POST), F32),
        pltpu.VMEM((tt, C_A), F32),
        pltpu.VMEM((tt, LANES), F32),
        pltpu.VMEM((tt, LANES), F32),
        pltpu.VMEM((LANES, tt), F32),
        pltpu.VMEM((tt, 2 * HB * DKB), F32),
        pltpu.VMEM((tt, HB * DVB), F32),
        pltpu.VMEM((tt, HB * DKB), F32),
        pltpu.VMEM((tt, HA * DVA), F32),
        pltpu.VMEM((tt, HB * DVB), F32),
        mats(), mats(), mats(), mats(),
        pltpu.VMEM((tt, HA * 2 * DVA), F32),
        pltpu.VMEM((tt, HA * 2 * DVA), F32),
        pltpu.VMEM((tt, HA * DKA), F32),
        pltpu.VMEM((tt, HA * DKA), F32),
        mats(),
        pltpu.VMEM((n_inst, DKB, DVB), F32),
        pltpu.VMEM((tt, HB * DKB), F32),
        pltpu.VMEM((tt, HA * DVA + HB * DVB), BF16),
        pltpu.VMEM((tt, D_MODEL), BF16),
        pltpu.VMEM((n_inst, DKA, DVA), F32),
        pltpu.VMEM((n_inst, DKA, DVA), F32),
    ]
    return pl.pallas_call(
        functools.partial(_mixer_body, tt=tt, c=c),
        grid=(b, nt), in_specs=in_specs, out_specs=out_specs, out_shape=out_shape, scratch_shapes=scratch,
        compiler_params=pltpu.CompilerParams(dimension_semantics=("arbitrary", "arbitrary"),
                                             vmem_limit_bytes=VMEM_LIMIT),
        name="mixer",
    )(x, sa0, sb0, ca0, p["norm_mix"], p["w_conv_a"], p["alog_row"], p["dt_row"], p["b_gk"], p["onorm_a"],
      p["onorm_b"], p["w1"], p["w2"], p["w_a_out"], p["w_b_out"], p["w_o"])


def _ffn_core(x1, u_conv, gf, wdn_ref, nfin):
    act = _gelu_tanh(u_conv) * gf
    x2 = x1 + _dot(act, wdn_ref[...])
    return _rms(x2, nfin)


def _ffn_body(x1_ref, cf0_ref, nffn_ref, wcf_ref, bcf_ref, nfin_ref, wup_ref, wdn_ref,
              y_ref, cf_ref, ubuf, *, tt):
    t_idx = pl.program_id(1)
    base = SUBLANES - (CONV_F - 1)

    @pl.when(t_idx == 0)
    def _():
        ubuf[base:SUBLANES, :] = cf0_ref[...]

    x1 = x1_ref[...]
    h2 = _rms(x1, nffn_ref[...]).astype(BF16)
    ubuf[SUBLANES:SUBLANES + tt, :] = jnp.dot(h2, wup_ref[:, 0:D_FF], preferred_element_type=F32)
    gf = jnp.dot(h2, wup_ref[:, D_FF:], preferred_element_type=F32)
    acc = ubuf[base:base + tt, :] * wcf_ref[0:1, :]
    for i in range(1, CONV_F):
        acc = acc + ubuf[base + i:base + i + tt, :] * wcf_ref[i:i + 1, :]
    acc = acc + bcf_ref[...]
    tail = ubuf[tt + base:tt + SUBLANES, :]
    ubuf[base:SUBLANES, :] = tail
    cf_ref[...] = tail
    y_ref[...] = _ffn_core(x1, acc, gf, wdn_ref, nfin_ref[...])


def _ffn_tail_state_body(x1_ref, nffn_ref, wup_ref, u_ref):
    h2 = _rms(x1_ref[...], nffn_ref[...]).astype(BF16)
    u_ref[...] = jnp.dot(h2, wup_ref[...], preferred_element_type=F32)


def _ffn_tail_state_call(x1_rows, p):
    return pl.pallas_call(
        _ffn_tail_state_body,
        grid=(1,),
        in_specs=[pl.BlockSpec((SUBLANES, D_MODEL), lambda i: (0, 0)), pl.BlockSpec((1, D_MODEL), lambda i: (0, 0)),
                  pl.BlockSpec((D_MODEL, D_FF), lambda i: (0, 0))],
        out_specs=pl.BlockSpec((SUBLANES, D_FF), lambda i: (0, 0)),
        out_shape=jax.ShapeDtypeStruct((SUBLANES, D_FF), F32),
        compiler_params=pltpu.CompilerParams(dimension_semantics=("arbitrary",), vmem_limit_bytes=VMEM_LIMIT),
        name="meta_ffn_state",
    )(x1_rows, p["norm_ffn"], p["w_ffn_in"])


def _ffn_call(x1, cf0, p, tt):
    b, t, _ = x1.shape
    nt = t // tt
    row = lambda n: _const_spec((1, n))
    in_specs = [
        pl.BlockSpec((None, tt, D_MODEL), lambda i, j: (i, j, 0)),
        _const_spec((CONV_F - 1, D_FF)), row(D_MODEL), _const_spec((CONV_F, D_FF)), row(D_FF), row(D_MODEL),
        _const_spec((D_MODEL, 2 * D_FF)), _const_spec((D_FF, D_MODEL)),
    ]
    out_specs = [
        pl.BlockSpec((None, tt, D_MODEL), lambda i, j: (i, j, 0)),
        pl.BlockSpec((None, CONV_F - 1, D_FF), lambda i, j: (i, 0, 0)),
    ]
    out_shape = [jax.ShapeDtypeStruct((b, t, D_MODEL), F32), jax.ShapeDtypeStruct((b, CONV_F - 1, D_FF), F32)]
    return pl.pallas_call(
        functools.partial(_ffn_body, tt=tt),
        grid=(b, nt), in_specs=in_specs, out_specs=out_specs, out_shape=out_shape,
        scratch_shapes=[pltpu.VMEM((tt + SUBLANES, D_FF), F32)],
        compiler_params=pltpu.CompilerParams(dimension_semantics=("arbitrary", "arbitrary"),
                                             vmem_limit_bytes=VMEM_LIMIT),
        name="convffn",
    )(x1, cf0, p["norm_ffn"], p["w_conv_f"], p["b_conv_f"], p["norm_final"], p["w_ffn_in"], p["w_ffn_out"])


def _dec_head_body(x_ref, cs_ref, nmix_ref, wca_ref, alog_ref, dt_ref, bgk_ref, w1_ref, w2_ref,
                   qkva_ref, beta_ref, g_ref, qkvb_ref, lg_ref, post_ref, csn_ref):
    x = x_ref[...]
    hb = _rms(x, nmix_ref[...]).astype(BF16)
    pre = jnp.dot(hb, w1_ref[:, O_QKVA:O_QKVA + C_A], preferred_element_type=F32)
    acc = cs_ref[0] * wca_ref[0:1, :]
    for i in range(1, CONV_A - 1):
        acc = acc + cs_ref[i] * wca_ref[i:i + 1, :]
    acc = acc + pre * wca_ref[CONV_A - 1:CONV_A, :]
    for i in range(CONV_A - 2):
        csn_ref[i] = cs_ref[i + 1]
    csn_ref[CONV_A - 2] = pre
    qkva_ref[...] = _qkv_a_post(acc)
    small = jnp.dot(hb, w1_ref[:, O_SMALL:O_SMALL + LANES], preferred_element_type=F32)
    beta, g = _gates(small, alog_ref[...], dt_ref[...])
    beta_ref[...] = beta
    g_ref[...] = g
    lg_ref[...] = jax.nn.log_sigmoid(_dot(small, w2_ref[...]) + bgk_ref[...]) / GLA_GATE_NORM
    qkb = jnp.dot(hb, w1_ref[:, O_QB:O_QB + 2 * HB * DKB], preferred_element_type=F32)
    qkvb_ref[:, 0:HB * DKB] = qkb[:, 0:HB * DKB] * (DKB ** -0.5)
    qkvb_ref[:, HB * DKB:2 * HB * DKB] = qkb[:, HB * DKB:]
    qkvb_ref[:, 2 * HB * DKB:] = jnp.dot(hb, w1_ref[:, O_VB:O_VB + HB * DVB], preferred_element_type=F32)
    post_ref[...] = jnp.dot(hb, w1_ref[:, O_POST:O_POST + W_POST], preferred_element_type=F32)


def _dec_head_call(xs, cs, p):
    n = xs.shape[0]
    shapes = [(n, C_A), (n, LANES), (n, LANES), (n, 2 * HB * DKB + HB * DVB), (n, HB * DKB), (n, W_POST),
              (CONV_A - 1, n, C_A)]
    return pl.pallas_call(
        _dec_head_body,
        out_shape=[jax.ShapeDtypeStruct(s, F32) for s in shapes],
        compiler_params=pltpu.CompilerParams(vmem_limit_bytes=VMEM_LIMIT),
        name="decode_head",
    )(xs, cs, p["norm_mix"], p["w_conv_a"], p["alog_row"], p["dt_row"], p["b_gk"], p["w1"], p["w2"])


def _dec_rec_body(qkva_ref, beta_ref, g_ref, qkvb_ref, lg_ref, sa_ref, sb_ref,
                  oa_ref, ob_ref, san_ref, sbn_ref, *, tb):
    beta = beta_ref[...]
    eg = jnp.exp(g_ref[...])
    for h in range(HA):
        q_rows = qkva_ref[:, h * DKA:(h + 1) * DKA].astype(BF16)
        k_t = qkva_ref[:, HA * DKA + h * DKA:HA * DKA + (h + 1) * DKA].T
        for j in range(tb):
            k_col = k_t[:, j:j + 1]
            s = sa_ref[j, h] * eg[j:j + 1, LANE_G + h:LANE_G + h + 1]
            v = qkva_ref[j:j + 1, 2 * HA * DKA + h * DVA:2 * HA * DKA + (h + 1) * DVA]
            err = (v - jnp.sum(k_col * s, axis=0, keepdims=True)) * beta[j:j + 1, LANE_BETA + h:LANE_BETA + h + 1]
            s = s + k_col * err
            san_ref[j, h] = s
            o_all = jnp.dot(q_rows, s.astype(BF16), preferred_element_type=F32)
            oa_ref[j:j + 1, h * DVA:(h + 1) * DVA] = o_all[j:j + 1, :]
    for h in range(HB):
        q_rows = qkvb_ref[:, h * DKB:(h + 1) * DKB].astype(BF16)
        k_t = qkvb_ref[:, HB * DKB + h * DKB:HB * DKB + (h + 1) * DKB].T
        d_t = jnp.exp(lg_ref[:, h * DKB:(h + 1) * DKB]).T
        for j in range(tb):
            v = qkvb_ref[j:j + 1, 2 * HB * DKB + h * DVB:2 * HB * DKB + (h + 1) * DVB]
            s = sb_ref[j, h] * d_t[:, j:j + 1] + k_t[:, j:j + 1] * v
            sbn_ref[j, h] = s
            o_all = jnp.dot(q_rows, s.astype(BF16), preferred_element_type=F32)
            ob_ref[j:j + 1, h * DVB:(h + 1) * DVB] = o_all[j:j + 1, :]


def _dec_rec_call(qkva, beta, g, qkvb, lg, sa, sb, tb):
    n = qkva.shape[0]
    rows = lambda w: pl.BlockSpec((tb, w), lambda i: (i, 0))
    in_specs = [rows(C_A), rows(LANES), rows(LANES), rows(2 * HB * DKB + HB * DVB), rows(HB * DKB),
                pl.BlockSpec((tb, HA, DKA, DVA), lambda i: (i, 0, 0, 0)),
                pl.BlockSpec((tb, HB, DKB, DVB), lambda i: (i, 0, 0, 0))]
    out_specs = [rows(HA * DVA), rows(HB * DVB),
                 pl.BlockSpec((tb, HA, DKA, DVA), lambda i: (i, 0, 0, 0)),
                 pl.BlockSpec((tb, HB, DKB, DVB), lambda i: (i, 0, 0, 0))]
    out_shape = [jax.ShapeDtypeStruct((n, HA * DVA), F32), jax.ShapeDtypeStruct((n, HB * DVB), F32),
                 jax.ShapeDtypeStruct(sa.shape, F32), jax.ShapeDtypeStruct(sb.shape, F32)]
    return pl.pallas_call(
        functools.partial(_dec_rec_body, tb=tb),
        grid=(n // tb,), in_specs=in_specs, out_specs=out_specs, out_shape=out_shape,
        compiler_params=pltpu.CompilerParams(dimension_semantics=("arbitrary",), vmem_limit_bytes=VMEM_LIMIT),
        name="decode_recurrence",
    )(qkva, beta, g, qkvb, lg, sa, sb)


def _dec_tail_body(x_ref, oa_ref, ob_ref, post_ref, cf_ref, ona_ref, onb_ref, nffn_ref, wcf_ref, bcf_ref, nfin_ref,
                   wa_ref, wb_ref, wo_ref, wup_ref, wdn_ref, y_ref, cfn_ref):
    post = post_ref[...]
    y_a = _branch_a(oa_ref[...], post, ona_ref[...], wa_ref)
    y_b = _branch_b(ob_ref[...], post, onb_ref[...], wb_ref)
    x1 = _mix_residual(x_ref[...], y_a, y_b, post, wo_ref)
    h2 = _rms(x1, nffn_ref[...]).astype(BF16)
    u = jnp.dot(h2, wup_ref[:, 0:D_FF], preferred_element_type=F32)
    gf = jnp.dot(h2, wup_ref[:, D_FF:], preferred_element_type=F32)
    acc = cf_ref[:, 0:D_FF] * wcf_ref[0:1, :]
    for i in range(1, CONV_F - 1):
        acc = acc + cf_ref[:, i * D_FF:(i + 1) * D_FF] * wcf_ref[i:i + 1, :]
    acc = acc + u * wcf_ref[CONV_F - 1:CONV_F, :] + bcf_ref[...]
    for i in range(CONV_F - 2):
        cfn_ref[:, i * D_FF:(i + 1) * D_FF] = cf_ref[:, (i + 1) * D_FF:(i + 2) * D_FF]
    cfn_ref[:, (CONV_F - 2) * D_FF:] = u
    y_ref[...] = _ffn_core(x1, acc, gf, wdn_ref, nfin_ref[...])


def _dec_tail_call(xs, oa, ob, post, cf, p):
    n = xs.shape[0]
    return pl.pallas_call(
        _dec_tail_body,
        out_shape=[jax.ShapeDtypeStruct((n, D_MODEL), F32), jax.ShapeDtypeStruct((n, (CONV_F - 1) * D_FF), F32)],
        compiler_params=pltpu.CompilerParams(vmem_limit_bytes=VMEM_LIMIT),
        name="decode_tail",
    )(xs, oa, ob, post, cf, p["onorm_a"], p["onorm_b"], p["norm_ffn"], p["w_conv_f"], p["b_conv_f"],
      p["norm_final"], p["w_a_out"], p["w_b_out"], p["w_o"], p["w_ffn_in"], p["w_ffn_out"])


PROJ_SIZES = (HA * DKA, HA * DKA, HA * DVA, HA * DVA, HA, HA, HB * DKB, HB * DKB, HB * DVB, HB * DVB, GATE_RANK,
              D_MODEL, D_MODEL)
PROJ_ORDER = (0, 1, 2, 6, 7, 8, 4, 5, 10, None, 3, 9, 11, 12)
RELAYOUT_ROWS = 128


def _relayout_body(w_ref, o_ref):
    offs = [0]
    for s in PROJ_SIZES:
        offs.append(offs[-1] + s)
    dst = 0
    for seg in PROJ_ORDER:
        if seg is None:
            width = (-dst) % LANES
            o_ref[:, dst:dst + width] = jnp.zeros((o_ref.shape[0], width), BF16)
        else:
            width = PROJ_SIZES[seg]
            o_ref[:, dst:dst + width] = w_ref[:, offs[seg]:offs[seg] + width].astype(BF16)
        dst += width
    assert dst == W_IN_COLS


def _relayout_w_in(w):
    rows, cols = w.shape
    return pl.pallas_call(
        _relayout_body,
        grid=(rows // RELAYOUT_ROWS,),
        in_specs=[pl.BlockSpec((RELAYOUT_ROWS, cols), lambda i: (i, 0))],
        out_specs=pl.BlockSpec((RELAYOUT_ROWS, W_IN_COLS), lambda i: (i, 0)),
        out_shape=jax.ShapeDtypeStruct((rows, W_IN_COLS), BF16),
        compiler_params=pltpu.CompilerParams(dimension_semantics=("arbitrary",), vmem_limit_bytes=VMEM_LIMIT),
        name="relayout_w_in",
    )(w)


def _prep_params(l, norm_mix, w_in, w_conv_a, a_log, dt_bias, w_gk2, b_gk, onorm_a, onorm_b, w_a_out, w_b_out,
                 w_o, norm_ffn, w_ffn_in, w_conv_f, b_conv_f, w_ffn_out, norm_final):
    n_small = 2 * HA + GATE_RANK
    w1 = _relayout_w_in(w_in[l].astype(BF16))
    w2 = jnp.zeros((LANES, HB * DKB), F32).at[2 * HA:n_small].set(w_gk2[l]).astype(BF16)
    lane_row = lambda v: jnp.zeros((1, LANES), F32).at[0, LANE_G:LANE_G + HA].set(v)
    return dict(
        norm_mix=norm_mix[l][None], w_conv_a=w_conv_a[l], alog_row=lane_row(a_log[l]), dt_row=lane_row(dt_bias[l]),
        b_gk=b_gk[l][None], onorm_a=onorm_a[l][None], onorm_b=onorm_b[l][None], w1=w1, w2=w2,
        w_a_out=w_a_out[l].astype(BF16), w_b_out=w_b_out[l].astype(BF16), w_o=w_o[l].astype(BF16),
        norm_ffn=norm_ffn[l][None], w_ffn_in=w_ffn_in[l].astype(BF16), w_conv_f=w_conv_f[l],
        b_conv_f=b_conv_f[l][None], w_ffn_out=w_ffn_out[l].astype(BF16), norm_final=norm_final[None])


PROMPT_TILE = 256
FFN_TILE = 512
DECODE_TILE = 8


def kernel(x_prompt, x_sample, state_delta, state_delta_conv, state_gla, state_ffn_conv, meta_tokens, norm_mix, w_in, w_conv_a, a_log, dt_bias, w_gk2, b_gk, onorm_a, onorm_b, w_a_out, w_b_out, w_o, norm_ffn, w_ffn_in, w_conv_f, b_conv_f, w_ffn_out, norm_final):
    assert w_in.shape[0] == 1, "single layer only"
    l = 0
    p = _prep_params(l, norm_mix, w_in, w_conv_a, a_log, dt_bias, w_gk2, b_gk, onorm_a, onorm_b, w_a_out, w_b_out,
                     w_o, norm_ffn, w_ffn_in, w_conv_f, b_conv_f, w_ffn_out, norm_final)
    n_dec = x_sample.shape[0]

    xm = jnp.concatenate([jnp.zeros((CHUNK - N_META, D_MODEL), F32), meta_tokens.astype(F32)], axis=0)[None]
    x1m, sa0, ca0, sb0 = _mixer_call(xm, jnp.zeros((HA, DKA, DVA), F32), jnp.zeros((HB, DKB, DVB), F32),
                                     jnp.zeros((CONV_A - 1, C_A), F32), p, CHUNK)
    u_tail = _ffn_tail_state_call(x1m[0, CHUNK - SUBLANES:], p)
    cf0 = u_tail[SUBLANES - (CONV_F - 1):][None]

    x1, sa_p, ca_p, sb_p = _mixer_call(x_prompt, sa0[0], sb0[0], ca0[0], p, PROMPT_TILE)
    y_prompt, cf_p = _ffn_call(x1, cf0[0], p, FFN_TILE)

    xs = x_sample.reshape(n_dec, D_MODEL)
    cs = jnp.transpose(state_delta_conv[l], (1, 0, 2))
    cfs = state_ffn_conv[l].reshape(n_dec, (CONV_F - 1) * D_FF)
    qkva, beta, g, qkvb, lg, post, cs_new = _dec_head_call(xs, cs, p)
    oa, ob, sa_s, sb_s = _dec_rec_call(qkva, beta, g, qkvb, lg, state_delta[l], state_gla[l], DECODE_TILE)
    y_s, cf_s = _dec_tail_call(xs, oa, ob, post, cfs, p)

    return (y_prompt, y_s.reshape(n_dec, 1, D_MODEL),
            sa_p[None], ca_p[None], sb_p[None], cf_p[None],
            sa_s[None], jnp.transpose(cs_new, (1, 0, 2))[None], sb_s[None],
            cf_s.reshape(1, n_dec, CONV_F - 1, D_FF))
```

```python
import functools

import jax
import jax.numpy as jnp
from jax import lax
from jax.experimental import pallas as pl
from jax.experimental.pallas import tpu as pltpu

F32 = jnp.float32
BF16 = jnp.bfloat16

D_MODEL = 1024
N_META = 16
CHUNK = 64
HA, DKA, DVA = 4, 128, 128
HB, DKB, DVB = 4, 128, 256
CONV_A = 4
C_A = 2 * HA * DKA + HA * DVA
GATE_RANK = 16
GLA_GATE_NORM = 16.0
D_FF = 2816
CONV_F = 3
EPS = 1e-6
LANES = 128
SUBLANES = 8

O_QKVA = 0
O_QB = O_QKVA + C_A
O_KB = O_QB + HB * DKB
O_VB = O_KB + HB * DKB
O_SMALL = O_VB + HB * DVB
O_POST = O_SMALL + LANES
P_ZA = 0
P_RB = P_ZA + HA * DVA
P_GA = P_RB + HB * DVB
P_GB = P_GA + D_MODEL
W_POST = P_GB + D_MODEL
W_IN_COLS = O_POST + W_POST
LANE_BETA = 0
LANE_G = HA

VMEM_LIMIT = 56 * 1024 * 1024


def _dot(a, b):
    return jnp.dot(a.astype(BF16), b.astype(BF16), preferred_element_type=F32)


def _dot_nt(a, b):
    return lax.dot_general(a.astype(BF16), b.astype(BF16), (((1,), (1,)), ((), ())), preferred_element_type=F32)


def _dot_tn(a, b):
    return lax.dot_general(a.astype(BF16), b.astype(BF16), (((0,), (0,)), ((), ())), preferred_element_type=F32)


def _rms(x, w):
    return x * lax.rsqrt(jnp.mean(x * x, axis=-1, keepdims=True) + EPS) * w


def _silu(x):
    return x * jax.nn.sigmoid(x)


def _gelu_tanh(x):
    return 0.5 * x * (1.0 + jnp.tanh(0.7978845608028654 * (x + 0.044715 * (x * x * x))))


def _l2n(x):
    return x * lax.rsqrt(jnp.sum(x * x, axis=-1, keepdims=True) + EPS)


def _iota2(n, m, axis):
    return lax.broadcasted_iota(jnp.int32, (n, m), axis)


def _gates(small, alog_row, dt_row):
    beta = jax.nn.sigmoid(small)
    g = -jnp.exp(alog_row) * jax.nn.softplus(small + dt_row)
    return beta, g


def _qkv_a_post(conv_out):
    act = _silu(conv_out)
    parts = []
    for h in range(HA):
        parts.append(_l2n(act[:, h * DKA:(h + 1) * DKA]) * (DKA ** -0.5))
    for h in range(HA):
        o = HA * DKA + h * DKA
        parts.append(_l2n(act[:, o:o + DKA]))
    parts.append(act[:, 2 * HA * DKA:])
    return jnp.concatenate(parts, axis=1)


def _branch_a(oa, post, onorm_a, wa_ref):
    pa = [_rms(oa[:, h * DVA:(h + 1) * DVA], onorm_a) for h in range(HA)]
    return _dot(jnp.concatenate(pa, axis=1) * _silu(post[:, P_ZA:P_ZA + HA * DVA]), wa_ref[...])


def _branch_b(ob, post, onorm_b, wb_ref):
    pb = [_rms(ob[:, h * DVB:(h + 1) * DVB], onorm_b) for h in range(HB)]
    return _dot(jnp.concatenate(pb, axis=1) * _silu(post[:, P_RB:P_RB + HB * DVB]), wb_ref[...])


def _mix_residual(x, y_a, y_b, post, wo_ref):
    mix = (jax.nn.sigmoid(post[:, P_GA:P_GA + D_MODEL]) * y_a
           + jax.nn.sigmoid(post[:, P_GB:P_GB + D_MODEL]) * y_b)
    return x + _dot(mix, wo_ref[...])


def _cumsum_chunks(x, c):
    rowi = _iota2(x.shape[0], x.shape[1], 0) % c
    sh = 1
    while sh < c:
        x = x + jnp.where(rowi >= sh, pltpu.roll(x, sh, axis=0), 0.0)
        sh *= 2
    return x


def _delta_prepare(qkv_s, beta_s, gcum_s, gt_s, m_s, qk_s, rhs_s, qe_s, kdec_s, tt, c):
    ri = _iota2(c, 2 * c, 0)
    ci = _iota2(c, 2 * c, 1) % c
    first = _iota2(c, 2 * c, 1) < c
    causal = (ri >= ci).astype(F32)
    strict = (ri > ci).astype(F32)
    zeros = jnp.zeros((c, DKA), F32)
    for cp in range(tt // (2 * c)):
        rows2 = slice(2 * cp * c, (2 * cp + 2) * c)
        for h in range(HA):
            n = cp * HA + h
            hs = slice(h * DKA, (h + 1) * DKA)
            lhs, rhs_t, g_cols = [], [], []
            for par in range(2):
                rows = slice((2 * cp + par) * c, (2 * cp + par + 1) * c)
                g_blk = gcum_s[rows, :]
                q = qkv_s[rows, h * DKA:(h + 1) * DKA]
                k = qkv_s[rows, HA * DKA + h * DKA:HA * DKA + (h + 1) * DKA]
                v = qkv_s[rows, 2 * HA * DKA + h * DVA:2 * HA * DKA + (h + 1) * DVA]
                g_col = g_blk[:, LANE_G + h:LANE_G + h + 1]
                beta_col = beta_s[rows, LANE_BETA + h:LANE_BETA + h + 1]
                eg = jnp.exp(g_col)
                kb = k * beta_col
                lhs.append(jnp.concatenate([kb, q], axis=0))
                rhs_t.append(jnp.concatenate([k, zeros] if par == 0 else [zeros, k], axis=1))
                g_cols.append(g_col)
                rhs_s[rows, h * 2 * DVA:(h + 1) * 2 * DVA] = jnp.concatenate([v * beta_col, kb * eg], axis=1)
                qe_s[rows, hs] = q * eg
                kdec_s[rows, hs] = k * jnp.exp(g_col[c - 1:c, :] - g_col)
            kq = _dot_nt(jnp.concatenate(lhs, axis=1), jnp.concatenate(rhs_t, axis=0))
            g_row = gt_s[LANE_G + h:LANE_G + h + 1, rows2]
            decay = jnp.exp(jnp.minimum(jnp.where(first, g_cols[0], g_cols[1]) - g_row, 0.0))
            m_s[n] = kq[0:c] * (decay * strict)
            qk_s[n] = kq[c:] * (decay * causal)


def _inverse_stages(m_s, p_s, pw_s, n_inst, c):
    ri = _iota2(c, 2 * c, 0)
    lane = _iota2(c, 2 * c, 1)
    ci = lane % c
    left = (lane < c).astype(F32)
    right = 1.0 - left

    def bdiag(xp):
        return jnp.concatenate([xp * left, xp * right], axis=0)

    eye = (ri == ci).astype(F32)
    base = min(16, c)
    same = (ri // base) == (ci // base)
    neg_same = jnp.where(same, -1.0, 0.0)
    for n in range(n_inst):
        a = m_s[n] * neg_same
        pw_s[n] = _dot(a, bdiag(a))
        p_s[n] = eye + a
    k = 2
    while k < base:
        for n in range(n_inst):
            p = p_s[n]
            pw = pw_s[n]
            bd = bdiag(pw).astype(BF16)
            p_s[n] = p + _dot(p, bd)
            if 2 * k < base:
                pw_s[n] = _dot(pw, bd)
        k *= 2
    size = base
    while size < c:
        nxt = size * 2
        same_nxt = (ri // nxt) == (ci // nxt)
        off = (same_nxt & jnp.logical_not(same)).astype(F32)
        for n in range(n_inst):
            pw_s[n] = _dot(p_s[n], bdiag(m_s[n] * off))
        for n in range(n_inst):
            p = p_s[n]
            p_s[n] = p - _dot(pw_s[n], bdiag(p))
        same = same_nxt
        size = nxt


def _level_ref_rows(bc, half, c):
    w = bc.shape[1]
    parts = []
    if half >= SUBLANES // 2:
        for blk in range(c // (2 * half)):
            mrow = blk * 2 * half + half
            parts.append(jnp.broadcast_to(bc[mrow:mrow + 1, :], (2 * half, w)))
    else:
        sub = _iota2(SUBLANES, w, 0)
        for grp in range(c // SUBLANES):
            acc = None
            for blk in range(SUBLANES // (2 * half)):
                mrow = grp * SUBLANES + blk * 2 * half + half
                b = jnp.broadcast_to(bc[mrow:mrow + 1, :], (SUBLANES, w))
                acc = b if acc is None else jnp.where(sub >= blk * 2 * half, b, acc)
            parts.append(acc)
    return jnp.concatenate(parts, axis=0)


def _gla_prepare_steps(qkb_s, vb_s, bc_s, attn_s, ob_s, pkv_s, qeb_s, tt, c):
    ri = _iota2(c, c, 0)
    ci = _iota2(c, c, 1)
    n_ch = tt // c

    def operands(ch, h):
        rows = slice(ch * c, (ch + 1) * c)
        q = qkb_s[rows, h * DKB:(h + 1) * DKB]
        k = qkb_s[rows, HB * DKB + h * DKB:HB * DKB + (h + 1) * DKB]
        return rows, q, k

    def diag():
        for ch in range(n_ch):
            for h in range(HB):
                _, q, k = operands(ch, h)
                attn_s[ch * HB + h] = jnp.where(ri == ci, jnp.sum(q * k, axis=-1, keepdims=True), 0.0)

    def level(half):
        valid = (((ri // (2 * half)) == (ci // (2 * half))) & ((ri % (2 * half)) >= half)
                 & ((ci % (2 * half)) < half)).astype(F32)
        for ch in range(n_ch):
            bc_all = bc_s[ch * c:(ch + 1) * c, :]
            e_all = jnp.exp(-jnp.abs(bc_all - _level_ref_rows(bc_all, half, c)))
            for h in range(HB):
                _, q, k = operands(ch, h)
                e = e_all[:, h * DKB:(h + 1) * DKB]
                attn_s[ch * HB + h] += _dot_nt(q * e, k * e) * valid

    def finish():
        for ch in range(n_ch):
            for h in range(HB):
                rows, q, k = operands(ch, h)
                v = vb_s[rows, h * DVB:(h + 1) * DVB]
                bc = bc_s[rows, h * DKB:(h + 1) * DKB]
                ob_s[rows, h * DVB:(h + 1) * DVB] = _dot(attn_s[ch * HB + h], v)
                pkv_s[ch * HB + h] = _dot_tn(k * jnp.exp(bc[c - 1:c, :] - bc), v)
                qeb_s[rows, h * DKB:(h + 1) * DKB] = q * jnp.exp(bc)

    steps = [diag]
    half = c // 2
    while half >= 1:
        steps.append(functools.partial(level, half))
        half //= 2
    steps.append(finish)
    return steps


def _mixer_body(x_ref, sa0_ref, sb0_ref, ca0_ref, nmix_ref, wca_ref, alog_ref, dt_ref, bgk_ref, ona_ref, onb_ref,
                w1_ref, w2_ref, wa_ref, wb_ref, wo_ref,
                x1_ref, sa_ref, ca_ref, sb_ref,
                cbuf, hb_s, post_s, qkv_s, beta_s, gcum_s, gt_s, qkb_s, vb_s, bc_s, oa_s, ob_s,
                m_s, p_s, pw_s, qk_s, rhs_s, uw_s, qe_s, kdec_s, attn_s, pkv_s, qeb_s, gated_s, mix_s, ktu_s, ktw_s, *, tt, c):
    t_idx = pl.program_id(1)
    n_ch = tt // c

    @pl.when(t_idx == 0)
    def _():
        sa_ref[...] = sa0_ref[...]
        sb_ref[...] = sb0_ref[...]
        cbuf[SUBLANES - (CONV_A - 1):SUBLANES, :] = ca0_ref[...]

    rb = min(tt, 64)
    for r in range(tt // rb):
        rr = slice(r * rb, (r + 1) * rb)
        hb_s[rr, :] = _rms(x_ref[rr, :], nmix_ref[...]).astype(BF16)
    hb = hb_s[...]
    cbuf[SUBLANES:SUBLANES + tt, :] = jnp.dot(hb, w1_ref[:, O_QKVA:O_QKVA + C_A], preferred_element_type=F32)
    base = SUBLANES - (CONV_A - 1)
    n_blk = C_A // LANES
    post_w = 2 * LANES
    n_post = W_POST // post_w
    for j in range(max(n_blk, n_post)):
        if j < n_post:
            pc = slice(j * post_w, (j + 1) * post_w)
            post_s[:, pc] = jnp.dot(hb, w1_ref[:, O_POST + j * post_w:O_POST + (j + 1) * post_w],
                                    preferred_element_type=F32)
        if j < n_blk:
            cols = slice(j * LANES, (j + 1) * LANES)
            full = cbuf[:, cols]
            shifted = lambda k: full[SUBLANES:] if k == 0 else pltpu.roll(full, k, axis=0)[SUBLANES:]
            acc = shifted(CONV_A - 1) * wca_ref[0:1, cols]
            for i in range(1, CONV_A):
                acc = acc + shifted(CONV_A - 1 - i) * wca_ref[i:i + 1, cols]
            act = _silu(acc)
            if j < HA:
                act = _l2n(act) * (DKA ** -0.5)
            elif j < 2 * HA:
                act = _l2n(act)
            qkv_s[:, cols] = act
    tail = cbuf[tt + base:tt + SUBLANES, :]
    cbuf[base:SUBLANES, :] = tail
    ca_ref[...] = tail
    small = jnp.dot(hb, w1_ref[:, O_SMALL:O_SMALL + LANES], preferred_element_type=F32)
    beta, g = _gates(small, alog_ref[...], dt_ref[...])
    beta_s[...] = beta
    g_cum = _cumsum_chunks(g, c)
    gcum_s[...] = g_cum
    gt_s[...] = g_cum.T

    _delta_prepare(qkv_s, beta_s, gcum_s, gt_s, m_s, qk_s, rhs_s, qe_s, kdec_s, tt, c)
    n_pair = (n_ch // 2) * HA
    _inverse_stages(m_s, p_s, pw_s, n_pair, c)
    zeros_uw = jnp.zeros((c, 2 * DVA), BF16)

    def pair_bdiag(ref, cp, cols):
        x_e = ref[2 * cp * c:(2 * cp + 1) * c, cols].astype(BF16)
        x_o = ref[(2 * cp + 1) * c:(2 * cp + 2) * c, cols].astype(BF16)
        return jnp.concatenate([jnp.concatenate([x_e, zeros_uw], axis=1),
                                jnp.concatenate([zeros_uw, x_o], axis=1)], axis=0)

    for cp in range(n_ch // 2):
        rows_e = slice(2 * cp * c, (2 * cp + 1) * c)
        rows_o = slice((2 * cp + 1) * c, (2 * cp + 2) * c)
        for h in range(HA):
            cols = slice(h * 2 * DVA, (h + 1) * 2 * DVA)
            uw2 = _dot(p_s[cp * HA + h], pair_bdiag(rhs_s, cp, cols))
            uw_s[rows_e, cols] = uw2[:, 0:2 * DVA]
            uw_s[rows_o, cols] = uw2[:, 2 * DVA:]
    for cp in range(n_ch // 2):
        for h in range(HA):
            hs = slice(h * DKA, (h + 1) * DKA)
            cols = slice(h * 2 * DVA, (h + 1) * 2 * DVA)
            att2 = _dot(qk_s[cp * HA + h], pair_bdiag(uw_s, cp, cols))
            for par in range(2):
                ch = 2 * cp + par
                rows = slice(ch * c, (ch + 1) * c)
                att_uw = att2[:, par * 2 * DVA:(par + 1) * 2 * DVA]
                oa_s[rows, h * DVA:(h + 1) * DVA] = att_uw[:, 0:DVA]
                qe_s[rows, hs] = qe_s[rows, hs] - att_uw[:, DVA:]
                kt_uw = _dot_tn(kdec_s[rows, hs], uw_s[rows, cols])
                ktu_s[ch * HA + h] = kt_uw[:, 0:DVA]
                ktw_s[ch * HA + h] = kt_uw[:, DVA:]

    small_b = small.astype(BF16)
    for h in range(HB):
        hs = slice(h * DKB, (h + 1) * DKB)
        lg_pre = jnp.dot(small_b, w2_ref[:, hs], preferred_element_type=F32) + bgk_ref[:, hs]
        bc_s[:, hs] = _cumsum_chunks(jax.nn.log_sigmoid(lg_pre) / GLA_GATE_NORM, c)
    qkb_s[:, 0:HB * DKB] = jnp.dot(hb, w1_ref[:, O_QB:O_QB + HB * DKB], preferred_element_type=F32) * (DKB ** -0.5)
    qkb_s[:, HB * DKB:] = jnp.dot(hb, w1_ref[:, O_KB:O_KB + HB * DKB], preferred_element_type=F32)
    vb_s[...] = jnp.dot(hb, w1_ref[:, O_VB:O_VB + HB * DVB], preferred_element_type=F32)

    for step in _gla_prepare_steps(qkb_s, vb_s, bc_s, attn_s, ob_s, pkv_s, qeb_s, tt, c):
        step()

    for ch in range(n_ch):
        rows = slice(ch * c, (ch + 1) * c)
        last = slice(ch * c + c - 1, ch * c + c)
        for h in range(HA):
            n = ch * HA + h
            hs = slice(h * DKA, (h + 1) * DKA)
            s = sa_ref[h]
            prod = _dot(jnp.concatenate([ktw_s[n], qe_s[rows, hs]], axis=0), s)
            oa_s[rows, h * DVA:(h + 1) * DVA] += prod[DKA:]
            g_last = gcum_s[last, LANE_G + h:LANE_G + h + 1]
            sa_ref[h] = s * jnp.exp(g_last) + (ktu_s[n] - prod[0:DKA])

    for h in range(HA):
        hs = slice(h * DVA, (h + 1) * DVA)
        gated_s[:, hs] = (_rms(oa_s[:, hs], ona_ref[...]) * _silu(post_s[:, P_ZA + h * DVA:P_ZA + (h + 1) * DVA])
                          ).astype(BF16)

    for ch in range(n_ch):
        rows = slice(ch * c, (ch + 1) * c)
        last = slice(ch * c + c - 1, ch * c + c)
        for h in range(HB):
            s = sb_ref[h]
            ob_s[rows, h * DVB:(h + 1) * DVB] += _dot(qeb_s[rows, h * DKB:(h + 1) * DKB], s)
            e_col = jnp.exp(bc_s[last, h * DKB:(h + 1) * DKB]).T
            sb_ref[h] = s * e_col + pkv_s[ch * HB + h]

    for h in range(HB):
        hs = slice(h * DVB, (h + 1) * DVB)
        gated_s[:, HA * DVA + h * DVB:HA * DVA + (h + 1) * DVB] = (
            _rms(ob_s[:, hs], onb_ref[...]) * _silu(post_s[:, P_RB + h * DVB:P_RB + (h + 1) * DVB])).astype(BF16)
    out_w = 2 * LANES
    ga = gated_s[:, 0:HA * DVA]
    gb = gated_s[:, HA * DVA:]
    for n in range(D_MODEL // out_w):
        ns = slice(n * out_w, (n + 1) * out_w)
        y_a = jnp.dot(ga, wa_ref[:, ns], preferred_element_type=F32)
        y_b = jnp.dot(gb, wb_ref[:, ns], preferred_element_type=F32)
        mix_s[:, ns] = (jax.nn.sigmoid(post_s[:, P_GA + n * out_w:P_GA + (n + 1) * out_w]) * y_a
                        + jax.nn.sigmoid(post_s[:, P_GB + n * out_w:P_GB + (n + 1) * out_w]) * y_b).astype(BF16)
    mix = mix_s[...]
    for n in range(D_MODEL // out_w):
        ns = slice(n * out_w, (n + 1) * out_w)
        x1_ref[:, ns] = x_ref[:, ns] + jnp.dot(mix, wo_ref[:, ns], preferred_element_type=F32)


def _const_spec(shape):
    nd = len(shape)
    return pl.BlockSpec(shape, lambda *_: (0,) * nd, pipeline_mode=pl.Buffered(1))


def _mixer_call(x, sa0, sb0, ca0, p, tt):
    b, t, _ = x.shape
    c = CHUNK
    nt = t // tt
    n_inst = (tt // c) * HA
    row = lambda n: _const_spec((1, n))
    in_specs = [
        pl.BlockSpec((None, tt, D_MODEL), lambda i, j: (i, j, 0)),
        _const_spec((HA, DKA, DVA)), _const_spec((HB, DKB, DVB)), _const_spec((CONV_A - 1, C_A)),
        row(D_MODEL), _const_spec((CONV_A, C_A)), row(LANES), row(LANES), row(HB * DKB), row(DVA), row(DVB),
        _const_spec((D_MODEL, W_IN_COLS)), _const_spec((LANES, HB * DKB)),
        _const_spec((HA * DVA, D_MODEL)), _const_spec((HB * DVB, D_MODEL)), _const_spec((D_MODEL, D_MODEL)),
    ]
    out_specs = [
        pl.BlockSpec((None, tt, D_MODEL), lambda i, j: (i, j, 0)),
        pl.BlockSpec((None, HA, DKA, DVA), lambda i, j: (i, 0, 0, 0)),
        pl.BlockSpec((None, CONV_A - 1, C_A), lambda i, j: (i, 0, 0)),
        pl.BlockSpec((None, HB, DKB, DVB), lambda i, j: (i, 0, 0, 0)),
    ]
    out_shape = [
        jax.ShapeDtypeStruct((b, t, D_MODEL), F32),
        jax.ShapeDtypeStruct((b, HA, DKA, DVA), F32),
        jax.ShapeDtypeStruct((b, CONV_A - 1, C_A), F32),
        jax.ShapeDtypeStruct((b, HB, DKB, DVB), F32),
    ]
    assert (tt // c) % 2 == 0, "mixer A handles chunks in pairs"
    mats = lambda: pltpu.VMEM((n_inst, c, c), F32)
    pairs = lambda: pltpu.VMEM((n_inst // 2, c, 2 * c), F32)
    scratch = [
        pltpu.VMEM((tt + SUBLANES, C_A), F32),
        pltpu.VMEM((tt, D_MODEL), BF16),
        pltpu.VMEM((tt, W_POST), F32),
        pltpu.VMEM((tt, C_A), F32),
        pltpu.VMEM((tt, LANES), F32),
        pltpu.VMEM((tt, LANES), F32),
        pltpu.VMEM((LANES, tt), F32),
        pltpu.VMEM((tt, 2 * HB * DKB), F32),
        pltpu.VMEM((tt, HB * DVB), F32),
        pltpu.VMEM((tt, HB * DKB), F32),
        pltpu.VMEM((tt, HA * DVA), F32),
        pltpu.VMEM((tt, HB * DVB), F32),
        pairs(), pairs(), pairs(), pairs(),
        pltpu.VMEM((tt, HA * 2 * DVA), F32),
        pltpu.VMEM((tt, HA * 2 * DVA), F32),
        pltpu.VMEM((tt, HA * DKA), F32),
        pltpu.VMEM((tt, HA * DKA), F32),
        mats(),
        pltpu.VMEM((n_inst, DKB, DVB), F32),
        pltpu.VMEM((tt, HB * DKB), F32),
        pltpu.VMEM((tt, HA * DVA + HB * DVB), BF16),
        pltpu.VMEM((tt, D_MODEL), BF16),
        pltpu.VMEM((n_inst, DKA, DVA), F32),
        pltpu.VMEM((n_inst, DKA, DVA), F32),
    ]
    return pl.pallas_call(
        functools.partial(_mixer_body, tt=tt, c=c),
        grid=(b, nt), in_specs=in_specs, out_specs=out_specs, out_shape=out_shape, scratch_shapes=scratch,
        compiler_params=pltpu.CompilerParams(dimension_semantics=("arbitrary", "arbitrary"),
                                             vmem_limit_bytes=VMEM_LIMIT),
        name="mixer",
    )(x, sa0, sb0, ca0, p["norm_mix"], p["w_conv_a"], p["alog_row"], p["dt_row"], p["b_gk"], p["onorm_a"],
      p["onorm_b"], p["w1"], p["w2"], p["w_a_out"], p["w_b_out"], p["w_o"])


def _ffn_core(x1, u_conv, gf, wdn_ref, nfin):
    act = _gelu_tanh(u_conv) * gf
    x2 = x1 + _dot(act, wdn_ref[...])
    return _rms(x2, nfin)


def _ffn_body(x1_ref, cf0_ref, nffn_ref, wcf_ref, bcf_ref, nfin_ref, wup_ref, wdn_ref,
              y_ref, cf_ref, ubuf, *, tt):
    t_idx = pl.program_id(1)
    base = SUBLANES - (CONV_F - 1)

    @pl.when(t_idx == 0)
    def _():
        ubuf[base:SUBLANES, :] = cf0_ref[...]

    x1 = x1_ref[...]
    h2 = _rms(x1, nffn_ref[...]).astype(BF16)
    ubuf[SUBLANES:SUBLANES + tt, :] = jnp.dot(h2, wup_ref[:, 0:D_FF], preferred_element_type=F32)
    gf = jnp.dot(h2, wup_ref[:, D_FF:], preferred_element_type=F32)
    acc = ubuf[base:base + tt, :] * wcf_ref[0:1, :]
    for i in range(1, CONV_F):
        acc = acc + ubuf[base + i:base + i + tt, :] * wcf_ref[i:i + 1, :]
    acc = acc + bcf_ref[...]
    tail = ubuf[tt + base:tt + SUBLANES, :]
    ubuf[base:SUBLANES, :] = tail
    cf_ref[...] = tail
    y_ref[...] = _ffn_core(x1, acc, gf, wdn_ref, nfin_ref[...])


def _ffn_tail_state_body(x1_ref, nffn_ref, wup_ref, u_ref):
    h2 = _rms(x1_ref[...], nffn_ref[...]).astype(BF16)
    u_ref[...] = jnp.dot(h2, wup_ref[...], preferred_element_type=F32)


def _ffn_tail_state_call(x1_rows, p):
    return pl.pallas_call(
        _ffn_tail_state_body,
        grid=(1,),
        in_specs=[pl.BlockSpec((SUBLANES, D_MODEL), lambda i: (0, 0)), pl.BlockSpec((1, D_MODEL), lambda i: (0, 0)),
                  pl.BlockSpec((D_MODEL, D_FF), lambda i: (0, 0))],
        out_specs=pl.BlockSpec((SUBLANES, D_FF), lambda i: (0, 0)),
        out_shape=jax.ShapeDtypeStruct((SUBLANES, D_FF), F32),
        compiler_params=pltpu.CompilerParams(dimension_semantics=("arbitrary",), vmem_limit_bytes=VMEM_LIMIT),
        name="meta_ffn_state",
    )(x1_rows, p["norm_ffn"], p["w_ffn_in"])


def _ffn_call(x1, cf0, p, tt):
    b, t, _ = x1.shape
    nt = t // tt
    row = lambda n: _const_spec((1, n))
    in_specs = [
        pl.BlockSpec((None, tt, D_MODEL), lambda i, j: (i, j, 0)),
        _const_spec((CONV_F - 1, D_FF)), row(D_MODEL), _const_spec((CONV_F, D_FF)), row(D_FF), row(D_MODEL),
        _const_spec((D_MODEL, 2 * D_FF)), _const_spec((D_FF, D_MODEL)),
    ]
    out_specs = [
        pl.BlockSpec((None, tt, D_MODEL), lambda i, j: (i, j, 0)),
        pl.BlockSpec((None, CONV_F - 1, D_FF), lambda i, j: (i, 0, 0)),
    ]
    out_shape = [jax.ShapeDtypeStruct((b, t, D_MODEL), F32), jax.ShapeDtypeStruct((b, CONV_F - 1, D_FF), F32)]
    return pl.pallas_call(
        functools.partial(_ffn_body, tt=tt),
        grid=(b, nt), in_specs=in_specs, out_specs=out_specs, out_shape=out_shape,
        scratch_shapes=[pltpu.VMEM((tt + SUBLANES, D_FF), F32)],
        compiler_params=pltpu.CompilerParams(dimension_semantics=("arbitrary", "arbitrary"),
                                             vmem_limit_bytes=VMEM_LIMIT),
        name="convffn",
    )(x1, cf0, p["norm_ffn"], p["w_conv_f"], p["b_conv_f"], p["norm_final"], p["w_ffn_in"], p["w_ffn_out"])


def _dec_head_body(x_ref, cs_ref, nmix_ref, wca_ref, alog_ref, dt_ref, bgk_ref, w1_ref, w2_ref,
                   qkva_ref, beta_ref, g_ref, qkvb_ref, lg_ref, post_ref, csn_ref):
    x = x_ref[...]
    hb = _rms(x, nmix_ref[...]).astype(BF16)
    pre = jnp.dot(hb, w1_ref[:, O_QKVA:O_QKVA + C_A], preferred_element_type=F32)
    acc = cs_ref[0] * wca_ref[0:1, :]
    for i in range(1, CONV_A - 1):
        acc = acc + cs_ref[i] * wca_ref[i:i + 1, :]
    acc = acc + pre * wca_ref[CONV_A - 1:CONV_A, :]
    for i in range(CONV_A - 2):
        csn_ref[i] = cs_ref[i + 1]
    csn_ref[CONV_A - 2] = pre
    qkva_ref[...] = _qkv_a_post(acc)
    small = jnp.dot(hb, w1_ref[:, O_SMALL:O_SMALL + LANES], preferred_element_type=F32)
    beta, g = _gates(small, alog_ref[...], dt_ref[...])
    beta_ref[...] = beta
    g_ref[...] = g
    lg_ref[...] = jax.nn.log_sigmoid(_dot(small, w2_ref[...]) + bgk_ref[...]) / GLA_GATE_NORM
    qkb = jnp.dot(hb, w1_ref[:, O_QB:O_QB + 2 * HB * DKB], preferred_element_type=F32)
    qkvb_ref[:, 0:HB * DKB] = qkb[:, 0:HB * DKB] * (DKB ** -0.5)
    qkvb_ref[:, HB * DKB:2 * HB * DKB] = qkb[:, HB * DKB:]
    qkvb_ref[:, 2 * HB * DKB:] = jnp.dot(hb, w1_ref[:, O_VB:O_VB + HB * DVB], preferred_element_type=F32)
    post_ref[...] = jnp.dot(hb, w1_ref[:, O_POST:O_POST + W_POST], preferred_element_type=F32)


def _dec_head_call(xs, cs, p):
    n = xs.shape[0]
    shapes = [(n, C_A), (n, LANES), (n, LANES), (n, 2 * HB * DKB + HB * DVB), (n, HB * DKB), (n, W_POST),
              (CONV_A - 1, n, C_A)]
    return pl.pallas_call(
        _dec_head_body,
        out_shape=[jax.ShapeDtypeStruct(s, F32) for s in shapes],
        compiler_params=pltpu.CompilerParams(vmem_limit_bytes=VMEM_LIMIT),
        name="decode_head",
    )(xs, cs, p["norm_mix"], p["w_conv_a"], p["alog_row"], p["dt_row"], p["b_gk"], p["w1"], p["w2"])


def _dec_rec_body(qkva_ref, beta_ref, g_ref, qkvb_ref, lg_ref, sa_ref, sb_ref,
                  oa_ref, ob_ref, san_ref, sbn_ref, *, tb):
    beta = beta_ref[...]
    eg = jnp.exp(g_ref[...])
    for h in range(HA):
        q_rows = qkva_ref[:, h * DKA:(h + 1) * DKA].astype(BF16)
        k_t = qkva_ref[:, HA * DKA + h * DKA:HA * DKA + (h + 1) * DKA].T
        for j in range(tb):
            k_col = k_t[:, j:j + 1]
            s = sa_ref[j, h] * eg[j:j + 1, LANE_G + h:LANE_G + h + 1]
            v = qkva_ref[j:j + 1, 2 * HA * DKA + h * DVA:2 * HA * DKA + (h + 1) * DVA]
            err = (v - jnp.sum(k_col * s, axis=0, keepdims=True)) * beta[j:j + 1, LANE_BETA + h:LANE_BETA + h + 1]
            s = s + k_col * err
            san_ref[j, h] = s
            o_all = jnp.dot(q_rows, s.astype(BF16), preferred_element_type=F32)
            oa_ref[j:j + 1, h * DVA:(h + 1) * DVA] = o_all[j:j + 1, :]
    for h in range(HB):
        q_rows = qkvb_ref[:, h * DKB:(h + 1) * DKB].astype(BF16)
        k_t = qkvb_ref[:, HB * DKB + h * DKB:HB * DKB + (h + 1) * DKB].T
        d_t = jnp.exp(lg_ref[:, h * DKB:(h + 1) * DKB]).T
        for j in range(tb):
            v = qkvb_ref[j:j + 1, 2 * HB * DKB + h * DVB:2 * HB * DKB + (h + 1) * DVB]
            s = sb_ref[j, h] * d_t[:, j:j + 1] + k_t[:, j:j + 1] * v
            sbn_ref[j, h] = s
            o_all = jnp.dot(q_rows, s.astype(BF16), preferred_element_type=F32)
            ob_ref[j:j + 1, h * DVB:(h + 1) * DVB] = o_all[j:j + 1, :]


def _dec_rec_call(qkva, beta, g, qkvb, lg, sa, sb, tb):
    n = qkva.shape[0]
    rows = lambda w: pl.BlockSpec((tb, w), lambda i: (i, 0))
    in_specs = [rows(C_A), rows(LANES), rows(LANES), rows(2 * HB * DKB + HB * DVB), rows(HB * DKB),
                pl.BlockSpec((tb, HA, DKA, DVA), lambda i: (i, 0, 0, 0)),
                pl.BlockSpec((tb, HB, DKB, DVB), lambda i: (i, 0, 0, 0))]
    out_specs = [rows(HA * DVA), rows(HB * DVB),
                 pl.BlockSpec((tb, HA, DKA, DVA), lambda i: (i, 0, 0, 0)),
                 pl.BlockSpec((tb, HB, DKB, DVB), lambda i: (i, 0, 0, 0))]
    out_shape = [jax.ShapeDtypeStruct((n, HA * DVA), F32), jax.ShapeDtypeStruct((n, HB * DVB), F32),
                 jax.ShapeDtypeStruct(sa.shape, F32), jax.ShapeDtypeStruct(sb.shape, F32)]
    return pl.pallas_call(
        functools.partial(_dec_rec_body, tb=tb),
        grid=(n // tb,), in_specs=in_specs, out_specs=out_specs, out_shape=out_shape,
        compiler_params=pltpu.CompilerParams(dimension_semantics=("arbitrary",), vmem_limit_bytes=VMEM_LIMIT),
        name="decode_recurrence",
    )(qkva, beta, g, qkvb, lg, sa, sb)


def _dec_tail_body(x_ref, oa_ref, ob_ref, post_ref, cf_ref, ona_ref, onb_ref, nffn_ref, wcf_ref, bcf_ref, nfin_ref,
                   wa_ref, wb_ref, wo_ref, wup_ref, wdn_ref, y_ref, cfn_ref):
    post = post_ref[...]
    y_a = _branch_a(oa_ref[...], post, ona_ref[...], wa_ref)
    y_b = _branch_b(ob_ref[...], post, onb_ref[...], wb_ref)
    x1 = _mix_residual(x_ref[...], y_a, y_b, post, wo_ref)
    h2 = _rms(x1, nffn_ref[...]).astype(BF16)
    u = jnp.dot(h2, wup_ref[:, 0:D_FF], preferred_element_type=F32)
    gf = jnp.dot(h2, wup_ref[:, D_FF:], preferred_element_type=F32)
    acc = cf_ref[:, 0:D_FF] * wcf_ref[0:1, :]
    for i in range(1, CONV_F - 1):
        acc = acc + cf_ref[:, i * D_FF:(i + 1) * D_FF] * wcf_ref[i:i + 1, :]
    acc = acc + u * wcf_ref[CONV_F - 1:CONV_F, :] + bcf_ref[...]
    for i in range(CONV_F - 2):
        cfn_ref[:, i * D_FF:(i + 1) * D_FF] = cf_ref[:, (i + 1) * D_FF:(i + 2) * D_FF]
    cfn_ref[:, (CONV_F - 2) * D_FF:] = u
    y_ref[...] = _ffn_core(x1, acc, gf, wdn_ref, nfin_ref[...])


def _dec_tail_call(xs, oa, ob, post, cf, p):
    n = xs.shape[0]
    return pl.pallas_call(
        _dec_tail_body,
        out_shape=[jax.ShapeDtypeStruct((n, D_MODEL), F32), jax.ShapeDtypeStruct((n, (CONV_F - 1) * D_FF), F32)],
        compiler_params=pltpu.CompilerParams(vmem_limit_bytes=VMEM_LIMIT),
        name="decode_tail",
    )(xs, oa, ob, post, cf, p["onorm_a"], p["onorm_b"], p["norm_ffn"], p["w_conv_f"], p["b_conv_f"],
      p["norm_final"], p["w_a_out"], p["w_b_out"], p["w_o"], p["w_ffn_in"], p["w_ffn_out"])


PROJ_SIZES = (HA * DKA, HA * DKA, HA * DVA, HA * DVA, HA, HA, HB * DKB, HB * DKB, HB * DVB, HB * DVB, GATE_RANK,
              D_MODEL, D_MODEL)
PROJ_ORDER = (0, 1, 2, 6, 7, 8, 4, 5, 10, None, 3, 9, 11, 12)
RELAYOUT_ROWS = 128


def _relayout_body(w_ref, o_ref):
    offs = [0]
    for s in PROJ_SIZES:
        offs.append(offs[-1] + s)
    dst = 0
    for seg in PROJ_ORDER:
        if seg is None:
            width = (-dst) % LANES
            o_ref[:, dst:dst + width] = jnp.zeros((o_ref.shape[0], width), BF16)
        else:
            width = PROJ_SIZES[seg]
            o_ref[:, dst:dst + width] = w_ref[:, offs[seg]:offs[seg] + width].astype(BF16)
        dst += width
    assert dst == W_IN_COLS


def _relayout_w_in(w):
    rows, cols = w.shape
    return pl.pallas_call(
        _relayout_body,
        grid=(rows // RELAYOUT_ROWS,),
        in_specs=[pl.BlockSpec((RELAYOUT_ROWS, cols), lambda i: (i, 0))],
        out_specs=pl.BlockSpec((RELAYOUT_ROWS, W_IN_COLS), lambda i: (i, 0)),
        out_shape=jax.ShapeDtypeStruct((rows, W_IN_COLS), BF16),
        compiler_params=pltpu.CompilerParams(dimension_semantics=("arbitrary",), vmem_limit_bytes=VMEM_LIMIT),
        name="relayout_w_in",
    )(w)


def _prep_params(l, norm_mix, w_in, w_conv_a, a_log, dt_bias, w_gk2, b_gk, onorm_a, onorm_b, w_a_out, w_b_out,
                 w_o, norm_ffn, w_ffn_in, w_conv_f, b_conv_f, w_ffn_out, norm_final):
    n_small = 2 * HA + GATE_RANK
    w1 = _relayout_w_in(w_in[l].astype(BF16))
    w2 = jnp.zeros((LANES, HB * DKB), F32).at[2 * HA:n_small].set(w_gk2[l]).astype(BF16)
    lane_row = lambda v: jnp.zeros((1, LANES), F32).at[0, LANE_G:LANE_G + HA].set(v)
    return dict(
        norm_mix=norm_mix[l][None], w_conv_a=w_conv_a[l], alog_row=lane_row(a_log[l]), dt_row=lane_row(dt_bias[l]),
        b_gk=b_gk[l][None], onorm_a=onorm_a[l][None], onorm_b=onorm_b[l][None], w1=w1, w2=w2,
        w_a_out=w_a_out[l].astype(BF16), w_b_out=w_b_out[l].astype(BF16), w_o=w_o[l].astype(BF16),
        norm_ffn=norm_ffn[l][None], w_ffn_in=w_ffn_in[l].astype(BF16), w_conv_f=w_conv_f[l],
        b_conv_f=b_conv_f[l][None], w_ffn_out=w_ffn_out[l].astype(BF16), norm_final=norm_final[None])


PROMPT_TILE = 256
META_TILE = 2 * CHUNK
FFN_TILE = 512
DECODE_TILE = 8


def kernel(x_prompt, x_sample, state_delta, state_delta_conv, state_gla, state_ffn_conv, meta_tokens, norm_mix, w_in, w_conv_a, a_log, dt_bias, w_gk2, b_gk, onorm_a, onorm_b, w_a_out, w_b_out, w_o, norm_ffn, w_ffn_in, w_conv_f, b_conv_f, w_ffn_out, norm_final):
    assert w_in.shape[0] == 1, "single layer only"
    l = 0
    p = _prep_params(l, norm_mix, w_in, w_conv_a, a_log, dt_bias, w_gk2, b_gk, onorm_a, onorm_b, w_a_out, w_b_out,
                     w_o, norm_ffn, w_ffn_in, w_conv_f, b_conv_f, w_ffn_out, norm_final)
    n_dec = x_sample.shape[0]

    xm = jnp.concatenate([jnp.zeros((META_TILE - N_META, D_MODEL), F32), meta_tokens.astype(F32)], axis=0)[None]
    x1m, sa0, ca0, sb0 = _mixer_call(xm, jnp.zeros((HA, DKA, DVA), F32), jnp.zeros((HB, DKB, DVB), F32),
                                     jnp.zeros((CONV_A - 1, C_A), F32), p, META_TILE)
    u_tail = _ffn_tail_state_call(x1m[0, META_TILE - SUBLANES:], p)
    cf0 = u_tail[SUBLANES - (CONV_F - 1):][None]

    x1, sa_p, ca_p, sb_p = _mixer_call(x_prompt, sa0[0], sb0[0], ca0[0], p, PROMPT_TILE)
    y_prompt, cf_p = _ffn_call(x1, cf0[0], p, FFN_TILE)

    xs = x_sample.reshape(n_dec, D_MODEL)
    cs = jnp.transpose(state_delta_conv[l], (1, 0, 2))
    cfs = state_ffn_conv[l].reshape(n_dec, (CONV_F - 1) * D_FF)
    qkva, beta, g, qkvb, lg, post, cs_new = _dec_head_call(xs, cs, p)
    oa, ob, sa_s, sb_s = _dec_rec_call(qkva, beta, g, qkvb, lg, state_delta[l], state_gla[l], DECODE_TILE)
    y_s, cf_s = _dec_tail_call(xs, oa, ob, post, cfs, p)

    return (y_prompt, y_s.reshape(n_dec, 1, D_MODEL),
            sa_p[None], ca_p[None], sb_p[None], cf_p[None],
            sa_s[None], jnp.transpose(cs_new, (1, 0, 2))[None], sb_s[None],
            cf_s.reshape(1, n_dec, CONV_F - 1, D_FF))
```

```python
import functools

import jax
import jax.numpy as jnp
from jax import lax
from jax.experimental import pallas as pl
from jax.experimental.pallas import tpu as pltpu

F32 = jnp.float32
BF16 = jnp.bfloat16

D_MODEL = 1024
N_META = 16
CHUNK = 64
HA, DKA, DVA = 4, 128, 128
HB, DKB, DVB = 4, 128, 256
CONV_A = 4
C_A = 2 * HA * DKA + HA * DVA
GATE_RANK = 16
GLA_GATE_NORM = 16.0
D_FF = 2816
CONV_F = 3
EPS = 1e-6
LANES = 128
SUBLANES = 8

O_QKVA = 0
O_QB = O_QKVA + C_A
O_KB = O_QB + HB * DKB
O_VB = O_KB + HB * DKB
O_SMALL = O_VB + HB * DVB
O_POST = O_SMALL + LANES
P_ZA = 0
P_RB = P_ZA + HA * DVA
P_GA = P_RB + HB * DVB
P_GB = P_GA + D_MODEL
W_POST = P_GB + D_MODEL
W_IN_COLS = O_POST + W_POST
LANE_BETA = 0
LANE_G = HA

VMEM_LIMIT = 56 * 1024 * 1024


def _dot(a, b):
    return jnp.dot(a.astype(BF16), b.astype(BF16), preferred_element_type=F32)


def _dot_nt(a, b):
    return lax.dot_general(a.astype(BF16), b.astype(BF16), (((1,), (1,)), ((), ())), preferred_element_type=F32)


def _dot_tn(a, b):
    return lax.dot_general(a.astype(BF16), b.astype(BF16), (((0,), (0,)), ((), ())), preferred_element_type=F32)


def _rms(x, w):
    return x * lax.rsqrt(jnp.mean(x * x, axis=-1, keepdims=True) + EPS) * w


def _silu(x):
    return x * jax.nn.sigmoid(x)


def _gelu_tanh(x):
    return 0.5 * x * (1.0 + jnp.tanh(0.7978845608028654 * (x + 0.044715 * (x * x * x))))


def _l2n(x):
    return x * lax.rsqrt(jnp.sum(x * x, axis=-1, keepdims=True) + EPS)


def _iota2(n, m, axis):
    return lax.broadcasted_iota(jnp.int32, (n, m), axis)


def _gates(small, alog_row, dt_row):
    beta = jax.nn.sigmoid(small)
    g = -jnp.exp(alog_row) * jax.nn.softplus(small + dt_row)
    return beta, g


def _qkv_a_post(conv_out):
    act = _silu(conv_out)
    parts = []
    for h in range(HA):
        parts.append(_l2n(act[:, h * DKA:(h + 1) * DKA]) * (DKA ** -0.5))
    for h in range(HA):
        o = HA * DKA + h * DKA
        parts.append(_l2n(act[:, o:o + DKA]))
    parts.append(act[:, 2 * HA * DKA:])
    return jnp.concatenate(parts, axis=1)


def _branch_a(oa, post, onorm_a, wa_ref):
    pa = [_rms(oa[:, h * DVA:(h + 1) * DVA], onorm_a) for h in range(HA)]
    return _dot(jnp.concatenate(pa, axis=1) * _silu(post[:, P_ZA:P_ZA + HA * DVA]), wa_ref[...])


def _branch_b(ob, post, onorm_b, wb_ref):
    pb = [_rms(ob[:, h * DVB:(h + 1) * DVB], onorm_b) for h in range(HB)]
    return _dot(jnp.concatenate(pb, axis=1) * _silu(post[:, P_RB:P_RB + HB * DVB]), wb_ref[...])


def _mix_residual(x, y_a, y_b, post, wo_ref):
    mix = (jax.nn.sigmoid(post[:, P_GA:P_GA + D_MODEL]) * y_a
           + jax.nn.sigmoid(post[:, P_GB:P_GB + D_MODEL]) * y_b)
    return x + _dot(mix, wo_ref[...])


def _cumsum_chunks(x, c):
    rowi = _iota2(x.shape[0], x.shape[1], 0) % c
    sh = 1
    while sh < c:
        x = x + jnp.where(rowi >= sh, pltpu.roll(x, sh, axis=0), 0.0)
        sh *= 2
    return x


def _delta_prepare(qkv_s, beta_s, gcum_s, gt_s, m_s, qk_s, rhs_s, qe_s, kdec_s, tt, c):
    ri = _iota2(c, 2 * c, 0)
    ci = _iota2(c, 2 * c, 1) % c
    first = _iota2(c, 2 * c, 1) < c
    causal = (ri >= ci).astype(F32)
    strict = (ri > ci).astype(F32)
    zeros = jnp.zeros((c, DKA), F32)
    for cp in range(tt // (2 * c)):
        rows2 = slice(2 * cp * c, (2 * cp + 2) * c)
        for h in range(HA):
            n = cp * HA + h
            hs = slice(h * DKA, (h + 1) * DKA)
            lhs, rhs_t, g_cols = [], [], []
            for par in range(2):
                rows = slice((2 * cp + par) * c, (2 * cp + par + 1) * c)
                g_blk = gcum_s[rows, :]
                q = qkv_s[rows, h * DKA:(h + 1) * DKA]
                k = qkv_s[rows, HA * DKA + h * DKA:HA * DKA + (h + 1) * DKA]
                v = qkv_s[rows, 2 * HA * DKA + h * DVA:2 * HA * DKA + (h + 1) * DVA]
                g_col = g_blk[:, LANE_G + h:LANE_G + h + 1]
                beta_col = beta_s[rows, LANE_BETA + h:LANE_BETA + h + 1]
                eg = jnp.exp(g_col)
                kb = k * beta_col
                lhs.append(jnp.concatenate([kb, q], axis=0))
                rhs_t.append(jnp.concatenate([k, zeros] if par == 0 else [zeros, k], axis=1))
                g_cols.append(g_col)
                rhs_s[rows, h * 2 * DVA:(h + 1) * 2 * DVA] = jnp.concatenate([v * beta_col, kb * eg], axis=1)
                qe_s[rows, hs] = q * eg
                kdec_s[rows, hs] = k * jnp.exp(g_col[c - 1:c, :] - g_col)
            kq = _dot_nt(jnp.concatenate(lhs, axis=1), jnp.concatenate(rhs_t, axis=0))
            g_row = gt_s[LANE_G + h:LANE_G + h + 1, rows2]
            decay = jnp.exp(jnp.minimum(jnp.where(first, g_cols[0], g_cols[1]) - g_row, 0.0))
            m_s[n] = kq[0:c] * (decay * strict)
            qk_s[n] = kq[c:] * (decay * causal)


def _inverse_stages(m_s, p_s, pw_s, n_inst, c):
    ri = _iota2(c, 2 * c, 0)
    lane = _iota2(c, 2 * c, 1)
    ci = lane % c
    left = (lane < c).astype(F32)
    right = 1.0 - left

    def bdiag(xp):
        return jnp.concatenate([xp * left, xp * right], axis=0)

    eye = (ri == ci).astype(F32)
    base = min(16, c)
    same = (ri // base) == (ci // base)
    neg_same = jnp.where(same, -1.0, 0.0)
    for n in range(n_inst):
        a = m_s[n] * neg_same
        pw_s[n] = _dot(a, bdiag(a))
        p_s[n] = eye + a
    k = 2
    while k < base:
        for n in range(n_inst):
            p = p_s[n]
            pw = pw_s[n]
            bd = bdiag(pw).astype(BF16)
            p_s[n] = p + _dot(p, bd)
            if 2 * k < base:
                pw_s[n] = _dot(pw, bd)
        k *= 2
    size = base
    while size < c:
        nxt = size * 2
        same_nxt = (ri // nxt) == (ci // nxt)
        off = (same_nxt & jnp.logical_not(same)).astype(F32)
        for n in range(n_inst):
            pw_s[n] = _dot(p_s[n], bdiag(m_s[n] * off))
        for n in range(n_inst):
            p = p_s[n]
            p_s[n] = p - _dot(pw_s[n], bdiag(p))
        same = same_nxt
        size = nxt


def _level_ref_rows(bc, half, c):
    w = bc.shape[1]
    parts = []
    if half >= SUBLANES // 2:
        for blk in range(c // (2 * half)):
            mrow = blk * 2 * half + half
            parts.append(jnp.broadcast_to(bc[mrow:mrow + 1, :], (2 * half, w)))
    else:
        sub = _iota2(SUBLANES, w, 0)
        for grp in range(c // SUBLANES):
            acc = None
            for blk in range(SUBLANES // (2 * half)):
                mrow = grp * SUBLANES + blk * 2 * half + half
                b = jnp.broadcast_to(bc[mrow:mrow + 1, :], (SUBLANES, w))
                acc = b if acc is None else jnp.where(sub >= blk * 2 * half, b, acc)
            parts.append(acc)
    return jnp.concatenate(parts, axis=0)


def _gla_prepare_steps(qkb_s, vb_s, bc_s, attn_s, ob_s, pkv_s, qeb_s, tt, c):
    ri = _iota2(c, 2 * c, 0)
    ci = _iota2(c, 2 * c, 1) % c
    first = _iota2(c, 2 * c, 1) < c
    n_ch = tt // c
    zeros_k = jnp.zeros((c, DKB), BF16)
    zeros_v = jnp.zeros((c, DVB), BF16)

    def operands(ch, h):
        rows = slice(ch * c, (ch + 1) * c)
        q = qkb_s[rows, h * DKB:(h + 1) * DKB]
        k = qkb_s[rows, HB * DKB + h * DKB:HB * DKB + (h + 1) * DKB]
        return rows, q, k

    def diag():
        for cp in range(n_ch // 2):
            for h in range(HB):
                d = []
                for par in range(2):
                    _, q, k = operands(2 * cp + par, h)
                    d.append(jnp.sum(q * k, axis=-1, keepdims=True))
                attn_s[cp * HB + h] = jnp.where(ri == ci, jnp.where(first, d[0], d[1]), 0.0)

    def level(half):
        valid = (((ri // (2 * half)) == (ci // (2 * half))) & ((ri % (2 * half)) >= half)
                 & ((ci % (2 * half)) < half)).astype(F32)
        for cp in range(n_ch // 2):
            e_all = []
            for par in range(2):
                bc_all = bc_s[(2 * cp + par) * c:(2 * cp + par + 1) * c, :]
                e_all.append(jnp.exp(-jnp.abs(bc_all - _level_ref_rows(bc_all, half, c))))
            for h in range(HB):
                qs, ks = [], []
                for par in range(2):
                    _, q, k = operands(2 * cp + par, h)
                    e = e_all[par][:, h * DKB:(h + 1) * DKB]
                    qs.append((q * e).astype(BF16))
                    ks.append((k * e).astype(BF16))
                k_bd = jnp.concatenate([jnp.concatenate([ks[0], zeros_k], axis=1),
                                        jnp.concatenate([zeros_k, ks[1]], axis=1)], axis=0)
                attn_s[cp * HB + h] += _dot_nt(jnp.concatenate(qs, axis=1), k_bd) * valid

    def finish():
        for cp in range(n_ch // 2):
            for h in range(HB):
                vs = []
                for par in range(2):
                    rows, q, k = operands(2 * cp + par, h)
                    v = vb_s[rows, h * DVB:(h + 1) * DVB]
                    bc = bc_s[rows, h * DKB:(h + 1) * DKB]
                    vs.append(v.astype(BF16))
                    pkv_s[(2 * cp + par) * HB + h] = _dot_tn(k * jnp.exp(bc[c - 1:c, :] - bc), v)
                    qeb_s[rows, h * DKB:(h + 1) * DKB] = q * jnp.exp(bc)
                v_bd = jnp.concatenate([jnp.concatenate([vs[0], zeros_v], axis=1),
                                        jnp.concatenate([zeros_v, vs[1]], axis=1)], axis=0)
                o2 = _dot(attn_s[cp * HB + h], v_bd)
                ob_s[2 * cp * c:(2 * cp + 1) * c, h * DVB:(h + 1) * DVB] = o2[:, 0:DVB]
                ob_s[(2 * cp + 1) * c:(2 * cp + 2) * c, h * DVB:(h + 1) * DVB] = o2[:, DVB:]

    steps = [diag]
    half = c // 2
    while half >= 1:
        steps.append(functools.partial(level, half))
        half //= 2
    steps.append(finish)
    return steps


def _mixer_body(x_ref, sa0_ref, sb0_ref, ca0_ref, nmix_ref, wca_ref, alog_ref, dt_ref, bgk_ref, ona_ref, onb_ref,
                w1_ref, w2_ref, wa_ref, wb_ref, wo_ref,
                x1_ref, sa_ref, ca_ref, sb_ref,
                cbuf, hb_s, post_s, qkv_s, beta_s, gcum_s, gt_s, qkb_s, vb_s, bc_s, oa_s, ob_s,
                m_s, p_s, pw_s, qk_s, rhs_s, uw_s, qe_s, kdec_s, attn_s, pkv_s, qeb_s, gated_s, mix_s, ktu_s, ktw_s, *, tt, c):
    t_idx = pl.program_id(1)
    n_ch = tt // c

    @pl.when(t_idx == 0)
    def _():
        sa_ref[...] = sa0_ref[...]
        sb_ref[...] = sb0_ref[...]
        cbuf[SUBLANES - (CONV_A - 1):SUBLANES, :] = ca0_ref[...]

    rb = min(tt, 64)
    for r in range(tt // rb):
        rr = slice(r * rb, (r + 1) * rb)
        hb_s[rr, :] = _rms(x_ref[rr, :], nmix_ref[...]).astype(BF16)
    hb = hb_s[...]
    cbuf[SUBLANES:SUBLANES + tt, :] = jnp.dot(hb, w1_ref[:, O_QKVA:O_QKVA + C_A], preferred_element_type=F32)
    base = SUBLANES - (CONV_A - 1)
    n_blk = C_A // LANES
    post_w = 2 * LANES
    n_post = W_POST // post_w
    for j in range(max(n_blk, n_post)):
        if j < n_post:
            pc = slice(j * post_w, (j + 1) * post_w)
            post_s[:, pc] = jnp.dot(hb, w1_ref[:, O_POST + j * post_w:O_POST + (j + 1) * post_w],
                                    preferred_element_type=F32)
        if j < n_blk:
            cols = slice(j * LANES, (j + 1) * LANES)
            full = cbuf[:, cols]
            shifted = lambda k: full[SUBLANES:] if k == 0 else pltpu.roll(full, k, axis=0)[SUBLANES:]
            acc = shifted(CONV_A - 1) * wca_ref[0:1, cols]
            for i in range(1, CONV_A):
                acc = acc + shifted(CONV_A - 1 - i) * wca_ref[i:i + 1, cols]
            act = _silu(acc)
            if j < HA:
                act = _l2n(act) * (DKA ** -0.5)
            elif j < 2 * HA:
                act = _l2n(act)
            qkv_s[:, cols] = act
    tail = cbuf[tt + base:tt + SUBLANES, :]
    cbuf[base:SUBLANES, :] = tail
    ca_ref[...] = tail
    small = jnp.dot(hb, w1_ref[:, O_SMALL:O_SMALL + LANES], preferred_element_type=F32)
    beta, g = _gates(small, alog_ref[...], dt_ref[...])
    beta_s[...] = beta
    g_cum = _cumsum_chunks(g, c)
    gcum_s[...] = g_cum
    gt_s[...] = g_cum.T

    _delta_prepare(qkv_s, beta_s, gcum_s, gt_s, m_s, qk_s, rhs_s, qe_s, kdec_s, tt, c)
    n_pair = (n_ch // 2) * HA
    _inverse_stages(m_s, p_s, pw_s, n_pair, c)
    zeros_uw = jnp.zeros((c, 2 * DVA), BF16)

    def pair_bdiag(ref, cp, cols):
        x_e = ref[2 * cp * c:(2 * cp + 1) * c, cols].astype(BF16)
        x_o = ref[(2 * cp + 1) * c:(2 * cp + 2) * c, cols].astype(BF16)
        return jnp.concatenate([jnp.concatenate([x_e, zeros_uw], axis=1),
                                jnp.concatenate([zeros_uw, x_o], axis=1)], axis=0)

    for cp in range(n_ch // 2):
        rows_e = slice(2 * cp * c, (2 * cp + 1) * c)
        rows_o = slice((2 * cp + 1) * c, (2 * cp + 2) * c)
        for h in range(HA):
            cols = slice(h * 2 * DVA, (h + 1) * 2 * DVA)
            uw2 = _dot(p_s[cp * HA + h], pair_bdiag(rhs_s, cp, cols))
            uw_s[rows_e, cols] = uw2[:, 0:2 * DVA]
            uw_s[rows_o, cols] = uw2[:, 2 * DVA:]
    for cp in range(n_ch // 2):
        for h in range(HA):
            hs = slice(h * DKA, (h + 1) * DKA)
            cols = slice(h * 2 * DVA, (h + 1) * 2 * DVA)
            att2 = _dot(qk_s[cp * HA + h], pair_bdiag(uw_s, cp, cols))
            for par in range(2):
                ch = 2 * cp + par
                rows = slice(ch * c, (ch + 1) * c)
                att_uw = att2[:, par * 2 * DVA:(par + 1) * 2 * DVA]
                oa_s[rows, h * DVA:(h + 1) * DVA] = att_uw[:, 0:DVA]
                qe_s[rows, hs] = qe_s[rows, hs] - att_uw[:, DVA:]
                kt_uw = _dot_tn(kdec_s[rows, hs], uw_s[rows, cols])
                ktu_s[ch * HA + h] = kt_uw[:, 0:DVA]
                ktw_s[ch * HA + h] = kt_uw[:, DVA:]

    small_b = small.astype(BF16)
    for h in range(HB):
        hs = slice(h * DKB, (h + 1) * DKB)
        lg_pre = jnp.dot(small_b, w2_ref[:, hs], preferred_element_type=F32) + bgk_ref[:, hs]
        bc_s[:, hs] = _cumsum_chunks(jax.nn.log_sigmoid(lg_pre) / GLA_GATE_NORM, c)
    qkb_s[:, 0:HB * DKB] = jnp.dot(hb, w1_ref[:, O_QB:O_QB + HB * DKB], preferred_element_type=F32) * (DKB ** -0.5)
    qkb_s[:, HB * DKB:] = jnp.dot(hb, w1_ref[:, O_KB:O_KB + HB * DKB], preferred_element_type=F32)
    vb_s[...] = jnp.dot(hb, w1_ref[:, O_VB:O_VB + HB * DVB], preferred_element_type=F32)

    for step in _gla_prepare_steps(qkb_s, vb_s, bc_s, attn_s, ob_s, pkv_s, qeb_s, tt, c):
        step()

    for ch in range(n_ch):
        rows = slice(ch * c, (ch + 1) * c)
        last = slice(ch * c + c - 1, ch * c + c)
        for h in range(HA):
            n = ch * HA + h
            hs = slice(h * DKA, (h + 1) * DKA)
            s = sa_ref[h]
            prod = _dot(jnp.concatenate([ktw_s[n], qe_s[rows, hs]], axis=0), s)
            oa_s[rows, h * DVA:(h + 1) * DVA] += prod[DKA:]
            g_last = gcum_s[last, LANE_G + h:LANE_G + h + 1]
            sa_ref[h] = s * jnp.exp(g_last) + (ktu_s[n] - prod[0:DKA])

    for h in range(HA):
        hs = slice(h * DVA, (h + 1) * DVA)
        gated_s[:, hs] = (_rms(oa_s[:, hs], ona_ref[...]) * _silu(post_s[:, P_ZA + h * DVA:P_ZA + (h + 1) * DVA])
                          ).astype(BF16)

    for ch in range(n_ch):
        rows = slice(ch * c, (ch + 1) * c)
        last = slice(ch * c + c - 1, ch * c + c)
        for h in range(HB):
            s = sb_ref[h]
            ob_s[rows, h * DVB:(h + 1) * DVB] += _dot(qeb_s[rows, h * DKB:(h + 1) * DKB], s)
            e_col = jnp.exp(bc_s[last, h * DKB:(h + 1) * DKB]).T
            sb_ref[h] = s * e_col + pkv_s[ch * HB + h]

    for h in range(HB):
        hs = slice(h * DVB, (h + 1) * DVB)
        gated_s[:, HA * DVA + h * DVB:HA * DVA + (h + 1) * DVB] = (
            _rms(ob_s[:, hs], onb_ref[...]) * _silu(post_s[:, P_RB + h * DVB:P_RB + (h + 1) * DVB])).astype(BF16)
    out_w = 2 * LANES
    ga = gated_s[:, 0:HA * DVA]
    gb = gated_s[:, HA * DVA:]
    for n in range(D_MODEL // out_w):
        ns = slice(n * out_w, (n + 1) * out_w)
        y_a = jnp.dot(ga, wa_ref[:, ns], preferred_element_type=F32)
        y_b = jnp.dot(gb, wb_ref[:, ns], preferred_element_type=F32)
        mix_s[:, ns] = (jax.nn.sigmoid(post_s[:, P_GA + n * out_w:P_GA + (n + 1) * out_w]) * y_a
                        + jax.nn.sigmoid(post_s[:, P_GB + n * out_w:P_GB + (n + 1) * out_w]) * y_b).astype(BF16)
    mix = mix_s[...]
    for n in range(D_MODEL // out_w):
        ns = slice(n * out_w, (n + 1) * out_w)
        x1_ref[:, ns] = x_ref[:, ns] + jnp.dot(mix, wo_ref[:, ns], preferred_element_type=F32)


def _const_spec(shape):
    nd = len(shape)
    return pl.BlockSpec(shape, lambda *_: (0,) * nd, pipeline_mode=pl.Buffered(1))


def _mixer_call(x, sa0, sb0, ca0, p, tt):
    b, t, _ = x.shape
    c = CHUNK
    nt = t // tt
    n_inst = (tt // c) * HA
    row = lambda n: _const_spec((1, n))
    in_specs = [
        pl.BlockSpec((None, tt, D_MODEL), lambda i, j: (i, j, 0)),
        _const_spec((HA, DKA, DVA)), _const_spec((HB, DKB, DVB)), _const_spec((CONV_A - 1, C_A)),
        row(D_MODEL), _const_spec((CONV_A, C_A)), row(LANES), row(LANES), row(HB * DKB), row(DVA), row(DVB),
        _const_spec((D_MODEL, W_IN_COLS)), _const_spec((LANES, HB * DKB)),
        _const_spec((HA * DVA, D_MODEL)), _const_spec((HB * DVB, D_MODEL)), _const_spec((D_MODEL, D_MODEL)),
    ]
    out_specs = [
        pl.BlockSpec((None, tt, D_MODEL), lambda i, j: (i, j, 0)),
        pl.BlockSpec((None, HA, DKA, DVA), lambda i, j: (i, 0, 0, 0)),
        pl.BlockSpec((None, CONV_A - 1, C_A), lambda i, j: (i, 0, 0)),
        pl.BlockSpec((None, HB, DKB, DVB), lambda i, j: (i, 0, 0, 0)),
    ]
    out_shape = [
        jax.ShapeDtypeStruct((b, t, D_MODEL), F32),
        jax.ShapeDtypeStruct((b, HA, DKA, DVA), F32),
        jax.ShapeDtypeStruct((b, CONV_A - 1, C_A), F32),
        jax.ShapeDtypeStruct((b, HB, DKB, DVB), F32),
    ]
    assert (tt // c) % 2 == 0, "mixer A handles chunks in pairs"
    pairs = lambda: pltpu.VMEM((n_inst // 2, c, 2 * c), F32)
    scratch = [
        pltpu.VMEM((tt + SUBLANES, C_A), F32),
        pltpu.VMEM((tt, D_MODEL), BF16),
        pltpu.VMEM((tt, W_POST), F32),
        pltpu.VMEM((tt, C_A), F32),
        pltpu.VMEM((tt, LANES), F32),
        pltpu.VMEM((tt, LANES), F32),
        pltpu.VMEM((LANES, tt), F32),
        pltpu.VMEM((tt, 2 * HB * DKB), F32),
        pltpu.VMEM((tt, HB * DVB), F32),
        pltpu.VMEM((tt, HB * DKB), F32),
        pltpu.VMEM((tt, HA * DVA), F32),
        pltpu.VMEM((tt, HB * DVB), F32),
        pairs(), pairs(), pairs(), pairs(),
        pltpu.VMEM((tt, HA * 2 * DVA), F32),
        pltpu.VMEM((tt, HA * 2 * DVA), F32),
        pltpu.VMEM((tt, HA * DKA), F32),
        pltpu.VMEM((tt, HA * DKA), F32),
        pairs(),
        pltpu.VMEM((n_inst, DKB, DVB), F32),
        pltpu.VMEM((tt, HB * DKB), F32),
        pltpu.VMEM((tt, HA * DVA + HB * DVB), BF16),
        pltpu.VMEM((tt, D_MODEL), BF16),
        pltpu.VMEM((n_inst, DKA, DVA), F32),
        pltpu.VMEM((n_inst, DKA, DVA), F32),
    ]
    return pl.pallas_call(
        functools.partial(_mixer_body, tt=tt, c=c),
        grid=(b, nt), in_specs=in_specs, out_specs=out_specs, out_shape=out_shape, scratch_shapes=scratch,
        compiler_params=pltpu.CompilerParams(dimension_semantics=("arbitrary", "arbitrary"),
                                             vmem_limit_bytes=VMEM_LIMIT),
        name="mixer",
    )(x, sa0, sb0, ca0, p["norm_mix"], p["w_conv_a"], p["alog_row"], p["dt_row"], p["b_gk"], p["onorm_a"],
      p["onorm_b"], p["w1"], p["w2"], p["w_a_out"], p["w_b_out"], p["w_o"])


def _ffn_core(x1, u_conv, gf, wdn_ref, nfin):
    act = _gelu_tanh(u_conv) * gf
    x2 = x1 + _dot(act, wdn_ref[...])
    return _rms(x2, nfin)


def _ffn_body(x1_ref, cf0_ref, nffn_ref, wcf_ref, bcf_ref, nfin_ref, wup_ref, wdn_ref,
              y_ref, cf_ref, ubuf, *, tt):
    t_idx = pl.program_id(1)
    base = SUBLANES - (CONV_F - 1)

    @pl.when(t_idx == 0)
    def _():
        ubuf[base:SUBLANES, :] = cf0_ref[...]

    x1 = x1_ref[...]
    h2 = _rms(x1, nffn_ref[...]).astype(BF16)
    ubuf[SUBLANES:SUBLANES + tt, :] = jnp.dot(h2, wup_ref[:, 0:D_FF], preferred_element_type=F32)
    gf = jnp.dot(h2, wup_ref[:, D_FF:], preferred_element_type=F32)
    acc = ubuf[base:base + tt, :] * wcf_ref[0:1, :]
    for i in range(1, CONV_F):
        acc = acc + ubuf[base + i:base + i + tt, :] * wcf_ref[i:i + 1, :]
    acc = acc + bcf_ref[...]
    tail = ubuf[tt + base:tt + SUBLANES, :]
    ubuf[base:SUBLANES, :] = tail
    cf_ref[...] = tail
    y_ref[...] = _ffn_core(x1, acc, gf, wdn_ref, nfin_ref[...])


def _ffn_tail_state_body(x1_ref, nffn_ref, wup_ref, u_ref):
    h2 = _rms(x1_ref[...], nffn_ref[...]).astype(BF16)
    u_ref[...] = jnp.dot(h2, wup_ref[...], preferred_element_type=F32)


def _ffn_tail_state_call(x1_rows, p):
    return pl.pallas_call(
        _ffn_tail_state_body,
        grid=(1,),
        in_specs=[pl.BlockSpec((SUBLANES, D_MODEL), lambda i: (0, 0)), pl.BlockSpec((1, D_MODEL), lambda i: (0, 0)),
                  pl.BlockSpec((D_MODEL, D_FF), lambda i: (0, 0))],
        out_specs=pl.BlockSpec((SUBLANES, D_FF), lambda i: (0, 0)),
        out_shape=jax.ShapeDtypeStruct((SUBLANES, D_FF), F32),
        compiler_params=pltpu.CompilerParams(dimension_semantics=("arbitrary",), vmem_limit_bytes=VMEM_LIMIT),
        name="meta_ffn_state",
    )(x1_rows, p["norm_ffn"], p["w_ffn_in"])


def _ffn_call(x1, cf0, p, tt):
    b, t, _ = x1.shape
    nt = t // tt
    row = lambda n: _const_spec((1, n))
    in_specs = [
        pl.BlockSpec((None, tt, D_MODEL), lambda i, j: (i, j, 0)),
        _const_spec((CONV_F - 1, D_FF)), row(D_MODEL), _const_spec((CONV_F, D_FF)), row(D_FF), row(D_MODEL),
        _const_spec((D_MODEL, 2 * D_FF)), _const_spec((D_FF, D_MODEL)),
    ]
    out_specs = [
        pl.BlockSpec((None, tt, D_MODEL), lambda i, j: (i, j, 0)),
        pl.BlockSpec((None, CONV_F - 1, D_FF), lambda i, j: (i, 0, 0)),
    ]
    out_shape = [jax.ShapeDtypeStruct((b, t, D_MODEL), F32), jax.ShapeDtypeStruct((b, CONV_F - 1, D_FF), F32)]
    return pl.pallas_call(
        functools.partial(_ffn_body, tt=tt),
        grid=(b, nt), in_specs=in_specs, out_specs=out_specs, out_shape=out_shape,
        scratch_shapes=[pltpu.VMEM((tt + SUBLANES, D_FF), F32)],
        compiler_params=pltpu.CompilerParams(dimension_semantics=("arbitrary", "arbitrary"),
                                             vmem_limit_bytes=VMEM_LIMIT),
        name="convffn",
    )(x1, cf0, p["norm_ffn"], p["w_conv_f"], p["b_conv_f"], p["norm_final"], p["w_ffn_in"], p["w_ffn_out"])


def _dec_head_body(x_ref, cs_ref, nmix_ref, wca_ref, alog_ref, dt_ref, bgk_ref, w1_ref, w2_ref,
                   qkva_ref, beta_ref, g_ref, qkvb_ref, lg_ref, post_ref, csn_ref):
    x = x_ref[...]
    hb = _rms(x, nmix_ref[...]).astype(BF16)
    pre = jnp.dot(hb, w1_ref[:, O_QKVA:O_QKVA + C_A], preferred_element_type=F32)
    acc = cs_ref[0] * wca_ref[0:1, :]
    for i in range(1, CONV_A - 1):
        acc = acc + cs_ref[i] * wca_ref[i:i + 1, :]
    acc = acc + pre * wca_ref[CONV_A - 1:CONV_A, :]
    for i in range(CONV_A - 2):
        csn_ref[i] = cs_ref[i + 1]
    csn_ref[CONV_A - 2] = pre
    qkva_ref[...] = _qkv_a_post(acc)
    small = jnp.dot(hb, w1_ref[:, O_SMALL:O_SMALL + LANES], preferred_element_type=F32)
    beta, g = _gates(small, alog_ref[...], dt_ref[...])
    beta_ref[...] = beta
    g_ref[...] = g
    lg_ref[...] = jax.nn.log_sigmoid(_dot(small, w2_ref[...]) + bgk_ref[...]) / GLA_GATE_NORM
    qkb = jnp.dot(hb, w1_ref[:, O_QB:O_QB + 2 * HB * DKB], preferred_element_type=F32)
    qkvb_ref[:, 0:HB * DKB] = qkb[:, 0:HB * DKB] * (DKB ** -0.5)
    qkvb_ref[:, HB * DKB:2 * HB * DKB] = qkb[:, HB * DKB:]
    qkvb_ref[:, 2 * HB * DKB:] = jnp.dot(hb, w1_ref[:, O_VB:O_VB + HB * DVB], preferred_element_type=F32)
    post_ref[...] = jnp.dot(hb, w1_ref[:, O_POST:O_POST + W_POST], preferred_element_type=F32)


def _dec_head_call(xs, cs, p):
    n = xs.shape[0]
    shapes = [(n, C_A), (n, LANES), (n, LANES), (n, 2 * HB * DKB + HB * DVB), (n, HB * DKB), (n, W_POST),
              (CONV_A - 1, n, C_A)]
    return pl.pallas_call(
        _dec_head_body,
        out_shape=[jax.ShapeDtypeStruct(s, F32) for s in shapes],
        compiler_params=pltpu.CompilerParams(vmem_limit_bytes=VMEM_LIMIT),
        name="decode_head",
    )(xs, cs, p["norm_mix"], p["w_conv_a"], p["alog_row"], p["dt_row"], p["b_gk"], p["w1"], p["w2"])


def _dec_rec_body(qkva_ref, beta_ref, g_ref, qkvb_ref, lg_ref, sa_ref, sb_ref,
                  oa_ref, ob_ref, san_ref, sbn_ref, *, tb):
    beta = beta_ref[...]
    eg = jnp.exp(g_ref[...])
    for h in range(HA):
        q_rows = qkva_ref[:, h * DKA:(h + 1) * DKA].astype(BF16)
        k_t = qkva_ref[:, HA * DKA + h * DKA:HA * DKA + (h + 1) * DKA].T
        for j in range(tb):
            k_col = k_t[:, j:j + 1]
            s = sa_ref[j, h] * eg[j:j + 1, LANE_G + h:LANE_G + h + 1]
            v = qkva_ref[j:j + 1, 2 * HA * DKA + h * DVA:2 * HA * DKA + (h + 1) * DVA]
            err = (v - jnp.sum(k_col * s, axis=0, keepdims=True)) * beta[j:j + 1, LANE_BETA + h:LANE_BETA + h + 1]
            s = s + k_col * err
            san_ref[j, h] = s
            o_all = jnp.dot(q_rows, s.astype(BF16), preferred_element_type=F32)
            oa_ref[j:j + 1, h * DVA:(h + 1) * DVA] = o_all[j:j + 1, :]
    for h in range(HB):
        q_rows = qkvb_ref[:, h * DKB:(h + 1) * DKB].astype(BF16)
        k_t = qkvb_ref[:, HB * DKB + h * DKB:HB * DKB + (h + 1) * DKB].T
        d_t = jnp.exp(lg_ref[:, h * DKB:(h + 1) * DKB]).T
        for j in range(tb):
            v = qkvb_ref[j:j + 1, 2 * HB * DKB + h * DVB:2 * HB * DKB + (h + 1) * DVB]
            s = sb_ref[j, h] * d_t[:, j:j + 1] + k_t[:, j:j + 1] * v
            sbn_ref[j, h] = s
            o_all = jnp.dot(q_rows, s.astype(BF16), preferred_element_type=F32)
            ob_ref[j:j + 1, h * DVB:(h + 1) * DVB] = o_all[j:j + 1, :]


def _dec_rec_call(qkva, beta, g, qkvb, lg, sa, sb, tb):
    n = qkva.shape[0]
    rows = lambda w: pl.BlockSpec((tb, w), lambda i: (i, 0))
    in_specs = [rows(C_A), rows(LANES), rows(LANES), rows(2 * HB * DKB + HB * DVB), rows(HB * DKB),
                pl.BlockSpec((tb, HA, DKA, DVA), lambda i: (i, 0, 0, 0)),
                pl.BlockSpec((tb, HB, DKB, DVB), lambda i: (i, 0, 0, 0))]
    out_specs = [rows(HA * DVA), rows(HB * DVB),
                 pl.BlockSpec((tb, HA, DKA, DVA), lambda i: (i, 0, 0, 0)),
                 pl.BlockSpec((tb, HB, DKB, DVB), lambda i: (i, 0, 0, 0))]
    out_shape = [jax.ShapeDtypeStruct((n, HA * DVA), F32), jax.ShapeDtypeStruct((n, HB * DVB), F32),
                 jax.ShapeDtypeStruct(sa.shape, F32), jax.ShapeDtypeStruct(sb.shape, F32)]
    return pl.pallas_call(
        functools.partial(_dec_rec_body, tb=tb),
        grid=(n // tb,), in_specs=in_specs, out_specs=out_specs, out_shape=out_shape,
        compiler_params=pltpu.CompilerParams(dimension_semantics=("arbitrary",), vmem_limit_bytes=VMEM_LIMIT),
        name="decode_recurrence",
    )(qkva, beta, g, qkvb, lg, sa, sb)


def _dec_tail_body(x_ref, oa_ref, ob_ref, post_ref, cf_ref, ona_ref, onb_ref, nffn_ref, wcf_ref, bcf_ref, nfin_ref,
                   wa_ref, wb_ref, wo_ref, wup_ref, wdn_ref, y_ref, cfn_ref):
    post = post_ref[...]
    y_a = _branch_a(oa_ref[...], post, ona_ref[...], wa_ref)
    y_b = _branch_b(ob_ref[...], post, onb_ref[...], wb_ref)
    x1 = _mix_residual(x_ref[...], y_a, y_b, post, wo_ref)
    h2 = _rms(x1, nffn_ref[...]).astype(BF16)
    u = jnp.dot(h2, wup_ref[:, 0:D_FF], preferred_element_type=F32)
    gf = jnp.dot(h2, wup_ref[:, D_FF:], preferred_element_type=F32)
    acc = cf_ref[:, 0:D_FF] * wcf_ref[0:1, :]
    for i in range(1, CONV_F - 1):
        acc = acc + cf_ref[:, i * D_FF:(i + 1) * D_FF] * wcf_ref[i:i + 1, :]
    acc = acc + u * wcf_ref[CONV_F - 1:CONV_F, :] + bcf_ref[...]
    for i in range(CONV_F - 2):
        cfn_ref[:, i * D_FF:(i + 1) * D_FF] = cf_ref[:, (i + 1) * D_FF:(i + 2) * D_FF]
    cfn_ref[:, (CONV_F - 2) * D_FF:] = u
    y_ref[...] = _ffn_core(x1, acc, gf, wdn_ref, nfin_ref[...])


def _dec_tail_call(xs, oa, ob, post, cf, p):
    n = xs.shape[0]
    return pl.pallas_call(
        _dec_tail_body,
        out_shape=[jax.ShapeDtypeStruct((n, D_MODEL), F32), jax.ShapeDtypeStruct((n, (CONV_F - 1) * D_FF), F32)],
        compiler_params=pltpu.CompilerParams(vmem_limit_bytes=VMEM_LIMIT),
        name="decode_tail",
    )(xs, oa, ob, post, cf, p["onorm_a"], p["onorm_b"], p["norm_ffn"], p["w_conv_f"], p["b_conv_f"],
      p["norm_final"], p["w_a_out"], p["w_b_out"], p["w_o"], p["w_ffn_in"], p["w_ffn_out"])


PROJ_SIZES = (HA * DKA, HA * DKA, HA * DVA, HA * DVA, HA, HA, HB * DKB, HB * DKB, HB * DVB, HB * DVB, GATE_RANK,
              D_MODEL, D_MODEL)
PROJ_ORDER = (0, 1, 2, 6, 7, 8, 4, 5, 10, None, 3, 9, 11, 12)
RELAYOUT_ROWS = 128


def _relayout_body(w_ref, o_ref):
    offs = [0]
    for s in PROJ_SIZES:
        offs.append(offs[-1] + s)
    dst = 0
    for seg in PROJ_ORDER:
        if seg is None:
            width = (-dst) % LANES
            o_ref[:, dst:dst + width] = jnp.zeros((o_ref.shape[0], width), BF16)
        else:
            width = PROJ_SIZES[seg]
            o_ref[:, dst:dst + width] = w_ref[:, offs[seg]:offs[seg] + width].astype(BF16)
        dst += width
    assert dst == W_IN_COLS


def _relayout_w_in(w):
    rows, cols = w.shape
    return pl.pallas_call(
        _relayout_body,
        grid=(rows // RELAYOUT_ROWS,),
        in_specs=[pl.BlockSpec((RELAYOUT_ROWS, cols), lambda i: (i, 0))],
        out_specs=pl.BlockSpec((RELAYOUT_ROWS, W_IN_COLS), lambda i: (i, 0)),
        out_shape=jax.ShapeDtypeStruct((rows, W_IN_COLS), BF16),
        compiler_params=pltpu.CompilerParams(dimension_semantics=("arbitrary",), vmem_limit_bytes=VMEM_LIMIT),
        name="relayout_w_in",
    )(w)


def _prep_params(l, norm_mix, w_in, w_conv_a, a_log, dt_bias, w_gk2, b_gk, onorm_a, onorm_b, w_a_out, w_b_out,
                 w_o, norm_ffn, w_ffn_in, w_conv_f, b_conv_f, w_ffn_out, norm_final):
    n_small = 2 * HA + GATE_RANK
    w1 = _relayout_w_in(w_in[l].astype(BF16))
    w2 = jnp.zeros((LANES, HB * DKB), F32).at[2 * HA:n_small].set(w_gk2[l]).astype(BF16)
    lane_row = lambda v: jnp.zeros((1, LANES), F32).at[0, LANE_G:LANE_G + HA].set(v)
    return dict(
        norm_mix=norm_mix[l][None], w_conv_a=w_conv_a[l], alog_row=lane_row(a_log[l]), dt_row=lane_row(dt_bias[l]),
        b_gk=b_gk[l][None], onorm_a=onorm_a[l][None], onorm_b=onorm_b[l][None], w1=w1, w2=w2,
        w_a_out=w_a_out[l].astype(BF16), w_b_out=w_b_out[l].astype(BF16), w_o=w_o[l].astype(BF16),
        norm_ffn=norm_ffn[l][None], w_ffn_in=w_ffn_in[l].astype(BF16), w_conv_f=w_conv_f[l],
        b_conv_f=b_conv_f[l][None], w_ffn_out=w_ffn_out[l].astype(BF16), norm_final=norm_final[None])


PROMPT_TILE = 256
META_TILE = 2 * CHUNK
FFN_TILE = 512
DECODE_TILE = 8


def kernel(x_prompt, x_sample, state_delta, state_delta_conv, state_gla, state_ffn_conv, meta_tokens, norm_mix, w_in, w_conv_a, a_log, dt_bias, w_gk2, b_gk, onorm_a, onorm_b, w_a_out, w_b_out, w_o, norm_ffn, w_ffn_in, w_conv_f, b_conv_f, w_ffn_out, norm_final):
    assert w_in.shape[0] == 1, "single layer only"
    l = 0
    p = _prep_params(l, norm_mix, w_in, w_conv_a, a_log, dt_bias, w_gk2, b_gk, onorm_a, onorm_b, w_a_out, w_b_out,
                     w_o, norm_ffn, w_ffn_in, w_conv_f, b_conv_f, w_ffn_out, norm_final)
    n_dec = x_sample.shape[0]

    xm = jnp.concatenate([jnp.zeros((META_TILE - N_META, D_MODEL), F32), meta_tokens.astype(F32)], axis=0)[None]
    x1m, sa0, ca0, sb0 = _mixer_call(xm, jnp.zeros((HA, DKA, DVA), F32), jnp.zeros((HB, DKB, DVB), F32),
                                     jnp.zeros((CONV_A - 1, C_A), F32), p, META_TILE)
    u_tail = _ffn_tail_state_call(x1m[0, META_TILE - SUBLANES:], p)
    cf0 = u_tail[SUBLANES - (CONV_F - 1):][None]

    x1, sa_p, ca_p, sb_p = _mixer_call(x_prompt, sa0[0], sb0[0], ca0[0], p, PROMPT_TILE)
    y_prompt, cf_p = _ffn_call(x1, cf0[0], p, FFN_TILE)

    xs = x_sample.reshape(n_dec, D_MODEL)
    cs = jnp.transpose(state_delta_conv[l], (1, 0, 2))
    cfs = state_ffn_conv[l].reshape(n_dec, (CONV_F - 1) * D_FF)
    qkva, beta, g, qkvb, lg, post, cs_new = _dec_head_call(xs, cs, p)
    oa, ob, sa_s, sb_s = _dec_rec_call(qkva, beta, g, qkvb, lg, state_delta[l], state_gla[l], DECODE_TILE)
    y_s, cf_s = _dec_tail_call(xs, oa, ob, post, cfs, p)

    return (y_prompt, y_s.reshape(n_dec, 1, D_MODEL),
            sa_p[None], ca_p[None], sb_p[None], cf_p[None],
            sa_s[None], jnp.transpose(cs_new, (1, 0, 2))[None], sb_s[None],
            cf_s.reshape(1, n_dec, CONV_F - 1, D_FF))
```

```python
import functools

import jax
import jax.numpy as jnp
from jax import lax
from jax.experimental import pallas as pl
from jax.experimental.pallas import tpu as pltpu

F32 = jnp.float32
BF16 = jnp.bfloat16

D_MODEL = 1024
N_META = 16
CHUNK = 64
HA, DKA, DVA = 4, 128, 128
HB, DKB, DVB = 4, 128, 256
CONV_A = 4
C_A = 2 * HA * DKA + HA * DVA
GATE_RANK = 16
GLA_GATE_NORM = 16.0
D_FF = 2816
CONV_F = 3
EPS = 1e-6
LANES = 128
SUBLANES = 8

O_QKVA = 0
O_QB = O_QKVA + C_A
O_KB = O_QB + HB * DKB
O_VB = O_KB + HB * DKB
O_SMALL = O_VB + HB * DVB
O_POST = O_SMALL + LANES
P_ZA = 0
P_RB = P_ZA + HA * DVA
P_GA = P_RB + HB * DVB
P_GB = P_GA + D_MODEL
W_POST = P_GB + D_MODEL
W_IN_COLS = O_POST + W_POST
LANE_BETA = 0
LANE_G = HA

VMEM_LIMIT = 56 * 1024 * 1024


def _dot(a, b):
    return jnp.dot(a.astype(BF16), b.astype(BF16), preferred_element_type=F32)


def _dot_nt(a, b):
    return lax.dot_general(a.astype(BF16), b.astype(BF16), (((1,), (1,)), ((), ())), preferred_element_type=F32)


def _dot_tn(a, b):
    return lax.dot_general(a.astype(BF16), b.astype(BF16), (((0,), (0,)), ((), ())), preferred_element_type=F32)


def _rms(x, w):
    return x * lax.rsqrt(jnp.mean(x * x, axis=-1, keepdims=True) + EPS) * w


def _silu(x):
    return x * jax.nn.sigmoid(x)


def _gelu_tanh(x):
    return 0.5 * x * (1.0 + jnp.tanh(0.7978845608028654 * (x + 0.044715 * (x * x * x))))


def _l2n(x):
    return x * lax.rsqrt(jnp.sum(x * x, axis=-1, keepdims=True) + EPS)


def _iota2(n, m, axis):
    return lax.broadcasted_iota(jnp.int32, (n, m), axis)


def _gates(small, alog_row, dt_row):
    beta = jax.nn.sigmoid(small)
    g = -jnp.exp(alog_row) * jax.nn.softplus(small + dt_row)
    return beta, g


def _qkv_a_post(conv_out):
    act = _silu(conv_out)
    parts = []
    for h in range(HA):
        parts.append(_l2n(act[:, h * DKA:(h + 1) * DKA]) * (DKA ** -0.5))
    for h in range(HA):
        o = HA * DKA + h * DKA
        parts.append(_l2n(act[:, o:o + DKA]))
    parts.append(act[:, 2 * HA * DKA:])
    return jnp.concatenate(parts, axis=1)


def _branch_a(oa, post, onorm_a, wa_ref):
    pa = [_rms(oa[:, h * DVA:(h + 1) * DVA], onorm_a) for h in range(HA)]
    return _dot(jnp.concatenate(pa, axis=1) * _silu(post[:, P_ZA:P_ZA + HA * DVA]), wa_ref[...])


def _branch_b(ob, post, onorm_b, wb_ref):
    pb = [_rms(ob[:, h * DVB:(h + 1) * DVB], onorm_b) for h in range(HB)]
    return _dot(jnp.concatenate(pb, axis=1) * _silu(post[:, P_RB:P_RB + HB * DVB]), wb_ref[...])


def _mix_residual(x, y_a, y_b, post, wo_ref):
    mix = (jax.nn.sigmoid(post[:, P_GA:P_GA + D_MODEL]) * y_a
           + jax.nn.sigmoid(post[:, P_GB:P_GB + D_MODEL]) * y_b)
    return x + _dot(mix, wo_ref[...])


def _cumsum_chunks(x, c):
    rowi = _iota2(x.shape[0], x.shape[1], 0) % c
    sh = 1
    while sh < c:
        x = x + jnp.where(rowi >= sh, pltpu.roll(x, sh, axis=0), 0.0)
        sh *= 2
    return x


def _delta_prepare(qkv_s, beta_s, gcum_s, gt_s, m_s, qk_s, rhs_s, qe_s, kdec_s, tt, c):
    ri = _iota2(c, 2 * c, 0)
    ci = _iota2(c, 2 * c, 1) % c
    first = _iota2(c, 2 * c, 1) < c
    causal = (ri >= ci).astype(F32)
    strict = (ri > ci).astype(F32)
    zeros = jnp.zeros((c, DKA), F32)
    for cp in range(tt // (2 * c)):
        rows2 = slice(2 * cp * c, (2 * cp + 2) * c)
        for h in range(HA):
            n = cp * HA + h
            hs = slice(h * DKA, (h + 1) * DKA)
            lhs, rhs_t, g_cols = [], [], []
            for par in range(2):
                rows = slice((2 * cp + par) * c, (2 * cp + par + 1) * c)
                g_blk = gcum_s[rows, :]
                q = qkv_s[rows, h * DKA:(h + 1) * DKA]
                k = qkv_s[rows, HA * DKA + h * DKA:HA * DKA + (h + 1) * DKA]
                v = qkv_s[rows, 2 * HA * DKA + h * DVA:2 * HA * DKA + (h + 1) * DVA]
                g_col = g_blk[:, LANE_G + h:LANE_G + h + 1]
                beta_col = beta_s[rows, LANE_BETA + h:LANE_BETA + h + 1]
                eg = jnp.exp(g_col)
                kb = k * beta_col
                lhs.append(jnp.concatenate([kb, q], axis=0))
                rhs_t.append(jnp.concatenate([k, zeros] if par == 0 else [zeros, k], axis=1))
                g_cols.append(g_col)
                rhs_s[rows, h * 2 * DVA:(h + 1) * 2 * DVA] = jnp.concatenate([v * beta_col, kb * eg], axis=1)
                qe_s[rows, hs] = q * eg
                kdec_s[rows, hs] = k * jnp.exp(g_col[c - 1:c, :] - g_col)
            kq = _dot_nt(jnp.concatenate(lhs, axis=1), jnp.concatenate(rhs_t, axis=0))
            g_row = gt_s[LANE_G + h:LANE_G + h + 1, rows2]
            decay = jnp.exp(jnp.minimum(jnp.where(first, g_cols[0], g_cols[1]) - g_row, 0.0))
            m_s[n] = kq[0:c] * (decay * strict)
            qk_s[n] = kq[c:] * (decay * causal)


def _inverse_stages(m_s, p_s, pw_s, n_inst, c):
    ri = _iota2(c, 2 * c, 0)
    lane = _iota2(c, 2 * c, 1)
    ci = lane % c
    left = (lane < c).astype(BF16)
    right = (lane >= c).astype(BF16)

    def bdiag(xp):
        xb = xp.astype(BF16)
        return jnp.concatenate([xb * left, xb * right], axis=0)

    eye = (ri == ci).astype(F32)
    base = min(16, c)
    same = (ri // base) == (ci // base)
    neg_same = jnp.where(same, -1.0, 0.0)
    for n in range(n_inst):
        a = m_s[n] * neg_same
        pw_s[n] = _dot(a, bdiag(a))
        p_s[n] = eye + a
    k = 2
    while k < base:
        for n in range(n_inst):
            p = p_s[n]
            pw = pw_s[n]
            bd = bdiag(pw)
            p_s[n] = p + _dot(p, bd)
            if 2 * k < base:
                pw_s[n] = _dot(pw, bd)
        k *= 2
    size = base
    while size < c:
        nxt = size * 2
        same_nxt = (ri // nxt) == (ci // nxt)
        off = (same_nxt & jnp.logical_not(same)).astype(F32)
        for n in range(n_inst):
            pw_s[n] = _dot(p_s[n], bdiag(m_s[n] * off))
        for n in range(n_inst):
            p = p_s[n]
            p_s[n] = p - _dot(pw_s[n], bdiag(p))
        same = same_nxt
        size = nxt


def _level_ref_rows(bc, half, c):
    w = bc.shape[1]
    parts = []
    if half >= SUBLANES // 2:
        for blk in range(c // (2 * half)):
            mrow = blk * 2 * half + half
            parts.append(jnp.broadcast_to(bc[mrow:mrow + 1, :], (2 * half, w)))
    else:
        sub = _iota2(SUBLANES, w, 0)
        for grp in range(c // SUBLANES):
            acc = None
            for blk in range(SUBLANES // (2 * half)):
                mrow = grp * SUBLANES + blk * 2 * half + half
                b = jnp.broadcast_to(bc[mrow:mrow + 1, :], (SUBLANES, w))
                acc = b if acc is None else jnp.where(sub >= blk * 2 * half, b, acc)
            parts.append(acc)
    return jnp.concatenate(parts, axis=0)


def _gla_prepare_steps(qkb_s, vb_s, bc_s, attn_s, ob_s, pkv_s, qeb_s, tt, c):
    ri = _iota2(c, 2 * c, 0)
    ci = _iota2(c, 2 * c, 1) % c
    first = _iota2(c, 2 * c, 1) < c
    n_ch = tt // c
    zeros_k = jnp.zeros((c, DKB), BF16)
    zeros_v = jnp.zeros((c, DVB), BF16)

    def operands(ch, h):
        rows = slice(ch * c, (ch + 1) * c)
        q = qkb_s[rows, h * DKB:(h + 1) * DKB]
        k = qkb_s[rows, HB * DKB + h * DKB:HB * DKB + (h + 1) * DKB]
        return rows, q, k

    def diag():
        for cp in range(n_ch // 2):
            for h in range(HB):
                d = []
                for par in range(2):
                    _, q, k = operands(2 * cp + par, h)
                    d.append(jnp.sum(q * k, axis=-1, keepdims=True))
                attn_s[cp * HB + h] = jnp.where(ri == ci, jnp.where(first, d[0], d[1]), 0.0)

    def level(half):
        valid = (((ri // (2 * half)) == (ci // (2 * half))) & ((ri % (2 * half)) >= half)
                 & ((ci % (2 * half)) < half)).astype(F32)
        for cp in range(n_ch // 2):
            e_all = []
            for par in range(2):
                bc_all = bc_s[(2 * cp + par) * c:(2 * cp + par + 1) * c, :]
                e_all.append(jnp.exp(-jnp.abs(bc_all - _level_ref_rows(bc_all, half, c))))
            for h in range(HB):
                qs, ks = [], []
                for par in range(2):
                    _, q, k = operands(2 * cp + par, h)
                    e = e_all[par][:, h * DKB:(h + 1) * DKB]
                    qs.append((q * e).astype(BF16))
                    ks.append((k * e).astype(BF16))
                k_bd = jnp.concatenate([jnp.concatenate([ks[0], zeros_k], axis=1),
                                        jnp.concatenate([zeros_k, ks[1]], axis=1)], axis=0)
                attn_s[cp * HB + h] += _dot_nt(jnp.concatenate(qs, axis=1), k_bd) * valid

    def finish():
        for cp in range(n_ch // 2):
            for h in range(HB):
                vs = []
                for par in range(2):
                    rows, q, k = operands(2 * cp + par, h)
                    v = vb_s[rows, h * DVB:(h + 1) * DVB]
                    bc = bc_s[rows, h * DKB:(h + 1) * DKB]
                    vs.append(v.astype(BF16))
                    pkv_s[(2 * cp + par) * HB + h] = _dot_tn(k * jnp.exp(bc[c - 1:c, :] - bc), v)
                    qeb_s[rows, h * DKB:(h + 1) * DKB] = q * jnp.exp(bc)
                v_bd = jnp.concatenate([jnp.concatenate([vs[0], zeros_v], axis=1),
                                        jnp.concatenate([zeros_v, vs[1]], axis=1)], axis=0)
                o2 = _dot(attn_s[cp * HB + h], v_bd)
                ob_s[2 * cp * c:(2 * cp + 1) * c, h * DVB:(h + 1) * DVB] = o2[:, 0:DVB]
                ob_s[(2 * cp + 1) * c:(2 * cp + 2) * c, h * DVB:(h + 1) * DVB] = o2[:, DVB:]

    steps = [diag]
    half = c // 2
    while half >= 1:
        steps.append(functools.partial(level, half))
        half //= 2
    steps.append(finish)
    return steps


def _mixer_body(x_ref, sa0_ref, sb0_ref, ca0_ref, nmix_ref, wca_ref, alog_ref, dt_ref, bgk_ref, ona_ref, onb_ref,
                w1_ref, w2_ref, wa_ref, wb_ref, wo_ref,
                x1_ref, sa_ref, ca_ref, sb_ref,
                cbuf, hb_s, post_s, qkv_s, beta_s, gcum_s, gt_s, qkb_s, vb_s, bc_s, oa_s, ob_s,
                m_s, p_s, pw_s, qk_s, rhs_s, uw_s, qe_s, kdec_s, attn_s, pkv_s, qeb_s, gated_s, mix_s, ktu_s, ktw_s, *, tt, c):
    t_idx = pl.program_id(1)
    n_ch = tt // c

    @pl.when(t_idx == 0)
    def _():
        sa_ref[...] = sa0_ref[...]
        sb_ref[...] = sb0_ref[...]
        cbuf[SUBLANES - (CONV_A - 1):SUBLANES, :] = ca0_ref[...]

    rb = min(tt, 64)
    for r in range(tt // rb):
        rr = slice(r * rb, (r + 1) * rb)
        hb_s[rr, :] = _rms(x_ref[rr, :], nmix_ref[...]).astype(BF16)
    hb = hb_s[...]
    cbuf[SUBLANES:SUBLANES + tt, :] = jnp.dot(hb, w1_ref[:, O_QKVA:O_QKVA + C_A], preferred_element_type=F32)
    base = SUBLANES - (CONV_A - 1)
    n_blk = C_A // LANES
    post_w = 2 * LANES
    n_post = W_POST // post_w
    for j in range(max(n_blk, n_post)):
        if j < n_post:
            pc = slice(j * post_w, (j + 1) * post_w)
            post_s[:, pc] = jnp.dot(hb, w1_ref[:, O_POST + j * post_w:O_POST + (j + 1) * post_w],
                                    preferred_element_type=F32)
        if j < n_blk:
            cols = slice(j * LANES, (j + 1) * LANES)
            full = cbuf[:, cols]
            shifted = lambda k: full[SUBLANES:] if k == 0 else pltpu.roll(full, k, axis=0)[SUBLANES:]
            acc = shifted(CONV_A - 1) * wca_ref[0:1, cols]
            for i in range(1, CONV_A):
                acc = acc + shifted(CONV_A - 1 - i) * wca_ref[i:i + 1, cols]
            act = _silu(acc)
            if j < HA:
                act = _l2n(act) * (DKA ** -0.5)
            elif j < 2 * HA:
                act = _l2n(act)
            qkv_s[:, cols] = act
    tail = cbuf[tt + base:tt + SUBLANES, :]
    cbuf[base:SUBLANES, :] = tail
    ca_ref[...] = tail
    small = jnp.dot(hb, w1_ref[:, O_SMALL:O_SMALL + LANES], preferred_element_type=F32)
    beta, g = _gates(small, alog_ref[...], dt_ref[...])
    beta_s[...] = beta
    g_cum = _cumsum_chunks(g, c)
    gcum_s[...] = g_cum
    gt_s[...] = g_cum.T

    _delta_prepare(qkv_s, beta_s, gcum_s, gt_s, m_s, qk_s, rhs_s, qe_s, kdec_s, tt, c)
    n_pair = (n_ch // 2) * HA
    _inverse_stages(m_s, p_s, pw_s, n_pair, c)
    zeros_uw = jnp.zeros((c, 2 * DVA), BF16)

    def pair_bdiag(ref, cp, cols):
        x_e = ref[2 * cp * c:(2 * cp + 1) * c, cols].astype(BF16)
        x_o = ref[(2 * cp + 1) * c:(2 * cp + 2) * c, cols].astype(BF16)
        return jnp.concatenate([jnp.concatenate([x_e, zeros_uw], axis=1),
                                jnp.concatenate([zeros_uw, x_o], axis=1)], axis=0)

    for cp in range(n_ch // 2):
        rows_e = slice(2 * cp * c, (2 * cp + 1) * c)
        rows_o = slice((2 * cp + 1) * c, (2 * cp + 2) * c)
        for h in range(HA):
            cols = slice(h * 2 * DVA, (h + 1) * 2 * DVA)
            uw2 = _dot(p_s[cp * HA + h], pair_bdiag(rhs_s, cp, cols))
            uw_s[rows_e, cols] = uw2[:, 0:2 * DVA]
            uw_s[rows_o, cols] = uw2[:, 2 * DVA:]
    for cp in range(n_ch // 2):
        for h in range(HA):
            hs = slice(h * DKA, (h + 1) * DKA)
            cols = slice(h * 2 * DVA, (h + 1) * 2 * DVA)
            att2 = _dot(qk_s[cp * HA + h], pair_bdiag(uw_s, cp, cols))
            for par in range(2):
                ch = 2 * cp + par
                rows = slice(ch * c, (ch + 1) * c)
                att_uw = att2[:, par * 2 * DVA:(par + 1) * 2 * DVA]
                oa_s[rows, h * DVA:(h + 1) * DVA] = att_uw[:, 0:DVA]
                qe_s[rows, hs] = qe_s[rows, hs] - att_uw[:, DVA:]
                kt_uw = _dot_tn(kdec_s[rows, hs], uw_s[rows, cols])
                ktu_s[ch * HA + h] = kt_uw[:, 0:DVA]
                ktw_s[ch * HA + h] = kt_uw[:, DVA:]

    small_b = small.astype(BF16)
    for h in range(HB):
        hs = slice(h * DKB, (h + 1) * DKB)
        lg_pre = jnp.dot(small_b, w2_ref[:, hs], preferred_element_type=F32) + bgk_ref[:, hs]
        bc_s[:, hs] = _cumsum_chunks(jax.nn.log_sigmoid(lg_pre) / GLA_GATE_NORM, c)
    qkb_s[:, 0:HB * DKB] = jnp.dot(hb, w1_ref[:, O_QB:O_QB + HB * DKB], preferred_element_type=F32) * (DKB ** -0.5)
    qkb_s[:, HB * DKB:] = jnp.dot(hb, w1_ref[:, O_KB:O_KB + HB * DKB], preferred_element_type=F32)
    vb_s[...] = jnp.dot(hb, w1_ref[:, O_VB:O_VB + HB * DVB], preferred_element_type=F32)

    for step in _gla_prepare_steps(qkb_s, vb_s, bc_s, attn_s, ob_s, pkv_s, qeb_s, tt, c):
        step()

    for ch in range(n_ch):
        rows = slice(ch * c, (ch + 1) * c)
        last = slice(ch * c + c - 1, ch * c + c)
        for h in range(HA):
            n = ch * HA + h
            hs = slice(h * DKA, (h + 1) * DKA)
            s = sa_ref[h]
            prod = _dot(jnp.concatenate([ktw_s[n], qe_s[rows, hs]], axis=0), s)
            oa_s[rows, h * DVA:(h + 1) * DVA] += prod[DKA:]
            g_last = gcum_s[last, LANE_G + h:LANE_G + h + 1]
            sa_ref[h] = s * jnp.exp(g_last) + (ktu_s[n] - prod[0:DKA])

    for h in range(HA):
        hs = slice(h * DVA, (h + 1) * DVA)
        gated_s[:, hs] = (_rms(oa_s[:, hs], ona_ref[...]) * _silu(post_s[:, P_ZA + h * DVA:P_ZA + (h + 1) * DVA])
                          ).astype(BF16)

    for ch in range(n_ch):
        rows = slice(ch * c, (ch + 1) * c)
        last = slice(ch * c + c - 1, ch * c + c)
        for h in range(HB):
            s = sb_ref[h]
            ob_s[rows, h * DVB:(h + 1) * DVB] += _dot(qeb_s[rows, h * DKB:(h + 1) * DKB], s)
            e_col = jnp.exp(bc_s[last, h * DKB:(h + 1) * DKB]).T
            sb_ref[h] = s * e_col + pkv_s[ch * HB + h]

    for h in range(HB):
        hs = slice(h * DVB, (h + 1) * DVB)
        gated_s[:, HA * DVA + h * DVB:HA * DVA + (h + 1) * DVB] = (
            _rms(ob_s[:, hs], onb_ref[...]) * _silu(post_s[:, P_RB + h * DVB:P_RB + (h + 1) * DVB])).astype(BF16)
    out_w = 2 * LANES
    ga = gated_s[:, 0:HA * DVA]
    gb = gated_s[:, HA * DVA:]
    for n in range(D_MODEL // out_w):
        ns = slice(n * out_w, (n + 1) * out_w)
        y_a = jnp.dot(ga, wa_ref[:, ns], preferred_element_type=F32)
        y_b = jnp.dot(gb, wb_ref[:, ns], preferred_element_type=F32)
        mix_s[:, ns] = (jax.nn.sigmoid(post_s[:, P_GA + n * out_w:P_GA + (n + 1) * out_w]) * y_a
                        + jax.nn.sigmoid(post_s[:, P_GB + n * out_w:P_GB + (n + 1) * out_w]) * y_b).astype(BF16)
    mix = mix_s[...]
    for n in range(D_MODEL // out_w):
        ns = slice(n * out_w, (n + 1) * out_w)
        x1_ref[:, ns] = x_ref[:, ns] + jnp.dot(mix, wo_ref[:, ns], preferred_element_type=F32)


def _const_spec(shape):
    nd = len(shape)
    return pl.BlockSpec(shape, lambda *_: (0,) * nd, pipeline_mode=pl.Buffered(1))


def _mixer_call(x, sa0, sb0, ca0, p, tt):
    b, t, _ = x.shape
    c = CHUNK
    nt = t // tt
    n_inst = (tt // c) * HA
    row = lambda n: _const_spec((1, n))
    in_specs = [
        pl.BlockSpec((None, tt, D_MODEL), lambda i, j: (i, j, 0)),
        _const_spec((HA, DKA, DVA)), _const_spec((HB, DKB, DVB)), _const_spec((CONV_A - 1, C_A)),
        row(D_MODEL), _const_spec((CONV_A, C_A)), row(LANES), row(LANES), row(HB * DKB), row(DVA), row(DVB),
        _const_spec((D_MODEL, W_IN_COLS)), _const_spec((LANES, HB * DKB)),
        _const_spec((HA * DVA, D_MODEL)), _const_spec((HB * DVB, D_MODEL)), _const_spec((D_MODEL, D_MODEL)),
    ]
    out_specs = [
        pl.BlockSpec((None, tt, D_MODEL), lambda i, j: (i, j, 0)),
        pl.BlockSpec((None, HA, DKA, DVA), lambda i, j: (i, 0, 0, 0)),
        pl.BlockSpec((None, CONV_A - 1, C_A), lambda i, j: (i, 0, 0)),
        pl.BlockSpec((None, HB, DKB, DVB), lambda i, j: (i, 0, 0, 0)),
    ]
    out_shape = [
        jax.ShapeDtypeStruct((b, t, D_MODEL), F32),
        jax.ShapeDtypeStruct((b, HA, DKA, DVA), F32),
        jax.ShapeDtypeStruct((b, CONV_A - 1, C_A), F32),
        jax.ShapeDtypeStruct((b, HB, DKB, DVB), F32),
    ]
    assert (tt // c) % 2 == 0, "mixer A handles chunks in pairs"
    pairs = lambda: pltpu.VMEM((n_inst // 2, c, 2 * c), F32)
    scratch = [
        pltpu.VMEM((tt + SUBLANES, C_A), F32),
        pltpu.VMEM((tt, D_MODEL), BF16),
        pltpu.VMEM((tt, W_POST), F32),
        pltpu.VMEM((tt, C_A), F32),
        pltpu.VMEM((tt, LANES), F32),
        pltpu.VMEM((tt, LANES), F32),
        pltpu.VMEM((LANES, tt), F32),
        pltpu.VMEM((tt, 2 * HB * DKB), F32),
        pltpu.VMEM((tt, HB * DVB), F32),
        pltpu.VMEM((tt, HB * DKB), F32),
        pltpu.VMEM((tt, HA * DVA), F32),
        pltpu.VMEM((tt, HB * DVB), F32),
        pairs(), pairs(), pairs(), pairs(),
        pltpu.VMEM((tt, HA * 2 * DVA), F32),
        pltpu.VMEM((tt, HA * 2 * DVA), F32),
        pltpu.VMEM((tt, HA * DKA), F32),
        pltpu.VMEM((tt, HA * DKA), F32),
        pairs(),
        pltpu.VMEM((n_inst, DKB, DVB), F32),
        pltpu.VMEM((tt, HB * DKB), F32),
        pltpu.VMEM((tt, HA * DVA + HB * DVB), BF16),
        pltpu.VMEM((tt, D_MODEL), BF16),
        pltpu.VMEM((n_inst, DKA, DVA), F32),
        pltpu.VMEM((n_inst, DKA, DVA), F32),
    ]
    return pl.pallas_call(
        functools.partial(_mixer_body, tt=tt, c=c),
        grid=(b, nt), in_specs=in_specs, out_specs=out_specs, out_shape=out_shape, scratch_shapes=scratch,
        compiler_params=pltpu.CompilerParams(dimension_semantics=("arbitrary", "arbitrary"),
                                             vmem_limit_bytes=VMEM_LIMIT),
        name="mixer",
    )(x, sa0, sb0, ca0, p["norm_mix"], p["w_conv_a"], p["alog_row"], p["dt_row"], p["b_gk"], p["onorm_a"],
      p["onorm_b"], p["w1"], p["w2"], p["w_a_out"], p["w_b_out"], p["w_o"])


def _ffn_core(x1, u_conv, gf, wdn_ref, nfin):
    act = _gelu_tanh(u_conv) * gf
    x2 = x1 + _dot(act, wdn_ref[...])
    return _rms(x2, nfin)


def _ffn_body(x1_ref, cf0_ref, nffn_ref, wcf_ref, bcf_ref, nfin_ref, wup_ref, wdn_ref,
              y_ref, cf_ref, ubuf, *, tt):
    t_idx = pl.program_id(1)
    base = SUBLANES - (CONV_F - 1)

    @pl.when(t_idx == 0)
    def _():
        ubuf[base:SUBLANES, :] = cf0_ref[...]

    x1 = x1_ref[...]
    h2 = _rms(x1, nffn_ref[...]).astype(BF16)
    ubuf[SUBLANES:SUBLANES + tt, :] = jnp.dot(h2, wup_ref[:, 0:D_FF], preferred_element_type=F32)
    gf = jnp.dot(h2, wup_ref[:, D_FF:], preferred_element_type=F32)
    acc = ubuf[base:base + tt, :] * wcf_ref[0:1, :]
    for i in range(1, CONV_F):
        acc = acc + ubuf[base + i:base + i + tt, :] * wcf_ref[i:i + 1, :]
    acc = acc + bcf_ref[...]
    tail = ubuf[tt + base:tt + SUBLANES, :]
    ubuf[base:SUBLANES, :] = tail
    cf_ref[...] = tail
    y_ref[...] = _ffn_core(x1, acc, gf, wdn_ref, nfin_ref[...])


def _ffn_tail_state_body(x1_ref, nffn_ref, wup_ref, u_ref):
    h2 = _rms(x1_ref[...], nffn_ref[...]).astype(BF16)
    u_ref[...] = jnp.dot(h2, wup_ref[...], preferred_element_type=F32)


def _ffn_tail_state_call(x1_rows, p):
    return pl.pallas_call(
        _ffn_tail_state_body,
        grid=(1,),
        in_specs=[pl.BlockSpec((SUBLANES, D_MODEL), lambda i: (0, 0)), pl.BlockSpec((1, D_MODEL), lambda i: (0, 0)),
                  pl.BlockSpec((D_MODEL, D_FF), lambda i: (0, 0))],
        out_specs=pl.BlockSpec((SUBLANES, D_FF), lambda i: (0, 0)),
        out_shape=jax.ShapeDtypeStruct((SUBLANES, D_FF), F32),
        compiler_params=pltpu.CompilerParams(dimension_semantics=("arbitrary",), vmem_limit_bytes=VMEM_LIMIT),
        name="meta_ffn_state",
    )(x1_rows, p["norm_ffn"], p["w_ffn_in"])


def _ffn_call(x1, cf0, p, tt):
    b, t, _ = x1.shape
    nt = t // tt
    row = lambda n: _const_spec((1, n))
    in_specs = [
        pl.BlockSpec((None, tt, D_MODEL), lambda i, j: (i, j, 0)),
        _const_spec((CONV_F - 1, D_FF)), row(D_MODEL), _const_spec((CONV_F, D_FF)), row(D_FF), row(D_MODEL),
        _const_spec((D_MODEL, 2 * D_FF)), _const_spec((D_FF, D_MODEL)),
    ]
    out_specs = [
        pl.BlockSpec((None, tt, D_MODEL), lambda i, j: (i, j, 0)),
        pl.BlockSpec((None, CONV_F - 1, D_FF), lambda i, j: (i, 0, 0)),
    ]
    out_shape = [jax.ShapeDtypeStruct((b, t, D_MODEL), F32), jax.ShapeDtypeStruct((b, CONV_F - 1, D_FF), F32)]
    return pl.pallas_call(
        functools.partial(_ffn_body, tt=tt),
        grid=(b, nt), in_specs=in_specs, out_specs=out_specs, out_shape=out_shape,
        scratch_shapes=[pltpu.VMEM((tt + SUBLANES, D_FF), F32)],
        compiler_params=pltpu.CompilerParams(dimension_semantics=("arbitrary", "arbitrary"),
                                             vmem_limit_bytes=VMEM_LIMIT),
        name="convffn",
    )(x1, cf0, p["norm_ffn"], p["w_conv_f"], p["b_conv_f"], p["norm_final"], p["w_ffn_in"], p["w_ffn_out"])


def _dec_head_body(x_ref, cs_ref, nmix_ref, wca_ref, alog_ref, dt_ref, bgk_ref, w1_ref, w2_ref,
                   qkva_ref, beta_ref, g_ref, qkvb_ref, lg_ref, post_ref, csn_ref):
    x = x_ref[...]
    hb = _rms(x, nmix_ref[...]).astype(BF16)
    pre = jnp.dot(hb, w1_ref[:, O_QKVA:O_QKVA + C_A], preferred_element_type=F32)
    acc = cs_ref[0] * wca_ref[0:1, :]
    for i in range(1, CONV_A - 1):
        acc = acc + cs_ref[i] * wca_ref[i:i + 1, :]
    acc = acc + pre * wca_ref[CONV_A - 1:CONV_A, :]
    for i in range(CONV_A - 2):
        csn_ref[i] = cs_ref[i + 1]
    csn_ref[CONV_A - 2] = pre
    qkva_ref[...] = _qkv_a_post(acc)
    small = jnp.dot(hb, w1_ref[:, O_SMALL:O_SMALL + LANES], preferred_element_type=F32)
    beta, g = _gates(small, alog_ref[...], dt_ref[...])
    beta_ref[...] = beta
    g_ref[...] = g
    lg_ref[...] = jax.nn.log_sigmoid(_dot(small, w2_ref[...]) + bgk_ref[...]) / GLA_GATE_NORM
    qkb = jnp.dot(hb, w1_ref[:, O_QB:O_QB + 2 * HB * DKB], preferred_element_type=F32)
    qkvb_ref[:, 0:HB * DKB] = qkb[:, 0:HB * DKB] * (DKB ** -0.5)
    qkvb_ref[:, HB * DKB:2 * HB * DKB] = qkb[:, HB * DKB:]
    qkvb_ref[:, 2 * HB * DKB:] = jnp.dot(hb, w1_ref[:, O_VB:O_VB + HB * DVB], preferred_element_type=F32)
    post_ref[...] = jnp.dot(hb, w1_ref[:, O_POST:O_POST + W_POST], preferred_element_type=F32)


def _dec_head_call(xs, cs, p):
    n = xs.shape[0]
    shapes = [(n, C_A), (n, LANES), (n, LANES), (n, 2 * HB * DKB + HB * DVB), (n, HB * DKB), (n, W_POST),
              (CONV_A - 1, n, C_A)]
    return pl.pallas_call(
        _dec_head_body,
        out_shape=[jax.ShapeDtypeStruct(s, F32) for s in shapes],
        compiler_params=pltpu.CompilerParams(vmem_limit_bytes=VMEM_LIMIT),
        name="decode_head",
    )(xs, cs, p["norm_mix"], p["w_conv_a"], p["alog_row"], p["dt_row"], p["b_gk"], p["w1"], p["w2"])


def _dec_rec_body(qkva_ref, beta_ref, g_ref, qkvb_ref, lg_ref, sa_ref, sb_ref,
                  oa_ref, ob_ref, san_ref, sbn_ref, *, tb):
    beta = beta_ref[...]
    eg = jnp.exp(g_ref[...])
    for h in range(HA):
        q_rows = qkva_ref[:, h * DKA:(h + 1) * DKA].astype(BF16)
        k_t = qkva_ref[:, HA * DKA + h * DKA:HA * DKA + (h + 1) * DKA].T
        for j in range(tb):
            k_col = k_t[:, j:j + 1]
            s = sa_ref[j, h] * eg[j:j + 1, LANE_G + h:LANE_G + h + 1]
            v = qkva_ref[j:j + 1, 2 * HA * DKA + h * DVA:2 * HA * DKA + (h + 1) * DVA]
            err = (v - jnp.sum(k_col * s, axis=0, keepdims=True)) * beta[j:j + 1, LANE_BETA + h:LANE_BETA + h + 1]
            s = s + k_col * err
            san_ref[j, h] = s
            o_all = jnp.dot(q_rows, s.astype(BF16), preferred_element_type=F32)
            oa_ref[j:j + 1, h * DVA:(h + 1) * DVA] = o_all[j:j + 1, :]
    for h in range(HB):
        q_rows = qkvb_ref[:, h * DKB:(h + 1) * DKB].astype(BF16)
        k_t = qkvb_ref[:, HB * DKB + h * DKB:HB * DKB + (h + 1) * DKB].T
        d_t = jnp.exp(lg_ref[:, h * DKB:(h + 1) * DKB]).T
        for j in range(tb):
            v = qkvb_ref[j:j + 1, 2 * HB * DKB + h * DVB:2 * HB * DKB + (h + 1) * DVB]
            s = sb_ref[j, h] * d_t[:, j:j + 1] + k_t[:, j:j + 1] * v
            sbn_ref[j, h] = s
            o_all = jnp.dot(q_rows, s.astype(BF16), preferred_element_type=F32)
            ob_ref[j:j + 1, h * DVB:(h + 1) * DVB] = o_all[j:j + 1, :]


def _dec_rec_call(qkva, beta, g, qkvb, lg, sa, sb, tb):
    n = qkva.shape[0]
    rows = lambda w: pl.BlockSpec((tb, w), lambda i: (i, 0))
    in_specs = [rows(C_A), rows(LANES), rows(LANES), rows(2 * HB * DKB + HB * DVB), rows(HB * DKB),
                pl.BlockSpec((tb, HA, DKA, DVA), lambda i: (i, 0, 0, 0)),
                pl.BlockSpec((tb, HB, DKB, DVB), lambda i: (i, 0, 0, 0))]
    out_specs = [rows(HA * DVA), rows(HB * DVB),
                 pl.BlockSpec((tb, HA, DKA, DVA), lambda i: (i, 0, 0, 0)),
                 pl.BlockSpec((tb, HB, DKB, DVB), lambda i: (i, 0, 0, 0))]
    out_shape = [jax.ShapeDtypeStruct((n, HA * DVA), F32), jax.ShapeDtypeStruct((n, HB * DVB), F32),
                 jax.ShapeDtypeStruct(sa.shape, F32), jax.ShapeDtypeStruct(sb.shape, F32)]
    return pl.pallas_call(
        functools.partial(_dec_rec_body, tb=tb),
        grid=(n // tb,), in_specs=in_specs, out_specs=out_specs, out_shape=out_shape,
        compiler_params=pltpu.CompilerParams(dimension_semantics=("arbitrary",), vmem_limit_bytes=VMEM_LIMIT),
        name="decode_recurrence",
    )(qkva, beta, g, qkvb, lg, sa, sb)


def _dec_tail_body(x_ref, oa_ref, ob_ref, post_ref, cf_ref, ona_ref, onb_ref, nffn_ref, wcf_ref, bcf_ref, nfin_ref,
                   wa_ref, wb_ref, wo_ref, wup_ref, wdn_ref, y_ref, cfn_ref):
    post = post_ref[...]
    y_a = _branch_a(oa_ref[...], post, ona_ref[...], wa_ref)
    y_b = _branch_b(ob_ref[...], post, onb_ref[...], wb_ref)
    x1 = _mix_residual(x_ref[...], y_a, y_b, post, wo_ref)
    h2 = _rms(x1, nffn_ref[...]).astype(BF16)
    u = jnp.dot(h2, wup_ref[:, 0:D_FF], preferred_element_type=F32)
    gf = jnp.dot(h2, wup_ref[:, D_FF:], preferred_element_type=F32)
    acc = cf_ref[:, 0:D_FF] * wcf_ref[0:1, :]
    for i in range(1, CONV_F - 1):
        acc = acc + cf_ref[:, i * D_FF:(i + 1) * D_FF] * wcf_ref[i:i + 1, :]
    acc = acc + u * wcf_ref[CONV_F - 1:CONV_F, :] + bcf_ref[...]
    for i in range(CONV_F - 2):
        cfn_ref[:, i * D_FF:(i + 1) * D_FF] = cf_ref[:, (i + 1) * D_FF:(i + 2) * D_FF]
    cfn_ref[:, (CONV_F - 2) * D_FF:] = u
    y_ref[...] = _ffn_core(x1, acc, gf, wdn_ref, nfin_ref[...])


def _dec_tail_call(xs, oa, ob, post, cf, p):
    n = xs.shape[0]
    return pl.pallas_call(
        _dec_tail_body,
        out_shape=[jax.ShapeDtypeStruct((n, D_MODEL), F32), jax.ShapeDtypeStruct((n, (CONV_F - 1) * D_FF), F32)],
        compiler_params=pltpu.CompilerParams(vmem_limit_bytes=VMEM_LIMIT),
        name="decode_tail",
    )(xs, oa, ob, post, cf, p["onorm_a"], p["onorm_b"], p["norm_ffn"], p["w_conv_f"], p["b_conv_f"],
      p["norm_final"], p["w_a_out"], p["w_b_out"], p["w_o"], p["w_ffn_in"], p["w_ffn_out"])


PROJ_SIZES = (HA * DKA, HA * DKA, HA * DVA, HA * DVA, HA, HA, HB * DKB, HB * DKB, HB * DVB, HB * DVB, GATE_RANK,
              D_MODEL, D_MODEL)
PROJ_ORDER = (0, 1, 2, 6, 7, 8, 4, 5, 10, None, 3, 9, 11, 12)
RELAYOUT_ROWS = 128


def _relayout_body(w_ref, o_ref):
    offs = [0]
    for s in PROJ_SIZES:
        offs.append(offs[-1] + s)
    dst = 0
    for seg in PROJ_ORDER:
        if seg is None:
            width = (-dst) % LANES
            o_ref[:, dst:dst + width] = jnp.zeros((o_ref.shape[0], width), BF16)
        else:
            width = PROJ_SIZES[seg]
            o_ref[:, dst:dst + width] = w_ref[:, offs[seg]:offs[seg] + width].astype(BF16)
        dst += width
    assert dst == W_IN_COLS


def _relayout_w_in(w):
    rows, cols = w.shape
    return pl.pallas_call(
        _relayout_body,
        grid=(rows // RELAYOUT_ROWS,),
        in_specs=[pl.BlockSpec((RELAYOUT_ROWS, cols), lambda i: (i, 0))],
        out_specs=pl.BlockSpec((RELAYOUT_ROWS, W_IN_COLS), lambda i: (i, 0)),
        out_shape=jax.ShapeDtypeStruct((rows, W_IN_COLS), BF16),
        compiler_params=pltpu.CompilerParams(dimension_semantics=("arbitrary",), vmem_limit_bytes=VMEM_LIMIT),
        name="relayout_w_in",
    )(w)


def _prep_params(l, norm_mix, w_in, w_conv_a, a_log, dt_bias, w_gk2, b_gk, onorm_a, onorm_b, w_a_out, w_b_out,
                 w_o, norm_ffn, w_ffn_in, w_conv_f, b_conv_f, w_ffn_out, norm_final):
    n_small = 2 * HA + GATE_RANK
    w1 = _relayout_w_in(w_in[l].astype(BF16))
    w2 = jnp.zeros((LANES, HB * DKB), F32).at[2 * HA:n_small].set(w_gk2[l]).astype(BF16)
    lane_row = lambda v: jnp.zeros((1, LANES), F32).at[0, LANE_G:LANE_G + HA].set(v)
    return dict(
        norm_mix=norm_mix[l][None], w_conv_a=w_conv_a[l], alog_row=lane_row(a_log[l]), dt_row=lane_row(dt_bias[l]),
        b_gk=b_gk[l][None], onorm_a=onorm_a[l][None], onorm_b=onorm_b[l][None], w1=w1, w2=w2,
        w_a_out=w_a_out[l].astype(BF16), w_b_out=w_b_out[l].astype(BF16), w_o=w_o[l].astype(BF16),
        norm_ffn=norm_ffn[l][None], w_ffn_in=w_ffn_in[l].astype(BF16), w_conv_f=w_conv_f[l],
        b_conv_f=b_conv_f[l][None], w_ffn_out=w_ffn_out[l].astype(BF16), norm_final=norm_final[None])


PROMPT_TILE = 256
META_TILE = 2 * CHUNK
FFN_TILE = 512
DECODE_TILE = 16


def kernel(x_prompt, x_sample, state_delta, state_delta_conv, state_gla, state_ffn_conv, meta_tokens, norm_mix, w_in, w_conv_a, a_log, dt_bias, w_gk2, b_gk, onorm_a, onorm_b, w_a_out, w_b_out, w_o, norm_ffn, w_ffn_in, w_conv_f, b_conv_f, w_ffn_out, norm_final):
    assert w_in.shape[0] == 1, "single layer only"
    l = 0
    p = _prep_params(l, norm_mix, w_in, w_conv_a, a_log, dt_bias, w_gk2, b_gk, onorm_a, onorm_b, w_a_out, w_b_out,
                     w_o, norm_ffn, w_ffn_in, w_conv_f, b_conv_f, w_ffn_out, norm_final)
    n_dec = x_sample.shape[0]

    xm = jnp.concatenate([jnp.zeros((META_TILE - N_META, D_MODEL), F32), meta_tokens.astype(F32)], axis=0)[None]
    x1m, sa0, ca0, sb0 = _mixer_call(xm, jnp.zeros((HA, DKA, DVA), F32), jnp.zeros((HB, DKB, DVB), F32),
                                     jnp.zeros((CONV_A - 1, C_A), F32), p, META_TILE)
    u_tail = _ffn_tail_state_call(x1m[0, META_TILE - SUBLANES:], p)
    cf0 = u_tail[SUBLANES - (CONV_F - 1):][None]

    x1, sa_p, ca_p, sb_p = _mixer_call(x_prompt, sa0[0], sb0[0], ca0[0], p, PROMPT_TILE)
    y_prompt, cf_p = _ffn_call(x1, cf0[0], p, FFN_TILE)

    xs = x_sample.reshape(n_dec, D_MODEL)
    cs = jnp.transpose(state_delta_conv[l], (1, 0, 2))
    cfs = state_ffn_conv[l].reshape(n_dec, (CONV_F - 1) * D_FF)
    qkva, beta, g, qkvb, lg, post, cs_new = _dec_head_call(xs, cs, p)
    oa, ob, sa_s, sb_s = _dec_rec_call(qkva, beta, g, qkvb, lg, state_delta[l], state_gla[l], DECODE_TILE)
    y_s, cf_s = _dec_tail_call(xs, oa, ob, post, cfs, p)

    return (y_prompt, y_s.reshape(n_dec, 1, D_MODEL),
            sa_p[None], ca_p[None], sb_p[None], cf_p[None],
            sa_s[None], jnp.transpose(cs_new, (1, 0, 2))[None], sb_s[None],
            cf_s.reshape(1, n_dec, CONV_F - 1, D_FF))
```

```python
import functools

import jax
import jax.numpy as jnp
from jax import lax
from jax.experimental import pallas as pl
from jax.experimental.pallas import tpu as pltpu

F32 = jnp.float32
BF16 = jnp.bfloat16

D_MODEL = 1024
N_META = 16
CHUNK = 64
HA, DKA, DVA = 4, 128, 128
HB, DKB, DVB = 4, 128, 256
CONV_A = 4
C_A = 2 * HA * DKA + HA * DVA
GATE_RANK = 16
GLA_GATE_NORM = 16.0
D_FF = 2816
CONV_F = 3
EPS = 1e-6
LANES = 128
SUBLANES = 8

O_QKVA = 0
O_QB = O_QKVA + C_A
O_KB = O_QB + HB * DKB
O_VB = O_KB + HB * DKB
O_SMALL = O_VB + HB * DVB
O_POST = O_SMALL + LANES
P_ZA = 0
P_RB = P_ZA + HA * DVA
P_GA = P_RB + HB * DVB
P_GB = P_GA + D_MODEL
W_POST = P_GB + D_MODEL
W_IN_COLS = O_POST + W_POST
LANE_BETA = 0
LANE_G = HA

VMEM_LIMIT = 56 * 1024 * 1024


def _dot(a, b):
    return jnp.dot(a.astype(BF16), b.astype(BF16), preferred_element_type=F32)


def _dot_nt(a, b):
    return lax.dot_general(a.astype(BF16), b.astype(BF16), (((1,), (1,)), ((), ())), preferred_element_type=F32)


def _dot_tn(a, b):
    return lax.dot_general(a.astype(BF16), b.astype(BF16), (((0,), (0,)), ((), ())), preferred_element_type=F32)


def _rms(x, w):
    return x * lax.rsqrt(jnp.mean(x * x, axis=-1, keepdims=True) + EPS) * w


def _silu(x):
    return x * jax.nn.sigmoid(x)


def _gelu_tanh(x):
    return 0.5 * x * (1.0 + jnp.tanh(0.7978845608028654 * (x + 0.044715 * (x * x * x))))


def _l2n(x):
    return x * lax.rsqrt(jnp.sum(x * x, axis=-1, keepdims=True) + EPS)


def _iota2(n, m, axis):
    return lax.broadcasted_iota(jnp.int32, (n, m), axis)


def _gates(small, alog_row, dt_row):
    beta = jax.nn.sigmoid(small)
    g = -jnp.exp(alog_row) * jax.nn.softplus(small + dt_row)
    return beta, g


def _qkv_a_post(conv_out):
    act = _silu(conv_out)
    parts = []
    for h in range(HA):
        parts.append(_l2n(act[:, h * DKA:(h + 1) * DKA]) * (DKA ** -0.5))
    for h in range(HA):
        o = HA * DKA + h * DKA
        parts.append(_l2n(act[:, o:o + DKA]))
    parts.append(act[:, 2 * HA * DKA:])
    return jnp.concatenate(parts, axis=1)


def _branch_a(oa, post, onorm_a, wa_ref):
    pa = [_rms(oa[:, h * DVA:(h + 1) * DVA], onorm_a) for h in range(HA)]
    return _dot(jnp.concatenate(pa, axis=1) * _silu(post[:, P_ZA:P_ZA + HA * DVA]), wa_ref[...])


def _branch_b(ob, post, onorm_b, wb_ref):
    pb = [_rms(ob[:, h * DVB:(h + 1) * DVB], onorm_b) for h in range(HB)]
    return _dot(jnp.concatenate(pb, axis=1) * _silu(post[:, P_RB:P_RB + HB * DVB]), wb_ref[...])


def _mix_residual(x, y_a, y_b, post, wo_ref):
    mix = (jax.nn.sigmoid(post[:, P_GA:P_GA + D_MODEL]) * y_a
           + jax.nn.sigmoid(post[:, P_GB:P_GB + D_MODEL]) * y_b)
    return x + _dot(mix, wo_ref[...])


def _cumsum_chunks(x, c):
    rowi = _iota2(x.shape[0], x.shape[1], 0) % c
    sh = 1
    while sh < c:
        x = x + jnp.where(rowi >= sh, pltpu.roll(x, sh, axis=0), 0.0)
        sh *= 2
    return x


def _delta_prepare(qkv_s, beta_s, gcum_s, gt_s, m_s, qk_s, rhs_s, qe_s, kdec_s, tt, c):
    ri = _iota2(c, 2 * c, 0)
    ci = _iota2(c, 2 * c, 1) % c
    first = _iota2(c, 2 * c, 1) < c
    causal = (ri >= ci).astype(F32)
    strict = (ri > ci).astype(F32)
    zeros = jnp.zeros((c, DKA), F32)
    for cp in range(tt // (2 * c)):
        rows2 = slice(2 * cp * c, (2 * cp + 2) * c)
        for h in range(HA):
            n = cp * HA + h
            hs = slice(h * DKA, (h + 1) * DKA)
            lhs, rhs_t, g_cols = [], [], []
            for par in range(2):
                rows = slice((2 * cp + par) * c, (2 * cp + par + 1) * c)
                g_blk = gcum_s[rows, :]
                q = qkv_s[rows, h * DKA:(h + 1) * DKA]
                k = qkv_s[rows, HA * DKA + h * DKA:HA * DKA + (h + 1) * DKA]
                v = qkv_s[rows, 2 * HA * DKA + h * DVA:2 * HA * DKA + (h + 1) * DVA]
                g_col = g_blk[:, LANE_G + h:LANE_G + h + 1]
                beta_col = beta_s[rows, LANE_BETA + h:LANE_BETA + h + 1]
                eg = jnp.exp(g_col)
                kb = k * beta_col
                lhs.append(jnp.concatenate([kb, q], axis=0))
                rhs_t.append(jnp.concatenate([k, zeros] if par == 0 else [zeros, k], axis=1))
                g_cols.append(g_col)
                rhs_s[rows, h * 2 * DVA:(h + 1) * 2 * DVA] = jnp.concatenate([v * beta_col, kb * eg], axis=1)
                qe_s[rows, hs] = q * eg
                kdec_s[rows, hs] = k * jnp.exp(g_col[c - 1:c, :] - g_col)
            kq = _dot_nt(jnp.concatenate(lhs, axis=1), jnp.concatenate(rhs_t, axis=0))
            g_row = gt_s[LANE_G + h:LANE_G + h + 1, rows2]
            decay = jnp.exp(jnp.minimum(jnp.where(first, g_cols[0], g_cols[1]) - g_row, 0.0))
            m_s[n] = kq[0:c] * (decay * strict)
            qk_s[n] = kq[c:] * (decay * causal)


def _inverse_stages(m_s, p_s, pw_s, n_inst, c):
    ri = _iota2(c, 2 * c, 0)
    lane = _iota2(c, 2 * c, 1)
    ci = lane % c
    left = (lane < c).astype(BF16)
    right = (lane >= c).astype(BF16)

    def bdiag(xp):
        xb = xp.astype(BF16)
        return jnp.concatenate([xb * left, xb * right], axis=0)

    eye = (ri == ci).astype(F32)
    base = min(16, c)
    same = (ri // base) == (ci // base)
    neg_same = jnp.where(same, -1.0, 0.0)
    for n in range(n_inst):
        a = m_s[n] * neg_same
        pw_s[n] = _dot(a, bdiag(a))
        p_s[n] = eye + a
    k = 2
    while k < base:
        for n in range(n_inst):
            p = p_s[n]
            pw = pw_s[n]
            bd = bdiag(pw)
            p_s[n] = p + _dot(p, bd)
            if 2 * k < base:
                pw_s[n] = _dot(pw, bd)
        k *= 2
    size = base
    while size < c:
        nxt = size * 2
        same_nxt = (ri // nxt) == (ci // nxt)
        off = (same_nxt & jnp.logical_not(same)).astype(F32)
        for n in range(n_inst):
            pw_s[n] = _dot(p_s[n], bdiag(m_s[n] * off))
        for n in range(n_inst):
            p = p_s[n]
            p_s[n] = p - _dot(pw_s[n], bdiag(p))
        same = same_nxt
        size = nxt


def _level_ref_rows(bc, half, c):
    w = bc.shape[1]
    parts = []
    if half >= SUBLANES // 2:
        for blk in range(c // (2 * half)):
            mrow = blk * 2 * half + half
            parts.append(jnp.broadcast_to(bc[mrow:mrow + 1, :], (2 * half, w)))
    else:
        sub = _iota2(SUBLANES, w, 0)
        for grp in range(c // SUBLANES):
            acc = None
            for blk in range(SUBLANES // (2 * half)):
                mrow = grp * SUBLANES + blk * 2 * half + half
                b = jnp.broadcast_to(bc[mrow:mrow + 1, :], (SUBLANES, w))
                acc = b if acc is None else jnp.where(sub >= blk * 2 * half, b, acc)
            parts.append(acc)
    return jnp.concatenate(parts, axis=0)


def _gla_prepare_steps(qkb_s, vb_s, bc_s, attn_s, ob_s, pkv_s, qeb_s, tt, c):
    ri = _iota2(c, 2 * c, 0)
    ci = _iota2(c, 2 * c, 1) % c
    first = _iota2(c, 2 * c, 1) < c
    n_ch = tt // c
    zeros_k = jnp.zeros((c, DKB), BF16)
    zeros_v = jnp.zeros((c, DVB), BF16)

    def operands(ch, h):
        rows = slice(ch * c, (ch + 1) * c)
        q = qkb_s[rows, h * DKB:(h + 1) * DKB]
        k = qkb_s[rows, HB * DKB + h * DKB:HB * DKB + (h + 1) * DKB]
        return rows, q, k

    def diag():
        for cp in range(n_ch // 2):
            for h in range(HB):
                d = []
                for par in range(2):
                    _, q, k = operands(2 * cp + par, h)
                    d.append(jnp.sum(q * k, axis=-1, keepdims=True))
                attn_s[cp * HB + h] = jnp.where(ri == ci, jnp.where(first, d[0], d[1]), 0.0)

    def level(half):
        valid = (((ri // (2 * half)) == (ci // (2 * half))) & ((ri % (2 * half)) >= half)
                 & ((ci % (2 * half)) < half)).astype(F32)
        for cp in range(n_ch // 2):
            e_all = []
            for par in range(2):
                bc_all = bc_s[(2 * cp + par) * c:(2 * cp + par + 1) * c, :]
                e_all.append(jnp.exp(-jnp.abs(bc_all - _level_ref_rows(bc_all, half, c))))
            for h in range(HB):
                qs, ks = [], []
                for par in range(2):
                    _, q, k = operands(2 * cp + par, h)
                    e = e_all[par][:, h * DKB:(h + 1) * DKB]
                    qs.append((q * e).astype(BF16))
                    ks.append((k * e).astype(BF16))
                k_bd = jnp.concatenate([jnp.concatenate([ks[0], zeros_k], axis=1),
                                        jnp.concatenate([zeros_k, ks[1]], axis=1)], axis=0)
                attn_s[cp * HB + h] += _dot_nt(jnp.concatenate(qs, axis=1), k_bd) * valid

    def finish():
        for cp in range(n_ch // 2):
            for h in range(HB):
                vs = []
                for par in range(2):
                    rows, q, k = operands(2 * cp + par, h)
                    v = vb_s[rows, h * DVB:(h + 1) * DVB]
                    bc = bc_s[rows, h * DKB:(h + 1) * DKB]
                    vs.append(v.astype(BF16))
                    pkv_s[(2 * cp + par) * HB + h] = _dot_tn(k * jnp.exp(bc[c - 1:c, :] - bc), v)
                    qeb_s[rows, h * DKB:(h + 1) * DKB] = q * jnp.exp(bc)
                v_bd = jnp.concatenate([jnp.concatenate([vs[0], zeros_v], axis=1),
                                        jnp.concatenate([zeros_v, vs[1]], axis=1)], axis=0)
                o2 = _dot(attn_s[cp * HB + h], v_bd)
                ob_s[2 * cp * c:(2 * cp + 1) * c, h * DVB:(h + 1) * DVB] = o2[:, 0:DVB]
                ob_s[(2 * cp + 1) * c:(2 * cp + 2) * c, h * DVB:(h + 1) * DVB] = o2[:, DVB:]

    steps = [diag]
    half = c // 2
    while half >= 1:
        steps.append(functools.partial(level, half))
        half //= 2
    steps.append(finish)
    return steps


def _mixer_body(x_ref, sa0_ref, sb0_ref, ca0_ref, nmix_ref, wca_ref, alog_ref, dt_ref, bgk_ref, ona_ref, onb_ref,
                w1_ref, w2_ref, wa_ref, wb_ref, wo_ref,
                x1_ref, sa_ref, ca_ref, sb_ref,
                cbuf, hb_s, post_s, qkv_s, beta_s, gcum_s, gt_s, qkb_s, vb_s, bc_s, oa_s, ob_s,
                m_s, p_s, pw_s, qk_s, rhs_s, uw_s, qe_s, kdec_s, attn_s, pkv_s, qeb_s, gated_s, mix_s, ktu_s, ktw_s, *, tt, c):
    t_idx = pl.program_id(1)
    n_ch = tt // c

    @pl.when(t_idx == 0)
    def _():
        sa_ref[...] = sa0_ref[...]
        sb_ref[...] = sb0_ref[...]
        cbuf[SUBLANES - (CONV_A - 1):SUBLANES, :] = ca0_ref[...]

    rb = min(tt, 64)
    for r in range(tt // rb):
        rr = slice(r * rb, (r + 1) * rb)
        hb_s[rr, :] = _rms(x_ref[rr, :], nmix_ref[...]).astype(BF16)
    hb = hb_s[...]
    cbuf[SUBLANES:SUBLANES + tt, :] = jnp.dot(hb, w1_ref[:, O_QKVA:O_QKVA + C_A], preferred_element_type=F32)
    base = SUBLANES - (CONV_A - 1)
    n_blk = C_A // LANES
    post_w = 2 * LANES
    n_post = W_POST // post_w
    for j in range(max(n_blk, n_post)):
        if j < n_post:
            pc = slice(j * post_w, (j + 1) * post_w)
            post_s[:, pc] = jnp.dot(hb, w1_ref[:, O_POST + j * post_w:O_POST + (j + 1) * post_w],
                                    preferred_element_type=F32)
        if j < n_blk:
            cols = slice(j * LANES, (j + 1) * LANES)
            full = cbuf[:, cols]
            shifted = lambda k: full[SUBLANES:] if k == 0 else pltpu.roll(full, k, axis=0)[SUBLANES:]
            acc = shifted(CONV_A - 1) * wca_ref[0:1, cols]
            for i in range(1, CONV_A):
                acc = acc + shifted(CONV_A - 1 - i) * wca_ref[i:i + 1, cols]
            act = _silu(acc)
            if j < HA:
                act = _l2n(act) * (DKA ** -0.5)
            elif j < 2 * HA:
                act = _l2n(act)
            qkv_s[:, cols] = act
    tail = cbuf[tt + base:tt + SUBLANES, :]
    cbuf[base:SUBLANES, :] = tail
    ca_ref[...] = tail
    small = jnp.dot(hb, w1_ref[:, O_SMALL:O_SMALL + LANES], preferred_element_type=F32)
    beta, g = _gates(small, alog_ref[...], dt_ref[...])
    beta_s[...] = beta
    g_cum = _cumsum_chunks(g, c)
    gcum_s[...] = g_cum
    gt_s[...] = g_cum.T

    _delta_prepare(qkv_s, beta_s, gcum_s, gt_s, m_s, qk_s, rhs_s, qe_s, kdec_s, tt, c)
    n_pair = (n_ch // 2) * HA
    _inverse_stages(m_s, p_s, pw_s, n_pair, c)
    zeros_uw = jnp.zeros((c, 2 * DVA), BF16)

    def pair_bdiag(ref, cp, cols):
        x_e = ref[2 * cp * c:(2 * cp + 1) * c, cols].astype(BF16)
        x_o = ref[(2 * cp + 1) * c:(2 * cp + 2) * c, cols].astype(BF16)
        return jnp.concatenate([jnp.concatenate([x_e, zeros_uw], axis=1),
                                jnp.concatenate([zeros_uw, x_o], axis=1)], axis=0)

    for cp in range(n_ch // 2):
        rows_e = slice(2 * cp * c, (2 * cp + 1) * c)
        rows_o = slice((2 * cp + 1) * c, (2 * cp + 2) * c)
        for h in range(HA):
            cols = slice(h * 2 * DVA, (h + 1) * 2 * DVA)
            uw2 = _dot(p_s[cp * HA + h], pair_bdiag(rhs_s, cp, cols))
            uw_s[rows_e, cols] = uw2[:, 0:2 * DVA]
            uw_s[rows_o, cols] = uw2[:, 2 * DVA:]
    for cp in range(n_ch // 2):
        for h in range(HA):
            hs = slice(h * DKA, (h + 1) * DKA)
            cols = slice(h * 2 * DVA, (h + 1) * 2 * DVA)
            att2 = _dot(qk_s[cp * HA + h], pair_bdiag(uw_s, cp, cols))
            for par in range(2):
                ch = 2 * cp + par
                rows = slice(ch * c, (ch + 1) * c)
                att_uw = att2[:, par * 2 * DVA:(par + 1) * 2 * DVA]
                oa_s[rows, h * DVA:(h + 1) * DVA] = att_uw[:, 0:DVA]
                qe_s[rows, hs] = qe_s[rows, hs] - att_uw[:, DVA:]
                kt_uw = _dot_tn(kdec_s[rows, hs], uw_s[rows, cols])
                ktu_s[ch * HA + h] = kt_uw[:, 0:DVA]
                ktw_s[ch * HA + h] = kt_uw[:, DVA:]

    small_b = small.astype(BF16)
    for h in range(HB):
        hs = slice(h * DKB, (h + 1) * DKB)
        lg_pre = jnp.dot(small_b, w2_ref[:, hs], preferred_element_type=F32) + bgk_ref[:, hs]
        bc_s[:, hs] = _cumsum_chunks(jax.nn.log_sigmoid(lg_pre) / GLA_GATE_NORM, c)
    qkb_s[:, 0:HB * DKB] = jnp.dot(hb, w1_ref[:, O_QB:O_QB + HB * DKB], preferred_element_type=F32) * (DKB ** -0.5)
    qkb_s[:, HB * DKB:] = jnp.dot(hb, w1_ref[:, O_KB:O_KB + HB * DKB], preferred_element_type=F32)
    vb_s[...] = jnp.dot(hb, w1_ref[:, O_VB:O_VB + HB * DVB], preferred_element_type=F32)

    for step in _gla_prepare_steps(qkb_s, vb_s, bc_s, attn_s, ob_s, pkv_s, qeb_s, tt, c):
        step()

    for ch in range(n_ch):
        rows = slice(ch * c, (ch + 1) * c)
        last = slice(ch * c + c - 1, ch * c + c)
        for h in range(HA):
            n = ch * HA + h
            hs = slice(h * DKA, (h + 1) * DKA)
            s = sa_ref[h]
            prod = _dot(jnp.concatenate([ktw_s[n], qe_s[rows, hs]], axis=0), s)
            oa_s[rows, h * DVA:(h + 1) * DVA] += prod[DKA:]
            g_last = gcum_s[last, LANE_G + h:LANE_G + h + 1]
            sa_ref[h] = s * jnp.exp(g_last) + (ktu_s[n] - prod[0:DKA])

    for h in range(HA):
        hs = slice(h * DVA, (h + 1) * DVA)
        gated_s[:, hs] = (_rms(oa_s[:, hs], ona_ref[...]) * _silu(post_s[:, P_ZA + h * DVA:P_ZA + (h + 1) * DVA])
                          ).astype(BF16)

    for ch in range(n_ch):
        rows = slice(ch * c, (ch + 1) * c)
        last = slice(ch * c + c - 1, ch * c + c)
        for h in range(HB):
            s = sb_ref[h]
            ob_s[rows, h * DVB:(h + 1) * DVB] += _dot(qeb_s[rows, h * DKB:(h + 1) * DKB], s)
            e_col = jnp.exp(bc_s[last, h * DKB:(h + 1) * DKB]).T
            sb_ref[h] = s * e_col + pkv_s[ch * HB + h]

    for h in range(HB):
        hs = slice(h * DVB, (h + 1) * DVB)
        gated_s[:, HA * DVA + h * DVB:HA * DVA + (h + 1) * DVB] = (
            _rms(ob_s[:, hs], onb_ref[...]) * _silu(post_s[:, P_RB + h * DVB:P_RB + (h + 1) * DVB])).astype(BF16)
    out_w = 2 * LANES
    ga = gated_s[:, 0:HA * DVA]
    gb = gated_s[:, HA * DVA:]
    for n in range(D_MODEL // out_w):
        ns = slice(n * out_w, (n + 1) * out_w)
        y_a = jnp.dot(ga, wa_ref[:, ns], preferred_element_type=F32)
        y_b = jnp.dot(gb, wb_ref[:, ns], preferred_element_type=F32)
        mix_s[:, ns] = (jax.nn.sigmoid(post_s[:, P_GA + n * out_w:P_GA + (n + 1) * out_w]) * y_a
                        + jax.nn.sigmoid(post_s[:, P_GB + n * out_w:P_GB + (n + 1) * out_w]) * y_b).astype(BF16)
    mix = mix_s[...]
    for n in range(D_MODEL // out_w):
        ns = slice(n * out_w, (n + 1) * out_w)
        x1_ref[:, ns] = x_ref[:, ns] + jnp.dot(mix, wo_ref[:, ns], preferred_element_type=F32)


def _const_spec(shape):
    nd = len(shape)
    return pl.BlockSpec(shape, lambda *_: (0,) * nd, pipeline_mode=pl.Buffered(1))


def _mixer_call(x, sa0, sb0, ca0, p, tt):
    b, t, _ = x.shape
    c = CHUNK
    nt = t // tt
    n_inst = (tt // c) * HA
    row = lambda n: _const_spec((1, n))
    in_specs = [
        pl.BlockSpec((None, tt, D_MODEL), lambda i, j: (i, j, 0)),
        _const_spec((HA, DKA, DVA)), _const_spec((HB, DKB, DVB)), _const_spec((CONV_A - 1, C_A)),
        row(D_MODEL), _const_spec((CONV_A, C_A)), row(LANES), row(LANES), row(HB * DKB), row(DVA), row(DVB),
        _const_spec((D_MODEL, W_IN_COLS)), _const_spec((LANES, HB * DKB)),
        _const_spec((HA * DVA, D_MODEL)), _const_spec((HB * DVB, D_MODEL)), _const_spec((D_MODEL, D_MODEL)),
    ]
    out_specs = [
        pl.BlockSpec((None, tt, D_MODEL), lambda i, j: (i, j, 0)),
        pl.BlockSpec((None, HA, DKA, DVA), lambda i, j: (i, 0, 0, 0)),
        pl.BlockSpec((None, CONV_A - 1, C_A), lambda i, j: (i, 0, 0)),
        pl.BlockSpec((None, HB, DKB, DVB), lambda i, j: (i, 0, 0, 0)),
    ]
    out_shape = [
        jax.ShapeDtypeStruct((b, t, D_MODEL), F32),
        jax.ShapeDtypeStruct((b, HA, DKA, DVA), F32),
        jax.ShapeDtypeStruct((b, CONV_A - 1, C_A), F32),
        jax.ShapeDtypeStruct((b, HB, DKB, DVB), F32),
    ]
    assert (tt // c) % 2 == 0, "mixer A handles chunks in pairs"
    pairs = lambda: pltpu.VMEM((n_inst // 2, c, 2 * c), F32)
    scratch = [
        pltpu.VMEM((tt + SUBLANES, C_A), F32),
        pltpu.VMEM((tt, D_MODEL), BF16),
        pltpu.VMEM((tt, W_POST), F32),
        pltpu.VMEM((tt, C_A), F32),
        pltpu.VMEM((tt, LANES), F32),
        pltpu.VMEM((tt, LANES), F32),
        pltpu.VMEM((LANES, tt), F32),
        pltpu.VMEM((tt, 2 * HB * DKB), F32),
        pltpu.VMEM((tt, HB * DVB), F32),
        pltpu.VMEM((tt, HB * DKB), F32),
        pltpu.VMEM((tt, HA * DVA), F32),
        pltpu.VMEM((tt, HB * DVB), F32),
        pairs(), pairs(), pairs(), pairs(),
        pltpu.VMEM((tt, HA * 2 * DVA), F32),
        pltpu.VMEM((tt, HA * 2 * DVA), F32),
        pltpu.VMEM((tt, HA * DKA), F32),
        pltpu.VMEM((tt, HA * DKA), F32),
        pairs(),
        pltpu.VMEM((n_inst, DKB, DVB), F32),
        pltpu.VMEM((tt, HB * DKB), F32),
        pltpu.VMEM((tt, HA * DVA + HB * DVB), BF16),
        pltpu.VMEM((tt, D_MODEL), BF16),
        pltpu.VMEM((n_inst, DKA, DVA), F32),
        pltpu.VMEM((n_inst, DKA, DVA), F32),
    ]
    return pl.pallas_call(
        functools.partial(_mixer_body, tt=tt, c=c),
        grid=(b, nt), in_specs=in_specs, out_specs=out_specs, out_shape=out_shape, scratch_shapes=scratch,
        compiler_params=pltpu.CompilerParams(dimension_semantics=("arbitrary", "arbitrary"),
                                             vmem_limit_bytes=VMEM_LIMIT),
        name="mixer",
    )(x, sa0, sb0, ca0, p["norm_mix"], p["w_conv_a"], p["alog_row"], p["dt_row"], p["b_gk"], p["onorm_a"],
      p["onorm_b"], p["w1"], p["w2"], p["w_a_out"], p["w_b_out"], p["w_o"])


def _ffn_core(x1, u_conv, gf, wdn_ref, nfin):
    act = _gelu_tanh(u_conv) * gf
    x2 = x1 + _dot(act, wdn_ref[...])
    return _rms(x2, nfin)


def _ffn_body(x1_ref, cf0_ref, nffn_ref, wcf_ref, bcf_ref, nfin_ref, wup_ref, wdn_ref,
              y_ref, cf_ref, ubuf, *, tt):
    t_idx = pl.program_id(1)
    base = SUBLANES - (CONV_F - 1)

    @pl.when(t_idx == 0)
    def _():
        ubuf[base:SUBLANES, :] = cf0_ref[...]

    x1 = x1_ref[...]
    h2 = _rms(x1, nffn_ref[...]).astype(BF16)
    ubuf[SUBLANES:SUBLANES + tt, :] = jnp.dot(h2, wup_ref[:, 0:D_FF], preferred_element_type=F32)
    gf = jnp.dot(h2, wup_ref[:, D_FF:], preferred_element_type=F32)
    acc = ubuf[base:base + tt, :] * wcf_ref[0:1, :]
    for i in range(1, CONV_F):
        acc = acc + ubuf[base + i:base + i + tt, :] * wcf_ref[i:i + 1, :]
    acc = acc + bcf_ref[...]
    tail = ubuf[tt + base:tt + SUBLANES, :]
    ubuf[base:SUBLANES, :] = tail
    cf_ref[...] = tail
    y_ref[...] = _ffn_core(x1, acc, gf, wdn_ref, nfin_ref[...])


def _ffn_tail_state_body(x1_ref, nffn_ref, wup_ref, u_ref):
    h2 = _rms(x1_ref[...], nffn_ref[...]).astype(BF16)
    u_ref[...] = jnp.dot(h2, wup_ref[...], preferred_element_type=F32)


def _ffn_tail_state_call(x1_rows, p):
    return pl.pallas_call(
        _ffn_tail_state_body,
        grid=(1,),
        in_specs=[pl.BlockSpec((SUBLANES, D_MODEL), lambda i: (0, 0)), pl.BlockSpec((1, D_MODEL), lambda i: (0, 0)),
                  pl.BlockSpec((D_MODEL, D_FF), lambda i: (0, 0))],
        out_specs=pl.BlockSpec((SUBLANES, D_FF), lambda i: (0, 0)),
        out_shape=jax.ShapeDtypeStruct((SUBLANES, D_FF), F32),
        compiler_params=pltpu.CompilerParams(dimension_semantics=("arbitrary",), vmem_limit_bytes=VMEM_LIMIT),
        name="meta_ffn_state",
    )(x1_rows, p["norm_ffn"], p["w_ffn_in"])


def _ffn_call(x1, cf0, p, tt):
    b, t, _ = x1.shape
    nt = t // tt
    row = lambda n: _const_spec((1, n))
    in_specs = [
        pl.BlockSpec((None, tt, D_MODEL), lambda i, j: (i, j, 0)),
        _const_spec((CONV_F - 1, D_FF)), row(D_MODEL), _const_spec((CONV_F, D_FF)), row(D_FF), row(D_MODEL),
        _const_spec((D_MODEL, 2 * D_FF)), _const_spec((D_FF, D_MODEL)),
    ]
    out_specs = [
        pl.BlockSpec((None, tt, D_MODEL), lambda i, j: (i, j, 0)),
        pl.BlockSpec((None, CONV_F - 1, D_FF), lambda i, j: (i, 0, 0)),
    ]
    out_shape = [jax.ShapeDtypeStruct((b, t, D_MODEL), F32), jax.ShapeDtypeStruct((b, CONV_F - 1, D_FF), F32)]
    return pl.pallas_call(
        functools.partial(_ffn_body, tt=tt),
        grid=(b, nt), in_specs=in_specs, out_specs=out_specs, out_shape=out_shape,
        scratch_shapes=[pltpu.VMEM((tt + SUBLANES, D_FF), F32)],
        compiler_params=pltpu.CompilerParams(dimension_semantics=("arbitrary", "arbitrary"),
                                             vmem_limit_bytes=VMEM_LIMIT),
        name="convffn",
    )(x1, cf0, p["norm_ffn"], p["w_conv_f"], p["b_conv_f"], p["norm_final"], p["w_ffn_in"], p["w_ffn_out"])


def _dec_head_body(x_ref, cs_ref, nmix_ref, wca_ref, alog_ref, dt_ref, bgk_ref, w1_ref, w2_ref,
                   qkva_ref, beta_ref, g_ref, qkvb_ref, lg_ref, post_ref, csn_ref):
    x = x_ref[...]
    hb = _rms(x, nmix_ref[...]).astype(BF16)
    pre = jnp.dot(hb, w1_ref[:, O_QKVA:O_QKVA + C_A], preferred_element_type=F32)
    acc = cs_ref[0] * wca_ref[0:1, :]
    for i in range(1, CONV_A - 1):
        acc = acc + cs_ref[i] * wca_ref[i:i + 1, :]
    acc = acc + pre * wca_ref[CONV_A - 1:CONV_A, :]
    for i in range(CONV_A - 2):
        csn_ref[i] = cs_ref[i + 1]
    csn_ref[CONV_A - 2] = pre
    qkva_ref[...] = _qkv_a_post(acc)
    small = jnp.dot(hb, w1_ref[:, O_SMALL:O_SMALL + LANES], preferred_element_type=F32)
    beta, g = _gates(small, alog_ref[...], dt_ref[...])
    beta_ref[...] = beta
    g_ref[...] = g
    lg_ref[...] = jax.nn.log_sigmoid(_dot(small, w2_ref[...]) + bgk_ref[...]) / GLA_GATE_NORM
    qkb = jnp.dot(hb, w1_ref[:, O_QB:O_QB + 2 * HB * DKB], preferred_element_type=F32)
    qkvb_ref[:, 0:HB * DKB] = qkb[:, 0:HB * DKB] * (DKB ** -0.5)
    qkvb_ref[:, HB * DKB:2 * HB * DKB] = qkb[:, HB * DKB:]
    qkvb_ref[:, 2 * HB * DKB:] = jnp.dot(hb, w1_ref[:, O_VB:O_VB + HB * DVB], preferred_element_type=F32)
    post_ref[...] = jnp.dot(hb, w1_ref[:, O_POST:O_POST + W_POST], preferred_element_type=F32)


def _dec_head_call(xs, cs, p):
    n = xs.shape[0]
    shapes = [(n, C_A), (n, LANES), (n, LANES), (n, 2 * HB * DKB + HB * DVB), (n, HB * DKB), (n, W_POST),
              (CONV_A - 1, n, C_A)]
    return pl.pallas_call(
        _dec_head_body,
        out_shape=[jax.ShapeDtypeStruct(s, F32) for s in shapes],
        compiler_params=pltpu.CompilerParams(vmem_limit_bytes=VMEM_LIMIT),
        name="decode_head",
    )(xs, cs, p["norm_mix"], p["w_conv_a"], p["alog_row"], p["dt_row"], p["b_gk"], p["w1"], p["w2"])


def _dec_rec_body(qkva_ref, beta_ref, g_ref, qkvb_ref, lg_ref, sa_ref, sb_ref,
                  oa_ref, ob_ref, san_ref, sbn_ref, *, tb):
    beta = beta_ref[...]
    eg = jnp.exp(g_ref[...])
    for h in range(HA):
        q_rows = qkva_ref[:, h * DKA:(h + 1) * DKA].astype(BF16)
        k_t = qkva_ref[:, HA * DKA + h * DKA:HA * DKA + (h + 1) * DKA].T
        for j in range(tb):
            k_col = k_t[:, j:j + 1]
            s = sa_ref[j, h] * eg[j:j + 1, LANE_G + h:LANE_G + h + 1]
            v = qkva_ref[j:j + 1, 2 * HA * DKA + h * DVA:2 * HA * DKA + (h + 1) * DVA]
            err = (v - jnp.sum(k_col * s, axis=0, keepdims=True)) * beta[j:j + 1, LANE_BETA + h:LANE_BETA + h + 1]
            s = s + k_col * err
            san_ref[j, h] = s
            o_all = jnp.dot(q_rows, s.astype(BF16), preferred_element_type=F32)
            oa_ref[j:j + 1, h * DVA:(h + 1) * DVA] = o_all[j:j + 1, :]
    for h in range(HB):
        q_rows = qkvb_ref[:, h * DKB:(h + 1) * DKB].astype(BF16)
        k_t = qkvb_ref[:, HB * DKB + h * DKB:HB * DKB + (h + 1) * DKB].T
        d_t = jnp.exp(lg_ref[:, h * DKB:(h + 1) * DKB]).T
        for j in range(tb):
            v = qkvb_ref[j:j + 1, 2 * HB * DKB + h * DVB:2 * HB * DKB + (h + 1) * DVB]
            s = sb_ref[j, h] * d_t[:, j:j + 1] + k_t[:, j:j + 1] * v
            sbn_ref[j, h] = s
            o_all = jnp.dot(q_rows, s.astype(BF16), preferred_element_type=F32)
            ob_ref[j:j + 1, h * DVB:(h + 1) * DVB] = o_all[j:j + 1, :]


def _dec_rec_call(qkva, beta, g, qkvb, lg, sa, sb, tb):
    n = qkva.shape[0]
    rows = lambda w: pl.BlockSpec((tb, w), lambda i: (i, 0))
    in_specs = [rows(C_A), rows(LANES), rows(LANES), rows(2 * HB * DKB + HB * DVB), rows(HB * DKB),
                pl.BlockSpec((tb, HA, DKA, DVA), lambda i: (i, 0, 0, 0)),
                pl.BlockSpec((tb, HB, DKB, DVB), lambda i: (i, 0, 0, 0))]
    out_specs = [rows(HA * DVA), rows(HB * DVB),
                 pl.BlockSpec((tb, HA, DKA, DVA), lambda i: (i, 0, 0, 0)),
                 pl.BlockSpec((tb, HB, DKB, DVB), lambda i: (i, 0, 0, 0))]
    out_shape = [jax.ShapeDtypeStruct((n, HA * DVA), F32), jax.ShapeDtypeStruct((n, HB * DVB), F32),
                 jax.ShapeDtypeStruct(sa.shape, F32), jax.ShapeDtypeStruct(sb.shape, F32)]
    return pl.pallas_call(
        functools.partial(_dec_rec_body, tb=tb),
        grid=(n // tb,), in_specs=in_specs, out_specs=out_specs, out_shape=out_shape,
        compiler_params=pltpu.CompilerParams(dimension_semantics=("arbitrary",), vmem_limit_bytes=VMEM_LIMIT),
        name="decode_recurrence",
    )(qkva, beta, g, qkvb, lg, sa, sb)


def _dec_tail_body(x_ref, oa_ref, ob_ref, post_ref, cf_ref, ona_ref, onb_ref, nffn_ref, wcf_ref, bcf_ref, nfin_ref,
                   wa_ref, wb_ref, wo_ref, wup_ref, wdn_ref, y_ref, cfn_ref):
    post = post_ref[...]
    y_a = _branch_a(oa_ref[...], post, ona_ref[...], wa_ref)
    y_b = _branch_b(ob_ref[...], post, onb_ref[...], wb_ref)
    x1 = _mix_residual(x_ref[...], y_a, y_b, post, wo_ref)
    h2 = _rms(x1, nffn_ref[...]).astype(BF16)
    u = jnp.dot(h2, wup_ref[:, 0:D_FF], preferred_element_type=F32)
    gf = jnp.dot(h2, wup_ref[:, D_FF:], preferred_element_type=F32)
    acc = cf_ref[:, 0:D_FF] * wcf_ref[0:1, :]
    for i in range(1, CONV_F - 1):
        acc = acc + cf_ref[:, i * D_FF:(i + 1) * D_FF] * wcf_ref[i:i + 1, :]
    acc = acc + u * wcf_ref[CONV_F - 1:CONV_F, :] + bcf_ref[...]
    for i in range(CONV_F - 2):
        cfn_ref[:, i * D_FF:(i + 1) * D_FF] = cf_ref[:, (i + 1) * D_FF:(i + 2) * D_FF]
    cfn_ref[:, (CONV_F - 2) * D_FF:] = u
    y_ref[...] = _ffn_core(x1, acc, gf, wdn_ref, nfin_ref[...])


def _dec_tail_call(xs, oa, ob, post, cf, p):
    n = xs.shape[0]
    return pl.pallas_call(
        _dec_tail_body,
        out_shape=[jax.ShapeDtypeStruct((n, D_MODEL), F32), jax.ShapeDtypeStruct((n, (CONV_F - 1) * D_FF), F32)],
        compiler_params=pltpu.CompilerParams(vmem_limit_bytes=VMEM_LIMIT),
        name="decode_tail",
    )(xs, oa, ob, post, cf, p["onorm_a"], p["onorm_b"], p["norm_ffn"], p["w_conv_f"], p["b_conv_f"],
      p["norm_final"], p["w_a_out"], p["w_b_out"], p["w_o"], p["w_ffn_in"], p["w_ffn_out"])


PROJ_SIZES = (HA * DKA, HA * DKA, HA * DVA, HA * DVA, HA, HA, HB * DKB, HB * DKB, HB * DVB, HB * DVB, GATE_RANK,
              D_MODEL, D_MODEL)
PROJ_ORDER = (0, 1, 2, 6, 7, 8, 4, 5, 10, None, 3, 9, 11, 12)
RELAYOUT_ROWS = 128


def _relayout_body(w_ref, o_ref):
    offs = [0]
    for s in PROJ_SIZES:
        offs.append(offs[-1] + s)
    dst = 0
    for seg in PROJ_ORDER:
        if seg is None:
            width = (-dst) % LANES
            o_ref[:, dst:dst + width] = jnp.zeros((o_ref.shape[0], width), BF16)
        else:
            width = PROJ_SIZES[seg]
            o_ref[:, dst:dst + width] = w_ref[:, offs[seg]:offs[seg] + width].astype(BF16)
        dst += width
    assert dst == W_IN_COLS


def _relayout_w_in(w):
    rows, cols = w.shape
    return pl.pallas_call(
        _relayout_body,
        grid=(rows // RELAYOUT_ROWS,),
        in_specs=[pl.BlockSpec((RELAYOUT_ROWS, cols), lambda i: (i, 0))],
        out_specs=pl.BlockSpec((RELAYOUT_ROWS, W_IN_COLS), lambda i: (i, 0)),
        out_shape=jax.ShapeDtypeStruct((rows, W_IN_COLS), BF16),
        compiler_params=pltpu.CompilerParams(dimension_semantics=("arbitrary",), vmem_limit_bytes=VMEM_LIMIT),
        name="relayout_w_in",
    )(w)


def _prep_params(l, norm_mix, w_in, w_conv_a, a_log, dt_bias, w_gk2, b_gk, onorm_a, onorm_b, w_a_out, w_b_out,
                 w_o, norm_ffn, w_ffn_in, w_conv_f, b_conv_f, w_ffn_out, norm_final):
    n_small = 2 * HA + GATE_RANK
    w1 = _relayout_w_in(w_in[l].astype(BF16))
    w2 = jnp.zeros((LANES, HB * DKB), F32).at[2 * HA:n_small].set(w_gk2[l]).astype(BF16)
    lane_row = lambda v: jnp.zeros((1, LANES), F32).at[0, LANE_G:LANE_G + HA].set(v)
    return dict(
        norm_mix=norm_mix[l][None], w_conv_a=w_conv_a[l], alog_row=lane_row(a_log[l]), dt_row=lane_row(dt_bias[l]),
        b_gk=b_gk[l][None], onorm_a=onorm_a[l][None], onorm_b=onorm_b[l][None], w1=w1, w2=w2,
        w_a_out=w_a_out[l].astype(BF16), w_b_out=w_b_out[l].astype(BF16), w_o=w_o[l].astype(BF16),
        norm_ffn=norm_ffn[l][None], w_ffn_in=w_ffn_in[l].astype(BF16), w_conv_f=w_conv_f[l],
        b_conv_f=b_conv_f[l][None], w_ffn_out=w_ffn_out[l].astype(BF16), norm_final=norm_final[None])


PROMPT_TILE = 256
META_TILE = 2 * CHUNK
FFN_TILE = 512
DECODE_TILE = 8


def kernel(x_prompt, x_sample, state_delta, state_delta_conv, state_gla, state_ffn_conv, meta_tokens, norm_mix, w_in, w_conv_a, a_log, dt_bias, w_gk2, b_gk, onorm_a, onorm_b, w_a_out, w_b_out, w_o, norm_ffn, w_ffn_in, w_conv_f, b_conv_f, w_ffn_out, norm_final):
    assert w_in.shape[0] == 1, "single layer only"
    l = 0
    p = _prep_params(l, norm_mix, w_in, w_conv_a, a_log, dt_bias, w_gk2, b_gk, onorm_a, onorm_b, w_a_out, w_b_out,
                     w_o, norm_ffn, w_ffn_in, w_conv_f, b_conv_f, w_ffn_out, norm_final)
    n_dec = x_sample.shape[0]

    xm = jnp.concatenate([jnp.zeros((META_TILE - N_META, D_MODEL), F32), meta_tokens.astype(F32)], axis=0)[None]
    x1m, sa0, ca0, sb0 = _mixer_call(xm, jnp.zeros((HA, DKA, DVA), F32), jnp.zeros((HB, DKB, DVB), F32),
                                     jnp.zeros((CONV_A - 1, C_A), F32), p, META_TILE)
    u_tail = _ffn_tail_state_call(x1m[0, META_TILE - SUBLANES:], p)
    cf0 = u_tail[SUBLANES - (CONV_F - 1):][None]

    x1, sa_p, ca_p, sb_p = _mixer_call(x_prompt, sa0[0], sb0[0], ca0[0], p, PROMPT_TILE)
    y_prompt, cf_p = _ffn_call(x1, cf0[0], p, FFN_TILE)

    xs = x_sample.reshape(n_dec, D_MODEL)
    cs = jnp.transpose(state_delta_conv[l], (1, 0, 2))
    cfs = state_ffn_conv[l].reshape(n_dec, (CONV_F - 1) * D_FF)
    qkva, beta, g, qkvb, lg, post, cs_new = _dec_head_call(xs, cs, p)
    oa, ob, sa_s, sb_s = _dec_rec_call(qkva, beta, g, qkvb, lg, state_delta[l], state_gla[l], DECODE_TILE)
    y_s, cf_s = _dec_tail_call(xs, oa, ob, post, cfs, p)

    return (y_prompt, y_s.reshape(n_dec, 1, D_MODEL),
            sa_p[None], ca_p[None], sb_p[None], cf_p[None],
            sa_s[None], jnp.transpose(cs_new, (1, 0, 2))[None], sb_s[None],
            cf_s.reshape(1, n_dec, CONV_F - 1, D_FF))
```

```python
import functools

import jax
import jax.numpy as jnp
from jax import lax
from jax.experimental import pallas as pl
from jax.experimental.pallas import tpu as pltpu

F32 = jnp.float32
BF16 = jnp.bfloat16

D_MODEL = 1024
N_META = 16
CHUNK = 64
HA, DKA, DVA = 4, 128, 128
HB, DKB, DVB = 4, 128, 256
CONV_A = 4
C_A = 2 * HA * DKA + HA * DVA
GATE_RANK = 16
GLA_GATE_NORM = 16.0
D_FF = 2816
CONV_F = 3
EPS = 1e-6
LANES = 128
SUBLANES = 8

O_QKVA = 0
O_QB = O_QKVA + C_A
O_KB = O_QB + HB * DKB
O_VB = O_KB + HB * DKB
O_SMALL = O_VB + HB * DVB
O_POST = O_SMALL + LANES
P_ZA = 0
P_RB = P_ZA + HA * DVA
P_GA = P_RB + HB * DVB
P_GB = P_GA + D_MODEL
W_POST = P_GB + D_MODEL
W_IN_COLS = O_POST + W_POST
LANE_BETA = 0
LANE_G = HA

VMEM_LIMIT = 56 * 1024 * 1024


def _dot(a, b):
    return jnp.dot(a.astype(BF16), b.astype(BF16), preferred_element_type=F32)


def _dot_nt(a, b):
    return lax.dot_general(a.astype(BF16), b.astype(BF16), (((1,), (1,)), ((), ())), preferred_element_type=F32)


def _dot_tn(a, b):
    return lax.dot_general(a.astype(BF16), b.astype(BF16), (((0,), (0,)), ((), ())), preferred_element_type=F32)


def _rms(x, w):
    return x * lax.rsqrt(jnp.mean(x * x, axis=-1, keepdims=True) + EPS) * w


def _silu(x):
    return x * jax.nn.sigmoid(x)


def _gelu_tanh(x):
    return 0.5 * x * (1.0 + jnp.tanh(0.7978845608028654 * (x + 0.044715 * (x * x * x))))


def _l2n(x):
    return x * lax.rsqrt(jnp.sum(x * x, axis=-1, keepdims=True) + EPS)


def _iota2(n, m, axis):
    return lax.broadcasted_iota(jnp.int32, (n, m), axis)


def _gates(small, alog_row, dt_row):
    beta = jax.nn.sigmoid(small)
    g = -jnp.exp(alog_row) * jax.nn.softplus(small + dt_row)
    return beta, g


def _qkv_a_post(conv_out):
    act = _silu(conv_out)
    parts = []
    for h in range(HA):
        parts.append(_l2n(act[:, h * DKA:(h + 1) * DKA]) * (DKA ** -0.5))
    for h in range(HA):
        o = HA * DKA + h * DKA
        parts.append(_l2n(act[:, o:o + DKA]))
    parts.append(act[:, 2 * HA * DKA:])
    return jnp.concatenate(parts, axis=1)


def _branch_a(oa, post, onorm_a, wa_ref):
    pa = [_rms(oa[:, h * DVA:(h + 1) * DVA], onorm_a) for h in range(HA)]
    return _dot(jnp.concatenate(pa, axis=1) * _silu(post[:, P_ZA:P_ZA + HA * DVA]), wa_ref[...])


def _branch_b(ob, post, onorm_b, wb_ref):
    pb = [_rms(ob[:, h * DVB:(h + 1) * DVB], onorm_b) for h in range(HB)]
    return _dot(jnp.concatenate(pb, axis=1) * _silu(post[:, P_RB:P_RB + HB * DVB]), wb_ref[...])


def _mix_residual(x, y_a, y_b, post, wo_ref):
    mix = (jax.nn.sigmoid(post[:, P_GA:P_GA + D_MODEL]) * y_a
           + jax.nn.sigmoid(post[:, P_GB:P_GB + D_MODEL]) * y_b)
    return x + _dot(mix, wo_ref[...])


def _cumsum_chunks(x, c):
    rowi = _iota2(x.shape[0], x.shape[1], 0) % c
    sh = 1
    while sh < c:
        x = x + jnp.where(rowi >= sh, pltpu.roll(x, sh, axis=0), 0.0)
        sh *= 2
    return x


def _delta_prepare(qkv_s, beta_s, gcum_s, gt_s, m_s, qk_s, rhs_s, qe_s, kdec_s, tt, c):
    ri = _iota2(c, 2 * c, 0)
    ci = _iota2(c, 2 * c, 1) % c
    first = _iota2(c, 2 * c, 1) < c
    causal = (ri >= ci).astype(F32)
    strict = (ri > ci).astype(F32)
    zeros = jnp.zeros((c, DKA), F32)
    for cp in range(tt // (2 * c)):
        rows2 = slice(2 * cp * c, (2 * cp + 2) * c)
        for h in range(HA):
            n = cp * HA + h
            hs = slice(h * DKA, (h + 1) * DKA)
            lhs, rhs_t, g_cols = [], [], []
            for par in range(2):
                rows = slice((2 * cp + par) * c, (2 * cp + par + 1) * c)
                g_blk = gcum_s[rows, :]
                q = qkv_s[rows, h * DKA:(h + 1) * DKA]
                k = qkv_s[rows, HA * DKA + h * DKA:HA * DKA + (h + 1) * DKA]
                v = qkv_s[rows, 2 * HA * DKA + h * DVA:2 * HA * DKA + (h + 1) * DVA]
                g_col = g_blk[:, LANE_G + h:LANE_G + h + 1]
                beta_col = beta_s[rows, LANE_BETA + h:LANE_BETA + h + 1]
                eg = jnp.exp(g_col)
                kb = k * beta_col
                lhs.append(jnp.concatenate([kb, q], axis=0))
                rhs_t.append(jnp.concatenate([k, zeros] if par == 0 else [zeros, k], axis=1))
                g_cols.append(g_col)
                rhs_s[rows, h * 2 * DVA:(h + 1) * 2 * DVA] = jnp.concatenate([v * beta_col, kb * eg], axis=1)
                qe_s[rows, hs] = q * eg
                kdec_s[rows, hs] = k * jnp.exp(g_col[c - 1:c, :] - g_col)
            kq = _dot_nt(jnp.concatenate(lhs, axis=1), jnp.concatenate(rhs_t, axis=0))
            g_row = gt_s[LANE_G + h:LANE_G + h + 1, rows2]
            decay = jnp.exp(jnp.minimum(jnp.where(first, g_cols[0], g_cols[1]) - g_row, 0.0))
            m_s[n] = kq[0:c] * (decay * strict)
            qk_s[n] = kq[c:] * (decay * causal)


def _inverse_stages(m_s, p_s, pw_s, n_inst, c):
    ri = _iota2(c, 2 * c, 0)
    lane = _iota2(c, 2 * c, 1)
    ci = lane % c
    left = (lane < c).astype(BF16)
    right = (lane >= c).astype(BF16)

    def bdiag(xp):
        xb = xp.astype(BF16)
        return jnp.concatenate([xb * left, xb * right], axis=0)

    eye = (ri == ci).astype(F32)
    base = min(16, c)
    same = (ri // base) == (ci // base)
    neg_same = jnp.where(same, -1.0, 0.0)
    for n in range(n_inst):
        a = m_s[n] * neg_same
        pw_s[n] = _dot(a, bdiag(a))
        p_s[n] = eye + a
    k = 2
    while k < base:
        for n in range(n_inst):
            p = p_s[n]
            pw = pw_s[n]
            bd = bdiag(pw)
            p_s[n] = p + _dot(p, bd)
            if 2 * k < base:
                pw_s[n] = _dot(pw, bd)
        k *= 2
    size = base
    while size < c:
        nxt = size * 2
        same_nxt = (ri // nxt) == (ci // nxt)
        off = (same_nxt & jnp.logical_not(same)).astype(F32)
        for n in range(n_inst):
            pw_s[n] = _dot(p_s[n], bdiag(m_s[n] * off))
        for n in range(n_inst):
            p = p_s[n]
            p_s[n] = p - _dot(pw_s[n], bdiag(p))
        same = same_nxt
        size = nxt


def _level_ref_rows(bc, half, c):
    w = bc.shape[1]
    parts = []
    if half >= SUBLANES // 2:
        for blk in range(c // (2 * half)):
            mrow = blk * 2 * half + half
            parts.append(jnp.broadcast_to(bc[mrow:mrow + 1, :], (2 * half, w)))
    else:
        sub = _iota2(SUBLANES, w, 0)
        for grp in range(c // SUBLANES):
            acc = None
            for blk in range(SUBLANES // (2 * half)):
                mrow = grp * SUBLANES + blk * 2 * half + half
                b = jnp.broadcast_to(bc[mrow:mrow + 1, :], (SUBLANES, w))
                acc = b if acc is None else jnp.where(sub >= blk * 2 * half, b, acc)
            parts.append(acc)
    return jnp.concatenate(parts, axis=0)


def _gla_prepare_steps(qkb_s, vb_s, bc_s, attn_s, ob_s, pkv_s, qeb_s, tt, c):
    ri = _iota2(c, 2 * c, 0)
    ci = _iota2(c, 2 * c, 1) % c
    first = _iota2(c, 2 * c, 1) < c
    n_ch = tt // c
    zeros_k = jnp.zeros((c, DKB), BF16)
    zeros_v = jnp.zeros((c, DVB), BF16)

    def operands(ch, h):
        rows = slice(ch * c, (ch + 1) * c)
        q = qkb_s[rows, h * DKB:(h + 1) * DKB]
        k = qkb_s[rows, HB * DKB + h * DKB:HB * DKB + (h + 1) * DKB]
        return rows, q, k

    def diag():
        for cp in range(n_ch // 2):
            for h in range(HB):
                d = []
                for par in range(2):
                    _, q, k = operands(2 * cp + par, h)
                    d.append(jnp.sum(q * k, axis=-1, keepdims=True))
                attn_s[cp * HB + h] = jnp.where(ri == ci, jnp.where(first, d[0], d[1]), 0.0)

    def level(half):
        valid = (((ri // (2 * half)) == (ci // (2 * half))) & ((ri % (2 * half)) >= half)
                 & ((ci % (2 * half)) < half)).astype(F32)
        for cp in range(n_ch // 2):
            e_all = []
            for par in range(2):
                bc_all = bc_s[(2 * cp + par) * c:(2 * cp + par + 1) * c, :]
                e_all.append(jnp.exp(-jnp.abs(bc_all - _level_ref_rows(bc_all, half, c))))
            for h in range(HB):
                qs, ks = [], []
                for par in range(2):
                    _, q, k = operands(2 * cp + par, h)
                    e = e_all[par][:, h * DKB:(h + 1) * DKB]
                    qs.append((q * e).astype(BF16))
                    ks.append((k * e).astype(BF16))
                k_bd = jnp.concatenate([jnp.concatenate([ks[0], zeros_k], axis=1),
                                        jnp.concatenate([zeros_k, ks[1]], axis=1)], axis=0)
                attn_s[cp * HB + h] += _dot_nt(jnp.concatenate(qs, axis=1), k_bd) * valid

    def finish():
        for cp in range(n_ch // 2):
            for h in range(HB):
                vs = []
                for par in range(2):
                    rows, q, k = operands(2 * cp + par, h)
                    v = vb_s[rows, h * DVB:(h + 1) * DVB]
                    bc = bc_s[rows, h * DKB:(h + 1) * DKB]
                    vs.append(v.astype(BF16))
                    pkv_s[(2 * cp + par) * HB + h] = _dot_tn(k * jnp.exp(bc[c - 1:c, :] - bc), v)
                    qeb_s[rows, h * DKB:(h + 1) * DKB] = q * jnp.exp(bc)
                v_bd = jnp.concatenate([jnp.concatenate([vs[0], zeros_v], axis=1),
                                        jnp.concatenate([zeros_v, vs[1]], axis=1)], axis=0)
                o2 = _dot(attn_s[cp * HB + h], v_bd)
                ob_s[2 * cp * c:(2 * cp + 1) * c, h * DVB:(h + 1) * DVB] = o2[:, 0:DVB]
                ob_s[(2 * cp + 1) * c:(2 * cp + 2) * c, h * DVB:(h + 1) * DVB] = o2[:, DVB:]

    steps = [diag]
    half = c // 2
    while half >= 1:
        steps.append(functools.partial(level, half))
        half //= 2
    steps.append(finish)
    return steps


def _mixer_body(x_ref, sa0_ref, sb0_ref, ca0_ref, nmix_ref, wca_ref, alog_ref, dt_ref, bgk_ref, ona_ref, onb_ref,
                w1_ref, w2_ref, wa_ref, wb_ref, wo_ref,
                x1_ref, sa_ref, ca_ref, sb_ref,
                cbuf, hb_s, post_s, qkv_s, beta_s, gcum_s, gt_s, qkb_s, vb_s, bc_s, oa_s, ob_s,
                m_s, p_s, pw_s, qk_s, rhs_s, uw_s, qe_s, kdec_s, attn_s, pkv_s, qeb_s, gated_s, mix_s, ktu_s, ktw_s, *, tt, c):
    t_idx = pl.program_id(1)
    n_ch = tt // c

    @pl.when(t_idx == 0)
    def _():
        sa_ref[...] = sa0_ref[...]
        sb_ref[...] = sb0_ref[...]
        cbuf[SUBLANES - (CONV_A - 1):SUBLANES, :] = ca0_ref[...]

    rb = min(tt, 64)
    for r in range(tt // rb):
        rr = slice(r * rb, (r + 1) * rb)
        hb_s[rr, :] = _rms(x_ref[rr, :], nmix_ref[...]).astype(BF16)
    hb = hb_s[...]
    cbuf[SUBLANES:SUBLANES + tt, :] = jnp.dot(hb, w1_ref[:, O_QKVA:O_QKVA + C_A], preferred_element_type=F32)
    base = SUBLANES - (CONV_A - 1)
    n_blk = C_A // LANES
    post_w = 2 * LANES
    n_post = W_POST // post_w
    for j in range(max(n_blk, n_post)):
        if j < n_post:
            pc = slice(j * post_w, (j + 1) * post_w)
            post_s[:, pc] = jnp.dot(hb, w1_ref[:, O_POST + j * post_w:O_POST + (j + 1) * post_w],
                                    preferred_element_type=F32)
        if j < n_blk:
            cols = slice(j * LANES, (j + 1) * LANES)
            full = cbuf[:, cols]
            assert CONV_A == 4
            prev = pltpu.roll(full, 1, axis=0)
            near = full * wca_ref[3:4, cols] + prev * wca_ref[2:3, cols]
            far = full * wca_ref[1:2, cols] + prev * wca_ref[0:1, cols]
            acc = (near + pltpu.roll(far, 2, axis=0))[SUBLANES:]
            act = _silu(acc)
            if j < HA:
                act = _l2n(act) * (DKA ** -0.5)
            elif j < 2 * HA:
                act = _l2n(act)
            qkv_s[:, cols] = act
    tail = cbuf[tt + base:tt + SUBLANES, :]
    cbuf[base:SUBLANES, :] = tail
    ca_ref[...] = tail
    small = jnp.dot(hb, w1_ref[:, O_SMALL:O_SMALL + LANES], preferred_element_type=F32)
    beta, g = _gates(small, alog_ref[...], dt_ref[...])
    beta_s[...] = beta
    g_cum = _cumsum_chunks(g, c)
    gcum_s[...] = g_cum
    gt_s[...] = g_cum.T

    _delta_prepare(qkv_s, beta_s, gcum_s, gt_s, m_s, qk_s, rhs_s, qe_s, kdec_s, tt, c)
    n_pair = (n_ch // 2) * HA
    _inverse_stages(m_s, p_s, pw_s, n_pair, c)
    zeros_uw = jnp.zeros((c, 2 * DVA), BF16)

    def pair_bdiag(ref, cp, cols):
        x_e = ref[2 * cp * c:(2 * cp + 1) * c, cols].astype(BF16)
        x_o = ref[(2 * cp + 1) * c:(2 * cp + 2) * c, cols].astype(BF16)
        return jnp.concatenate([jnp.concatenate([x_e, zeros_uw], axis=1),
                                jnp.concatenate([zeros_uw, x_o], axis=1)], axis=0)

    for cp in range(n_ch // 2):
        rows_e = slice(2 * cp * c, (2 * cp + 1) * c)
        rows_o = slice((2 * cp + 1) * c, (2 * cp + 2) * c)
        for h in range(HA):
            cols = slice(h * 2 * DVA, (h + 1) * 2 * DVA)
            uw2 = _dot(p_s[cp * HA + h], pair_bdiag(rhs_s, cp, cols))
            uw_s[rows_e, cols] = uw2[:, 0:2 * DVA]
            uw_s[rows_o, cols] = uw2[:, 2 * DVA:]
    for cp in range(n_ch // 2):
        for h in range(HA):
            hs = slice(h * DKA, (h + 1) * DKA)
            cols = slice(h * 2 * DVA, (h + 1) * 2 * DVA)
            att2 = _dot(qk_s[cp * HA + h], pair_bdiag(uw_s, cp, cols))
            for par in range(2):
                ch = 2 * cp + par
                rows = slice(ch * c, (ch + 1) * c)
                att_uw = att2[:, par * 2 * DVA:(par + 1) * 2 * DVA]
                oa_s[rows, h * DVA:(h + 1) * DVA] = att_uw[:, 0:DVA]
                qe_s[rows, hs] = qe_s[rows, hs] - att_uw[:, DVA:]
                kt_uw = _dot_tn(kdec_s[rows, hs], uw_s[rows, cols])
                ktu_s[ch * HA + h] = kt_uw[:, 0:DVA]
                ktw_s[ch * HA + h] = kt_uw[:, DVA:]

    small_b = small.astype(BF16)
    for h in range(HB):
        hs = slice(h * DKB, (h + 1) * DKB)
        lg_pre = jnp.dot(small_b, w2_ref[:, hs], preferred_element_type=F32) + bgk_ref[:, hs]
        bc_s[:, hs] = _cumsum_chunks(jax.nn.log_sigmoid(lg_pre) / GLA_GATE_NORM, c)
    qkb_s[:, 0:HB * DKB] = jnp.dot(hb, w1_ref[:, O_QB:O_QB + HB * DKB], preferred_element_type=F32) * (DKB ** -0.5)
    qkb_s[:, HB * DKB:] = jnp.dot(hb, w1_ref[:, O_KB:O_KB + HB * DKB], preferred_element_type=F32)
    vb_s[...] = jnp.dot(hb, w1_ref[:, O_VB:O_VB + HB * DVB], preferred_element_type=F32)

    for step in _gla_prepare_steps(qkb_s, vb_s, bc_s, attn_s, ob_s, pkv_s, qeb_s, tt, c):
        step()

    for ch in range(n_ch):
        rows = slice(ch * c, (ch + 1) * c)
        last = slice(ch * c + c - 1, ch * c + c)
        for h in range(HA):
            n = ch * HA + h
            hs = slice(h * DKA, (h + 1) * DKA)
            s = sa_ref[h]
            prod = _dot(jnp.concatenate([ktw_s[n], qe_s[rows, hs]], axis=0), s)
            oa_s[rows, h * DVA:(h + 1) * DVA] += prod[DKA:]
            g_last = gcum_s[last, LANE_G + h:LANE_G + h + 1]
            sa_ref[h] = s * jnp.exp(g_last) + (ktu_s[n] - prod[0:DKA])

    for h in range(HA):
        hs = slice(h * DVA, (h + 1) * DVA)
        gated_s[:, hs] = (_rms(oa_s[:, hs], ona_ref[...]) * _silu(post_s[:, P_ZA + h * DVA:P_ZA + (h + 1) * DVA])
                          ).astype(BF16)

    for ch in range(n_ch):
        rows = slice(ch * c, (ch + 1) * c)
        last = slice(ch * c + c - 1, ch * c + c)
        for h in range(HB):
            s = sb_ref[h]
            ob_s[rows, h * DVB:(h + 1) * DVB] += _dot(qeb_s[rows, h * DKB:(h + 1) * DKB], s)
            e_col = jnp.exp(bc_s[last, h * DKB:(h + 1) * DKB]).T
            sb_ref[h] = s * e_col + pkv_s[ch * HB + h]

    for h in range(HB):
        hs = slice(h * DVB, (h + 1) * DVB)
        gated_s[:, HA * DVA + h * DVB:HA * DVA + (h + 1) * DVB] = (
            _rms(ob_s[:, hs], onb_ref[...]) * _silu(post_s[:, P_RB + h * DVB:P_RB + (h + 1) * DVB])).astype(BF16)
    out_w = 2 * LANES
    ga = gated_s[:, 0:HA * DVA]
    gb = gated_s[:, HA * DVA:]
    for n in range(D_MODEL // out_w):
        ns = slice(n * out_w, (n + 1) * out_w)
        y_a = jnp.dot(ga, wa_ref[:, ns], preferred_element_type=F32)
        y_b = jnp.dot(gb, wb_ref[:, ns], preferred_element_type=F32)
        mix_s[:, ns] = (jax.nn.sigmoid(post_s[:, P_GA + n * out_w:P_GA + (n + 1) * out_w]) * y_a
                        + jax.nn.sigmoid(post_s[:, P_GB + n * out_w:P_GB + (n + 1) * out_w]) * y_b).astype(BF16)
    mix = mix_s[...]
    for n in range(D_MODEL // out_w):
        ns = slice(n * out_w, (n + 1) * out_w)
        x1_ref[:, ns] = x_ref[:, ns] + jnp.dot(mix, wo_ref[:, ns], preferred_element_type=F32)


def _const_spec(shape):
    nd = len(shape)
    return pl.BlockSpec(shape, lambda *_: (0,) * nd, pipeline_mode=pl.Buffered(1))


def _mixer_call(x, sa0, sb0, ca0, p, tt):
    b, t, _ = x.shape
    c = CHUNK
    nt = t // tt
    n_inst = (tt // c) * HA
    row = lambda n: _const_spec((1, n))
    in_specs = [
        pl.BlockSpec((None, tt, D_MODEL), lambda i, j: (i, j, 0)),
        _const_spec((HA, DKA, DVA)), _const_spec((HB, DKB, DVB)), _const_spec((CONV_A - 1, C_A)),
        row(D_MODEL), _const_spec((CONV_A, C_A)), row(LANES), row(LANES), row(HB * DKB), row(DVA), row(DVB),
        _const_spec((D_MODEL, W_IN_COLS)), _const_spec((LANES, HB * DKB)),
        _const_spec((HA * DVA, D_MODEL)), _const_spec((HB * DVB, D_MODEL)), _const_spec((D_MODEL, D_MODEL)),
    ]
    out_specs = [
        pl.BlockSpec((None, tt, D_MODEL), lambda i, j: (i, j, 0)),
        pl.BlockSpec((None, HA, DKA, DVA), lambda i, j: (i, 0, 0, 0)),
        pl.BlockSpec((None, CONV_A - 1, C_A), lambda i, j: (i, 0, 0)),
        pl.BlockSpec((None, HB, DKB, DVB), lambda i, j: (i, 0, 0, 0)),
    ]
    out_shape = [
        jax.ShapeDtypeStruct((b, t, D_MODEL), F32),
        jax.ShapeDtypeStruct((b, HA, DKA, DVA), F32),
        jax.ShapeDtypeStruct((b, CONV_A - 1, C_A), F32),
        jax.ShapeDtypeStruct((b, HB, DKB, DVB), F32),
    ]
    assert (tt // c) % 2 == 0, "mixer A handles chunks in pairs"
    pairs = lambda: pltpu.VMEM((n_inst // 2, c, 2 * c), F32)
    scratch = [
        pltpu.VMEM((tt + SUBLANES, C_A), F32),
        pltpu.VMEM((tt, D_MODEL), BF16),
        pltpu.VMEM((tt, W_POST), F32),
        pltpu.VMEM((tt, C_A), F32),
        pltpu.VMEM((tt, LANES), F32),
        pltpu.VMEM((tt, LANES), F32),
        pltpu.VMEM((LANES, tt), F32),
        pltpu.VMEM((tt, 2 * HB * DKB), F32),
        pltpu.VMEM((tt, HB * DVB), F32),
        pltpu.VMEM((tt, HB * DKB), F32),
        pltpu.VMEM((tt, HA * DVA), F32),
        pltpu.VMEM((tt, HB * DVB), F32),
        pairs(), pairs(), pairs(), pairs(),
        pltpu.VMEM((tt, HA * 2 * DVA), F32),
        pltpu.VMEM((tt, HA * 2 * DVA), F32),
        pltpu.VMEM((tt, HA * DKA), F32),
        pltpu.VMEM((tt, HA * DKA), F32),
        pairs(),
        pltpu.VMEM((n_inst, DKB, DVB), F32),
        pltpu.VMEM((tt, HB * DKB), F32),
        pltpu.VMEM((tt, HA * DVA + HB * DVB), BF16),
        pltpu.VMEM((tt, D_MODEL), BF16),
        pltpu.VMEM((n_inst, DKA, DVA), F32),
        pltpu.VMEM((n_inst, DKA, DVA), F32),
    ]
    return pl.pallas_call(
        functools.partial(_mixer_body, tt=tt, c=c),
        grid=(b, nt), in_specs=in_specs, out_specs=out_specs, out_shape=out_shape, scratch_shapes=scratch,
        compiler_params=pltpu.CompilerParams(dimension_semantics=("arbitrary", "arbitrary"),
                                             vmem_limit_bytes=VMEM_LIMIT),
        name="mixer",
    )(x, sa0, sb0, ca0, p["norm_mix"], p["w_conv_a"], p["alog_row"], p["dt_row"], p["b_gk"], p["onorm_a"],
      p["onorm_b"], p["w1"], p["w2"], p["w_a_out"], p["w_b_out"], p["w_o"])


def _ffn_core(x1, u_conv, gf, wdn_ref, nfin):
    act = _gelu_tanh(u_conv) * gf
    x2 = x1 + _dot(act, wdn_ref[...])
    return _rms(x2, nfin)


def _ffn_body(x1_ref, cf0_ref, nffn_ref, wcf_ref, bcf_ref, nfin_ref, wup_ref, wdn_ref,
              y_ref, cf_ref, ubuf, *, tt):
    t_idx = pl.program_id(1)
    base = SUBLANES - (CONV_F - 1)

    @pl.when(t_idx == 0)
    def _():
        ubuf[base:SUBLANES, :] = cf0_ref[...]

    x1 = x1_ref[...]
    h2 = _rms(x1, nffn_ref[...]).astype(BF16)
    ubuf[SUBLANES:SUBLANES + tt, :] = jnp.dot(h2, wup_ref[:, 0:D_FF], preferred_element_type=F32)
    gf = jnp.dot(h2, wup_ref[:, D_FF:], preferred_element_type=F32)
    acc = ubuf[base:base + tt, :] * wcf_ref[0:1, :]
    for i in range(1, CONV_F):
        acc = acc + ubuf[base + i:base + i + tt, :] * wcf_ref[i:i + 1, :]
    acc = acc + bcf_ref[...]
    tail = ubuf[tt + base:tt + SUBLANES, :]
    ubuf[base:SUBLANES, :] = tail
    cf_ref[...] = tail
    y_ref[...] = _ffn_core(x1, acc, gf, wdn_ref, nfin_ref[...])


def _ffn_tail_state_body(x1_ref, nffn_ref, wup_ref, u_ref):
    h2 = _rms(x1_ref[...], nffn_ref[...]).astype(BF16)
    u_ref[...] = jnp.dot(h2, wup_ref[...], preferred_element_type=F32)


def _ffn_tail_state_call(x1_rows, p):
    return pl.pallas_call(
        _ffn_tail_state_body,
        grid=(1,),
        in_specs=[pl.BlockSpec((SUBLANES, D_MODEL), lambda i: (0, 0)), pl.BlockSpec((1, D_MODEL), lambda i: (0, 0)),
                  pl.BlockSpec((D_MODEL, D_FF), lambda i: (0, 0))],
        out_specs=pl.BlockSpec((SUBLANES, D_FF), lambda i: (0, 0)),
        out_shape=jax.ShapeDtypeStruct((SUBLANES, D_FF), F32),
        compiler_params=pltpu.CompilerParams(dimension_semantics=("arbitrary",), vmem_limit_bytes=VMEM_LIMIT),
        name="meta_ffn_state",
    )(x1_rows, p["norm_ffn"], p["w_ffn_in"])


def _ffn_call(x1, cf0, p, tt):
    b, t, _ = x1.shape
    nt = t // tt
    row = lambda n: _const_spec((1, n))
    in_specs = [
        pl.BlockSpec((None, tt, D_MODEL), lambda i, j: (i, j, 0)),
        _const_spec((CONV_F - 1, D_FF)), row(D_MODEL), _const_spec((CONV_F, D_FF)), row(D_FF), row(D_MODEL),
        _const_spec((D_MODEL, 2 * D_FF)), _const_spec((D_FF, D_MODEL)),
    ]
    out_specs = [
        pl.BlockSpec((None, tt, D_MODEL), lambda i, j: (i, j, 0)),
        pl.BlockSpec((None, CONV_F - 1, D_FF), lambda i, j: (i, 0, 0)),
    ]
    out_shape = [jax.ShapeDtypeStruct((b, t, D_MODEL), F32), jax.ShapeDtypeStruct((b, CONV_F - 1, D_FF), F32)]
    return pl.pallas_call(
        functools.partial(_ffn_body, tt=tt),
        grid=(b, nt), in_specs=in_specs, out_specs=out_specs, out_shape=out_shape,
        scratch_shapes=[pltpu.VMEM((tt + SUBLANES, D_FF), F32)],
        compiler_params=pltpu.CompilerParams(dimension_semantics=("arbitrary", "arbitrary"),
                                             vmem_limit_bytes=VMEM_LIMIT),
        name="convffn",
    )(x1, cf0, p["norm_ffn"], p["w_conv_f"], p["b_conv_f"], p["norm_final"], p["w_ffn_in"], p["w_ffn_out"])


def _dec_head_body(x_ref, cs_ref, nmix_ref, wca_ref, alog_ref, dt_ref, bgk_ref, w1_ref, w2_ref,
                   qkva_ref, beta_ref, g_ref, qkvb_ref, lg_ref, post_ref, csn_ref):
    x = x_ref[...]
    hb = _rms(x, nmix_ref[...]).astype(BF16)
    pre = jnp.dot(hb, w1_ref[:, O_QKVA:O_QKVA + C_A], preferred_element_type=F32)
    acc = cs_ref[0] * wca_ref[0:1, :]
    for i in range(1, CONV_A - 1):
        acc = acc + cs_ref[i] * wca_ref[i:i + 1, :]
    acc = acc + pre * wca_ref[CONV_A - 1:CONV_A, :]
    for i in range(CONV_A - 2):
        csn_ref[i] = cs_ref[i + 1]
    csn_ref[CONV_A - 2] = pre
    qkva_ref[...] = _qkv_a_post(acc)
    small = jnp.dot(hb, w1_ref[:, O_SMALL:O_SMALL + LANES], preferred_element_type=F32)
    beta, g = _gates(small, alog_ref[...], dt_ref[...])
    beta_ref[...] = beta
    g_ref[...] = g
    lg_ref[...] = jax.nn.log_sigmoid(_dot(small, w2_ref[...]) + bgk_ref[...]) / GLA_GATE_NORM
    qkb = jnp.dot(hb, w1_ref[:, O_QB:O_QB + 2 * HB * DKB], preferred_element_type=F32)
    qkvb_ref[:, 0:HB * DKB] = qkb[:, 0:HB * DKB] * (DKB ** -0.5)
    qkvb_ref[:, HB * DKB:2 * HB * DKB] = qkb[:, HB * DKB:]
    qkvb_ref[:, 2 * HB * DKB:] = jnp.dot(hb, w1_ref[:, O_VB:O_VB + HB * DVB], preferred_element_type=F32)
    post_ref[...] = jnp.dot(hb, w1_ref[:, O_POST:O_POST + W_POST], preferred_element_type=F32)


def _dec_head_call(xs, cs, p):
    n = xs.shape[0]
    shapes = [(n, C_A), (n, LANES), (n, LANES), (n, 2 * HB * DKB + HB * DVB), (n, HB * DKB), (n, W_POST),
              (CONV_A - 1, n, C_A)]
    return pl.pallas_call(
        _dec_head_body,
        out_shape=[jax.ShapeDtypeStruct(s, F32) for s in shapes],
        compiler_params=pltpu.CompilerParams(vmem_limit_bytes=VMEM_LIMIT),
        name="decode_head",
    )(xs, cs, p["norm_mix"], p["w_conv_a"], p["alog_row"], p["dt_row"], p["b_gk"], p["w1"], p["w2"])


def _dec_rec_body(qkva_ref, beta_ref, g_ref, qkvb_ref, lg_ref, sa_ref, sb_ref,
                  oa_ref, ob_ref, san_ref, sbn_ref, *, tb):
    beta = beta_ref[...]
    eg = jnp.exp(g_ref[...])
    for h in range(HA):
        q_rows = qkva_ref[:, h * DKA:(h + 1) * DKA].astype(BF16)
        k_t = qkva_ref[:, HA * DKA + h * DKA:HA * DKA + (h + 1) * DKA].T
        for j in range(tb):
            k_col = k_t[:, j:j + 1]
            s = sa_ref[j, h] * eg[j:j + 1, LANE_G + h:LANE_G + h + 1]
            v = qkva_ref[j:j + 1, 2 * HA * DKA + h * DVA:2 * HA * DKA + (h + 1) * DVA]
            err = (v - jnp.sum(k_col * s, axis=0, keepdims=True)) * beta[j:j + 1, LANE_BETA + h:LANE_BETA + h + 1]
            s = s + k_col * err
            san_ref[j, h] = s
            o_all = jnp.dot(q_rows, s.astype(BF16), preferred_element_type=F32)
            oa_ref[j:j + 1, h * DVA:(h + 1) * DVA] = o_all[j:j + 1, :]
    for h in range(HB):
        q_rows = qkvb_ref[:, h * DKB:(h + 1) * DKB].astype(BF16)
        k_t = qkvb_ref[:, HB * DKB + h * DKB:HB * DKB + (h + 1) * DKB].T
        d_t = jnp.exp(lg_ref[:, h * DKB:(h + 1) * DKB]).T
        for j in range(tb):
            v = qkvb_ref[j:j + 1, 2 * HB * DKB + h * DVB:2 * HB * DKB + (h + 1) * DVB]
            s = sb_ref[j, h] * d_t[:, j:j + 1] + k_t[:, j:j + 1] * v
            sbn_ref[j, h] = s
            o_all = jnp.dot(q_rows, s.astype(BF16), preferred_element_type=F32)
            ob_ref[j:j + 1, h * DVB:(h + 1) * DVB] = o_all[j:j + 1, :]


def _dec_rec_call(qkva, beta, g, qkvb, lg, sa, sb, tb):
    n = qkva.shape[0]
    rows = lambda w: pl.BlockSpec((tb, w), lambda i: (i, 0))
    in_specs = [rows(C_A), rows(LANES), rows(LANES), rows(2 * HB * DKB + HB * DVB), rows(HB * DKB),
                pl.BlockSpec((tb, HA, DKA, DVA), lambda i: (i, 0, 0, 0)),
                pl.BlockSpec((tb, HB, DKB, DVB), lambda i: (i, 0, 0, 0))]
    out_specs = [rows(HA * DVA), rows(HB * DVB),
                 pl.BlockSpec((tb, HA, DKA, DVA), lambda i: (i, 0, 0, 0)),
                 pl.BlockSpec((tb, HB, DKB, DVB), lambda i: (i, 0, 0, 0))]
    out_shape = [jax.ShapeDtypeStruct((n, HA * DVA), F32), jax.ShapeDtypeStruct((n, HB * DVB), F32),
                 jax.ShapeDtypeStruct(sa.shape, F32), jax.ShapeDtypeStruct(sb.shape, F32)]
    return pl.pallas_call(
        functools.partial(_dec_rec_body, tb=tb),
        grid=(n // tb,), in_specs=in_specs, out_specs=out_specs, out_shape=out_shape,
        compiler_params=pltpu.CompilerParams(dimension_semantics=("arbitrary",), vmem_limit_bytes=VMEM_LIMIT),
        name="decode_recurrence",
    )(qkva, beta, g, qkvb, lg, sa, sb)


def _dec_tail_body(x_ref, oa_ref, ob_ref, post_ref, cf_ref, ona_ref, onb_ref, nffn_ref, wcf_ref, bcf_ref, nfin_ref,
                   wa_ref, wb_ref, wo_ref, wup_ref, wdn_ref, y_ref, cfn_ref):
    post = post_ref[...]
    y_a = _branch_a(oa_ref[...], post, ona_ref[...], wa_ref)
    y_b = _branch_b(ob_ref[...], post, onb_ref[...], wb_ref)
    x1 = _mix_residual(x_ref[...], y_a, y_b, post, wo_ref)
    h2 = _rms(x1, nffn_ref[...]).astype(BF16)
    u = jnp.dot(h2, wup_ref[:, 0:D_FF], preferred_element_type=F32)
    gf = jnp.dot(h2, wup_ref[:, D_FF:], preferred_element_type=F32)
    acc = cf_ref[:, 0:D_FF] * wcf_ref[0:1, :]
    for i in range(1, CONV_F - 1):
        acc = acc + cf_ref[:, i * D_FF:(i + 1) * D_FF] * wcf_ref[i:i + 1, :]
    acc = acc + u * wcf_ref[CONV_F - 1:CONV_F, :] + bcf_ref[...]
    for i in range(CONV_F - 2):
        cfn_ref[:, i * D_FF:(i + 1) * D_FF] = cf_ref[:, (i + 1) * D_FF:(i + 2) * D_FF]
    cfn_ref[:, (CONV_F - 2) * D_FF:] = u
    y_ref[...] = _ffn_core(x1, acc, gf, wdn_ref, nfin_ref[...])


def _dec_tail_call(xs, oa, ob, post, cf, p):
    n = xs.shape[0]
    return pl.pallas_call(
        _dec_tail_body,
        out_shape=[jax.ShapeDtypeStruct((n, D_MODEL), F32), jax.ShapeDtypeStruct((n, (CONV_F - 1) * D_FF), F32)],
        compiler_params=pltpu.CompilerParams(vmem_limit_bytes=VMEM_LIMIT),
        name="decode_tail",
    )(xs, oa, ob, post, cf, p["onorm_a"], p["onorm_b"], p["norm_ffn"], p["w_conv_f"], p["b_conv_f"],
      p["norm_final"], p["w_a_out"], p["w_b_out"], p["w_o"], p["w_ffn_in"], p["w_ffn_out"])


PROJ_SIZES = (HA * DKA, HA * DKA, HA * DVA, HA * DVA, HA, HA, HB * DKB, HB * DKB, HB * DVB, HB * DVB, GATE_RANK,
              D_MODEL, D_MODEL)
PROJ_ORDER = (0, 1, 2, 6, 7, 8, 4, 5, 10, None, 3, 9, 11, 12)
RELAYOUT_ROWS = 128


def _relayout_body(w_ref, o_ref):
    offs = [0]
    for s in PROJ_SIZES:
        offs.append(offs[-1] + s)
    dst = 0
    for seg in PROJ_ORDER:
        if seg is None:
            width = (-dst) % LANES
            o_ref[:, dst:dst + width] = jnp.zeros((o_ref.shape[0], width), BF16)
        else:
            width = PROJ_SIZES[seg]
            o_ref[:, dst:dst + width] = w_ref[:, offs[seg]:offs[seg] + width].astype(BF16)
        dst += width
    assert dst == W_IN_COLS


def _relayout_w_in(w):
    rows, cols = w.shape
    return pl.pallas_call(
        _relayout_body,
        grid=(rows // RELAYOUT_ROWS,),
        in_specs=[pl.BlockSpec((RELAYOUT_ROWS, cols), lambda i: (i, 0))],
        out_specs=pl.BlockSpec((RELAYOUT_ROWS, W_IN_COLS), lambda i: (i, 0)),
        out_shape=jax.ShapeDtypeStruct((rows, W_IN_COLS), BF16),
        compiler_params=pltpu.CompilerParams(dimension_semantics=("arbitrary",), vmem_limit_bytes=VMEM_LIMIT),
        name="relayout_w_in",
    )(w)


def _prep_params(l, norm_mix, w_in, w_conv_a, a_log, dt_bias, w_gk2, b_gk, onorm_a, onorm_b, w_a_out, w_b_out,
                 w_o, norm_ffn, w_ffn_in, w_conv_f, b_conv_f, w_ffn_out, norm_final):
    n_small = 2 * HA + GATE_RANK
    w1 = _relayout_w_in(w_in[l].astype(BF16))
    w2 = jnp.zeros((LANES, HB * DKB), F32).at[2 * HA:n_small].set(w_gk2[l]).astype(BF16)
    lane_row = lambda v: jnp.zeros((1, LANES), F32).at[0, LANE_G:LANE_G + HA].set(v)
    return dict(
        norm_mix=norm_mix[l][None], w_conv_a=w_conv_a[l], alog_row=lane_row(a_log[l]), dt_row=lane_row(dt_bias[l]),
        b_gk=b_gk[l][None], onorm_a=onorm_a[l][None], onorm_b=onorm_b[l][None], w1=w1, w2=w2,
        w_a_out=w_a_out[l].astype(BF16), w_b_out=w_b_out[l].astype(BF16), w_o=w_o[l].astype(BF16),
        norm_ffn=norm_ffn[l][None], w_ffn_in=w_ffn_in[l].astype(BF16), w_conv_f=w_conv_f[l],
        b_conv_f=b_conv_f[l][None], w_ffn_out=w_ffn_out[l].astype(BF16), norm_final=norm_final[None])


PROMPT_TILE = 256
META_TILE = 2 * CHUNK
FFN_TILE = 512
DECODE_TILE = 8


def kernel(x_prompt, x_sample, state_delta, state_delta_conv, state_gla, state_ffn_conv, meta_tokens, norm_mix, w_in, w_conv_a, a_log, dt_bias, w_gk2, b_gk, onorm_a, onorm_b, w_a_out, w_b_out, w_o, norm_ffn, w_ffn_in, w_conv_f, b_conv_f, w_ffn_out, norm_final):
    assert w_in.shape[0] == 1, "single layer only"
    l = 0
    p = _prep_params(l, norm_mix, w_in, w_conv_a, a_log, dt_bias, w_gk2, b_gk, onorm_a, onorm_b, w_a_out, w_b_out,
                     w_o, norm_ffn, w_ffn_in, w_conv_f, b_conv_f, w_ffn_out, norm_final)
    n_dec = x_sample.shape[0]

    xm = jnp.concatenate([jnp.zeros((META_TILE - N_META, D_MODEL), F32), meta_tokens.astype(F32)], axis=0)[None]
    x1m, sa0, ca0, sb0 = _mixer_call(xm, jnp.zeros((HA, DKA, DVA), F32), jnp.zeros((HB, DKB, DVB), F32),
                                     jnp.zeros((CONV_A - 1, C_A), F32), p, META_TILE)
    u_tail = _ffn_tail_state_call(x1m[0, META_TILE - SUBLANES:], p)
    cf0 = u_tail[SUBLANES - (CONV_F - 1):][None]

    x1, sa_p, ca_p, sb_p = _mixer_call(x_prompt, sa0[0], sb0[0], ca0[0], p, PROMPT_TILE)
    y_prompt, cf_p = _ffn_call(x1, cf0[0], p, FFN_TILE)

    xs = x_sample.reshape(n_dec, D_MODEL)
    cs = jnp.transpose(state_delta_conv[l], (1, 0, 2))
    cfs = state_ffn_conv[l].reshape(n_dec, (CONV_F - 1) * D_FF)
    qkva, beta, g, qkvb, lg, post, cs_new = _dec_head_call(xs, cs, p)
    oa, ob, sa_s, sb_s = _dec_rec_call(qkva, beta, g, qkvb, lg, state_delta[l], state_gla[l], DECODE_TILE)
    y_s, cf_s = _dec_tail_call(xs, oa, ob, post, cfs, p)

    return (y_prompt, y_s.reshape(n_dec, 1, D_MODEL),
            sa_p[None], ca_p[None], sb_p[None], cf_p[None],
            sa_s[None], jnp.transpose(cs_new, (1, 0, 2))[None], sb_s[None],
            cf_s.reshape(1, n_dec, CONV_F - 1, D_FF))
```

```python
import functools

import jax
import jax.numpy as jnp
from jax import lax
from jax.experimental import pallas as pl
from jax.experimental.pallas import tpu as pltpu

F32 = jnp.float32
BF16 = jnp.bfloat16

D_MODEL = 1024
N_META = 16
CHUNK = 64
HA, DKA, DVA = 4, 128, 128
HB, DKB, DVB = 4, 128, 256
CONV_A = 4
C_A = 2 * HA * DKA + HA * DVA
GATE_RANK = 16
GLA_GATE_NORM = 16.0
D_FF = 2816
CONV_F = 3
EPS = 1e-6
LANES = 128
SUBLANES = 8

O_QKVA = 0
O_QB = O_QKVA + C_A
O_KB = O_QB + HB * DKB
O_VB = O_KB + HB * DKB
O_SMALL = O_VB + HB * DVB
O_POST = O_SMALL + LANES
P_ZA = 0
P_RB = P_ZA + HA * DVA
P_GA = P_RB + HB * DVB
P_GB = P_GA + D_MODEL
W_POST = P_GB + D_MODEL
W_IN_COLS = O_POST + W_POST
LANE_BETA = 0
LANE_G = HA

VMEM_LIMIT = 56 * 1024 * 1024


def _dot(a, b):
    return jnp.dot(a.astype(BF16), b.astype(BF16), preferred_element_type=F32)


def _dot_nt(a, b):
    return lax.dot_general(a.astype(BF16), b.astype(BF16), (((1,), (1,)), ((), ())), preferred_element_type=F32)


def _dot_tn(a, b):
    return lax.dot_general(a.astype(BF16), b.astype(BF16), (((0,), (0,)), ((), ())), preferred_element_type=F32)


def _rms(x, w):
    return x * lax.rsqrt(jnp.mean(x * x, axis=-1, keepdims=True) + EPS) * w


def _silu(x):
    return x * jax.nn.sigmoid(x)


def _gelu_tanh(x):
    return 0.5 * x * (1.0 + jnp.tanh(0.7978845608028654 * (x + 0.044715 * (x * x * x))))


def _l2n(x):
    return x * lax.rsqrt(jnp.sum(x * x, axis=-1, keepdims=True) + EPS)


def _iota2(n, m, axis):
    return lax.broadcasted_iota(jnp.int32, (n, m), axis)


def _gates(small, alog_row, dt_row):
    beta = jax.nn.sigmoid(small)
    g = -jnp.exp(alog_row) * jax.nn.softplus(small + dt_row)
    return beta, g


def _qkv_a_post(conv_out):
    act = _silu(conv_out)
    parts = []
    for h in range(HA):
        parts.append(_l2n(act[:, h * DKA:(h + 1) * DKA]) * (DKA ** -0.5))
    for h in range(HA):
        o = HA * DKA + h * DKA
        parts.append(_l2n(act[:, o:o + DKA]))
    parts.append(act[:, 2 * HA * DKA:])
    return jnp.concatenate(parts, axis=1)


def _branch_a(oa, post, onorm_a, wa_ref):
    pa = [_rms(oa[:, h * DVA:(h + 1) * DVA], onorm_a) for h in range(HA)]
    return _dot(jnp.concatenate(pa, axis=1) * _silu(post[:, P_ZA:P_ZA + HA * DVA]), wa_ref[...])


def _branch_b(ob, post, onorm_b, wb_ref):
    pb = [_rms(ob[:, h * DVB:(h + 1) * DVB], onorm_b) for h in range(HB)]
    return _dot(jnp.concatenate(pb, axis=1) * _silu(post[:, P_RB:P_RB + HB * DVB]), wb_ref[...])


def _mix_residual(x, y_a, y_b, post, wo_ref):
    mix = (jax.nn.sigmoid(post[:, P_GA:P_GA + D_MODEL]) * y_a
           + jax.nn.sigmoid(post[:, P_GB:P_GB + D_MODEL]) * y_b)
    return x + _dot(mix, wo_ref[...])


def _cumsum_chunks(x, c):
    rowi = _iota2(x.shape[0], x.shape[1], 0) % c
    sh = 1
    while sh < c:
        x = x + jnp.where(rowi >= sh, pltpu.roll(x, sh, axis=0), 0.0)
        sh *= 2
    return x


def _delta_prepare(qkv_s, beta_s, gcum_s, gt_s, m_s, qk_s, rhs_s, qe_s, kdec_s, tt, c):
    ri = _iota2(c, 2 * c, 0)
    ci = _iota2(c, 2 * c, 1) % c
    first = _iota2(c, 2 * c, 1) < c
    causal = (ri >= ci).astype(F32)
    strict = (ri > ci).astype(F32)
    zeros = jnp.zeros((c, DKA), F32)
    for cp in range(tt // (2 * c)):
        rows2 = slice(2 * cp * c, (2 * cp + 2) * c)
        for h in range(HA):
            n = cp * HA + h
            hs = slice(h * DKA, (h + 1) * DKA)
            lhs, rhs_t, g_cols = [], [], []
            for par in range(2):
                rows = slice((2 * cp + par) * c, (2 * cp + par + 1) * c)
                g_blk = gcum_s[rows, :]
                q = qkv_s[rows, h * DKA:(h + 1) * DKA]
                k = qkv_s[rows, HA * DKA + h * DKA:HA * DKA + (h + 1) * DKA]
                v = qkv_s[rows, 2 * HA * DKA + h * DVA:2 * HA * DKA + (h + 1) * DVA]
                g_col = g_blk[:, LANE_G + h:LANE_G + h + 1]
                beta_col = beta_s[rows, LANE_BETA + h:LANE_BETA + h + 1]
                eg = jnp.exp(g_col)
                kb = k * beta_col
                lhs.append(jnp.concatenate([kb, q], axis=0))
                rhs_t.append(jnp.concatenate([k, zeros] if par == 0 else [zeros, k], axis=1))
                g_cols.append(g_col)
                rhs_s[rows, h * 2 * DVA:(h + 1) * 2 * DVA] = jnp.concatenate([v * beta_col, kb * eg], axis=1)
                qe_s[rows, hs] = q * eg
                kdec_s[rows, hs] = k * jnp.exp(g_col[c - 1:c, :] - g_col)
            kq = _dot_nt(jnp.concatenate(lhs, axis=1), jnp.concatenate(rhs_t, axis=0))
            g_row = gt_s[LANE_G + h:LANE_G + h + 1, rows2]
            decay = jnp.exp(jnp.minimum(jnp.where(first, g_cols[0], g_cols[1]) - g_row, 0.0))
            m_s[n] = kq[0:c] * (decay * strict)
            qk_s[n] = kq[c:] * (decay * causal)


def _inverse_stages(m_s, p_s, pw_s, n_inst, c):
    ri = _iota2(c, 2 * c, 0)
    lane = _iota2(c, 2 * c, 1)
    ci = lane % c
    left = (lane < c).astype(BF16)
    right = (lane >= c).astype(BF16)

    def bdiag(xp):
        xb = xp.astype(BF16)
        return jnp.concatenate([xb * left, xb * right], axis=0)

    eye = (ri == ci).astype(F32)
    base = min(16, c)
    same = (ri // base) == (ci // base)
    neg_same = jnp.where(same, -1.0, 0.0)
    for n in range(n_inst):
        a = m_s[n] * neg_same
        pw_s[n] = _dot(a, bdiag(a))
        p_s[n] = eye + a
    k = 2
    while k < base:
        for n in range(n_inst):
            p = p_s[n]
            pw = pw_s[n]
            bd = bdiag(pw)
            p_s[n] = p + _dot(p, bd)
            if 2 * k < base:
                pw_s[n] = _dot(pw, bd)
        k *= 2
    size = base
    while size < c:
        nxt = size * 2
        same_nxt = (ri // nxt) == (ci // nxt)
        off = (same_nxt & jnp.logical_not(same)).astype(F32)
        for n in range(n_inst):
            pw_s[n] = _dot(p_s[n], bdiag(m_s[n] * off))
        for n in range(n_inst):
            p = p_s[n]
            p_s[n] = p - _dot(pw_s[n], bdiag(p))
        same = same_nxt
        size = nxt


def _level_ref_rows(bc, half, c):
    w = bc.shape[1]
    parts = []
    if half >= SUBLANES // 2:
        for blk in range(c // (2 * half)):
            mrow = blk * 2 * half + half
            parts.append(jnp.broadcast_to(bc[mrow:mrow + 1, :], (2 * half, w)))
    else:
        sub = _iota2(SUBLANES, w, 0)
        for grp in range(c // SUBLANES):
            acc = None
            for blk in range(SUBLANES // (2 * half)):
                mrow = grp * SUBLANES + blk * 2 * half + half
                b = jnp.broadcast_to(bc[mrow:mrow + 1, :], (SUBLANES, w))
                acc = b if acc is None else jnp.where(sub >= blk * 2 * half, b, acc)
            parts.append(acc)
    return jnp.concatenate(parts, axis=0)


def _gla_prepare_steps(qkb_s, vb_s, bc_s, attn_s, ob_s, pkv_s, qeb_s, tt, c):
    ri = _iota2(c, 2 * c, 0)
    ci = _iota2(c, 2 * c, 1) % c
    first = _iota2(c, 2 * c, 1) < c
    n_ch = tt // c
    zeros_k = jnp.zeros((c, DKB), BF16)
    zeros_v = jnp.zeros((c, DVB), BF16)

    def operands(ch, h):
        rows = slice(ch * c, (ch + 1) * c)
        q = qkb_s[rows, h * DKB:(h + 1) * DKB]
        k = qkb_s[rows, HB * DKB + h * DKB:HB * DKB + (h + 1) * DKB]
        return rows, q, k

    def diag():
        for cp in range(n_ch // 2):
            for h in range(HB):
                d = []
                for par in range(2):
                    _, q, k = operands(2 * cp + par, h)
                    d.append(jnp.sum(q * k, axis=-1, keepdims=True))
                attn_s[cp * HB + h] = jnp.where(ri == ci, jnp.where(first, d[0], d[1]), 0.0)

    def level(half):
        valid = (((ri // (2 * half)) == (ci // (2 * half))) & ((ri % (2 * half)) >= half)
                 & ((ci % (2 * half)) < half)).astype(F32)
        for cp in range(n_ch // 2):
            e_all = []
            for par in range(2):
                bc_all = bc_s[(2 * cp + par) * c:(2 * cp + par + 1) * c, :]
                e_all.append(jnp.exp(-jnp.abs(bc_all - _level_ref_rows(bc_all, half, c))))
            for h in range(HB):
                qs, ks = [], []
                for par in range(2):
                    _, q, k = operands(2 * cp + par, h)
                    e = e_all[par][:, h * DKB:(h + 1) * DKB]
                    qs.append((q * e).astype(BF16))
                    ks.append((k * e).astype(BF16))
                k_bd = jnp.concatenate([jnp.concatenate([ks[0], zeros_k], axis=1),
                                        jnp.concatenate([zeros_k, ks[1]], axis=1)], axis=0)
                attn_s[cp * HB + h] += _dot_nt(jnp.concatenate(qs, axis=1), k_bd) * valid

    def finish():
        for cp in range(n_ch // 2):
            for h in range(HB):
                vs = []
                for par in range(2):
                    rows, q, k = operands(2 * cp + par, h)
                    v = vb_s[rows, h * DVB:(h + 1) * DVB]
                    bc = bc_s[rows, h * DKB:(h + 1) * DKB]
                    vs.append(v.astype(BF16))
                    pkv_s[(2 * cp + par) * HB + h] = _dot_tn(k * jnp.exp(bc[c - 1:c, :] - bc), v)
                    qeb_s[rows, h * DKB:(h + 1) * DKB] = q * jnp.exp(bc)
                v_bd = jnp.concatenate([jnp.concatenate([vs[0], zeros_v], axis=1),
                                        jnp.concatenate([zeros_v, vs[1]], axis=1)], axis=0)
                o2 = _dot(attn_s[cp * HB + h], v_bd)
                ob_s[2 * cp * c:(2 * cp + 1) * c, h * DVB:(h + 1) * DVB] = o2[:, 0:DVB]
                ob_s[(2 * cp + 1) * c:(2 * cp + 2) * c, h * DVB:(h + 1) * DVB] = o2[:, DVB:]

    steps = [diag]
    half = c // 2
    while half >= 1:
        steps.append(functools.partial(level, half))
        half //= 2
    steps.append(finish)
    return steps


def _mixer_body(x_ref, sa0_ref, sb0_ref, ca0_ref, nmix_ref, wca_ref, alog_ref, dt_ref, bgk_ref, ona_ref, onb_ref,
                w1_ref, w2_ref, wa_ref, wb_ref, wo_ref,
                x1_ref, sa_ref, ca_ref, sb_ref,
                cbuf, hb_s, post_s, qkv_s, beta_s, gcum_s, gt_s, qkb_s, vb_s, bc_s, oa_s, ob_s,
                m_s, p_s, pw_s, qk_s, rhs_s, uw_s, qe_s, kdec_s, attn_s, pkv_s, qeb_s, gated_s, mix_s, ktu_s, ktw_s, *, tt, c):
    t_idx = pl.program_id(1)
    n_ch = tt // c

    @pl.when(t_idx == 0)
    def _():
        sa_ref[...] = sa0_ref[...]
        sb_ref[...] = sb0_ref[...]
        cbuf[SUBLANES - (CONV_A - 1):SUBLANES, :] = ca0_ref[...]

    rb = min(tt, 64)
    for r in range(tt // rb):
        rr = slice(r * rb, (r + 1) * rb)
        hb_s[rr, :] = _rms(x_ref[rr, :], nmix_ref[...]).astype(BF16)
    hb = hb_s[...]
    cbuf[SUBLANES:SUBLANES + tt, :] = jnp.dot(hb, w1_ref[:, O_QKVA:O_QKVA + C_A], preferred_element_type=F32)
    base = SUBLANES - (CONV_A - 1)
    n_blk = C_A // LANES
    post_w = 2 * LANES
    n_post = W_POST // post_w
    for j in range(max(n_blk, n_post)):
        if j < n_post:
            pc = slice(j * post_w, (j + 1) * post_w)
            post_s[:, pc] = jnp.dot(hb, w1_ref[:, O_POST + j * post_w:O_POST + (j + 1) * post_w],
                                    preferred_element_type=F32)
        if j < n_blk:
            cols = slice(j * LANES, (j + 1) * LANES)
            full = cbuf[:, cols]
            assert CONV_A == 4
            prev = pltpu.roll(full, 1, axis=0)
            near = full * wca_ref[3:4, cols] + prev * wca_ref[2:3, cols]
            far = full * wca_ref[1:2, cols] + prev * wca_ref[0:1, cols]
            acc = (near + pltpu.roll(far, 2, axis=0))[SUBLANES:]
            act = _silu(acc)
            if j < HA:
                act = _l2n(act) * (DKA ** -0.5)
            elif j < 2 * HA:
                act = _l2n(act)
            qkv_s[:, cols] = act
    tail = cbuf[tt + base:tt + SUBLANES, :]
    cbuf[base:SUBLANES, :] = tail
    ca_ref[...] = tail
    small = jnp.dot(hb, w1_ref[:, O_SMALL:O_SMALL + LANES], preferred_element_type=F32)
    beta, g = _gates(small, alog_ref[...], dt_ref[...])
    beta_s[...] = beta
    g_cum = _cumsum_chunks(g, c)
    gcum_s[...] = g_cum
    gt_s[...] = g_cum.T

    _delta_prepare(qkv_s, beta_s, gcum_s, gt_s, m_s, qk_s, rhs_s, qe_s, kdec_s, tt, c)
    n_pair = (n_ch // 2) * HA
    _inverse_stages(m_s, p_s, pw_s, n_pair, c)
    zeros_uw = jnp.zeros((c, 2 * DVA), BF16)

    def pair_bdiag(ref, cp, cols):
        x_e = ref[2 * cp * c:(2 * cp + 1) * c, cols].astype(BF16)
        x_o = ref[(2 * cp + 1) * c:(2 * cp + 2) * c, cols].astype(BF16)
        return jnp.concatenate([jnp.concatenate([x_e, zeros_uw], axis=1),
                                jnp.concatenate([zeros_uw, x_o], axis=1)], axis=0)

    for cp in range(n_ch // 2):
        rows_e = slice(2 * cp * c, (2 * cp + 1) * c)
        rows_o = slice((2 * cp + 1) * c, (2 * cp + 2) * c)
        for h in range(HA):
            cols = slice(h * 2 * DVA, (h + 1) * 2 * DVA)
            uw2 = _dot(p_s[cp * HA + h], pair_bdiag(rhs_s, cp, cols))
            uw_s[rows_e, cols] = uw2[:, 0:2 * DVA]
            uw_s[rows_o, cols] = uw2[:, 2 * DVA:]
    for cp in range(n_ch // 2):
        for h in range(HA):
            hs = slice(h * DKA, (h + 1) * DKA)
            cols = slice(h * 2 * DVA, (h + 1) * 2 * DVA)
            att2 = _dot(qk_s[cp * HA + h], pair_bdiag(uw_s, cp, cols))
            for par in range(2):
                ch = 2 * cp + par
                rows = slice(ch * c, (ch + 1) * c)
                att_uw = att2[:, par * 2 * DVA:(par + 1) * 2 * DVA]
                oa_s[rows, h * DVA:(h + 1) * DVA] = att_uw[:, 0:DVA]
                qe_s[rows, hs] = qe_s[rows, hs] - att_uw[:, DVA:]
                kt_uw = _dot_tn(kdec_s[rows, hs], uw_s[rows, cols])
                ktu_s[ch * HA + h] = kt_uw[:, 0:DVA]
                ktw_s[ch * HA + h] = kt_uw[:, DVA:]

    small_b = small.astype(BF16)
    for h in range(HB):
        hs = slice(h * DKB, (h + 1) * DKB)
        lg_pre = jnp.dot(small_b, w2_ref[:, hs], preferred_element_type=F32) + bgk_ref[:, hs]
        bc_s[:, hs] = _cumsum_chunks(jax.nn.log_sigmoid(lg_pre) / GLA_GATE_NORM, c)
    qkb_s[:, 0:HB * DKB] = jnp.dot(hb, w1_ref[:, O_QB:O_QB + HB * DKB], preferred_element_type=F32) * (DKB ** -0.5)
    qkb_s[:, HB * DKB:] = jnp.dot(hb, w1_ref[:, O_KB:O_KB + HB * DKB], preferred_element_type=F32)
    vb_s[...] = jnp.dot(hb, w1_ref[:, O_VB:O_VB + HB * DVB], preferred_element_type=F32)

    for step in _gla_prepare_steps(qkb_s, vb_s, bc_s, attn_s, ob_s, pkv_s, qeb_s, tt, c):
        step()

    for ch in range(n_ch):
        rows = slice(ch * c, (ch + 1) * c)
        last = slice(ch * c + c - 1, ch * c + c)
        for h in range(HA):
            n = ch * HA + h
            hs = slice(h * DKA, (h + 1) * DKA)
            s = sa_ref[h]
            prod = _dot(jnp.concatenate([ktw_s[n], qe_s[rows, hs]], axis=0), s)
            oa_s[rows, h * DVA:(h + 1) * DVA] += prod[DKA:]
            g_last = gcum_s[last, LANE_G + h:LANE_G + h + 1]
            sa_ref[h] = s * jnp.exp(g_last) + (ktu_s[n] - prod[0:DKA])

    for h in range(HA):
        hs = slice(h * DVA, (h + 1) * DVA)
        gated_s[:, hs] = (_rms(oa_s[:, hs], ona_ref[...]) * _silu(post_s[:, P_ZA + h * DVA:P_ZA + (h + 1) * DVA])
                          ).astype(BF16)

    for ch in range(n_ch):
        rows = slice(ch * c, (ch + 1) * c)
        last = slice(ch * c + c - 1, ch * c + c)
        for h in range(HB):
            s = sb_ref[h]
            ob_s[rows, h * DVB:(h + 1) * DVB] += _dot(qeb_s[rows, h * DKB:(h + 1) * DKB], s)
            e_col = jnp.exp(bc_s[last, h * DKB:(h + 1) * DKB]).T
            sb_ref[h] = s * e_col + pkv_s[ch * HB + h]

    for h in range(HB):
        hs = slice(h * DVB, (h + 1) * DVB)
        gated_s[:, HA * DVA + h * DVB:HA * DVA + (h + 1) * DVB] = (
            _rms(ob_s[:, hs], onb_ref[...]) * _silu(post_s[:, P_RB + h * DVB:P_RB + (h + 1) * DVB])).astype(BF16)
    out_w = 2 * LANES
    ga = gated_s[:, 0:HA * DVA]
    gb = gated_s[:, HA * DVA:]
    for n in range(D_MODEL // out_w):
        ns = slice(n * out_w, (n + 1) * out_w)
        y_a = jnp.dot(ga, wa_ref[:, ns], preferred_element_type=F32)
        y_b = jnp.dot(gb, wb_ref[:, ns], preferred_element_type=F32)
        mix_s[:, ns] = (jax.nn.sigmoid(post_s[:, P_GA + n * out_w:P_GA + (n + 1) * out_w]) * y_a
                        + jax.nn.sigmoid(post_s[:, P_GB + n * out_w:P_GB + (n + 1) * out_w]) * y_b).astype(BF16)
    mix = mix_s[...]
    for n in range(D_MODEL // out_w):
        ns = slice(n * out_w, (n + 1) * out_w)
        x1_ref[:, ns] = x_ref[:, ns] + jnp.dot(mix, wo_ref[:, ns], preferred_element_type=F32)


def _const_spec(shape):
    nd = len(shape)
    return pl.BlockSpec(shape, lambda *_: (0,) * nd, pipeline_mode=pl.Buffered(1))


def _mixer_call(x, sa0, sb0, ca0, p, tt):
    b, t, _ = x.shape
    c = CHUNK
    nt = t // tt
    n_inst = (tt // c) * HA
    row = lambda n: _const_spec((1, n))
    in_specs = [
        pl.BlockSpec((None, tt, D_MODEL), lambda i, j: (i, j, 0)),
        _const_spec((HA, DKA, DVA)), _const_spec((HB, DKB, DVB)), _const_spec((CONV_A - 1, C_A)),
        row(D_MODEL), _const_spec((CONV_A, C_A)), row(LANES), row(LANES), row(HB * DKB), row(DVA), row(DVB),
        _const_spec((D_MODEL, W_IN_COLS)), _const_spec((LANES, HB * DKB)),
        _const_spec((HA * DVA, D_MODEL)), _const_spec((HB * DVB, D_MODEL)), _const_spec((D_MODEL, D_MODEL)),
    ]
    out_specs = [
        pl.BlockSpec((None, tt, D_MODEL), lambda i, j: (i, j, 0)),
        pl.BlockSpec((None, HA, DKA, DVA), lambda i, j: (i, 0, 0, 0)),
        pl.BlockSpec((None, CONV_A - 1, C_A), lambda i, j: (i, 0, 0)),
        pl.BlockSpec((None, HB, DKB, DVB), lambda i, j: (i, 0, 0, 0)),
    ]
    out_shape = [
        jax.ShapeDtypeStruct((b, t, D_MODEL), F32),
        jax.ShapeDtypeStruct((b, HA, DKA, DVA), F32),
        jax.ShapeDtypeStruct((b, CONV_A - 1, C_A), F32),
        jax.ShapeDtypeStruct((b, HB, DKB, DVB), F32),
    ]
    assert (tt // c) % 2 == 0, "mixer A handles chunks in pairs"
    pairs = lambda: pltpu.VMEM((n_inst // 2, c, 2 * c), F32)
    scratch = [
        pltpu.VMEM((tt + SUBLANES, C_A), F32),
        pltpu.VMEM((tt, D_MODEL), BF16),
        pltpu.VMEM((tt, W_POST), F32),
        pltpu.VMEM((tt, C_A), F32),
        pltpu.VMEM((tt, LANES), F32),
        pltpu.VMEM((tt, LANES), F32),
        pltpu.VMEM((LANES, tt), F32),
        pltpu.VMEM((tt, 2 * HB * DKB), F32),
        pltpu.VMEM((tt, HB * DVB), F32),
        pltpu.VMEM((tt, HB * DKB), F32),
        pltpu.VMEM((tt, HA * DVA), F32),
        pltpu.VMEM((tt, HB * DVB), F32),
        pairs(), pairs(), pairs(), pairs(),
        pltpu.VMEM((tt, HA * 2 * DVA), F32),
        pltpu.VMEM((tt, HA * 2 * DVA), F32),
        pltpu.VMEM((tt, HA * DKA), F32),
        pltpu.VMEM((tt, HA * DKA), F32),
        pairs(),
        pltpu.VMEM((n_inst, DKB, DVB), F32),
        pltpu.VMEM((tt, HB * DKB), F32),
        pltpu.VMEM((tt, HA * DVA + HB * DVB), BF16),
        pltpu.VMEM((tt, D_MODEL), BF16),
        pltpu.VMEM((n_inst, DKA, DVA), F32),
        pltpu.VMEM((n_inst, DKA, DVA), F32),
    ]
    return pl.pallas_call(
        functools.partial(_mixer_body, tt=tt, c=c),
        grid=(b, nt), in_specs=in_specs, out_specs=out_specs, out_shape=out_shape, scratch_shapes=scratch,
        compiler_params=pltpu.CompilerParams(dimension_semantics=("arbitrary", "arbitrary"),
                                             vmem_limit_bytes=VMEM_LIMIT),
        name="mixer",
    )(x, sa0, sb0, ca0, p["norm_mix"], p["w_conv_a"], p["alog_row"], p["dt_row"], p["b_gk"], p["onorm_a"],
      p["onorm_b"], p["w1"], p["w2"], p["w_a_out"], p["w_b_out"], p["w_o"])


def _ffn_core(x1, u_conv, gf, wdn_ref, nfin):
    act = _gelu_tanh(u_conv) * gf
    x2 = x1 + _dot(act, wdn_ref[...])
    return _rms(x2, nfin)


def _ffn_body(x1_ref, cf0_ref, nffn_ref, wcf_ref, bcf_ref, nfin_ref, wup_ref, wdn_ref,
              y_ref, cf_ref, ubuf, *, tt):
    t_idx = pl.program_id(1)
    base = SUBLANES - (CONV_F - 1)

    @pl.when(t_idx == 0)
    def _():
        ubuf[base:SUBLANES, :] = cf0_ref[...]

    x1 = x1_ref[...]
    h2 = _rms(x1, nffn_ref[...]).astype(BF16)
    ubuf[SUBLANES:SUBLANES + tt, :] = jnp.dot(h2, wup_ref[:, 0:D_FF], preferred_element_type=F32)
    gf = jnp.dot(h2, wup_ref[:, D_FF:], preferred_element_type=F32)
    acc = ubuf[base:base + tt, :] * wcf_ref[0:1, :]
    for i in range(1, CONV_F):
        acc = acc + ubuf[base + i:base + i + tt, :] * wcf_ref[i:i + 1, :]
    acc = acc + bcf_ref[...]
    tail = ubuf[tt + base:tt + SUBLANES, :]
    ubuf[base:SUBLANES, :] = tail
    cf_ref[...] = tail
    y_ref[...] = _ffn_core(x1, acc, gf, wdn_ref, nfin_ref[...])


def _ffn_tail_state_body(x1_ref, nffn_ref, wup_ref, u_ref):
    h2 = _rms(x1_ref[...], nffn_ref[...]).astype(BF16)
    u_ref[...] = jnp.dot(h2, wup_ref[...], preferred_element_type=F32)


def _ffn_tail_state_call(x1_rows, p):
    return pl.pallas_call(
        _ffn_tail_state_body,
        grid=(1,),
        in_specs=[pl.BlockSpec((SUBLANES, D_MODEL), lambda i: (0, 0)), pl.BlockSpec((1, D_MODEL), lambda i: (0, 0)),
                  pl.BlockSpec((D_MODEL, D_FF), lambda i: (0, 0))],
        out_specs=pl.BlockSpec((SUBLANES, D_FF), lambda i: (0, 0)),
        out_shape=jax.ShapeDtypeStruct((SUBLANES, D_FF), F32),
        compiler_params=pltpu.CompilerParams(dimension_semantics=("arbitrary",), vmem_limit_bytes=VMEM_LIMIT),
        name="meta_ffn_state",
    )(x1_rows, p["norm_ffn"], p["w_ffn_in"])


def _ffn_call(x1, cf0, p, tt):
    b, t, _ = x1.shape
    nt = t // tt
    row = lambda n: _const_spec((1, n))
    in_specs = [
        pl.BlockSpec((None, tt, D_MODEL), lambda i, j: (i, j, 0)),
        _const_spec((CONV_F - 1, D_FF)), row(D_MODEL), _const_spec((CONV_F, D_FF)), row(D_FF), row(D_MODEL),
        _const_spec((D_MODEL, 2 * D_FF)), _const_spec((D_FF, D_MODEL)),
    ]
    out_specs = [
        pl.BlockSpec((None, tt, D_MODEL), lambda i, j: (i, j, 0)),
        pl.BlockSpec((None, CONV_F - 1, D_FF), lambda i, j: (i, 0, 0)),
    ]
    out_shape = [jax.ShapeDtypeStruct((b, t, D_MODEL), F32), jax.ShapeDtypeStruct((b, CONV_F - 1, D_FF), F32)]
    return pl.pallas_call(
        functools.partial(_ffn_body, tt=tt),
        grid=(b, nt), in_specs=in_specs, out_specs=out_specs, out_shape=out_shape,
        scratch_shapes=[pltpu.VMEM((tt + SUBLANES, D_FF), F32)],
        compiler_params=pltpu.CompilerParams(dimension_semantics=("arbitrary", "arbitrary"),
                                             vmem_limit_bytes=VMEM_LIMIT),
        name="convffn",
    )(x1, cf0, p["norm_ffn"], p["w_conv_f"], p["b_conv_f"], p["norm_final"], p["w_ffn_in"], p["w_ffn_out"])


def _dec_head_body(x_ref, cs_ref, nmix_ref, wca_ref, alog_ref, dt_ref, bgk_ref, w1_ref, w2_ref,
                   qkva_ref, beta_ref, g_ref, qkvb_ref, lg_ref, post_ref, csn_ref):
    x = x_ref[...]
    hb = _rms(x, nmix_ref[...]).astype(BF16)
    pre = jnp.dot(hb, w1_ref[:, O_QKVA:O_QKVA + C_A], preferred_element_type=F32)
    acc = cs_ref[0] * wca_ref[0:1, :]
    for i in range(1, CONV_A - 1):
        acc = acc + cs_ref[i] * wca_ref[i:i + 1, :]
    acc = acc + pre * wca_ref[CONV_A - 1:CONV_A, :]
    for i in range(CONV_A - 2):
        csn_ref[i] = cs_ref[i + 1]
    csn_ref[CONV_A - 2] = pre
    qkva_ref[...] = _qkv_a_post(acc)
    small = jnp.dot(hb, w1_ref[:, O_SMALL:O_SMALL + LANES], preferred_element_type=F32)
    beta, g = _gates(small, alog_ref[...], dt_ref[...])
    beta_ref[...] = beta
    g_ref[...] = g
    lg_ref[...] = jax.nn.log_sigmoid(_dot(small, w2_ref[...]) + bgk_ref[...]) / GLA_GATE_NORM
    qkb = jnp.dot(hb, w1_ref[:, O_QB:O_QB + 2 * HB * DKB], preferred_element_type=F32)
    qkvb_ref[:, 0:HB * DKB] = qkb[:, 0:HB * DKB] * (DKB ** -0.5)
    qkvb_ref[:, HB * DKB:2 * HB * DKB] = qkb[:, HB * DKB:]
    qkvb_ref[:, 2 * HB * DKB:] = jnp.dot(hb, w1_ref[:, O_VB:O_VB + HB * DVB], preferred_element_type=F32)
    post_ref[...] = jnp.dot(hb, w1_ref[:, O_POST:O_POST + W_POST], preferred_element_type=F32)


def _dec_head_call(xs, cs, p):
    n = xs.shape[0]
    shapes = [(n, C_A), (n, LANES), (n, LANES), (n, 2 * HB * DKB + HB * DVB), (n, HB * DKB), (n, W_POST),
              (CONV_A - 1, n, C_A)]
    return pl.pallas_call(
        _dec_head_body,
        out_shape=[jax.ShapeDtypeStruct(s, F32) for s in shapes],
        compiler_params=pltpu.CompilerParams(vmem_limit_bytes=VMEM_LIMIT),
        name="decode_head",
    )(xs, cs, p["norm_mix"], p["w_conv_a"], p["alog_row"], p["dt_row"], p["b_gk"], p["w1"], p["w2"])


def _dec_rec_body(qkva_ref, beta_ref, g_ref, qkvb_ref, lg_ref, sa_ref, sb_ref,
                  oa_ref, ob_ref, san_ref, sbn_ref, *, tb):
    beta = beta_ref[...]
    eg = jnp.exp(g_ref[...])
    for h in range(HA):
        q_rows = qkva_ref[:, h * DKA:(h + 1) * DKA].astype(BF16)
        k_t = qkva_ref[:, HA * DKA + h * DKA:HA * DKA + (h + 1) * DKA].T
        for j in range(tb):
            k_col = k_t[:, j:j + 1]
            s = sa_ref[j, h] * eg[j:j + 1, LANE_G + h:LANE_G + h + 1]
            v = qkva_ref[j:j + 1, 2 * HA * DKA + h * DVA:2 * HA * DKA + (h + 1) * DVA]
            err = (v - jnp.sum(k_col * s, axis=0, keepdims=True)) * beta[j:j + 1, LANE_BETA + h:LANE_BETA + h + 1]
            s = s + k_col * err
            san_ref[j, h] = s
            o_all = jnp.dot(q_rows, s.astype(BF16), preferred_element_type=F32)
            oa_ref[j:j + 1, h * DVA:(h + 1) * DVA] = o_all[j:j + 1, :]
    for h in range(HB):
        q_rows = qkvb_ref[:, h * DKB:(h + 1) * DKB].astype(BF16)
        k_t = qkvb_ref[:, HB * DKB + h * DKB:HB * DKB + (h + 1) * DKB].T
        d_t = jnp.exp(lg_ref[:, h * DKB:(h + 1) * DKB]).T
        for j in range(tb):
            v = qkvb_ref[j:j + 1, 2 * HB * DKB + h * DVB:2 * HB * DKB + (h + 1) * DVB]
            s = sb_ref[j, h] * d_t[:, j:j + 1] + k_t[:, j:j + 1] * v
            sbn_ref[j, h] = s
            o_all = jnp.dot(q_rows, s.astype(BF16), preferred_element_type=F32)
            ob_ref[j:j + 1, h * DVB:(h + 1) * DVB] = o_all[j:j + 1, :]


def _dec_rec_call(qkva, beta, g, qkvb, lg, sa, sb, tb):
    n = qkva.shape[0]
    rows = lambda w: pl.BlockSpec((tb, w), lambda i: (i, 0))
    in_specs = [rows(C_A), rows(LANES), rows(LANES), rows(2 * HB * DKB + HB * DVB), rows(HB * DKB),
                pl.BlockSpec((tb, HA, DKA, DVA), lambda i: (i, 0, 0, 0)),
                pl.BlockSpec((tb, HB, DKB, DVB), lambda i: (i, 0, 0, 0))]
    out_specs = [rows(HA * DVA), rows(HB * DVB),
                 pl.BlockSpec((tb, HA, DKA, DVA), lambda i: (i, 0, 0, 0)),
                 pl.BlockSpec((tb, HB, DKB, DVB), lambda i: (i, 0, 0, 0))]
    out_shape = [jax.ShapeDtypeStruct((n, HA * DVA), F32), jax.ShapeDtypeStruct((n, HB * DVB), F32),
                 jax.ShapeDtypeStruct(sa.shape, F32), jax.ShapeDtypeStruct(sb.shape, F32)]
    return pl.pallas_call(
        functools.partial(_dec_rec_body, tb=tb),
        grid=(n // tb,), in_specs=in_specs, out_specs=out_specs, out_shape=out_shape,
        compiler_params=pltpu.CompilerParams(dimension_semantics=("arbitrary",), vmem_limit_bytes=VMEM_LIMIT),
        name="decode_recurrence",
    )(qkva, beta, g, qkvb, lg, sa, sb)


def _dec_tail_body(x_ref, oa_ref, ob_ref, post_ref, cf_ref, ona_ref, onb_ref, nffn_ref, wcf_ref, bcf_ref, nfin_ref,
                   wa_ref, wb_ref, wo_ref, wup_ref, wdn_ref, y_ref, cfn_ref):
    post = post_ref[...]
    y_a = _branch_a(oa_ref[...], post, ona_ref[...], wa_ref)
    y_b = _branch_b(ob_ref[...], post, onb_ref[...], wb_ref)
    x1 = _mix_residual(x_ref[...], y_a, y_b, post, wo_ref)
    h2 = _rms(x1, nffn_ref[...]).astype(BF16)
    u = jnp.dot(h2, wup_ref[:, 0:D_FF], preferred_element_type=F32)
    gf = jnp.dot(h2, wup_ref[:, D_FF:], preferred_element_type=F32)
    acc = cf_ref[0] * wcf_ref[0:1, :]
    for i in range(1, CONV_F - 1):
        acc = acc + cf_ref[i] * wcf_ref[i:i + 1, :]
    acc = acc + u * wcf_ref[CONV_F - 1:CONV_F, :] + bcf_ref[...]
    for i in range(CONV_F - 2):
        cfn_ref[i] = cf_ref[i + 1]
    cfn_ref[CONV_F - 2] = u
    y_ref[...] = _ffn_core(x1, acc, gf, wdn_ref, nfin_ref[...])


def _dec_tail_call(xs, oa, ob, post, cf, p):
    n = xs.shape[0]
    return pl.pallas_call(
        _dec_tail_body,
        out_shape=[jax.ShapeDtypeStruct((n, D_MODEL), F32), jax.ShapeDtypeStruct((CONV_F - 1, n, D_FF), F32)],
        compiler_params=pltpu.CompilerParams(vmem_limit_bytes=VMEM_LIMIT),
        name="decode_tail",
    )(xs, oa, ob, post, cf, p["onorm_a"], p["onorm_b"], p["norm_ffn"], p["w_conv_f"], p["b_conv_f"],
      p["norm_final"], p["w_a_out"], p["w_b_out"], p["w_o"], p["w_ffn_in"], p["w_ffn_out"])


PROJ_SIZES = (HA * DKA, HA * DKA, HA * DVA, HA * DVA, HA, HA, HB * DKB, HB * DKB, HB * DVB, HB * DVB, GATE_RANK,
              D_MODEL, D_MODEL)
PROJ_ORDER = (0, 1, 2, 6, 7, 8, 4, 5, 10, None, 3, 9, 11, 12)
RELAYOUT_ROWS = 128


def _relayout_body(w_ref, o_ref):
    offs = [0]
    for s in PROJ_SIZES:
        offs.append(offs[-1] + s)
    dst = 0
    for seg in PROJ_ORDER:
        if seg is None:
            width = (-dst) % LANES
            o_ref[:, dst:dst + width] = jnp.zeros((o_ref.shape[0], width), BF16)
        else:
            width = PROJ_SIZES[seg]
            o_ref[:, dst:dst + width] = w_ref[:, offs[seg]:offs[seg] + width].astype(BF16)
        dst += width
    assert dst == W_IN_COLS


def _relayout_w_in(w):
    rows, cols = w.shape
    return pl.pallas_call(
        _relayout_body,
        grid=(rows // RELAYOUT_ROWS,),
        in_specs=[pl.BlockSpec((RELAYOUT_ROWS, cols), lambda i: (i, 0))],
        out_specs=pl.BlockSpec((RELAYOUT_ROWS, W_IN_COLS), lambda i: (i, 0)),
        out_shape=jax.ShapeDtypeStruct((rows, W_IN_COLS), BF16),
        compiler_params=pltpu.CompilerParams(dimension_semantics=("arbitrary",), vmem_limit_bytes=VMEM_LIMIT),
        name="relayout_w_in",
    )(w)


def _prep_params(l, norm_mix, w_in, w_conv_a, a_log, dt_bias, w_gk2, b_gk, onorm_a, onorm_b, w_a_out, w_b_out,
                 w_o, norm_ffn, w_ffn_in, w_conv_f, b_conv_f, w_ffn_out, norm_final):
    n_small = 2 * HA + GATE_RANK
    w1 = _relayout_w_in(w_in[l].astype(BF16))
    w2 = jnp.zeros((LANES, HB * DKB), F32).at[2 * HA:n_small].set(w_gk2[l]).astype(BF16)
    lane_row = lambda v: jnp.zeros((1, LANES), F32).at[0, LANE_G:LANE_G + HA].set(v)
    return dict(
        norm_mix=norm_mix[l][None], w_conv_a=w_conv_a[l], alog_row=lane_row(a_log[l]), dt_row=lane_row(dt_bias[l]),
        b_gk=b_gk[l][None], onorm_a=onorm_a[l][None], onorm_b=onorm_b[l][None], w1=w1, w2=w2,
        w_a_out=w_a_out[l].astype(BF16), w_b_out=w_b_out[l].astype(BF16), w_o=w_o[l].astype(BF16),
        norm_ffn=norm_ffn[l][None], w_ffn_in=w_ffn_in[l].astype(BF16), w_conv_f=w_conv_f[l],
        b_conv_f=b_conv_f[l][None], w_ffn_out=w_ffn_out[l].astype(BF16), norm_final=norm_final[None])


PROMPT_TILE = 256
META_TILE = 2 * CHUNK
FFN_TILE = 512
DECODE_TILE = 8


def kernel(x_prompt, x_sample, state_delta, state_delta_conv, state_gla, state_ffn_conv, meta_tokens, norm_mix, w_in, w_conv_a, a_log, dt_bias, w_gk2, b_gk, onorm_a, onorm_b, w_a_out, w_b_out, w_o, norm_ffn, w_ffn_in, w_conv_f, b_conv_f, w_ffn_out, norm_final):
    assert w_in.shape[0] == 1, "single layer only"
    l = 0
    p = _prep_params(l, norm_mix, w_in, w_conv_a, a_log, dt_bias, w_gk2, b_gk, onorm_a, onorm_b, w_a_out, w_b_out,
                     w_o, norm_ffn, w_ffn_in, w_conv_f, b_conv_f, w_ffn_out, norm_final)
    n_dec = x_sample.shape[0]

    xm = jnp.concatenate([jnp.zeros((META_TILE - N_META, D_MODEL), F32), meta_tokens.astype(F32)], axis=0)[None]
    x1m, sa0, ca0, sb0 = _mixer_call(xm, jnp.zeros((HA, DKA, DVA), F32), jnp.zeros((HB, DKB, DVB), F32),
                                     jnp.zeros((CONV_A - 1, C_A), F32), p, META_TILE)
    u_tail = _ffn_tail_state_call(x1m[0, META_TILE - SUBLANES:], p)
    cf0 = u_tail[SUBLANES - (CONV_F - 1):][None]

    x1, sa_p, ca_p, sb_p = _mixer_call(x_prompt, sa0[0], sb0[0], ca0[0], p, PROMPT_TILE)
    y_prompt, cf_p = _ffn_call(x1, cf0[0], p, FFN_TILE)

    xs = x_sample.reshape(n_dec, D_MODEL)
    cs = jnp.transpose(state_delta_conv[l], (1, 0, 2))
    cfs = jnp.transpose(state_ffn_conv[l], (1, 0, 2))
    qkva, beta, g, qkvb, lg, post, cs_new = _dec_head_call(xs, cs, p)
    oa, ob, sa_s, sb_s = _dec_rec_call(qkva, beta, g, qkvb, lg, state_delta[l], state_gla[l], DECODE_TILE)
    y_s, cf_s = _dec_tail_call(xs, oa, ob, post, cfs, p)

    return (y_prompt, y_s.reshape(n_dec, 1, D_MODEL),
            sa_p[None], ca_p[None], sb_p[None], cf_p[None],
            sa_s[None], jnp.transpose(cs_new, (1, 0, 2))[None], sb_s[None],
            jnp.transpose(cf_s, (1, 0, 2))[None])
```

```python
import functools

import jax
import jax.numpy as jnp
from jax import lax
from jax.experimental import pallas as pl
from jax.experimental.pallas import tpu as pltpu

F32 = jnp.float32
BF16 = jnp.bfloat16

D_MODEL = 1024
N_META = 16
CHUNK = 64
HA, DKA, DVA = 4, 128, 128
HB, DKB, DVB = 4, 128, 256
CONV_A = 4
C_A = 2 * HA * DKA + HA * DVA
GATE_RANK = 16
GLA_GATE_NORM = 16.0
D_FF = 2816
CONV_F = 3
EPS = 1e-6
LANES = 128
SUBLANES = 8

O_QKVA = 0
O_QB = O_QKVA + C_A
O_KB = O_QB + HB * DKB
O_VB = O_KB + HB * DKB
O_SMALL = O_VB + HB * DVB
O_POST = O_SMALL + LANES
P_ZA = 0
P_RB = P_ZA + HA * DVA
P_GA = P_RB + HB * DVB
P_GB = P_GA + D_MODEL
W_POST = P_GB + D_MODEL
W_IN_COLS = O_POST + W_POST
LANE_BETA = 0
LANE_G = HA

VMEM_LIMIT = 56 * 1024 * 1024


def _dot(a, b):
    return jnp.dot(a.astype(BF16), b.astype(BF16), preferred_element_type=F32)


def _dot_nt(a, b):
    return lax.dot_general(a.astype(BF16), b.astype(BF16), (((1,), (1,)), ((), ())), preferred_element_type=F32)


def _dot_tn(a, b):
    return lax.dot_general(a.astype(BF16), b.astype(BF16), (((0,), (0,)), ((), ())), preferred_element_type=F32)


def _rms(x, w):
    return x * lax.rsqrt(jnp.mean(x * x, axis=-1, keepdims=True) + EPS) * w


def _silu(x):
    return x * jax.nn.sigmoid(x)


def _gelu_tanh(x):
    return 0.5 * x * (1.0 + jnp.tanh(0.7978845608028654 * (x + 0.044715 * (x * x * x))))


def _l2n(x):
    return x * lax.rsqrt(jnp.sum(x * x, axis=-1, keepdims=True) + EPS)


def _iota2(n, m, axis):
    return lax.broadcasted_iota(jnp.int32, (n, m), axis)


def _gates(small, alog_row, dt_row):
    beta = jax.nn.sigmoid(small)
    g = -jnp.exp(alog_row) * jax.nn.softplus(small + dt_row)
    return beta, g


def _qkv_a_post(conv_out):
    act = _silu(conv_out)
    parts = []
    for h in range(HA):
        parts.append(_l2n(act[:, h * DKA:(h + 1) * DKA]) * (DKA ** -0.5))
    for h in range(HA):
        o = HA * DKA + h * DKA
        parts.append(_l2n(act[:, o:o + DKA]))
    parts.append(act[:, 2 * HA * DKA:])
    return jnp.concatenate(parts, axis=1)


def _branch_a(oa, post, onorm_a, wa_ref):
    pa = [_rms(oa[:, h * DVA:(h + 1) * DVA], onorm_a) for h in range(HA)]
    return _dot(jnp.concatenate(pa, axis=1) * _silu(post[:, P_ZA:P_ZA + HA * DVA]), wa_ref[...])


def _branch_b(ob, post, onorm_b, wb_ref):
    pb = [_rms(ob[:, h * DVB:(h + 1) * DVB], onorm_b) for h in range(HB)]
    return _dot(jnp.concatenate(pb, axis=1) * _silu(post[:, P_RB:P_RB + HB * DVB]), wb_ref[...])


def _mix_residual(x, y_a, y_b, post, wo_ref):
    mix = (jax.nn.sigmoid(post[:, P_GA:P_GA + D_MODEL]) * y_a
           + jax.nn.sigmoid(post[:, P_GB:P_GB + D_MODEL]) * y_b)
    return x + _dot(mix, wo_ref[...])


def _cumsum_chunks(x, c):
    rowi = _iota2(x.shape[0], x.shape[1], 0) % c
    sh = 1
    while sh < c:
        x = x + jnp.where(rowi >= sh, pltpu.roll(x, sh, axis=0), 0.0)
        sh *= 2
    return x


def _delta_prepare(qkv_s, beta_s, gcum_s, gt_s, m_s, qk_s, rhs_s, qe_s, kdec_s, tt, c):
    ri = _iota2(c, 2 * c, 0)
    ci = _iota2(c, 2 * c, 1) % c
    first = _iota2(c, 2 * c, 1) < c
    causal = (ri >= ci).astype(F32)
    strict = (ri > ci).astype(F32)
    zeros = jnp.zeros((c, DKA), F32)
    for cp in range(tt // (2 * c)):
        rows2 = slice(2 * cp * c, (2 * cp + 2) * c)
        for h in range(HA):
            n = cp * HA + h
            hs = slice(h * DKA, (h + 1) * DKA)
            lhs, rhs_t, g_cols = [], [], []
            for par in range(2):
                rows = slice((2 * cp + par) * c, (2 * cp + par + 1) * c)
                g_blk = gcum_s[rows, :]
                q = qkv_s[rows, h * DKA:(h + 1) * DKA]
                k = qkv_s[rows, HA * DKA + h * DKA:HA * DKA + (h + 1) * DKA]
                v = qkv_s[rows, 2 * HA * DKA + h * DVA:2 * HA * DKA + (h + 1) * DVA]
                g_col = g_blk[:, LANE_G + h:LANE_G + h + 1]
                beta_col = beta_s[rows, LANE_BETA + h:LANE_BETA + h + 1]
                eg = jnp.exp(g_col)
                kb = k * beta_col
                lhs.append(jnp.concatenate([kb, q], axis=0))
                rhs_t.append(jnp.concatenate([k, zeros] if par == 0 else [zeros, k], axis=1))
                g_cols.append(g_col)
                rhs_s[rows, h * 2 * DVA:(h + 1) * 2 * DVA] = jnp.concatenate([v * beta_col, kb * eg], axis=1)
                qe_s[rows, hs] = q * eg
                kdec_s[rows, hs] = k * jnp.exp(g_col[c - 1:c, :] - g_col)
            kq = _dot_nt(jnp.concatenate(lhs, axis=1), jnp.concatenate(rhs_t, axis=0))
            g_row = gt_s[LANE_G + h:LANE_G + h + 1, rows2]
            decay = jnp.exp(jnp.minimum(jnp.where(first, g_cols[0], g_cols[1]) - g_row, 0.0))
            m_s[n] = kq[0:c] * (decay * strict)
            qk_s[n] = kq[c:] * (decay * causal)


def _inverse_stages(m_s, p_s, pw_s, n_inst, c):
    ri = _iota2(c, 2 * c, 0)
    lane = _iota2(c, 2 * c, 1)
    ci = lane % c
    left = (lane < c).astype(BF16)
    right = (lane >= c).astype(BF16)

    def bdiag(xp):
        xb = xp.astype(BF16)
        return jnp.concatenate([xb * left, xb * right], axis=0)

    eye = (ri == ci).astype(F32)
    base = min(16, c)
    same = (ri // base) == (ci // base)
    neg_same = jnp.where(same, -1.0, 0.0)
    for n in range(n_inst):
        a = m_s[n] * neg_same
        pw_s[n] = _dot(a, bdiag(a))
        p_s[n] = eye + a
    k = 2
    while k < base:
        for n in range(n_inst):
            p = p_s[n]
            pw = pw_s[n]
            bd = bdiag(pw)
            p_s[n] = p + _dot(p, bd)
            if 2 * k < base:
                pw_s[n] = _dot(pw, bd)
        k *= 2
    size = base
    while size < c:
        nxt = size * 2
        same_nxt = (ri // nxt) == (ci // nxt)
        off = (same_nxt & jnp.logical_not(same)).astype(F32)
        for n in range(n_inst):
            pw_s[n] = _dot(p_s[n], bdiag(m_s[n] * off))
        for n in range(n_inst):
            p = p_s[n]
            p_s[n] = p - _dot(pw_s[n], bdiag(p))
        same = same_nxt
        size = nxt


def _level_ref_rows(bc, half, c):
    w = bc.shape[1]
    parts = []
    if half >= SUBLANES // 2:
        for blk in range(c // (2 * half)):
            mrow = blk * 2 * half + half
            parts.append(jnp.broadcast_to(bc[mrow:mrow + 1, :], (2 * half, w)))
    else:
        sub = _iota2(SUBLANES, w, 0)
        for grp in range(c // SUBLANES):
            acc = None
            for blk in range(SUBLANES // (2 * half)):
                mrow = grp * SUBLANES + blk * 2 * half + half
                b = jnp.broadcast_to(bc[mrow:mrow + 1, :], (SUBLANES, w))
                acc = b if acc is None else jnp.where(sub >= blk * 2 * half, b, acc)
            parts.append(acc)
    return jnp.concatenate(parts, axis=0)


def _gla_prepare_steps(qkb_s, vb_s, bc_s, attn_s, ob_s, pkv_s, qeb_s, tt, c):
    ri = _iota2(c, 2 * c, 0)
    ci = _iota2(c, 2 * c, 1) % c
    first = _iota2(c, 2 * c, 1) < c
    n_ch = tt // c
    zeros_k = jnp.zeros((c, DKB), BF16)
    zeros_v = jnp.zeros((c, DVB), BF16)

    def operands(ch, h):
        rows = slice(ch * c, (ch + 1) * c)
        q = qkb_s[rows, h * DKB:(h + 1) * DKB]
        k = qkb_s[rows, HB * DKB + h * DKB:HB * DKB + (h + 1) * DKB]
        return rows, q, k

    def diag():
        for cp in range(n_ch // 2):
            for h in range(HB):
                d = []
                for par in range(2):
                    _, q, k = operands(2 * cp + par, h)
                    d.append(jnp.sum(q * k, axis=-1, keepdims=True))
                attn_s[cp * HB + h] = jnp.where(ri == ci, jnp.where(first, d[0], d[1]), 0.0)

    def level(half):
        valid = (((ri // (2 * half)) == (ci // (2 * half))) & ((ri % (2 * half)) >= half)
                 & ((ci % (2 * half)) < half)).astype(F32)
        for cp in range(n_ch // 2):
            e_all = []
            for par in range(2):
                bc_all = bc_s[(2 * cp + par) * c:(2 * cp + par + 1) * c, :]
                e_all.append(jnp.exp(-jnp.abs(bc_all - _level_ref_rows(bc_all, half, c))))
            for h in range(HB):
                qs, ks = [], []
                for par in range(2):
                    _, q, k = operands(2 * cp + par, h)
                    e = e_all[par][:, h * DKB:(h + 1) * DKB]
                    qs.append((q * e).astype(BF16))
                    ks.append((k * e).astype(BF16))
                k_bd = jnp.concatenate([jnp.concatenate([ks[0], zeros_k], axis=1),
                                        jnp.concatenate([zeros_k, ks[1]], axis=1)], axis=0)
                attn_s[cp * HB + h] += _dot_nt(jnp.concatenate(qs, axis=1), k_bd) * valid

    def finish():
        for cp in range(n_ch // 2):
            for h in range(HB):
                vs = []
                for par in range(2):
                    rows, q, k = operands(2 * cp + par, h)
                    v = vb_s[rows, h * DVB:(h + 1) * DVB]
                    bc = bc_s[rows, h * DKB:(h + 1) * DKB]
                    vs.append(v.astype(BF16))
                    pkv_s[(2 * cp + par) * HB + h] = _dot_tn(k * jnp.exp(bc[c - 1:c, :] - bc), v)
                    qeb_s[rows, h * DKB:(h + 1) * DKB] = q * jnp.exp(bc)
                v_bd = jnp.concatenate([jnp.concatenate([vs[0], zeros_v], axis=1),
                                        jnp.concatenate([zeros_v, vs[1]], axis=1)], axis=0)
                o2 = _dot(attn_s[cp * HB + h], v_bd)
                ob_s[2 * cp * c:(2 * cp + 1) * c, h * DVB:(h + 1) * DVB] = o2[:, 0:DVB]
                ob_s[(2 * cp + 1) * c:(2 * cp + 2) * c, h * DVB:(h + 1) * DVB] = o2[:, DVB:]

    steps = [diag]
    half = c // 2
    while half >= 1:
        steps.append(functools.partial(level, half))
        half //= 2
    steps.append(finish)
    return steps


def _mixer_body(x_ref, sa0_ref, sb0_ref, ca0_ref, nmix_ref, wca_ref, alog_ref, dt_ref, bgk_ref, ona_ref, onb_ref,
                w1_ref, w2_ref, wa_ref, wb_ref, wo_ref,
                x1_ref, sa_ref, ca_ref, sb_ref,
                cbuf, hb_s, post_s, qkv_s, beta_s, gcum_s, gt_s, qkb_s, vb_s, bc_s, oa_s, ob_s,
                m_s, p_s, pw_s, qk_s, rhs_s, uw_s, qe_s, kdec_s, attn_s, pkv_s, qeb_s, gated_s, mix_s, ktu_s, ktw_s, *, tt, c):
    t_idx = pl.program_id(1)
    n_ch = tt // c

    @pl.when(t_idx == 0)
    def _():
        sa_ref[...] = sa0_ref[...]
        sb_ref[...] = sb0_ref[...]
        cbuf[SUBLANES - (CONV_A - 1):SUBLANES, :] = ca0_ref[...]

    rb = min(tt, 64)
    for r in range(tt // rb):
        rr = slice(r * rb, (r + 1) * rb)
        hb_s[rr, :] = _rms(x_ref[rr, :], nmix_ref[...]).astype(BF16)
    hb = hb_s[...]
    cbuf[SUBLANES:SUBLANES + tt, :] = jnp.dot(hb, w1_ref[:, O_QKVA:O_QKVA + C_A], preferred_element_type=F32)
    base = SUBLANES - (CONV_A - 1)
    n_blk = C_A // LANES
    post_w = 2 * LANES
    n_post = W_POST // post_w
    for j in range(max(n_blk, n_post)):
        if j < n_post:
            pc = slice(j * post_w, (j + 1) * post_w)
            post_s[:, pc] = jnp.dot(hb, w1_ref[:, O_POST + j * post_w:O_POST + (j + 1) * post_w],
                                    preferred_element_type=F32)
        if j < n_blk:
            cols = slice(j * LANES, (j + 1) * LANES)
            full = cbuf[:, cols]
            assert CONV_A == 4
            prev = pltpu.roll(full, 1, axis=0)
            near = full * wca_ref[3:4, cols] + prev * wca_ref[2:3, cols]
            far = full * wca_ref[1:2, cols] + prev * wca_ref[0:1, cols]
            acc = (near + pltpu.roll(far, 2, axis=0))[SUBLANES:]
            act = _silu(acc)
            if j < HA:
                act = _l2n(act) * (DKA ** -0.5)
            elif j < 2 * HA:
                act = _l2n(act)
            qkv_s[:, cols] = act
    tail = cbuf[tt + base:tt + SUBLANES, :]
    cbuf[base:SUBLANES, :] = tail
    ca_ref[...] = tail
    small = jnp.dot(hb, w1_ref[:, O_SMALL:O_SMALL + LANES], preferred_element_type=F32)
    beta, g = _gates(small, alog_ref[...], dt_ref[...])
    beta_s[...] = beta
    g_cum = _cumsum_chunks(g, c)
    gcum_s[...] = g_cum
    gt_s[...] = g_cum.T

    _delta_prepare(qkv_s, beta_s, gcum_s, gt_s, m_s, qk_s, rhs_s, qe_s, kdec_s, tt, c)
    n_pair = (n_ch // 2) * HA
    _inverse_stages(m_s, p_s, pw_s, n_pair, c)
    zeros_uw = jnp.zeros((c, 2 * DVA), BF16)

    def pair_bdiag(ref, cp, cols):
        x_e = ref[2 * cp * c:(2 * cp + 1) * c, cols].astype(BF16)
        x_o = ref[(2 * cp + 1) * c:(2 * cp + 2) * c, cols].astype(BF16)
        return jnp.concatenate([jnp.concatenate([x_e, zeros_uw], axis=1),
                                jnp.concatenate([zeros_uw, x_o], axis=1)], axis=0)

    for cp in range(n_ch // 2):
        rows_e = slice(2 * cp * c, (2 * cp + 1) * c)
        rows_o = slice((2 * cp + 1) * c, (2 * cp + 2) * c)
        for h in range(HA):
            cols = slice(h * 2 * DVA, (h + 1) * 2 * DVA)
            uw2 = _dot(p_s[cp * HA + h], pair_bdiag(rhs_s, cp, cols))
            uw_s[rows_e, cols] = uw2[:, 0:2 * DVA]
            uw_s[rows_o, cols] = uw2[:, 2 * DVA:]
    for cp in range(n_ch // 2):
        for h in range(HA):
            hs = slice(h * DKA, (h + 1) * DKA)
            cols = slice(h * 2 * DVA, (h + 1) * 2 * DVA)
            att2 = _dot(qk_s[cp * HA + h], pair_bdiag(uw_s, cp, cols))
            for par in range(2):
                ch = 2 * cp + par
                rows = slice(ch * c, (ch + 1) * c)
                att_uw = att2[:, par * 2 * DVA:(par + 1) * 2 * DVA]
                oa_s[rows, h * DVA:(h + 1) * DVA] = att_uw[:, 0:DVA]
                qe_s[rows, hs] = qe_s[rows, hs] - att_uw[:, DVA:]
                kt_uw = _dot_tn(kdec_s[rows, hs], uw_s[rows, cols])
                ktu_s[ch * HA + h] = kt_uw[:, 0:DVA]
                ktw_s[ch * HA + h] = kt_uw[:, DVA:]

    small_b = small.astype(BF16)
    for h in range(HB):
        hs = slice(h * DKB, (h + 1) * DKB)
        lg_pre = jnp.dot(small_b, w2_ref[:, hs], preferred_element_type=F32) + bgk_ref[:, hs]
        bc_s[:, hs] = _cumsum_chunks(jax.nn.log_sigmoid(lg_pre) / GLA_GATE_NORM, c)
    qkb_s[:, 0:HB * DKB] = jnp.dot(hb, w1_ref[:, O_QB:O_QB + HB * DKB], preferred_element_type=F32) * (DKB ** -0.5)
    qkb_s[:, HB * DKB:] = jnp.dot(hb, w1_ref[:, O_KB:O_KB + HB * DKB], preferred_element_type=F32)
    vb_s[...] = jnp.dot(hb, w1_ref[:, O_VB:O_VB + HB * DVB], preferred_element_type=F32)

    for step in _gla_prepare_steps(qkb_s, vb_s, bc_s, attn_s, ob_s, pkv_s, qeb_s, tt, c):
        step()

    for ch in range(n_ch):
        rows = slice(ch * c, (ch + 1) * c)
        last = slice(ch * c + c - 1, ch * c + c)
        for h in range(HA):
            n = ch * HA + h
            hs = slice(h * DKA, (h + 1) * DKA)
            s = sa_ref[h]
            prod = _dot(jnp.concatenate([ktw_s[n], qe_s[rows, hs]], axis=0), s)
            oa_s[rows, h * DVA:(h + 1) * DVA] += prod[DKA:]
            g_last = gcum_s[last, LANE_G + h:LANE_G + h + 1]
            sa_ref[h] = s * jnp.exp(g_last) + (ktu_s[n] - prod[0:DKA])

    for h in range(HA):
        hs = slice(h * DVA, (h + 1) * DVA)
        gated_s[:, hs] = (_rms(oa_s[:, hs], ona_ref[...]) * _silu(post_s[:, P_ZA + h * DVA:P_ZA + (h + 1) * DVA])
                          ).astype(BF16)

    for ch in range(n_ch):
        rows = slice(ch * c, (ch + 1) * c)
        last = slice(ch * c + c - 1, ch * c + c)
        for h in range(HB):
            s = sb_ref[h]
            ob_s[rows, h * DVB:(h + 1) * DVB] += _dot(qeb_s[rows, h * DKB:(h + 1) * DKB], s)
            e_col = jnp.exp(bc_s[last, h * DKB:(h + 1) * DKB]).T
            sb_ref[h] = s * e_col + pkv_s[ch * HB + h]

    for h in range(HB):
        hs = slice(h * DVB, (h + 1) * DVB)
        gated_s[:, HA * DVA + h * DVB:HA * DVA + (h + 1) * DVB] = (
            _rms(ob_s[:, hs], onb_ref[...]) * _silu(post_s[:, P_RB + h * DVB:P_RB + (h + 1) * DVB])).astype(BF16)
    out_w = 2 * LANES
    ga = gated_s[:, 0:HA * DVA]
    gb = gated_s[:, HA * DVA:]
    for n in range(D_MODEL // out_w):
        ns = slice(n * out_w, (n + 1) * out_w)
        y_a = jnp.dot(ga, wa_ref[:, ns], preferred_element_type=F32)
        y_b = jnp.dot(gb, wb_ref[:, ns], preferred_element_type=F32)
        mix_s[:, ns] = (jax.nn.sigmoid(post_s[:, P_GA + n * out_w:P_GA + (n + 1) * out_w]) * y_a
                        + jax.nn.sigmoid(post_s[:, P_GB + n * out_w:P_GB + (n + 1) * out_w]) * y_b).astype(BF16)
    mix = mix_s[...]
    for n in range(D_MODEL // out_w):
        ns = slice(n * out_w, (n + 1) * out_w)
        x1_ref[:, ns] = x_ref[:, ns] + jnp.dot(mix, wo_ref[:, ns], preferred_element_type=F32)


def _const_spec(shape):
    nd = len(shape)
    return pl.BlockSpec(shape, lambda *_: (0,) * nd, pipeline_mode=pl.Buffered(1))


def _mixer_call(x, sa0, sb0, ca0, p, tt):
    b, t, _ = x.shape
    c = CHUNK
    nt = t // tt
    n_inst = (tt // c) * HA
    row = lambda n: _const_spec((1, n))
    in_specs = [
        pl.BlockSpec((None, tt, D_MODEL), lambda i, j: (i, j, 0)),
        _const_spec((HA, DKA, DVA)), _const_spec((HB, DKB, DVB)), _const_spec((CONV_A - 1, C_A)),
        row(D_MODEL), _const_spec((CONV_A, C_A)), row(LANES), row(LANES), row(HB * DKB), row(DVA), row(DVB),
        _const_spec((D_MODEL, W_IN_COLS)), _const_spec((LANES, HB * DKB)),
        _const_spec((HA * DVA, D_MODEL)), _const_spec((HB * DVB, D_MODEL)), _const_spec((D_MODEL, D_MODEL)),
    ]
    out_specs = [
        pl.BlockSpec((None, tt, D_MODEL), lambda i, j: (i, j, 0)),
        pl.BlockSpec((None, HA, DKA, DVA), lambda i, j: (i, 0, 0, 0)),
        pl.BlockSpec((None, CONV_A - 1, C_A), lambda i, j: (i, 0, 0)),
        pl.BlockSpec((None, HB, DKB, DVB), lambda i, j: (i, 0, 0, 0)),
    ]
    out_shape = [
        jax.ShapeDtypeStruct((b, t, D_MODEL), F32),
        jax.ShapeDtypeStruct((b, HA, DKA, DVA), F32),
        jax.ShapeDtypeStruct((b, CONV_A - 1, C_A), F32),
        jax.ShapeDtypeStruct((b, HB, DKB, DVB), F32),
    ]
    assert (tt // c) % 2 == 0, "mixer A handles chunks in pairs"
    pairs = lambda: pltpu.VMEM((n_inst // 2, c, 2 * c), F32)
    scratch = [
        pltpu.VMEM((tt + SUBLANES, C_A), F32),
        pltpu.VMEM((tt, D_MODEL), BF16),
        pltpu.VMEM((tt, W_POST), F32),
        pltpu.VMEM((tt, C_A), F32),
        pltpu.VMEM((tt, LANES), F32),
        pltpu.VMEM((tt, LANES), F32),
        pltpu.VMEM((LANES, tt), F32),
        pltpu.VMEM((tt, 2 * HB * DKB), F32),
        pltpu.VMEM((tt, HB * DVB), F32),
        pltpu.VMEM((tt, HB * DKB), F32),
        pltpu.VMEM((tt, HA * DVA), F32),
        pltpu.VMEM((tt, HB * DVB), F32),
        pairs(), pairs(), pairs(), pairs(),
        pltpu.VMEM((tt, HA * 2 * DVA), F32),
        pltpu.VMEM((tt, HA * 2 * DVA), F32),
        pltpu.VMEM((tt, HA * DKA), F32),
        pltpu.VMEM((tt, HA * DKA), F32),
        pairs(),
        pltpu.VMEM((n_inst, DKB, DVB), F32),
        pltpu.VMEM((tt, HB * DKB), F32),
        pltpu.VMEM((tt, HA * DVA + HB * DVB), BF16),
        pltpu.VMEM((tt, D_MODEL), BF16),
        pltpu.VMEM((n_inst, DKA, DVA), F32),
        pltpu.VMEM((n_inst, DKA, DVA), F32),
    ]
    return pl.pallas_call(
        functools.partial(_mixer_body, tt=tt, c=c),
        grid=(b, nt), in_specs=in_specs, out_specs=out_specs, out_shape=out_shape, scratch_shapes=scratch,
        compiler_params=pltpu.CompilerParams(dimension_semantics=("arbitrary", "arbitrary"),
                                             vmem_limit_bytes=VMEM_LIMIT),
        name="mixer",
    )(x, sa0, sb0, ca0, p["norm_mix"], p["w_conv_a"], p["alog_row"], p["dt_row"], p["b_gk"], p["onorm_a"],
      p["onorm_b"], p["w1"], p["w2"], p["w_a_out"], p["w_b_out"], p["w_o"])


def _ffn_core(x1, u_conv, gf, wdn_ref, nfin):
    act = _gelu_tanh(u_conv) * gf
    x2 = x1 + _dot(act, wdn_ref[...])
    return _rms(x2, nfin)


def _ffn_body(x1_ref, cf0_ref, nffn_ref, wcf_ref, bcf_ref, nfin_ref, wup_ref, wdn_ref,
              y_ref, cf_ref, ubuf, *, tt):
    t_idx = pl.program_id(1)
    base = SUBLANES - (CONV_F - 1)

    @pl.when(t_idx == 0)
    def _():
        ubuf[base:SUBLANES, :] = cf0_ref[...]

    x1 = x1_ref[...]
    h2 = _rms(x1, nffn_ref[...]).astype(BF16)
    ubuf[SUBLANES:SUBLANES + tt, :] = jnp.dot(h2, wup_ref[:, 0:D_FF], preferred_element_type=F32)
    gf = jnp.dot(h2, wup_ref[:, D_FF:], preferred_element_type=F32)
    acc = ubuf[base:base + tt, :] * wcf_ref[0:1, :]
    for i in range(1, CONV_F):
        acc = acc + ubuf[base + i:base + i + tt, :] * wcf_ref[i:i + 1, :]
    acc = acc + bcf_ref[...]
    tail = ubuf[tt + base:tt + SUBLANES, :]
    ubuf[base:SUBLANES, :] = tail
    cf_ref[...] = tail
    y_ref[...] = _ffn_core(x1, acc, gf, wdn_ref, nfin_ref[...])


def _ffn_tail_state_body(x1_ref, nffn_ref, wup_ref, u_ref):
    h2 = _rms(x1_ref[...], nffn_ref[...]).astype(BF16)
    u_ref[...] = jnp.dot(h2, wup_ref[...], preferred_element_type=F32)


def _ffn_tail_state_call(x1_rows, p):
    return pl.pallas_call(
        _ffn_tail_state_body,
        grid=(1,),
        in_specs=[pl.BlockSpec((SUBLANES, D_MODEL), lambda i: (0, 0)), pl.BlockSpec((1, D_MODEL), lambda i: (0, 0)),
                  pl.BlockSpec((D_MODEL, D_FF), lambda i: (0, 0))],
        out_specs=pl.BlockSpec((SUBLANES, D_FF), lambda i: (0, 0)),
        out_shape=jax.ShapeDtypeStruct((SUBLANES, D_FF), F32),
        compiler_params=pltpu.CompilerParams(dimension_semantics=("arbitrary",), vmem_limit_bytes=VMEM_LIMIT),
        name="meta_ffn_state",
    )(x1_rows, p["norm_ffn"], p["w_ffn_in"])


def _ffn_call(x1, cf0, p, tt):
    b, t, _ = x1.shape
    nt = t // tt
    row = lambda n: _const_spec((1, n))
    in_specs = [
        pl.BlockSpec((None, tt, D_MODEL), lambda i, j: (i, j, 0)),
        _const_spec((CONV_F - 1, D_FF)), row(D_MODEL), _const_spec((CONV_F, D_FF)), row(D_FF), row(D_MODEL),
        _const_spec((D_MODEL, 2 * D_FF)), _const_spec((D_FF, D_MODEL)),
    ]
    out_specs = [
        pl.BlockSpec((None, tt, D_MODEL), lambda i, j: (i, j, 0)),
        pl.BlockSpec((None, CONV_F - 1, D_FF), lambda i, j: (i, 0, 0)),
    ]
    out_shape = [jax.ShapeDtypeStruct((b, t, D_MODEL), F32), jax.ShapeDtypeStruct((b, CONV_F - 1, D_FF), F32)]
    return pl.pallas_call(
        functools.partial(_ffn_body, tt=tt),
        grid=(b, nt), in_specs=in_specs, out_specs=out_specs, out_shape=out_shape,
        scratch_shapes=[pltpu.VMEM((tt + SUBLANES, D_FF), F32)],
        compiler_params=pltpu.CompilerParams(dimension_semantics=("arbitrary", "arbitrary"),
                                             vmem_limit_bytes=VMEM_LIMIT),
        name="convffn",
    )(x1, cf0, p["norm_ffn"], p["w_conv_f"], p["b_conv_f"], p["norm_final"], p["w_ffn_in"], p["w_ffn_out"])


def _dec_head_body(x_ref, cs_ref, nmix_ref, wca_ref, alog_ref, dt_ref, bgk_ref, w1_ref, w2_ref,
                   qkva_ref, beta_ref, g_ref, qkvb_ref, lg_ref, post_ref, csn_ref):
    x = x_ref[...]
    hb = _rms(x, nmix_ref[...]).astype(BF16)
    pre = jnp.dot(hb, w1_ref[:, O_QKVA:O_QKVA + C_A], preferred_element_type=F32)
    acc = cs_ref[0] * wca_ref[0:1, :]
    for i in range(1, CONV_A - 1):
        acc = acc + cs_ref[i] * wca_ref[i:i + 1, :]
    acc = acc + pre * wca_ref[CONV_A - 1:CONV_A, :]
    for i in range(CONV_A - 2):
        csn_ref[i] = cs_ref[i + 1]
    csn_ref[CONV_A - 2] = pre
    qkva_ref[...] = _qkv_a_post(acc)
    small = jnp.dot(hb, w1_ref[:, O_SMALL:O_SMALL + LANES], preferred_element_type=F32)
    beta, g = _gates(small, alog_ref[...], dt_ref[...])
    beta_ref[...] = beta
    g_ref[...] = g
    lg_ref[...] = jax.nn.log_sigmoid(_dot(small, w2_ref[...]) + bgk_ref[...]) / GLA_GATE_NORM
    qkb = jnp.dot(hb, w1_ref[:, O_QB:O_QB + 2 * HB * DKB], preferred_element_type=F32)
    qkvb_ref[:, 0:HB * DKB] = qkb[:, 0:HB * DKB] * (DKB ** -0.5)
    qkvb_ref[:, HB * DKB:2 * HB * DKB] = qkb[:, HB * DKB:]
    qkvb_ref[:, 2 * HB * DKB:] = jnp.dot(hb, w1_ref[:, O_VB:O_VB + HB * DVB], preferred_element_type=F32)
    post_ref[...] = jnp.dot(hb, w1_ref[:, O_POST:O_POST + W_POST], preferred_element_type=F32)


def _dec_head_call(xs, cs, p):
    n = xs.shape[0]
    shapes = [(n, C_A), (n, LANES), (n, LANES), (n, 2 * HB * DKB + HB * DVB), (n, HB * DKB), (n, W_POST),
              (CONV_A - 1, n, C_A)]
    return pl.pallas_call(
        _dec_head_body,
        out_shape=[jax.ShapeDtypeStruct(s, F32) for s in shapes],
        compiler_params=pltpu.CompilerParams(vmem_limit_bytes=VMEM_LIMIT),
        name="decode_head",
    )(xs, cs, p["norm_mix"], p["w_conv_a"], p["alog_row"], p["dt_row"], p["b_gk"], p["w1"], p["w2"])


def _dec_rec_body(qkva_ref, beta_ref, g_ref, qkvb_ref, lg_ref, sa_ref, sb_ref,
                  oa_ref, ob_ref, san_ref, sbn_ref, *, tb):
    beta = beta_ref[...]
    eg = jnp.exp(g_ref[...])
    for h in range(HA):
        q_rows = qkva_ref[:, h * DKA:(h + 1) * DKA].astype(BF16)
        k_t = qkva_ref[:, HA * DKA + h * DKA:HA * DKA + (h + 1) * DKA].T
        for j in range(tb):
            k_col = k_t[:, j:j + 1]
            s = sa_ref[j, h] * eg[j:j + 1, LANE_G + h:LANE_G + h + 1]
            v = qkva_ref[j:j + 1, 2 * HA * DKA + h * DVA:2 * HA * DKA + (h + 1) * DVA]
            err = (v - jnp.sum(k_col * s, axis=0, keepdims=True)) * beta[j:j + 1, LANE_BETA + h:LANE_BETA + h + 1]
            s = s + k_col * err
            san_ref[j, h] = s
            o_all = jnp.dot(q_rows, s.astype(BF16), preferred_element_type=F32)
            oa_ref[j:j + 1, h * DVA:(h + 1) * DVA] = o_all[j:j + 1, :]
    for h in range(HB):
        q_rows = qkvb_ref[:, h * DKB:(h + 1) * DKB].astype(BF16)
        k_t = qkvb_ref[:, HB * DKB + h * DKB:HB * DKB + (h + 1) * DKB].T
        d_t = jnp.exp(lg_ref[:, h * DKB:(h + 1) * DKB]).T
        for j in range(tb):
            v = qkvb_ref[j:j + 1, 2 * HB * DKB + h * DVB:2 * HB * DKB + (h + 1) * DVB]
            s = sb_ref[j, h] * d_t[:, j:j + 1] + k_t[:, j:j + 1] * v
            sbn_ref[j, h] = s
            o_all = jnp.dot(q_rows, s.astype(BF16), preferred_element_type=F32)
            ob_ref[j:j + 1, h * DVB:(h + 1) * DVB] = o_all[j:j + 1, :]


def _dec_rec_call(qkva, beta, g, qkvb, lg, sa, sb, tb):
    n = qkva.shape[0]
    rows = lambda w: pl.BlockSpec((tb, w), lambda i: (i, 0))
    in_specs = [rows(C_A), rows(LANES), rows(LANES), rows(2 * HB * DKB + HB * DVB), rows(HB * DKB),
                pl.BlockSpec((tb, HA, DKA, DVA), lambda i: (i, 0, 0, 0)),
                pl.BlockSpec((tb, HB, DKB, DVB), lambda i: (i, 0, 0, 0))]
    out_specs = [rows(HA * DVA), rows(HB * DVB),
                 pl.BlockSpec((tb, HA, DKA, DVA), lambda i: (i, 0, 0, 0)),
                 pl.BlockSpec((tb, HB, DKB, DVB), lambda i: (i, 0, 0, 0))]
    out_shape = [jax.ShapeDtypeStruct((n, HA * DVA), F32), jax.ShapeDtypeStruct((n, HB * DVB), F32),
                 jax.ShapeDtypeStruct(sa.shape, F32), jax.ShapeDtypeStruct(sb.shape, F32)]
    return pl.pallas_call(
        functools.partial(_dec_rec_body, tb=tb),
        grid=(n // tb,), in_specs=in_specs, out_specs=out_specs, out_shape=out_shape,
        compiler_params=pltpu.CompilerParams(dimension_semantics=("arbitrary",), vmem_limit_bytes=VMEM_LIMIT),
        name="decode_recurrence",
    )(qkva, beta, g, qkvb, lg, sa, sb)


def _dec_tail_body(x_ref, oa_ref, ob_ref, post_ref, cf_ref, ona_ref, onb_ref, nffn_ref, wcf_ref, bcf_ref, nfin_ref,
                   wa_ref, wb_ref, wo_ref, wup_ref, wdn_ref, y_ref, cfn_ref):
    post = post_ref[...]
    y_a = _branch_a(oa_ref[...], post, ona_ref[...], wa_ref)
    y_b = _branch_b(ob_ref[...], post, onb_ref[...], wb_ref)
    x1 = _mix_residual(x_ref[...], y_a, y_b, post, wo_ref)
    h2 = _rms(x1, nffn_ref[...]).astype(BF16)
    u = jnp.dot(h2, wup_ref[:, 0:D_FF], preferred_element_type=F32)
    gf = jnp.dot(h2, wup_ref[:, D_FF:], preferred_element_type=F32)
    acc = cf_ref[0] * wcf_ref[0:1, :]
    for i in range(1, CONV_F - 1):
        acc = acc + cf_ref[i] * wcf_ref[i:i + 1, :]
    acc = acc + u * wcf_ref[CONV_F - 1:CONV_F, :] + bcf_ref[...]
    for i in range(CONV_F - 2):
        cfn_ref[i] = cf_ref[i + 1]
    cfn_ref[CONV_F - 2] = u
    y_ref[...] = _ffn_core(x1, acc, gf, wdn_ref, nfin_ref[...])


def _dec_tail_call(xs, oa, ob, post, cf, p):
    n = xs.shape[0]
    return pl.pallas_call(
        _dec_tail_body,
        out_shape=[jax.ShapeDtypeStruct((n, D_MODEL), F32), jax.ShapeDtypeStruct((CONV_F - 1, n, D_FF), F32)],
        compiler_params=pltpu.CompilerParams(vmem_limit_bytes=VMEM_LIMIT),
        name="decode_tail",
    )(xs, oa, ob, post, cf, p["onorm_a"], p["onorm_b"], p["norm_ffn"], p["w_conv_f"], p["b_conv_f"],
      p["norm_final"], p["w_a_out"], p["w_b_out"], p["w_o"], p["w_ffn_in"], p["w_ffn_out"])


PROJ_SIZES = (HA * DKA, HA * DKA, HA * DVA, HA * DVA, HA, HA, HB * DKB, HB * DKB, HB * DVB, HB * DVB, GATE_RANK,
              D_MODEL, D_MODEL)
PROJ_ORDER = (0, 1, 2, 6, 7, 8, 4, 5, 10, None, 3, 9, 11, 12)
RELAYOUT_ROWS = 128


def _relayout_body(w_ref, o_ref):
    offs = [0]
    for s in PROJ_SIZES:
        offs.append(offs[-1] + s)
    dst = 0
    for seg in PROJ_ORDER:
        if seg is None:
            width = (-dst) % LANES
            o_ref[:, dst:dst + width] = jnp.zeros((o_ref.shape[0], width), BF16)
        else:
            width = PROJ_SIZES[seg]
            o_ref[:, dst:dst + width] = w_ref[:, offs[seg]:offs[seg] + width].astype(BF16)
        dst += width
    assert dst == W_IN_COLS


def _relayout_w_in(w):
    rows, cols = w.shape
    return pl.pallas_call(
        _relayout_body,
        grid=(rows // RELAYOUT_ROWS,),
        in_specs=[pl.BlockSpec((RELAYOUT_ROWS, cols), lambda i: (i, 0))],
        out_specs=pl.BlockSpec((RELAYOUT_ROWS, W_IN_COLS), lambda i: (i, 0)),
        out_shape=jax.ShapeDtypeStruct((rows, W_IN_COLS), BF16),
        compiler_params=pltpu.CompilerParams(dimension_semantics=("arbitrary",), vmem_limit_bytes=VMEM_LIMIT),
        name="relayout_w_in",
    )(w)


def _relayout_t_body(wt_ref, o_ref):
    offs = [0]
    for s in PROJ_SIZES:
        offs.append(offs[-1] + s)
    n_small = 2 * HA + GATE_RANK
    blocks = []
    for seg in PROJ_ORDER:
        if seg in (4, 5, 10, None):
            continue
        if seg == 3:
            blocks.append("small")
        for r in range(0, PROJ_SIZES[seg], LANES):
            blocks.append(offs[seg] + r)
    assert len(blocks) * LANES == W_IN_COLS
    for j, src in enumerate(blocks):
        if src == "small":
            blk = jnp.concatenate([wt_ref[offs[4]:offs[6], :], wt_ref[offs[10]:offs[11], :],
                                   jnp.zeros((LANES - n_small, wt_ref.shape[1]), F32)], axis=0)
        else:
            blk = wt_ref[src:src + LANES, :]
        o_ref[:, j * LANES:(j + 1) * LANES] = blk.T.astype(BF16)


def _relayout_w_in_t(wt):
    return pl.pallas_call(
        _relayout_t_body,
        out_shape=jax.ShapeDtypeStruct((wt.shape[1], W_IN_COLS), BF16),
        compiler_params=pltpu.CompilerParams(vmem_limit_bytes=VMEM_LIMIT),
        name="relayout_w_in_t",
    )(wt)


def _prep_params(l, norm_mix, w_in, w_conv_a, a_log, dt_bias, w_gk2, b_gk, onorm_a, onorm_b, w_a_out, w_b_out,
                 w_o, norm_ffn, w_ffn_in, w_conv_f, b_conv_f, w_ffn_out, norm_final):
    n_small = 2 * HA + GATE_RANK
    w1 = _relayout_w_in_t(jnp.transpose(w_in[l]))
    w2 = jnp.zeros((LANES, HB * DKB), F32).at[2 * HA:n_small].set(w_gk2[l]).astype(BF16)
    lane_row = lambda v: jnp.zeros((1, LANES), F32).at[0, LANE_G:LANE_G + HA].set(v)
    return dict(
        norm_mix=norm_mix[l][None], w_conv_a=w_conv_a[l], alog_row=lane_row(a_log[l]), dt_row=lane_row(dt_bias[l]),
        b_gk=b_gk[l][None], onorm_a=onorm_a[l][None], onorm_b=onorm_b[l][None], w1=w1, w2=w2,
        w_a_out=w_a_out[l].astype(BF16), w_b_out=w_b_out[l].astype(BF16), w_o=w_o[l].astype(BF16),
        norm_ffn=norm_ffn[l][None], w_ffn_in=w_ffn_in[l].astype(BF16), w_conv_f=w_conv_f[l],
        b_conv_f=b_conv_f[l][None], w_ffn_out=w_ffn_out[l].astype(BF16), norm_final=norm_final[None])


PROMPT_TILE = 256
META_TILE = 2 * CHUNK
FFN_TILE = 512
DECODE_TILE = 8


def kernel(x_prompt, x_sample, state_delta, state_delta_conv, state_gla, state_ffn_conv, meta_tokens, norm_mix, w_in, w_conv_a, a_log, dt_bias, w_gk2, b_gk, onorm_a, onorm_b, w_a_out, w_b_out, w_o, norm_ffn, w_ffn_in, w_conv_f, b_conv_f, w_ffn_out, norm_final):
    assert w_in.shape[0] == 1, "single layer only"
    l = 0
    p = _prep_params(l, norm_mix, w_in, w_conv_a, a_log, dt_bias, w_gk2, b_gk, onorm_a, onorm_b, w_a_out, w_b_out,
                     w_o, norm_ffn, w_ffn_in, w_conv_f, b_conv_f, w_ffn_out, norm_final)
    n_dec = x_sample.shape[0]

    xm = jnp.concatenate([jnp.zeros((META_TILE - N_META, D_MODEL), F32), meta_tokens.astype(F32)], axis=0)[None]
    x1m, sa0, ca0, sb0 = _mixer_call(xm, jnp.zeros((HA, DKA, DVA), F32), jnp.zeros((HB, DKB, DVB), F32),
                                     jnp.zeros((CONV_A - 1, C_A), F32), p, META_TILE)
    u_tail = _ffn_tail_state_call(x1m[0, META_TILE - SUBLANES:], p)
    cf0 = u_tail[SUBLANES - (CONV_F - 1):][None]

    x1, sa_p, ca_p, sb_p = _mixer_call(x_prompt, sa0[0], sb0[0], ca0[0], p, PROMPT_TILE)
    y_prompt, cf_p = _ffn_call(x1, cf0[0], p, FFN_TILE)

    xs = x_sample.reshape(n_dec, D_MODEL)
    cs = jnp.transpose(state_delta_conv[l], (1, 0, 2))
    cfs = jnp.transpose(state_ffn_conv[l], (1, 0, 2))
    qkva, beta, g, qkvb, lg, post, cs_new = _dec_head_call(xs, cs, p)
    oa, ob, sa_s, sb_s = _dec_rec_call(qkva, beta, g, qkvb, lg, state_delta[l], state_gla[l], DECODE_TILE)
    y_s, cf_s = _dec_tail_call(xs, oa, ob, post, cfs, p)

    return (y_prompt, y_s.reshape(n_dec, 1, D_MODEL),
            sa_p[None], ca_p[None], sb_p[None], cf_p[None],
            sa_s[None], jnp.transpose(cs_new, (1, 0, 2))[None], sb_s[None],
            jnp.transpose(cf_s, (1, 0, 2))[None])
```

```python
import functools

import jax
import jax.numpy as jnp
from jax import lax
from jax.experimental import pallas as pl
from jax.experimental.pallas import tpu as pltpu

F32 = jnp.float32
BF16 = jnp.bfloat16

D_MODEL = 1024
N_META = 16
CHUNK = 64
HA, DKA, DVA = 4, 128, 128
HB, DKB, DVB = 4, 128, 256
CONV_A = 4
C_A = 2 * HA * DKA + HA * DVA
GATE_RANK = 16
GLA_GATE_NORM = 16.0
D_FF = 2816
CONV_F = 3
EPS = 1e-6
LANES = 128
SUBLANES = 8

O_QKVA = 0
O_QB = O_QKVA + C_A
O_KB = O_QB + HB * DKB
O_VB = O_KB + HB * DKB
O_SMALL = O_VB + HB * DVB
O_POST = O_SMALL + LANES
P_ZA = 0
P_RB = P_ZA + HA * DVA
P_GA = P_RB + HB * DVB
P_GB = P_GA + D_MODEL
W_POST = P_GB + D_MODEL
W_IN_COLS = O_POST + W_POST
LANE_BETA = 0
LANE_G = HA

VMEM_LIMIT = 56 * 1024 * 1024


def _dot(a, b):
    return jnp.dot(a.astype(BF16), b.astype(BF16), preferred_element_type=F32)


def _dot_nt(a, b):
    return lax.dot_general(a.astype(BF16), b.astype(BF16), (((1,), (1,)), ((), ())), preferred_element_type=F32)


def _dot_tn(a, b):
    return lax.dot_general(a.astype(BF16), b.astype(BF16), (((0,), (0,)), ((), ())), preferred_element_type=F32)


def _rms(x, w):
    return x * lax.rsqrt(jnp.mean(x * x, axis=-1, keepdims=True) + EPS) * w


def _silu(x):
    return x * jax.nn.sigmoid(x)


def _gelu_tanh(x):
    return 0.5 * x * (1.0 + jnp.tanh(0.7978845608028654 * (x + 0.044715 * (x * x * x))))


def _l2n(x):
    return x * lax.rsqrt(jnp.sum(x * x, axis=-1, keepdims=True) + EPS)


def _iota2(n, m, axis):
    return lax.broadcasted_iota(jnp.int32, (n, m), axis)


def _gates(small, alog_row, dt_row):
    beta = jax.nn.sigmoid(small)
    g = -jnp.exp(alog_row) * jax.nn.softplus(small + dt_row)
    return beta, g


def _qkv_a_post(conv_out):
    act = _silu(conv_out)
    parts = []
    for h in range(HA):
        parts.append(_l2n(act[:, h * DKA:(h + 1) * DKA]) * (DKA ** -0.5))
    for h in range(HA):
        o = HA * DKA + h * DKA
        parts.append(_l2n(act[:, o:o + DKA]))
    parts.append(act[:, 2 * HA * DKA:])
    return jnp.concatenate(parts, axis=1)


def _branch_a(oa, post, onorm_a, wa_ref):
    pa = [_rms(oa[:, h * DVA:(h + 1) * DVA], onorm_a) for h in range(HA)]
    return _dot(jnp.concatenate(pa, axis=1) * _silu(post[:, P_ZA:P_ZA + HA * DVA]), wa_ref[...])


def _branch_b(ob, post, onorm_b, wb_ref):
    pb = [_rms(ob[:, h * DVB:(h + 1) * DVB], onorm_b) for h in range(HB)]
    return _dot(jnp.concatenate(pb, axis=1) * _silu(post[:, P_RB:P_RB + HB * DVB]), wb_ref[...])


def _mix_residual(x, y_a, y_b, post, wo_ref):
    mix = (jax.nn.sigmoid(post[:, P_GA:P_GA + D_MODEL]) * y_a
           + jax.nn.sigmoid(post[:, P_GB:P_GB + D_MODEL]) * y_b)
    return x + _dot(mix, wo_ref[...])


def _cumsum_chunks(x, c):
    rowi = _iota2(x.shape[0], x.shape[1], 0) % c
    sh = 1
    while sh < c:
        x = x + jnp.where(rowi >= sh, pltpu.roll(x, sh, axis=0), 0.0)
        sh *= 2
    return x


def _delta_prepare(qkv_s, beta_s, gcum_s, gt_s, m_s, qk_s, rhs_s, qe_s, kdec_s, tt, c):
    ri = _iota2(c, 2 * c, 0)
    ci = _iota2(c, 2 * c, 1) % c
    first = _iota2(c, 2 * c, 1) < c
    causal = (ri >= ci).astype(F32)
    strict = (ri > ci).astype(F32)
    zeros = jnp.zeros((c, DKA), F32)
    for cp in range(tt // (2 * c)):
        rows2 = slice(2 * cp * c, (2 * cp + 2) * c)
        for h in range(HA):
            n = cp * HA + h
            hs = slice(h * DKA, (h + 1) * DKA)
            lhs, rhs_t, g_cols = [], [], []
            for par in range(2):
                rows = slice((2 * cp + par) * c, (2 * cp + par + 1) * c)
                g_blk = gcum_s[rows, :]
                q = qkv_s[rows, h * DKA:(h + 1) * DKA]
                k = qkv_s[rows, HA * DKA + h * DKA:HA * DKA + (h + 1) * DKA]
                v = qkv_s[rows, 2 * HA * DKA + h * DVA:2 * HA * DKA + (h + 1) * DVA]
                g_col = g_blk[:, LANE_G + h:LANE_G + h + 1]
                beta_col = beta_s[rows, LANE_BETA + h:LANE_BETA + h + 1]
                eg = jnp.exp(g_col)
                kb = k * beta_col
                lhs.append(jnp.concatenate([kb, q], axis=0))
                rhs_t.append(jnp.concatenate([k, zeros] if par == 0 else [zeros, k], axis=1))
                g_cols.append(g_col)
                rhs_s[rows, h * 2 * DVA:(h + 1) * 2 * DVA] = jnp.concatenate([v * beta_col, kb * eg], axis=1)
                qe_s[rows, hs] = q * eg
                kdec_s[rows, hs] = k * jnp.exp(g_col[c - 1:c, :] - g_col)
            kq = _dot_nt(jnp.concatenate(lhs, axis=1), jnp.concatenate(rhs_t, axis=0))
            g_row = gt_s[LANE_G + h:LANE_G + h + 1, rows2]
            decay = jnp.exp(jnp.minimum(jnp.where(first, g_cols[0], g_cols[1]) - g_row, 0.0))
            m_s[n] = kq[0:c] * (decay * strict)
            qk_s[n] = kq[c:] * (decay * causal)


def _inverse_stages(m_s, p_s, pw_s, n_inst, c):
    ri = _iota2(c, 2 * c, 0)
    lane = _iota2(c, 2 * c, 1)
    ci = lane % c
    left = (lane < c).astype(BF16)
    right = (lane >= c).astype(BF16)

    def bdiag(xp):
        xb = xp.astype(BF16)
        return jnp.concatenate([xb * left, xb * right], axis=0)

    eye = (ri == ci).astype(F32)
    base = min(16, c)
    same = (ri // base) == (ci // base)
    neg_same = jnp.where(same, -1.0, 0.0)
    for n in range(n_inst):
        a = m_s[n] * neg_same
        pw_s[n] = _dot(a, bdiag(a))
        p_s[n] = eye + a
    k = 2
    while k < base:
        for n in range(n_inst):
            p = p_s[n]
            pw = pw_s[n]
            bd = bdiag(pw)
            p_s[n] = p + _dot(p, bd)
            if 2 * k < base:
                pw_s[n] = _dot(pw, bd)
        k *= 2
    size = base
    while size < c:
        nxt = size * 2
        same_nxt = (ri // nxt) == (ci // nxt)
        off = (same_nxt & jnp.logical_not(same)).astype(F32)
        for n in range(n_inst):
            pw_s[n] = _dot(p_s[n], bdiag(m_s[n] * off))
        for n in range(n_inst):
            p = p_s[n]
            p_s[n] = p - _dot(pw_s[n], bdiag(p))
        same = same_nxt
        size = nxt


def _level_ref_rows(bc, half, c):
    w = bc.shape[1]
    parts = []
    if half >= SUBLANES // 2:
        for blk in range(c // (2 * half)):
            mrow = blk * 2 * half + half
            parts.append(jnp.broadcast_to(bc[mrow:mrow + 1, :], (2 * half, w)))
    else:
        sub = _iota2(SUBLANES, w, 0)
        for grp in range(c // SUBLANES):
            acc = None
            for blk in range(SUBLANES // (2 * half)):
                mrow = grp * SUBLANES + blk * 2 * half + half
                b = jnp.broadcast_to(bc[mrow:mrow + 1, :], (SUBLANES, w))
                acc = b if acc is None else jnp.where(sub >= blk * 2 * half, b, acc)
            parts.append(acc)
    return jnp.concatenate(parts, axis=0)


def _gla_prepare_steps(qkb_s, vb_s, bc_s, attn_s, ob_s, pkv_s, qeb_s, tt, c):
    ri = _iota2(c, 2 * c, 0)
    ci = _iota2(c, 2 * c, 1) % c
    first = _iota2(c, 2 * c, 1) < c
    n_ch = tt // c
    zeros_k = jnp.zeros((c, DKB), BF16)
    zeros_v = jnp.zeros((c, DVB), BF16)

    def operands(ch, h):
        rows = slice(ch * c, (ch + 1) * c)
        q = qkb_s[rows, h * DKB:(h + 1) * DKB]
        k = qkb_s[rows, HB * DKB + h * DKB:HB * DKB + (h + 1) * DKB]
        return rows, q, k

    def diag():
        for cp in range(n_ch // 2):
            for h in range(HB):
                d = []
                for par in range(2):
                    _, q, k = operands(2 * cp + par, h)
                    d.append(jnp.sum(q * k, axis=-1, keepdims=True))
                attn_s[cp * HB + h] = jnp.where(ri == ci, jnp.where(first, d[0], d[1]), 0.0)

    def level(half):
        valid = (((ri // (2 * half)) == (ci // (2 * half))) & ((ri % (2 * half)) >= half)
                 & ((ci % (2 * half)) < half)).astype(F32)
        for cp in range(n_ch // 2):
            e_all = []
            for par in range(2):
                bc_all = bc_s[(2 * cp + par) * c:(2 * cp + par + 1) * c, :]
                e_all.append(jnp.exp(-jnp.abs(bc_all - _level_ref_rows(bc_all, half, c))))
            for h in range(HB):
                qs, ks = [], []
                for par in range(2):
                    _, q, k = operands(2 * cp + par, h)
                    e = e_all[par][:, h * DKB:(h + 1) * DKB]
                    qs.append((q * e).astype(BF16))
                    ks.append((k * e).astype(BF16))
                k_bd = jnp.concatenate([jnp.concatenate([ks[0], zeros_k], axis=1),
                                        jnp.concatenate([zeros_k, ks[1]], axis=1)], axis=0)
                attn_s[cp * HB + h] += _dot_nt(jnp.concatenate(qs, axis=1), k_bd) * valid

    def finish():
        for cp in range(n_ch // 2):
            for h in range(HB):
                vs = []
                for par in range(2):
                    rows, q, k = operands(2 * cp + par, h)
                    v = vb_s[rows, h * DVB:(h + 1) * DVB]
                    bc = bc_s[rows, h * DKB:(h + 1) * DKB]
                    vs.append(v.astype(BF16))
                    pkv_s[(2 * cp + par) * HB + h] = _dot_tn(k * jnp.exp(bc[c - 1:c, :] - bc), v)
                    qeb_s[rows, h * DKB:(h + 1) * DKB] = q * jnp.exp(bc)
                v_bd = jnp.concatenate([jnp.concatenate([vs[0], zeros_v], axis=1),
                                        jnp.concatenate([zeros_v, vs[1]], axis=1)], axis=0)
                o2 = _dot(attn_s[cp * HB + h], v_bd)
                ob_s[2 * cp * c:(2 * cp + 1) * c, h * DVB:(h + 1) * DVB] = o2[:, 0:DVB]
                ob_s[(2 * cp + 1) * c:(2 * cp + 2) * c, h * DVB:(h + 1) * DVB] = o2[:, DVB:]

    steps = [diag]
    half = c // 2
    while half >= 1:
        steps.append(functools.partial(level, half))
        half //= 2
    steps.append(finish)
    return steps


def _mixer_body(x_ref, sa0_ref, sb0_ref, ca0_ref, nmix_ref, wca_ref, alog_ref, dt_ref, bgk_ref, ona_ref, onb_ref,
                w1_ref, w2_ref, wa_ref, wb_ref, wo_ref,
                x1_ref, sa_ref, ca_ref, sb_ref,
                cbuf, hb_s, post_s, qkv_s, beta_s, gcum_s, gt_s, qkb_s, vb_s, bc_s, oa_s, ob_s,
                m_s, p_s, pw_s, qk_s, rhs_s, uw_s, qe_s, kdec_s, attn_s, pkv_s, qeb_s, gated_s, mix_s, ktu_s, ktw_s, *, tt, c):
    t_idx = pl.program_id(1)
    n_ch = tt // c

    @pl.when(t_idx == 0)
    def _():
        sa_ref[...] = sa0_ref[...]
        sb_ref[...] = sb0_ref[...]
        cbuf[SUBLANES - (CONV_A - 1):SUBLANES, :] = ca0_ref[...]

    rb = min(tt, 64)
    for r in range(tt // rb):
        rr = slice(r * rb, (r + 1) * rb)
        hb_s[rr, :] = _rms(x_ref[rr, :], nmix_ref[...]).astype(BF16)
    hb = hb_s[...]
    cbuf[SUBLANES:SUBLANES + tt, :] = jnp.dot(hb, w1_ref[:, O_QKVA:O_QKVA + C_A], preferred_element_type=F32)
    base = SUBLANES - (CONV_A - 1)
    n_blk = C_A // LANES
    post_w = 2 * LANES
    n_post = W_POST // post_w
    for j in range(max(n_blk, n_post)):
        if j < n_post:
            pc = slice(j * post_w, (j + 1) * post_w)
            post_s[:, pc] = jnp.dot(hb, w1_ref[:, O_POST + j * post_w:O_POST + (j + 1) * post_w],
                                    preferred_element_type=F32)
        if j < n_blk:
            cols = slice(j * LANES, (j + 1) * LANES)
            full = cbuf[:, cols]
            assert CONV_A == 4
            prev = pltpu.roll(full, 1, axis=0)
            near = full * wca_ref[3:4, cols] + prev * wca_ref[2:3, cols]
            far = full * wca_ref[1:2, cols] + prev * wca_ref[0:1, cols]
            acc = (near + pltpu.roll(far, 2, axis=0))[SUBLANES:]
            act = _silu(acc)
            if j < HA:
                act = _l2n(act) * (DKA ** -0.5)
            elif j < 2 * HA:
                act = _l2n(act)
            qkv_s[:, cols] = act
    tail = cbuf[tt + base:tt + SUBLANES, :]
    cbuf[base:SUBLANES, :] = tail
    ca_ref[...] = tail
    small = jnp.dot(hb, w1_ref[:, O_SMALL:O_SMALL + LANES], preferred_element_type=F32)
    beta, g = _gates(small, alog_ref[...], dt_ref[...])
    beta_s[...] = beta
    g_cum = _cumsum_chunks(g, c)
    gcum_s[...] = g_cum
    gt_s[...] = g_cum.T

    _delta_prepare(qkv_s, beta_s, gcum_s, gt_s, m_s, qk_s, rhs_s, qe_s, kdec_s, tt, c)
    n_pair = (n_ch // 2) * HA
    _inverse_stages(m_s, p_s, pw_s, n_pair, c)
    zeros_uw = jnp.zeros((c, 2 * DVA), BF16)

    def pair_bdiag(ref, cp, cols):
        x_e = ref[2 * cp * c:(2 * cp + 1) * c, cols].astype(BF16)
        x_o = ref[(2 * cp + 1) * c:(2 * cp + 2) * c, cols].astype(BF16)
        return jnp.concatenate([jnp.concatenate([x_e, zeros_uw], axis=1),
                                jnp.concatenate([zeros_uw, x_o], axis=1)], axis=0)

    for cp in range(n_ch // 2):
        rows_e = slice(2 * cp * c, (2 * cp + 1) * c)
        rows_o = slice((2 * cp + 1) * c, (2 * cp + 2) * c)
        for h in range(HA):
            cols = slice(h * 2 * DVA, (h + 1) * 2 * DVA)
            uw2 = _dot(p_s[cp * HA + h], pair_bdiag(rhs_s, cp, cols))
            uw_s[rows_e, cols] = uw2[:, 0:2 * DVA]
            uw_s[rows_o, cols] = uw2[:, 2 * DVA:]
    for cp in range(n_ch // 2):
        for h in range(HA):
            hs = slice(h * DKA, (h + 1) * DKA)
            cols = slice(h * 2 * DVA, (h + 1) * 2 * DVA)
            att2 = _dot(qk_s[cp * HA + h], pair_bdiag(uw_s, cp, cols))
            for par in range(2):
                ch = 2 * cp + par
                rows = slice(ch * c, (ch + 1) * c)
                att_uw = att2[:, par * 2 * DVA:(par + 1) * 2 * DVA]
                oa_s[rows, h * DVA:(h + 1) * DVA] = att_uw[:, 0:DVA]
                qe_s[rows, hs] = qe_s[rows, hs] - att_uw[:, DVA:]
                kt_uw = _dot_tn(kdec_s[rows, hs], uw_s[rows, cols])
                ktu_s[ch * HA + h] = kt_uw[:, 0:DVA]
                ktw_s[ch * HA + h] = kt_uw[:, DVA:]

    small_b = small.astype(BF16)
    for h in range(HB):
        hs = slice(h * DKB, (h + 1) * DKB)
        lg_pre = jnp.dot(small_b, w2_ref[:, hs], preferred_element_type=F32) + bgk_ref[:, hs]
        bc_s[:, hs] = _cumsum_chunks(jax.nn.log_sigmoid(lg_pre) / GLA_GATE_NORM, c)
    qkb_s[:, 0:HB * DKB] = jnp.dot(hb, w1_ref[:, O_QB:O_QB + HB * DKB], preferred_element_type=F32) * (DKB ** -0.5)
    qkb_s[:, HB * DKB:] = jnp.dot(hb, w1_ref[:, O_KB:O_KB + HB * DKB], preferred_element_type=F32)
    vb_s[...] = jnp.dot(hb, w1_ref[:, O_VB:O_VB + HB * DVB], preferred_element_type=F32)

    for step in _gla_prepare_steps(qkb_s, vb_s, bc_s, attn_s, ob_s, pkv_s, qeb_s, tt, c):
        step()

    for ch in range(n_ch):
        rows = slice(ch * c, (ch + 1) * c)
        last = slice(ch * c + c - 1, ch * c + c)
        for h in range(HA):
            n = ch * HA + h
            hs = slice(h * DKA, (h + 1) * DKA)
            s = sa_ref[h]
            prod = _dot(jnp.concatenate([ktw_s[n], qe_s[rows, hs]], axis=0), s)
            oa_s[rows, h * DVA:(h + 1) * DVA] += prod[DKA:]
            g_last = gcum_s[last, LANE_G + h:LANE_G + h + 1]
            sa_ref[h] = s * jnp.exp(g_last) + (ktu_s[n] - prod[0:DKA])

    for h in range(HA):
        hs = slice(h * DVA, (h + 1) * DVA)
        gated_s[:, hs] = (_rms(oa_s[:, hs], ona_ref[...]) * _silu(post_s[:, P_ZA + h * DVA:P_ZA + (h + 1) * DVA])
                          ).astype(BF16)

    for ch in range(n_ch):
        rows = slice(ch * c, (ch + 1) * c)
        last = slice(ch * c + c - 1, ch * c + c)
        for h in range(HB):
            s = sb_ref[h]
            ob_s[rows, h * DVB:(h + 1) * DVB] += _dot(qeb_s[rows, h * DKB:(h + 1) * DKB], s)
            e_col = jnp.exp(bc_s[last, h * DKB:(h + 1) * DKB]).T
            sb_ref[h] = s * e_col + pkv_s[ch * HB + h]

    for h in range(HB):
        hs = slice(h * DVB, (h + 1) * DVB)
        gated_s[:, HA * DVA + h * DVB:HA * DVA + (h + 1) * DVB] = (
            _rms(ob_s[:, hs], onb_ref[...]) * _silu(post_s[:, P_RB + h * DVB:P_RB + (h + 1) * DVB])).astype(BF16)
    out_w = 2 * LANES
    ga = gated_s[:, 0:HA * DVA]
    gb = gated_s[:, HA * DVA:]
    for n in range(D_MODEL // out_w):
        ns = slice(n * out_w, (n + 1) * out_w)
        y_a = jnp.dot(ga, wa_ref[:, ns], preferred_element_type=F32)
        y_b = jnp.dot(gb, wb_ref[:, ns], preferred_element_type=F32)
        mix_s[:, ns] = (jax.nn.sigmoid(post_s[:, P_GA + n * out_w:P_GA + (n + 1) * out_w]) * y_a
                        + jax.nn.sigmoid(post_s[:, P_GB + n * out_w:P_GB + (n + 1) * out_w]) * y_b).astype(BF16)
    mix = mix_s[...]
    for n in range(D_MODEL // out_w):
        ns = slice(n * out_w, (n + 1) * out_w)
        x1_ref[:, ns] = x_ref[:, ns] + jnp.dot(mix, wo_ref[:, ns], preferred_element_type=F32)


def _const_spec(shape):
    nd = len(shape)
    return pl.BlockSpec(shape, lambda *_: (0,) * nd, pipeline_mode=pl.Buffered(1))


def _mixer_call(x, sa0, sb0, ca0, p, tt):
    b, t, _ = x.shape
    c = CHUNK
    nt = t // tt
    n_inst = (tt // c) * HA
    row = lambda n: _const_spec((1, n))
    in_specs = [
        pl.BlockSpec((None, tt, D_MODEL), lambda i, j: (i, j, 0)),
        _const_spec((HA, DKA, DVA)), _const_spec((HB, DKB, DVB)), _const_spec((CONV_A - 1, C_A)),
        row(D_MODEL), _const_spec((CONV_A, C_A)), row(LANES), row(LANES), row(HB * DKB), row(DVA), row(DVB),
        _const_spec((D_MODEL, W_IN_COLS)), _const_spec((LANES, HB * DKB)),
        _const_spec((HA * DVA, D_MODEL)), _const_spec((HB * DVB, D_MODEL)), _const_spec((D_MODEL, D_MODEL)),
    ]
    out_specs = [
        pl.BlockSpec((None, tt, D_MODEL), lambda i, j: (i, j, 0)),
        pl.BlockSpec((None, HA, DKA, DVA), lambda i, j: (i, 0, 0, 0)),
        pl.BlockSpec((None, CONV_A - 1, C_A), lambda i, j: (i, 0, 0)),
        pl.BlockSpec((None, HB, DKB, DVB), lambda i, j: (i, 0, 0, 0)),
    ]
    out_shape = [
        jax.ShapeDtypeStruct((b, t, D_MODEL), F32),
        jax.ShapeDtypeStruct((b, HA, DKA, DVA), F32),
        jax.ShapeDtypeStruct((b, CONV_A - 1, C_A), F32),
        jax.ShapeDtypeStruct((b, HB, DKB, DVB), F32),
    ]
    assert (tt // c) % 2 == 0, "mixer A handles chunks in pairs"
    pairs = lambda: pltpu.VMEM((n_inst // 2, c, 2 * c), F32)
    scratch = [
        pltpu.VMEM((tt + SUBLANES, C_A), F32),
        pltpu.VMEM((tt, D_MODEL), BF16),
        pltpu.VMEM((tt, W_POST), F32),
        pltpu.VMEM((tt, C_A), F32),
        pltpu.VMEM((tt, LANES), F32),
        pltpu.VMEM((tt, LANES), F32),
        pltpu.VMEM((LANES, tt), F32),
        pltpu.VMEM((tt, 2 * HB * DKB), F32),
        pltpu.VMEM((tt, HB * DVB), F32),
        pltpu.VMEM((tt, HB * DKB), F32),
        pltpu.VMEM((tt, HA * DVA), F32),
        pltpu.VMEM((tt, HB * DVB), F32),
        pairs(), pairs(), pairs(), pairs(),
        pltpu.VMEM((tt, HA * 2 * DVA), F32),
        pltpu.VMEM((tt, HA * 2 * DVA), F32),
        pltpu.VMEM((tt, HA * DKA), F32),
        pltpu.VMEM((tt, HA * DKA), F32),
        pairs(),
        pltpu.VMEM((n_inst, DKB, DVB), F32),
        pltpu.VMEM((tt, HB * DKB), F32),
        pltpu.VMEM((tt, HA * DVA + HB * DVB), BF16),
        pltpu.VMEM((tt, D_MODEL), BF16),
        pltpu.VMEM((n_inst, DKA, DVA), F32),
        pltpu.VMEM((n_inst, DKA, DVA), F32),
    ]
    return pl.pallas_call(
        functools.partial(_mixer_body, tt=tt, c=c),
        grid=(b, nt), in_specs=in_specs, out_specs=out_specs, out_shape=out_shape, scratch_shapes=scratch,
        compiler_params=pltpu.CompilerParams(dimension_semantics=("arbitrary", "arbitrary"),
                                             vmem_limit_bytes=VMEM_LIMIT),
        name="mixer",
    )(x, sa0, sb0, ca0, p["norm_mix"], p["w_conv_a"], p["alog_row"], p["dt_row"], p["b_gk"], p["onorm_a"],
      p["onorm_b"], p["w1"], p["w2"], p["w_a_out"], p["w_b_out"], p["w_o"])


def _ffn_core(x1, u_conv, gf, wdn_ref, nfin):
    act = _gelu_tanh(u_conv) * gf
    x2 = x1 + _dot(act, wdn_ref[...])
    return _rms(x2, nfin)


def _ffn_body(x1_ref, cf0_ref, nffn_ref, wcf_ref, bcf_ref, nfin_ref, wup_ref, wdn_ref,
              y_ref, cf_ref, ubuf, *, tt):
    t_idx = pl.program_id(1)
    base = SUBLANES - (CONV_F - 1)

    @pl.when(t_idx == 0)
    def _():
        ubuf[base:SUBLANES, :] = cf0_ref[...]

    x1 = x1_ref[...]
    h2 = _rms(x1, nffn_ref[...]).astype(BF16)
    ubuf[SUBLANES:SUBLANES + tt, :] = jnp.dot(h2, wup_ref[:, 0:D_FF], preferred_element_type=F32)
    gf = jnp.dot(h2, wup_ref[:, D_FF:], preferred_element_type=F32)
    acc = ubuf[base:base + tt, :] * wcf_ref[0:1, :]
    for i in range(1, CONV_F):
        acc = acc + ubuf[base + i:base + i + tt, :] * wcf_ref[i:i + 1, :]
    acc = acc + bcf_ref[...]
    tail = ubuf[tt + base:tt + SUBLANES, :]
    ubuf[base:SUBLANES, :] = tail
    cf_ref[...] = tail
    y_ref[...] = _ffn_core(x1, acc, gf, wdn_ref, nfin_ref[...])


def _ffn_tail_state_body(x1_ref, nffn_ref, wup_ref, u_ref):
    h2 = _rms(x1_ref[...], nffn_ref[...]).astype(BF16)
    u_ref[...] = jnp.dot(h2, wup_ref[...], preferred_element_type=F32)


def _ffn_tail_state_call(x1_rows, p):
    return pl.pallas_call(
        _ffn_tail_state_body,
        grid=(1,),
        in_specs=[pl.BlockSpec((SUBLANES, D_MODEL), lambda i: (0, 0)), pl.BlockSpec((1, D_MODEL), lambda i: (0, 0)),
                  pl.BlockSpec((D_MODEL, D_FF), lambda i: (0, 0))],
        out_specs=pl.BlockSpec((SUBLANES, D_FF), lambda i: (0, 0)),
        out_shape=jax.ShapeDtypeStruct((SUBLANES, D_FF), F32),
        compiler_params=pltpu.CompilerParams(dimension_semantics=("arbitrary",), vmem_limit_bytes=VMEM_LIMIT),
        name="meta_ffn_state",
    )(x1_rows, p["norm_ffn"], p["w_ffn_in"])


def _ffn_call(x1, cf0, p, tt):
    b, t, _ = x1.shape
    nt = t // tt
    row = lambda n: _const_spec((1, n))
    in_specs = [
        pl.BlockSpec((None, tt, D_MODEL), lambda i, j: (i, j, 0)),
        _const_spec((CONV_F - 1, D_FF)), row(D_MODEL), _const_spec((CONV_F, D_FF)), row(D_FF), row(D_MODEL),
        _const_spec((D_MODEL, 2 * D_FF)), _const_spec((D_FF, D_MODEL)),
    ]
    out_specs = [
        pl.BlockSpec((None, tt, D_MODEL), lambda i, j: (i, j, 0)),
        pl.BlockSpec((None, CONV_F - 1, D_FF), lambda i, j: (i, 0, 0)),
    ]
    out_shape = [jax.ShapeDtypeStruct((b, t, D_MODEL), F32), jax.ShapeDtypeStruct((b, CONV_F - 1, D_FF), F32)]
    return pl.pallas_call(
        functools.partial(_ffn_body, tt=tt),
        grid=(b, nt), in_specs=in_specs, out_specs=out_specs, out_shape=out_shape,
        scratch_shapes=[pltpu.VMEM((tt + SUBLANES, D_FF), F32)],
        compiler_params=pltpu.CompilerParams(dimension_semantics=("arbitrary", "arbitrary"),
                                             vmem_limit_bytes=VMEM_LIMIT),
        name="convffn",
    )(x1, cf0, p["norm_ffn"], p["w_conv_f"], p["b_conv_f"], p["norm_final"], p["w_ffn_in"], p["w_ffn_out"])


def _dec_head_body(x_ref, cs_ref, nmix_ref, wca_ref, alog_ref, dt_ref, bgk_ref, w1_ref, w2_ref,
                   qkva_ref, beta_ref, g_ref, qkvb_ref, lg_ref, post_ref, csn_ref):
    x = x_ref[...]
    hb = _rms(x, nmix_ref[...]).astype(BF16)
    pre = jnp.dot(hb, w1_ref[:, O_QKVA:O_QKVA + C_A], preferred_element_type=F32)
    acc = cs_ref[0] * wca_ref[0:1, :]
    for i in range(1, CONV_A - 1):
        acc = acc + cs_ref[i] * wca_ref[i:i + 1, :]
    acc = acc + pre * wca_ref[CONV_A - 1:CONV_A, :]
    for i in range(CONV_A - 2):
        csn_ref[i] = cs_ref[i + 1]
    csn_ref[CONV_A - 2] = pre
    qkva_ref[...] = _qkv_a_post(acc)
    small = jnp.dot(hb, w1_ref[:, O_SMALL:O_SMALL + LANES], preferred_element_type=F32)
    beta, g = _gates(small, alog_ref[...], dt_ref[...])
    beta_ref[...] = beta
    g_ref[...] = g
    lg_ref[...] = jax.nn.log_sigmoid(_dot(small, w2_ref[...]) + bgk_ref[...]) / GLA_GATE_NORM
    qkb = jnp.dot(hb, w1_ref[:, O_QB:O_QB + 2 * HB * DKB], preferred_element_type=F32)
    qkvb_ref[:, 0:HB * DKB] = qkb[:, 0:HB * DKB] * (DKB ** -0.5)
    qkvb_ref[:, HB * DKB:2 * HB * DKB] = qkb[:, HB * DKB:]
    qkvb_ref[:, 2 * HB * DKB:] = jnp.dot(hb, w1_ref[:, O_VB:O_VB + HB * DVB], preferred_element_type=F32)
    post_ref[...] = jnp.dot(hb, w1_ref[:, O_POST:O_POST + W_POST], preferred_element_type=F32)


def _dec_head_call(xs, cs, p):
    n = xs.shape[0]
    shapes = [(n, C_A), (n, LANES), (n, LANES), (n, 2 * HB * DKB + HB * DVB), (n, HB * DKB), (n, W_POST),
              (CONV_A - 1, n, C_A)]
    return pl.pallas_call(
        _dec_head_body,
        out_shape=[jax.ShapeDtypeStruct(s, F32) for s in shapes],
        compiler_params=pltpu.CompilerParams(vmem_limit_bytes=VMEM_LIMIT),
        name="decode_head",
    )(xs, cs, p["norm_mix"], p["w_conv_a"], p["alog_row"], p["dt_row"], p["b_gk"], p["w1"], p["w2"])


def _dec_rec_body(qkva_ref, beta_ref, g_ref, qkvb_ref, lg_ref, sa_ref, sb_ref,
                  oa_ref, ob_ref, san_ref, sbn_ref, *, tb):
    beta = beta_ref[...]
    eg = jnp.exp(g_ref[...])
    for h in range(HA):
        q_rows = qkva_ref[:, h * DKA:(h + 1) * DKA].astype(BF16)
        k_t = qkva_ref[:, HA * DKA + h * DKA:HA * DKA + (h + 1) * DKA].T
        for j in range(tb):
            k_col = k_t[:, j:j + 1]
            s = sa_ref[j, h] * eg[j:j + 1, LANE_G + h:LANE_G + h + 1]
            v = qkva_ref[j:j + 1, 2 * HA * DKA + h * DVA:2 * HA * DKA + (h + 1) * DVA]
            err = (v - jnp.sum(k_col * s, axis=0, keepdims=True)) * beta[j:j + 1, LANE_BETA + h:LANE_BETA + h + 1]
            s = s + k_col * err
            san_ref[j, h] = s
            o_all = jnp.dot(q_rows, s.astype(BF16), preferred_element_type=F32)
            oa_ref[j:j + 1, h * DVA:(h + 1) * DVA] = o_all[j:j + 1, :]
    for h in range(HB):
        q_rows = qkvb_ref[:, h * DKB:(h + 1) * DKB].astype(BF16)
        k_t = qkvb_ref[:, HB * DKB + h * DKB:HB * DKB + (h + 1) * DKB].T
        d_t = jnp.exp(lg_ref[:, h * DKB:(h + 1) * DKB]).T
        for j in range(tb):
            v = qkvb_ref[j:j + 1, 2 * HB * DKB + h * DVB:2 * HB * DKB + (h + 1) * DVB]
            s = sb_ref[j, h] * d_t[:, j:j + 1] + k_t[:, j:j + 1] * v
            sbn_ref[j, h] = s
            o_all = jnp.dot(q_rows, s.astype(BF16), preferred_element_type=F32)
            ob_ref[j:j + 1, h * DVB:(h + 1) * DVB] = o_all[j:j + 1, :]


def _dec_rec_call(qkva, beta, g, qkvb, lg, sa, sb, tb):
    n = qkva.shape[0]
    rows = lambda w: pl.BlockSpec((tb, w), lambda i: (i, 0))
    in_specs = [rows(C_A), rows(LANES), rows(LANES), rows(2 * HB * DKB + HB * DVB), rows(HB * DKB),
                pl.BlockSpec((tb, HA, DKA, DVA), lambda i: (i, 0, 0, 0)),
                pl.BlockSpec((tb, HB, DKB, DVB), lambda i: (i, 0, 0, 0))]
    out_specs = [rows(HA * DVA), rows(HB * DVB),
                 pl.BlockSpec((tb, HA, DKA, DVA), lambda i: (i, 0, 0, 0)),
                 pl.BlockSpec((tb, HB, DKB, DVB), lambda i: (i, 0, 0, 0))]
    out_shape = [jax.ShapeDtypeStruct((n, HA * DVA), F32), jax.ShapeDtypeStruct((n, HB * DVB), F32),
                 jax.ShapeDtypeStruct(sa.shape, F32), jax.ShapeDtypeStruct(sb.shape, F32)]
    return pl.pallas_call(
        functools.partial(_dec_rec_body, tb=tb),
        grid=(n // tb,), in_specs=in_specs, out_specs=out_specs, out_shape=out_shape,
        compiler_params=pltpu.CompilerParams(dimension_semantics=("arbitrary",), vmem_limit_bytes=VMEM_LIMIT),
        name="decode_recurrence",
    )(qkva, beta, g, qkvb, lg, sa, sb)


def _dec_tail_body(x_ref, oa_ref, ob_ref, post_ref, cf_ref, ona_ref, onb_ref, nffn_ref, wcf_ref, bcf_ref, nfin_ref,
                   wa_ref, wb_ref, wo_ref, wup_ref, wdn_ref, y_ref, cfn_ref):
    post = post_ref[...]
    y_a = _branch_a(oa_ref[...], post, ona_ref[...], wa_ref)
    y_b = _branch_b(ob_ref[...], post, onb_ref[...], wb_ref)
    x1 = _mix_residual(x_ref[...], y_a, y_b, post, wo_ref)
    h2 = _rms(x1, nffn_ref[...]).astype(BF16)
    u = jnp.dot(h2, wup_ref[:, 0:D_FF], preferred_element_type=F32)
    gf = jnp.dot(h2, wup_ref[:, D_FF:], preferred_element_type=F32)
    acc = cf_ref[0] * wcf_ref[0:1, :]
    for i in range(1, CONV_F - 1):
        acc = acc + cf_ref[i] * wcf_ref[i:i + 1, :]
    acc = acc + u * wcf_ref[CONV_F - 1:CONV_F, :] + bcf_ref[...]
    for i in range(CONV_F - 2):
        cfn_ref[i] = cf_ref[i + 1]
    cfn_ref[CONV_F - 2] = u
    y_ref[...] = _ffn_core(x1, acc, gf, wdn_ref, nfin_ref[...])


def _dec_tail_call(xs, oa, ob, post, cf, p):
    n = xs.shape[0]
    return pl.pallas_call(
        _dec_tail_body,
        out_shape=[jax.ShapeDtypeStruct((n, D_MODEL), F32), jax.ShapeDtypeStruct((CONV_F - 1, n, D_FF), F32)],
        compiler_params=pltpu.CompilerParams(vmem_limit_bytes=VMEM_LIMIT),
        name="decode_tail",
    )(xs, oa, ob, post, cf, p["onorm_a"], p["onorm_b"], p["norm_ffn"], p["w_conv_f"], p["b_conv_f"],
      p["norm_final"], p["w_a_out"], p["w_b_out"], p["w_o"], p["w_ffn_in"], p["w_ffn_out"])


PROJ_SIZES = (HA * DKA, HA * DKA, HA * DVA, HA * DVA, HA, HA, HB * DKB, HB * DKB, HB * DVB, HB * DVB, GATE_RANK,
              D_MODEL, D_MODEL)
PROJ_ORDER = (0, 1, 2, 6, 7, 8, 4, 5, 10, None, 3, 9, 11, 12)


def _relayout_t_body(wt_ref, o_ref):
    offs = [0]
    for s in PROJ_SIZES:
        offs.append(offs[-1] + s)
    n_small = 2 * HA + GATE_RANK
    blocks = []
    for seg in PROJ_ORDER:
        if seg in (4, 5, 10, None):
            continue
        if seg == 3:
            blocks.append("small")
        for r in range(0, PROJ_SIZES[seg], LANES):
            blocks.append(offs[seg] + r)
    assert len(blocks) * LANES == W_IN_COLS
    for j, src in enumerate(blocks):
        if src == "small":
            blk = jnp.concatenate([wt_ref[offs[4]:offs[6], :], wt_ref[offs[10]:offs[11], :],
                                   jnp.zeros((LANES - n_small, wt_ref.shape[1]), F32)], axis=0)
        else:
            blk = wt_ref[src:src + LANES, :]
        o_ref[:, j * LANES:(j + 1) * LANES] = blk.T.astype(BF16)


def _relayout_w_in_t(wt):
    return pl.pallas_call(
        _relayout_t_body,
        out_shape=jax.ShapeDtypeStruct((wt.shape[1], W_IN_COLS), BF16),
        compiler_params=pltpu.CompilerParams(vmem_limit_bytes=VMEM_LIMIT),
        name="relayout_w_in_t",
    )(wt)


def _prep_params(l, norm_mix, w_in, w_conv_a, a_log, dt_bias, w_gk2, b_gk, onorm_a, onorm_b, w_a_out, w_b_out,
                 w_o, norm_ffn, w_ffn_in, w_conv_f, b_conv_f, w_ffn_out, norm_final):
    n_small = 2 * HA + GATE_RANK
    w1 = _relayout_w_in_t(jnp.transpose(w_in[l]))
    w2 = jnp.zeros((LANES, HB * DKB), F32).at[2 * HA:n_small].set(w_gk2[l]).astype(BF16)
    lane_row = lambda v: jnp.zeros((1, LANES), F32).at[0, LANE_G:LANE_G + HA].set(v)
    return dict(
        norm_mix=norm_mix[l][None], w_conv_a=w_conv_a[l], alog_row=lane_row(a_log[l]), dt_row=lane_row(dt_bias[l]),
        b_gk=b_gk[l][None], onorm_a=onorm_a[l][None], onorm_b=onorm_b[l][None], w1=w1, w2=w2,
        w_a_out=w_a_out[l].astype(BF16), w_b_out=w_b_out[l].astype(BF16), w_o=w_o[l].astype(BF16),
        norm_ffn=norm_ffn[l][None], w_ffn_in=w_ffn_in[l].astype(BF16), w_conv_f=w_conv_f[l],
        b_conv_f=b_conv_f[l][None], w_ffn_out=w_ffn_out[l].astype(BF16), norm_final=norm_final[None])


PROMPT_TILE = 256
META_TILE = 2 * CHUNK
FFN_TILE = 512
DECODE_TILE = 8


def kernel(x_prompt, x_sample, state_delta, state_delta_conv, state_gla, state_ffn_conv, meta_tokens, norm_mix, w_in, w_conv_a, a_log, dt_bias, w_gk2, b_gk, onorm_a, onorm_b, w_a_out, w_b_out, w_o, norm_ffn, w_ffn_in, w_conv_f, b_conv_f, w_ffn_out, norm_final):
    assert w_in.shape[0] == 1, "single layer only"
    l = 0
    p = _prep_params(l, norm_mix, w_in, w_conv_a, a_log, dt_bias, w_gk2, b_gk, onorm_a, onorm_b, w_a_out, w_b_out,
                     w_o, norm_ffn, w_ffn_in, w_conv_f, b_conv_f, w_ffn_out, norm_final)
    n_dec = x_sample.shape[0]

    xm = jnp.concatenate([jnp.zeros((META_TILE - N_META, D_MODEL), F32), meta_tokens.astype(F32)], axis=0)[None]
    x1m, sa0, ca0, sb0 = _mixer_call(xm, jnp.zeros((HA, DKA, DVA), F32), jnp.zeros((HB, DKB, DVB), F32),
                                     jnp.zeros((CONV_A - 1, C_A), F32), p, META_TILE)
    u_tail = _ffn_tail_state_call(x1m[0, META_TILE - SUBLANES:], p)
    cf0 = u_tail[SUBLANES - (CONV_F - 1):][None]

    x1, sa_p, ca_p, sb_p = _mixer_call(x_prompt, sa0[0], sb0[0], ca0[0], p, PROMPT_TILE)
    y_prompt, cf_p = _ffn_call(x1, cf0[0], p, FFN_TILE)

    xs = x_sample.reshape(n_dec, D_MODEL)
    cs = jnp.transpose(state_delta_conv[l], (1, 0, 2))
    cfs = jnp.transpose(state_ffn_conv[l], (1, 0, 2))
    qkva, beta, g, qkvb, lg, post, cs_new = _dec_head_call(xs, cs, p)
    oa, ob, sa_s, sb_s = _dec_rec_call(qkva, beta, g, qkvb, lg, state_delta[l], state_gla[l], DECODE_TILE)
    y_s, cf_s = _dec_tail_call(xs, oa, ob, post, cfs, p)

    return (y_prompt, y_s.reshape(n_dec, 1, D_MODEL),
            sa_p[None], ca_p[None], sb_p[None], cf_p[None],
            sa_s[None], jnp.transpose(cs_new, (1, 0, 2))[None], sb_s[None],
            jnp.transpose(cf_s, (1, 0, 2))[None])
```
